```python
import math
import jax, jax.numpy as jnp
from jax import lax
import numpy as np

D_MODEL = 1024
BATCH = 1
SEQ = 16384
DEPTH = 4

GRID_W = 64
CTX_LEN = 256
EPS = 1e-6
ROPE_THETA = 10000.0
Q_BLOCK = 128
N_BRANCH = 3

MLA_HEADS = 8
MLA_Q_RANK = 256
MLA_KV_RANK = 128
MLA_NOPE = 64
MLA_ROPE = 32
MLA_V = 64
MLA_QK = MLA_NOPE + MLA_ROPE
MLA_WIDTH = MLA_HEADS * MLA_V

DIF_HEADS = 4
DIF_HEAD_DIM = 64
DIF_WIDTH = DIF_HEADS * 2 * DIF_HEAD_DIM

SSM_HEADS = 8
SSM_HEAD_DIM = 64
SSM_WIDTH = SSM_HEADS * SSM_HEAD_DIM
SSM_GROUPS = 2
SSM_STATE = 128
SSM_CONV = 5
SSM_CHUNK = 128

N_EXPERTS = 32
TOP_K = 4
D_FF = 1024
SWIGLU_LIMIT = 7.0
SWIGLU_ALPHA = 1.702
MOE_BLOCK = 256

MLA_COLS = MLA_Q_RANK + MLA_KV_RANK + MLA_ROPE
DIF_COLS = 3 * DIF_WIDTH
SSM_COLS = SSM_WIDTH + SSM_WIDTH + 2 * SSM_GROUPS * SSM_STATE + 2 * SSM_HEADS
GATE_COLS = N_BRANCH * D_MODEL
IN_COLS = MLA_COLS + DIF_COLS + SSM_COLS + GATE_COLS

kernel_name = 'hybrid_mla_diff_ssd_moe_diffusion_trunk'


def _rms(x, g):
    xf = x.astype(jnp.float32)
    y = xf * lax.rsqrt(jnp.mean(xf * xf, axis=-1, keepdims=True) + EPS)
    return (y * g.astype(jnp.float32)).astype(x.dtype)


def _modulate(x, g, shift, scale):
    return _rms(x, g) * (1.0 + scale) + shift


def _split(x, sizes):
    idx = np.cumsum(sizes)[:-1].tolist()
    return jnp.split(x, idx, axis=-1)


def _axial_rope_tables(seq_len, rot_dim):
    n_rows = seq_len // GRID_W
    row = jnp.repeat(jnp.arange(n_rows), GRID_W).astype(jnp.float32)
    col = jnp.tile(jnp.arange(GRID_W), n_rows).astype(jnp.float32)
    axis_dim = rot_dim // 2
    inv = ROPE_THETA ** (-jnp.arange(0, axis_dim, 2, dtype=jnp.float32) / axis_dim)
    ang_r = row[:, None] * inv
    ang_c = col[:, None] * inv
    return (jnp.cos(ang_r), jnp.sin(ang_r), jnp.cos(ang_c), jnp.sin(ang_c))


def _rotate(x, cos, sin):
    shape = (x.shape[1],) + (1,) * (x.ndim - 3) + (cos.shape[-1],)
    cos = cos.reshape(shape)
    sin = sin.reshape(shape)
    x1, x2 = jnp.split(x.astype(jnp.float32), 2, axis=-1)
    return jnp.concatenate([x1 * cos - x2 * sin, x2 * cos + x1 * sin], axis=-1).astype(x.dtype)


def _axial_rope(x, tabs):
    cr, sr, cc, sc = tabs
    xr, xc = jnp.split(x, 2, axis=-1)
    return jnp.concatenate([_rotate(xr, cr, sr), _rotate(xc, cc, sc)], axis=-1)


def _sweep_query_blocks(step, q):
    bsz, s = q.shape[:2]
    nb = s // Q_BLOCK
    qb = jnp.swapaxes(q.reshape((bsz, nb, Q_BLOCK) + q.shape[2:]), 0, 1)
    out = lax.map(step, qb)
    return jnp.swapaxes(out, 0, 1).reshape((bsz, s) + out.shape[3:])


def _softmax_attend(q, k, v):
    s = jnp.einsum('bqhd,bkhd->bhqk', q, k).astype(jnp.float32) * (q.shape[-1] ** -0.5)
    p = jax.nn.softmax(s, axis=-1).astype(v.dtype)
    return jnp.einsum('bhqk,bkhe->bqhe', p, v)


def _diff_attend(q, k, v, lam):
    s = jnp.einsum('bqhmd,bkhmd->bmhqk', q, k).astype(jnp.float32) * (q.shape[-1] ** -0.5)
    p = jax.nn.softmax(s, axis=-1)
    w = (p[:, 0] - lam * p[:, 1]).astype(v.dtype)
    return jnp.einsum('bhqk,bkhe->bqhe', w, v)


def _rope_tail(t, tabs, n_rot):
    return jnp.concatenate([t[..., :-n_rot], _axial_rope(t[..., -n_rot:], tabs)], axis=-1)


def _mla_q(cq, g_q, w_uq, g_qn):
    q = _rms(cq, g_q) @ w_uq
    q = q.reshape(q.shape[:2] + (MLA_HEADS, MLA_QK))
    return _rms(q, g_qn)


def _mla_kv(ckv, k_rope, g_kv, w_ukv, g_kn):
    kv = (_rms(ckv, g_kv) @ w_ukv).reshape(ckv.shape[:2] + (MLA_HEADS, MLA_NOPE + MLA_V))
    k_nope, v = jnp.split(kv, [MLA_NOPE], axis=-1)
    k_pe = jnp.broadcast_to(k_rope[:, :, None, :], k_nope.shape[:3] + (MLA_ROPE,))
    k = _rms(jnp.concatenate([k_nope, k_pe], axis=-1), g_kn)
    return k, v


def _mla_branch(cols_ctx, cols_lat, rope, g_q, w_uq, g_kv, w_ukv, g_qn, g_kn, emit_ctx):
    cq_l, ckv_l, kr_l = _split(cols_lat, [MLA_Q_RANK, MLA_KV_RANK, MLA_ROPE])
    cq_c, ckv_c, kr_c = _split(cols_ctx, [MLA_Q_RANK, MLA_KV_RANK, MLA_ROPE])
    k_c, v_c = _mla_kv(ckv_c, kr_c, g_kv, w_ukv, g_kn)
    q_l = _rope_tail(_mla_q(cq_l, g_q, w_uq, g_qn), rope, MLA_ROPE)
    k_l, v_l = _mla_kv(ckv_l, kr_l, g_kv, w_ukv, g_kn)
    k_l = _rope_tail(k_l, rope, MLA_ROPE)
    k_all = jnp.concatenate([k_l, k_c], axis=1)
    v_all = jnp.concatenate([v_l, v_c], axis=1)
    y_l = _sweep_query_blocks(lambda qb: _softmax_attend(qb, k_all, v_all), q_l)
    y_l = y_l.reshape(y_l.shape[:2] + (MLA_WIDTH,))
    y_c = None
    if emit_ctx:
        y_c = _softmax_attend(_mla_q(cq_c, g_q, w_uq, g_qn), k_c, v_c)
        y_c = y_c.reshape(y_c.shape[:2] + (MLA_WIDTH,))
    return y_l, y_c


def _diff_qk(t, g, rope):
    t = _rms(t.reshape(t.shape[:2] + (DIF_HEADS, 2, DIF_HEAD_DIM)), g)
    return t if rope is None else _axial_rope(t, rope)


def _diff_v(t):
    return t.reshape(t.shape[:2] + (DIF_HEADS, 2 * DIF_HEAD_DIM))


def _diff_branch(cols_ctx, cols_lat, rope, g_qn, g_kn, lam_p, g_sub, lam_init, emit_ctx):
    q_l, k_l, v_l = _split(cols_lat, [DIF_WIDTH] * 3)
    q_c, k_c, v_c = _split(cols_ctx, [DIF_WIDTH] * 3)
    lp = lam_p.astype(jnp.float32)
    lam = jnp.exp(jnp.sum(lp[0] * lp[1])) - jnp.exp(jnp.sum(lp[2] * lp[3])) + lam_init
    k_c = _diff_qk(k_c, g_kn, None)
    v_c = _diff_v(v_c)
    q_l = _diff_qk(q_l, g_qn, rope)
    k_l = _diff_qk(k_l, g_kn, rope)
    k_all = jnp.concatenate([k_l, k_c], axis=1)
    v_all = jnp.concatenate([_diff_v(v_l), v_c], axis=1)

    def finish(o):
        return (_rms(o, g_sub) * (1.0 - lam_init)).reshape(o.shape[:2] + (DIF_WIDTH,))

    y_l = finish(_sweep_query_blocks(lambda qb: _diff_attend(qb, k_all, v_all, lam), q_l))
    y_c = finish(_diff_attend(_diff_qk(q_c, g_qn, None), k_c, v_c, lam)) if emit_ctx else None
    return y_l, y_c


def _dwconv_centred(u, w, b):
    ch = u.shape[-1]
    k = w.shape[0]
    y = lax.conv_general_dilated(u, w[:, None, :], window_strides=(1,),
                                 padding=[(k // 2, k // 2)],
                                 dimension_numbers=('NWC', 'WIO', 'NWC'),
                                 feature_group_count=ch)
    return y + b


def _ssd(xs, dt, a, bm, cm, h0):
    bsz, seq, nh, hd = xs.shape
    nc = seq // SSM_CHUNK
    xd = (xs * dt[..., None]).reshape(bsz, nc, SSM_CHUNK, nh, hd)
    bm = bm.reshape(bsz, nc, SSM_CHUNK, nh, SSM_STATE)
    cm = cm.reshape(bsz, nc, SSM_CHUNK, nh, SSM_STATE)
    a_cum = jnp.cumsum(jnp.moveaxis((dt * a).reshape(bsz, nc, SSM_CHUNK, nh), 3, 1), axis=-1)
    causal = jnp.tril(jnp.ones((SSM_CHUNK, SSM_CHUNK), bool))
    seg = jnp.exp(jnp.where(causal, a_cum[..., :, None] - a_cum[..., None, :], -jnp.inf))
    scores = jnp.einsum('bclhn,bcshn->bhcls', cm, bm) * seg
    y_diag = jnp.einsum('bhcls,bcshp->bclhp', scores, xd)
    decay_to_end = jnp.exp(a_cum[..., -1:] - a_cum)
    chunk_states = jnp.einsum('bclhn,bhcl,bclhp->bchpn', bm, decay_to_end, xd)
    chunk_decay = jnp.exp(a_cum[..., -1])

    def carry_state(h, inp):
        s_c, d_c = inp
        return (h * d_c[..., None, None] + s_c).astype(h.dtype), h

    h_final, h_in = lax.scan(carry_state, h0,
                             (jnp.moveaxis(chunk_states, 1, 0), jnp.moveaxis(chunk_decay, 2, 0)))
    y_off = jnp.einsum('bclhn,bchpn,bhcl->bclhp', cm, jnp.moveaxis(h_in, 0, 1), jnp.exp(a_cum))
    return (y_diag + y_off).reshape(bsz, seq, nh, hd), h_final


def _ssm_inputs(cols, conv_w, conv_b, dt_bias):
    z, xbc, dt = _split(cols, [SSM_WIDTH, SSM_WIDTH + 2 * SSM_GROUPS * SSM_STATE, 2 * SSM_HEADS])
    xbc = jax.nn.silu(_dwconv_centred(xbc, conv_w, conv_b))
    xs, bs, cs = _split(xbc, [SSM_WIDTH, SSM_GROUPS * SSM_STATE, SSM_GROUPS * SSM_STATE])
    bsz, l = cols.shape[:2]
    rep = SSM_HEADS // SSM_GROUPS
    xs = xs.reshape(bsz, l, SSM_HEADS, SSM_HEAD_DIM)
    bs = jnp.repeat(bs.reshape(bsz, l, SSM_GROUPS, SSM_STATE), rep, axis=2)
    cs = jnp.repeat(cs.reshape(bsz, l, SSM_GROUPS, SSM_STATE), rep, axis=2)
    dt = jax.nn.softplus(dt.reshape(bsz, l, 2, SSM_HEADS) + dt_bias)
    return z, xs, bs, cs, dt


def _flip_if(t, rev):
    return jnp.flip(t, axis=1) if rev else t


def _gated_group_norm(y, z, g):
    bsz, l = y.shape[:2]
    y = y.reshape(bsz, l, SSM_WIDTH) * jax.nn.silu(z)
    y = _rms(y.reshape(bsz, l, SSM_GROUPS, SSM_WIDTH // SSM_GROUPS), g.reshape(SSM_GROUPS, -1))
    return y.reshape(bsz, l, SSM_WIDTH)


def _ssm_branch(cols_ctx, cols_lat, conv_w, conv_b, dt_bias, a_log, d_skip, g_norm, emit_ctx):
    zc, xc, bc, cc, dtc = _ssm_inputs(cols_ctx, conv_w, conv_b, dt_bias)
    zl, xl, bl, cl, dtl = _ssm_inputs(cols_lat, conv_w, conv_b, dt_bias)
    a = -jnp.exp(a_log)
    h0 = jnp.zeros((xl.shape[0], SSM_HEADS, SSM_HEAD_DIM, SSM_STATE), xl.dtype)
    y_l = 0.0
    y_c = 0.0
    for d in range(2):
        rev = d == 1
        skip = d_skip[d][None, None, :, None]
        yc_d, hc = _ssd(_flip_if(xc, rev), _flip_if(dtc[:, :, d], rev), a[d],
                        _flip_if(bc, rev), _flip_if(cc, rev), h0)
        yl_d, _ = _ssd(_flip_if(xl, rev), _flip_if(dtl[:, :, d], rev), a[d],
                       _flip_if(bl, rev), _flip_if(cl, rev), hc)
        y_l = y_l + _flip_if(yl_d, rev) + skip * xl
        y_c = y_c + _flip_if(yc_d, rev) + skip * xc
    out_l = _gated_group_norm(y_l, zl, g_norm)
    out_c = _gated_group_norm(y_c, zc, g_norm) if emit_ctx else None
    return out_l, out_c


def _merge(ya, yb, yc, gate_cols, p):
    g = jax.nn.sigmoid(gate_cols + p['b_gate'])
    g = g.reshape(gate_cols.shape[:-1] + (N_BRANCH, D_MODEL))
    m = (g[..., 0, :] * (ya @ p['w_up_mla']) + g[..., 1, :] * (yb @ p['w_up_dif'])
         + g[..., 2, :] * (yc @ p['w_up_ssm']))
    return m @ p['w_out']


def _moe(h, w_router, b_router, w_gu, b_gu, w_down, b_down):
    n_tok, dm = h.shape
    logits = (h @ w_router + b_router).astype(jnp.float32)
    top_val, top_idx = lax.top_k(logits, TOP_K)
    gates = jax.nn.softmax(top_val, axis=-1).astype(h.dtype)
    n_assign = n_tok * TOP_K
    flat_e = top_idx.reshape(-1).astype(jnp.int32)
    flat_t = jnp.arange(n_assign, dtype=jnp.int32) // TOP_K
    flat_g = gates.reshape(-1)
    counts = jnp.zeros((N_EXPERTS,), jnp.int32).at[flat_e].add(1)
    padded = (counts + MOE_BLOCK - 1) // MOE_BLOCK * MOE_BLOCK
    pad_end = jnp.cumsum(padded)
    pad_start = pad_end - padded
    raw_start = jnp.cumsum(counts) - counts
    order = jnp.argsort(flat_e)
    sorted_e = flat_e[order]
    dest = pad_start[sorted_e] + jnp.arange(n_assign, dtype=jnp.int32) - raw_start[sorted_e]
    n_blocks = -(-(n_assign + N_EXPERTS * (MOE_BLOCK - 1)) // MOE_BLOCK)
    n_slots = n_blocks * MOE_BLOCK
    slot_tok = jnp.full((n_slots,), n_tok, jnp.int32).at[dest].set(flat_t[order])
    slot_gate = jnp.zeros((n_slots,), h.dtype).at[dest].set(flat_g[order])
    block_start = jnp.arange(n_blocks, dtype=jnp.int32) * MOE_BLOCK
    block_e = jnp.minimum(jnp.searchsorted(pad_end, block_start, side='right'), N_EXPERTS - 1)
    h_pad = jnp.concatenate([h, jnp.zeros((1, dm), h.dtype)], axis=0)
    xb = h_pad[slot_tok].reshape(n_blocks, MOE_BLOCK, dm)

    def expert_block(args):
        xblk, e = args
        gu = xblk @ w_gu[e] + b_gu[e]
        glu, lin = jnp.split(gu, 2, axis=-1)
        glu = jnp.minimum(glu, SWIGLU_LIMIT)
        lin = jnp.clip(lin, -SWIGLU_LIMIT, SWIGLU_LIMIT)
        act = glu * jax.nn.sigmoid(SWIGLU_ALPHA * glu) * (lin + 1.0)
        return act @ w_down[e] + b_down[e]

    yb = lax.map(expert_block, (xb, block_e))
    y_slots = yb.reshape(n_slots, dm) * slot_gate[:, None]
    return jnp.zeros((n_tok + 1, dm), h.dtype).at[slot_tok].add(y_slots)[:n_tok]


def _layer(x_lat, x_ctx, c, c_ctx, rope_mla, rope_dif, lam_init, emit_ctx, p):
    mod_l = (jax.nn.silu(c) @ p['w_mod'] + p['b_mod'])[:, None, :]
    mod_c = jax.nn.silu(c_ctx) @ p['w_mod'] + p['b_mod']
    sh1_l, sc1_l, gt1_l, sh2_l, sc2_l, gt2_l = jnp.split(mod_l, 6, axis=-1)
    sh1_c, sc1_c, gt1_c, sh2_c, sc2_c, gt2_c = jnp.split(mod_c, 6, axis=-1)

    h_l = _modulate(x_lat, p['g_norm1'], sh1_l, sc1_l)
    h_c = _modulate(x_ctx, p['g_norm1'], sh1_c, sc1_c)
    u_l = h_l @ p['w_in']
    u_c = h_c @ p['w_in']
    mla_l, dif_l, ssm_l, gate_l = _split(u_l, [MLA_COLS, DIF_COLS, SSM_COLS, GATE_COLS])
    mla_c, dif_c, ssm_c, gate_c = _split(u_c, [MLA_COLS, DIF_COLS, SSM_COLS, GATE_COLS])

    ya_l, ya_c = _mla_branch(mla_c, mla_l, rope_mla, p['mla_g_q'], p['mla_w_uq'], p['mla_g_kv'],
                             p['mla_w_ukv'], p['mla_g_qn'], p['mla_g_kn'], emit_ctx)
    yb_l, yb_c = _diff_branch(dif_c, dif_l, rope_dif, p['dif_g_qn'], p['dif_g_kn'],
                              p['dif_lambda'], p['dif_g_sub'], lam_init, emit_ctx)
    yc_l, yc_c = _ssm_branch(ssm_c, ssm_l, p['ssm_conv_w'], p['ssm_conv_b'], p['ssm_dt_bias'],
                             p['ssm_a_log'], p['ssm_d'], p['ssm_g_norm'], emit_ctx)

    x_lat = x_lat + gt1_l * _merge(ya_l, yb_l, yc_l, gate_l, p)
    f_l = _modulate(x_lat, p['g_norm2'], sh2_l, sc2_l)
    moe_args = (p['moe_w_router'], p['moe_b_router'], p['moe_w_gu'], p['moe_b_gu'],
                p['moe_w_down'], p['moe_b_down'])
    if emit_ctx:
        x_ctx = x_ctx + gt1_c * _merge(ya_c, yb_c, yc_c, gate_c, p)
        f_c = _modulate(x_ctx, p['g_norm2'], sh2_c, sc2_c)
        n_c = f_c.shape[0] * f_c.shape[1]
        y = _moe(jnp.concatenate([f_c.reshape(-1, D_MODEL), f_l.reshape(-1, D_MODEL)], axis=0), *moe_args)
        x_ctx = x_ctx + gt2_c * y[:n_c].reshape(x_ctx.shape)
        y_l = y[n_c:]
    else:
        y_l = _moe(f_l.reshape(-1, D_MODEL), *moe_args)
    x_lat = x_lat + gt2_l * y_l.reshape(x_lat.shape)
    return x_lat, x_ctx


def setup_inputs(seed: int = 0) -> dict:
    key = jax.random.key(seed)
    ks = iter(jax.random.split(key, 48))
    L = DEPTH

    def nrm(shape, scale):
        return jax.random.normal(next(ks), shape, jnp.float32) * scale

    def gain(shape):
        return 1.0 + nrm(shape, 0.05)

    x = nrm((BATCH, SEQ, D_MODEL), 1.0)
    c = nrm((BATCH, D_MODEL), 1.0)
    ctx = nrm((BATCH, CTX_LEN, D_MODEL), 1.0)
    c_ctx = nrm((D_MODEL,), 1.0)
    w_mod = nrm((L, D_MODEL, 6 * D_MODEL), 0.5 * D_MODEL ** -0.5)
    b_mod = nrm((L, 6 * D_MODEL), 0.02)
    g_norm1 = gain((L, D_MODEL))
    g_norm2 = gain((L, D_MODEL))
    w_in = nrm((L, D_MODEL, IN_COLS), D_MODEL ** -0.5)
    b_gate = nrm((L, GATE_COLS), 0.02)
    mla_g_q = gain((L, MLA_Q_RANK))
    mla_w_uq = nrm((L, MLA_Q_RANK, MLA_HEADS * MLA_QK), MLA_Q_RANK ** -0.5)
    mla_g_kv = gain((L, MLA_KV_RANK))
    mla_w_ukv = nrm((L, MLA_KV_RANK, MLA_HEADS * (MLA_NOPE + MLA_V)), MLA_KV_RANK ** -0.5)
    mla_g_qn = gain((L, MLA_QK))
    mla_g_kn = gain((L, MLA_QK))
    dif_g_qn = gain((L, DIF_HEAD_DIM))
    dif_g_kn = gain((L, DIF_HEAD_DIM))
    dif_lambda = nrm((L, 4, DIF_HEAD_DIM), 0.1)
    dif_g_sub = gain((L, 2 * DIF_HEAD_DIM))
    ssm_conv_w = nrm((L, SSM_CONV, SSM_WIDTH + 2 * SSM_GROUPS * SSM_STATE), SSM_CONV ** -0.5)
    ssm_conv_b = nrm((L, SSM_WIDTH + 2 * SSM_GROUPS * SSM_STATE), 0.02)
    dt0 = jnp.exp(jax.random.uniform(next(ks), (L, 2, SSM_HEADS), jnp.float32,
                                     minval=math.log(1e-3), maxval=math.log(1e-1)))
    ssm_dt_bias = dt0 + jnp.log(-jnp.expm1(-dt0))
    ssm_a_log = jnp.log(jax.random.uniform(next(ks), (L, 2, SSM_HEADS), jnp.float32,
                                           minval=1.0, maxval=16.0))
    ssm_d = gain((L, 2, SSM_HEADS))
    ssm_g_norm = gain((L, SSM_WIDTH))
    w_up_mla = nrm((L, MLA_WIDTH, D_MODEL), MLA_WIDTH ** -0.5)
    w_up_dif = nrm((L, DIF_WIDTH, D_MODEL), DIF_WIDTH ** -0.5)
    w_up_ssm = nrm((L, SSM_WIDTH, D_MODEL), SSM_WIDTH ** -0.5)
    w_out = nrm((L, D_MODEL, D_MODEL), D_MODEL ** -0.5)
    moe_w_router = nrm((L, D_MODEL, N_EXPERTS), D_MODEL ** -0.5)
    moe_b_router = nrm((L, N_EXPERTS), 0.01)
    moe_w_gu = nrm((L, N_EXPERTS, D_MODEL, 2 * D_FF), D_MODEL ** -0.5)
    moe_b_gu = nrm((L, N_EXPERTS, 2 * D_FF), 0.02)
    moe_w_down = nrm((L, N_EXPERTS, D_FF, D_MODEL), D_FF ** -0.5)
    moe_b_down = nrm((L, N_EXPERTS, D_MODEL), 0.02)
    return {'x': x, 'c': c, 'ctx': ctx, 'c_ctx': c_ctx, 'w_mod': w_mod, 'b_mod': b_mod,
            'g_norm1': g_norm1, 'g_norm2': g_norm2, 'w_in': w_in, 'b_gate': b_gate,
            'mla_g_q': mla_g_q, 'mla_w_uq': mla_w_uq, 'mla_g_kv': mla_g_kv, 'mla_w_ukv': mla_w_ukv,
            'mla_g_qn': mla_g_qn, 'mla_g_kn': mla_g_kn, 'dif_g_qn': dif_g_qn, 'dif_g_kn': dif_g_kn,
            'dif_lambda': dif_lambda, 'dif_g_sub': dif_g_sub, 'ssm_conv_w': ssm_conv_w,
            'ssm_conv_b': ssm_conv_b, 'ssm_dt_bias': ssm_dt_bias, 'ssm_a_log': ssm_a_log,
            'ssm_d': ssm_d, 'ssm_g_norm': ssm_g_norm, 'w_up_mla': w_up_mla, 'w_up_dif': w_up_dif,
            'w_up_ssm': w_up_ssm, 'w_out': w_out, 'moe_w_router': moe_w_router,
            'moe_b_router': moe_b_router, 'moe_w_gu': moe_w_gu, 'moe_b_gu': moe_b_gu,
            'moe_w_down': moe_w_down, 'moe_b_down': moe_b_down}


def reference(x, c, ctx, c_ctx, w_mod, b_mod, g_norm1, g_norm2, w_in, b_gate, mla_g_q, mla_w_uq,
              mla_g_kv, mla_w_ukv, mla_g_qn, mla_g_kn, dif_g_qn, dif_g_kn, dif_lambda, dif_g_sub,
              ssm_conv_w, ssm_conv_b, ssm_dt_bias, ssm_a_log, ssm_d, ssm_g_norm, w_up_mla, w_up_dif,
              w_up_ssm, w_out, moe_w_router, moe_b_router, moe_w_gu, moe_b_gu, moe_w_down,
              moe_b_down):
    rope_mla = _axial_rope_tables(x.shape[1], MLA_ROPE)
    rope_dif = _axial_rope_tables(x.shape[1], DIF_HEAD_DIM)
    x_lat, x_ctx = x, ctx
    for i in range(DEPTH):
        p = {'w_mod': w_mod[i], 'b_mod': b_mod[i], 'g_norm1': g_norm1[i], 'g_norm2': g_norm2[i],
             'w_in': w_in[i], 'b_gate': b_gate[i], 'mla_g_q': mla_g_q[i], 'mla_w_uq': mla_w_uq[i],
             'mla_g_kv': mla_g_kv[i], 'mla_w_ukv': mla_w_ukv[i], 'mla_g_qn': mla_g_qn[i],
             'mla_g_kn': mla_g_kn[i], 'dif_g_qn': dif_g_qn[i], 'dif_g_kn': dif_g_kn[i],
             'dif_lambda': dif_lambda[i], 'dif_g_sub': dif_g_sub[i], 'ssm_conv_w': ssm_conv_w[i],
             'ssm_conv_b': ssm_conv_b[i], 'ssm_dt_bias': ssm_dt_bias[i], 'ssm_a_log': ssm_a_log[i],
             'ssm_d': ssm_d[i], 'ssm_g_norm': ssm_g_norm[i], 'w_up_mla': w_up_mla[i],
             'w_up_dif': w_up_dif[i], 'w_up_ssm': w_up_ssm[i], 'w_out': w_out[i],
             'moe_w_router': moe_w_router[i], 'moe_b_router': moe_b_router[i],
             'moe_w_gu': moe_w_gu[i], 'moe_b_gu': moe_b_gu[i], 'moe_w_down': moe_w_down[i],
             'moe_b_down': moe_b_down[i]}
        lam_init = 0.8 - 0.6 * math.exp(-0.3 * i)
        x_lat, x_ctx = _layer(x_lat, x_ctx, c, c_ctx, rope_mla, rope_dif, lam_init,
                              i < DEPTH - 1, p)
    return x_lat
```

```python
import functools
import math

import numpy as np
import jax
import jax.numpy as jnp
from jax import lax
from jax.experimental import pallas as pl
from jax.experimental.pallas import tpu as pltpu

F32 = jnp.float32
BF16 = jnp.bfloat16
LANES = 128
VMEM_LIMIT = 52 * 1024 * 1024

D_MODEL = 1024
EPS = 1e-6
ROPE_THETA = 10000.0
GRID_W = 64
N_BRANCH = 3

MLA_HEADS = 8
MLA_Q_RANK = 256
MLA_KV_RANK = 128
MLA_NOPE = 64
MLA_ROPE = 32
MLA_V = 64
MLA_QK = MLA_NOPE + MLA_ROPE
MLA_WIDTH = MLA_HEADS * MLA_V

DIF_HEADS = 4
DIF_HEAD_DIM = 64
DIF_WIDTH = DIF_HEADS * 2 * DIF_HEAD_DIM

SSM_HEADS = 8
SSM_HEAD_DIM = 64
SSM_WIDTH = SSM_HEADS * SSM_HEAD_DIM
SSM_GROUPS = 2
SSM_STATE = 128
SSM_CONV = 5
SSM_CHUNK = 128
SSM_XBC = SSM_WIDTH + 2 * SSM_GROUPS * SSM_STATE

N_EXPERTS = 32
TOP_K = 4
D_FF = 1024
SWIGLU_LIMIT = 7.0
SWIGLU_ALPHA = 1.702
MOE_BLOCK = 256

MLA_COLS = MLA_Q_RANK + MLA_KV_RANK + MLA_ROPE
DIF_COLS = 3 * DIF_WIDTH
SSM_COLS = SSM_WIDTH + SSM_XBC + 2 * SSM_HEADS
GATE_COLS = N_BRANCH * D_MODEL

ROW_TILE = 256
KEY_CHUNK = 256
LOG2E = 1.4426950408889634

U_GATE, U_DQ, U_DK, U_XBC, U_MLA, U_DV, U_Z, U_DT = 0, 3072, 4096, 5120, 6144, 6656, 7168, 7680
U_COLS = 8192
U_TILE_N = 1024


def _in_proj_columns():
    src = np.full((U_COLS,), -1, np.int64)
    dif0 = MLA_COLS
    ssm0 = MLA_COLS + DIF_COLS
    gate0 = ssm0 + SSM_COLS
    src[U_GATE:U_GATE + GATE_COLS] = gate0 + np.arange(GATE_COLS)
    for a in range(2 * DIF_HEADS):
        src[U_DQ + LANES * a:U_DQ + LANES * a + DIF_HEAD_DIM] = dif0 + DIF_HEAD_DIM * a + np.arange(DIF_HEAD_DIM)
        src[U_DK + LANES * a:U_DK + LANES * a + DIF_HEAD_DIM] = (dif0 + DIF_WIDTH + DIF_HEAD_DIM * a
                                                                  + np.arange(DIF_HEAD_DIM))
    src[U_DV:U_DV + DIF_WIDTH] = dif0 + 2 * DIF_WIDTH + np.arange(DIF_WIDTH)
    src[U_MLA:U_MLA + MLA_Q_RANK + MLA_KV_RANK] = np.arange(MLA_Q_RANK + MLA_KV_RANK)
    pe0 = U_MLA + MLA_Q_RANK + MLA_KV_RANK + MLA_NOPE
    src[pe0:pe0 + MLA_ROPE] = MLA_Q_RANK + MLA_KV_RANK + np.arange(MLA_ROPE)
    src[U_Z:U_Z + SSM_WIDTH] = ssm0 + np.arange(SSM_WIDTH)
    src[U_XBC:U_XBC + SSM_XBC] = ssm0 + SSM_WIDTH + np.arange(SSM_XBC)
    src[U_DT:U_DT + 2 * SSM_HEADS] = ssm0 + SSM_WIDTH + SSM_XBC + np.arange(2 * SSM_HEADS)
    return src


def _take_columns(w, src):
    cols = jnp.take(w, jnp.asarray(np.maximum(src, 0)), axis=-1)
    return jnp.where(jnp.asarray(src >= 0), cols, 0.0)


def _head_columns(n_heads, src_stride, src_off, width, dst_stride):
    src = np.full((n_heads * dst_stride,), -1, np.int64)
    for h in range(n_heads):
        src[h * dst_stride:h * dst_stride + width] = h * src_stride + src_off + np.arange(width)
    return src


def _pad_lanes(v, n=LANES):
    return jnp.pad(v, [(0, 0)] * (v.ndim - 1) + [(0, n - v.shape[-1])])


def _row_tile(n, cap):
    best = 8
    for t in range(8, cap + 1, 8):
        if n % t == 0:
            best = t
    return best


def _dot(a, b):
    return jnp.dot(a, b, preferred_element_type=F32)


def _dot_nt(a, b):
    return lax.dot_general(a, b, (((1,), (1,)), ((), ())), preferred_element_type=F32)


def _sigmoid(x):
    return 1.0 / (1.0 + jnp.exp(-x))


def _silu(x):
    return x * _sigmoid(x)


def _rms_rows(x, n):
    return x * lax.rsqrt(jnp.sum(x * x, axis=-1, keepdims=True) * (1.0 / n) + EPS)


def _params(*sem):
    return pltpu.CompilerParams(dimension_semantics=sem, vmem_limit_bytes=VMEM_LIMIT)


def _mod_kernel(a_ref, w_ref, b_ref, o_ref):
    a = _silu(a_ref[...]).astype(BF16)
    o_ref[0] = _dot(a, w_ref[0].astype(BF16)) + b_ref[0]


def _mod_vectors(cc, w_mod, b_mod):
    depth, d, n = w_mod.shape
    tn = 1536
    return pl.pallas_call(
        _mod_kernel,
        out_shape=jax.ShapeDtypeStruct((depth, 8, n), F32),
        grid=(depth, n // tn),
        in_specs=[pl.BlockSpec((8, d), lambda l, j: (0, 0)),
                  pl.BlockSpec((1, d, tn), lambda l, j: (l, 0, j)),
                  pl.BlockSpec((1, 1, tn), lambda l, j: (l, 0, j))],
        out_specs=pl.BlockSpec((1, 8, tn), lambda l, j: (l, 0, j)),
        compiler_params=_params("parallel", "parallel"),
        name="mod_vectors",
    )(cc, w_mod, b_mod.reshape(depth, 1, n))


def _modulated_norm(x, g, mod_lat, mod_ctx, row0, n_ctx):
    rows = x.shape[0]
    is_ctx = (row0 + lax.broadcasted_iota(jnp.int32, (rows, 1), 0)) < n_ctx
    shift = jnp.where(is_ctx, mod_ctx[0:1, :], mod_lat[0:1, :])
    scale = jnp.where(is_ctx, mod_ctx[1:2, :], mod_lat[1:2, :])
    return _rms_rows(x, x.shape[1]) * g * (1.0 + scale) + shift


def _in_proj_kernel(x_ref, g_ref, ml_ref, mc_ref, w_ref, o_ref, h_ref, *, n_ctx, tm):
    @pl.when(pl.program_id(1) == 0)
    def _():
        h = _modulated_norm(x_ref[...], g_ref[...], ml_ref[...], mc_ref[...], pl.program_id(0) * tm, n_ctx)
        h_ref[...] = h.astype(BF16)

    o_ref[...] = _dot(h_ref[...], w_ref[...])


def _in_proj(x_all, g, mod_lat, mod_ctx, w, n_ctx):
    t, d = x_all.shape
    n = w.shape[1]
    tm = _row_tile(t, 1280)
    return pl.pallas_call(
        functools.partial(_in_proj_kernel, n_ctx=n_ctx, tm=tm),
        out_shape=jax.ShapeDtypeStruct((t, n), F32),
        grid=(t // tm, n // U_TILE_N),
        in_specs=[pl.BlockSpec((tm, d), lambda i, j: (i, 0)),
                  pl.BlockSpec((1, d), lambda i, j: (0, 0)),
                  pl.BlockSpec((2, d), lambda i, j: (0, 0)),
                  pl.BlockSpec((2, d), lambda i, j: (0, 0)),
                  pl.BlockSpec((d, U_TILE_N), lambda i, j: (0, j))],
        out_specs=pl.BlockSpec((tm, U_TILE_N), lambda i, j: (i, j)),
        scratch_shapes=[pltpu.VMEM((tm, d), BF16)],
        compiler_params=_params("parallel", "arbitrary"),
        name="in_proj",
    )(x_all, g, mod_lat, mod_ctx, w)


def _rope_tables(seq_len, n_ctx, rot_dim, lane0):
    n_rows = seq_len // GRID_W
    row = jnp.repeat(jnp.arange(n_rows), GRID_W).astype(F32)
    col = jnp.tile(jnp.arange(GRID_W), n_rows).astype(F32)
    axis_dim = rot_dim // 2
    half = axis_dim // 2
    inv = ROPE_THETA ** (-jnp.arange(0, axis_dim, 2, dtype=F32) / axis_dim)
    ang_r = row[:, None] * inv
    ang_c = col[:, None] * inv
    zeros = jnp.zeros((seq_len, half), F32)
    cos = jnp.concatenate([jnp.cos(ang_r), jnp.cos(ang_r), jnp.cos(ang_c), jnp.cos(ang_c)], axis=1)
    s1 = jnp.concatenate([zeros, jnp.sin(ang_r), zeros, jnp.sin(ang_c)], axis=1)
    s2 = jnp.concatenate([-jnp.sin(ang_r), zeros, -jnp.sin(ang_c), zeros], axis=1)

    def place(tab, fill):
        full = jnp.full((seq_len, LANES), fill, F32).at[:, lane0:lane0 + rot_dim].set(tab)
        ctx = jnp.full((n_ctx, LANES), fill, F32)
        return jnp.concatenate([ctx, full], axis=0)

    return place(cos, 1.0), place(s1, 0.0), place(s2, 0.0)


def _rope(x, cos, s1, s2, half):
    return x * cos + pltpu.roll(x, half, 1) * s1 + pltpu.roll(x, LANES - half, 1) * s2


def _mla_prep_kernel(u_ref, gq_ref, wuq_ref, gkv_ref, wk_ref, wv_ref, gqn_ref, gkn_ref,
                     cos_ref, s1_ref, s2_ref, qt_ref, k_ref, vt_ref):
    u = u_ref[...]
    cq = u[:, :MLA_Q_RANK]
    ckv = u[:, MLA_Q_RANK:MLA_Q_RANK + MLA_KV_RANK]
    pe = u[:, MLA_Q_RANK + MLA_KV_RANK:]
    q = _dot((_rms_rows(cq, MLA_Q_RANK) * gq_ref[...]).astype(BF16), wuq_ref[...])
    kv_in = (_rms_rows(ckv, MLA_KV_RANK) * gkv_ref[...]).astype(BF16)
    kn = _dot(kv_in, wk_ref[...])
    v = _dot(kv_in, wv_ref[...])
    cos, s1, s2 = cos_ref[...], s1_ref[...], s2_ref[...]
    half = MLA_ROPE // 4
    q_scale = MLA_QK ** -0.5 * LOG2E
    for h in range(MLA_HEADS):
        qh = _rms_rows(q[:, LANES * h:LANES * (h + 1)], MLA_QK) * gqn_ref[...]
        qh = _rope(qh, cos, s1, s2, half) * q_scale
        qt_ref[h] = qh.T.astype(BF16)
        kh = _rms_rows(kn[:, LANES * h:LANES * (h + 1)] + pe, MLA_QK) * gkn_ref[...]
        k_ref[h, 0] = _rope(kh, cos, s1, s2, half).astype(BF16)
    tm = v.shape[0]
    vt_ref[:, 0] = v.T.reshape(MLA_HEADS, MLA_V, tm).astype(BF16)


def _mla_prep(u, gq, wuq, gkv, wk, wv, gqn, gkn, tabs):
    t = u.shape[0]
    tm = ROW_TILE
    nt = t // tm
    const = lambda i: (0, 0)
    rows = lambda i: (i, 0)
    return pl.pallas_call(
        _mla_prep_kernel,
        out_shape=(jax.ShapeDtypeStruct((MLA_HEADS, LANES, t), BF16),
                   jax.ShapeDtypeStruct((MLA_HEADS, nt, tm, LANES), BF16),
                   jax.ShapeDtypeStruct((MLA_HEADS, nt, MLA_V, tm), BF16)),
        grid=(nt,),
        in_specs=[pl.BlockSpec((tm, 512), lambda i: (i, U_MLA // 512)),
                  pl.BlockSpec((1, MLA_Q_RANK), const),
                  pl.BlockSpec(wuq.shape, const),
                  pl.BlockSpec((1, MLA_KV_RANK), const),
                  pl.BlockSpec(wk.shape, const),
                  pl.BlockSpec(wv.shape, const),
                  pl.BlockSpec((1, LANES), const),
                  pl.BlockSpec((1, LANES), const),
                  pl.BlockSpec((tm, LANES), rows),
                  pl.BlockSpec((tm, LANES), rows),
                  pl.BlockSpec((tm, LANES), rows)],
        out_specs=(pl.BlockSpec((MLA_HEADS, LANES, tm), lambda i: (0, 0, i)),
                   pl.BlockSpec((MLA_HEADS, 1, tm, LANES), lambda i: (0, i, 0, 0)),
                   pl.BlockSpec((MLA_HEADS, 1, MLA_V, tm), lambda i: (0, i, 0, 0))),
        compiler_params=_params("parallel"),
        name="mla_prep",
    )(u, gq, wuq, gkv, wk, wv, gqn, gkn, *tabs)


def _dif_prep_kernel(q_ref, k_ref, v_ref, gq_ref, gk_ref, cos_ref, s1_ref, s2_ref,
                     qt_out, k_out, vt_out):
    cos, s1, s2 = cos_ref[...], s1_ref[...], s2_ref[...]
    half = DIF_HEAD_DIM // 4
    q_scale = DIF_HEAD_DIM ** -0.5 * LOG2E
    for a in range(2 * DIF_HEADS):
        qa = _rms_rows(q_ref[:, LANES * a:LANES * (a + 1)], DIF_HEAD_DIM) * gq_ref[...]
        qt_out[a] = (_rope(qa, cos, s1, s2, half) * q_scale).T.astype(BF16)
        ka = _rms_rows(k_ref[:, LANES * a:LANES * (a + 1)], DIF_HEAD_DIM) * gk_ref[...]
        k_out[a, 0] = _rope(ka, cos, s1, s2, half).astype(BF16)
    v = v_ref[...]
    vt_out[:, 0] = v.T.reshape(DIF_HEADS, 2 * DIF_HEAD_DIM, v.shape[0]).astype(BF16)


def _dif_prep(u, gq, gk, tabs):
    t = u.shape[0]
    tm = ROW_TILE
    nt = t // tm
    nsub = 2 * DIF_HEADS
    const = lambda i: (0, 0)
    rows = lambda i: (i, 0)
    return pl.pallas_call(
        _dif_prep_kernel,
        out_shape=(jax.ShapeDtypeStruct((nsub, LANES, t), BF16),
                   jax.ShapeDtypeStruct((nsub, nt, tm, LANES), BF16),
                   jax.ShapeDtypeStruct((DIF_HEADS, nt, 2 * DIF_HEAD_DIM, tm), BF16)),
        grid=(nt,),
        in_specs=[pl.BlockSpec((tm, 1024), lambda i: (i, U_DQ // 1024)),
                  pl.BlockSpec((tm, 1024), lambda i: (i, U_DK // 1024)),
                  pl.BlockSpec((tm, 512), lambda i: (i, U_DV // 512)),
                  pl.BlockSpec((1, LANES), const),
                  pl.BlockSpec((1, LANES), const),
                  pl.BlockSpec((tm, LANES), rows),
                  pl.BlockSpec((tm, LANES), rows),
                  pl.BlockSpec((tm, LANES), rows)],
        out_specs=(pl.BlockSpec((nsub, LANES, tm), lambda i: (0, 0, i)),
                   pl.BlockSpec((nsub, 1, tm, LANES), lambda i: (0, i, 0, 0)),
                   pl.BlockSpec((DIF_HEADS, 1, 2 * DIF_HEAD_DIM, tm), lambda i: (0, i, 0, 0))),
        compiler_params=_params("parallel"),
        name="dif_prep",
    )(u, u, u, gq, gk, *tabs)


def _attn_pair_state(qt_ref, k_ref, vt_ref, v_of_sub, n_chunks):
    tq = qt_ref.shape[2]
    dv = vt_ref.shape[2]
    qts = (qt_ref[0], qt_ref[1])

    def chunk(j, carry):
        out = []
        for sub in range(2):
            m, l, acc = carry[sub]
            s = _dot(k_ref[sub, j], qts[sub])
            m_new = jnp.maximum(m, jnp.max(s, axis=0, keepdims=True))
            alpha = jnp.exp2(m - m_new)
            p = jnp.exp2(s - m_new)
            l = alpha * l + jnp.sum(p, axis=0, keepdims=True)
            acc = alpha * acc + _dot(vt_ref[v_of_sub[sub], j], p.astype(BF16))
            out.append((m_new, l, acc))
        return tuple(out)

    init = tuple((jnp.full((1, tq), -jnp.inf, F32), jnp.zeros((1, tq), F32), jnp.zeros((dv, tq), F32))
                 for _ in range(2))
    carry = chunk(0, init)
    n_lat = jnp.where(pl.program_id(1) > 0, n_chunks - 1, 0)
    carry = lax.fori_loop(0, n_lat, lambda j, c: chunk(j + 1, c), carry)
    return [(l, acc) for (_, l, acc) in carry]


def _mla_attn_kernel(qt_ref, k_ref, vt_ref, o_ref, *, n_chunks):
    (l0, a0), (l1, a1) = _attn_pair_state(qt_ref, k_ref, vt_ref, (0, 1), n_chunks)
    o = jnp.concatenate([a0 * (1.0 / l0), a1 * (1.0 / l1)], axis=0)
    o_ref[...] = o.T


def _dif_attn_kernel(lam_ref, gsub_ref, qt_ref, k_ref, vt_ref, o_ref, *, n_chunks, lam_init):
    (l0, a0), (l1, a1) = _attn_pair_state(qt_ref, k_ref, vt_ref, (0, 0), n_chunks)
    lp = lam_ref[...]
    lam = (jnp.exp(jnp.sum(lp[0:1] * lp[1:2], axis=-1, keepdims=True))
           - jnp.exp(jnp.sum(lp[2:3] * lp[3:4], axis=-1, keepdims=True)) + lam_init)
    o = a0 * (1.0 / l0) - lam * (a1 * (1.0 / l1))
    o = o * lax.rsqrt(jnp.mean(o * o, axis=0, keepdims=True) + EPS)
    o_ref[...] = o.T * (gsub_ref[...] * (1.0 - lam_init))


def _attention(qt, k, vt, *, lam=None, gsub=None, lam_init=None):
    nsub, _, t = qt.shape
    n_chunks = k.shape[1]
    tq = KEY_CHUNK
    dv = vt.shape[2]
    pairs = nsub // 2
    specs = [pl.BlockSpec((2, LANES, tq), lambda p, i: (p, 0, i)),
             pl.BlockSpec((2, n_chunks, KEY_CHUNK, LANES), lambda p, i: (p, 0, 0, 0))]
    if lam is None:
        body = functools.partial(_mla_attn_kernel, n_chunks=n_chunks)
        specs.append(pl.BlockSpec((2, n_chunks, dv, KEY_CHUNK), lambda p, i: (p, 0, 0, 0)))
        args = (qt, k, vt)
    else:
        body = functools.partial(_dif_attn_kernel, n_chunks=n_chunks, lam_init=lam_init)
        specs = [pl.BlockSpec((8, LANES), lambda p, i: (0, 0)),
                 pl.BlockSpec((1, LANES), lambda p, i: (0, 0))] + specs
        specs.append(pl.BlockSpec((1, n_chunks, dv, KEY_CHUNK), lambda p, i: (p, 0, 0, 0)))
        args = (lam, gsub, qt, k, vt)
    return pl.pallas_call(
        body,
        out_shape=jax.ShapeDtypeStruct((t, pairs * LANES), F32),
        grid=(pairs, t // tq),
        in_specs=specs,
        out_specs=pl.BlockSpec((tq, LANES), lambda p, i: (i, p)),
        compiler_params=_params("parallel", "arbitrary"),
        name="mla_attention" if lam is None else "dif_attention",
    )(*args)


def _ssm_prep_kernel(x_ref, prev_ref, next_ref, dt_ref, w_ref, b_ref, dtb_ref, xo_ref, dto_ref, e_ref,
                     *, n_ctx, n_tok, tm):
    row0 = pl.program_id(0) * tm
    pad = SSM_CONV // 2
    has_prev = jnp.logical_and(row0 != 0, row0 != n_ctx)
    has_next = jnp.logical_and(row0 + tm != n_ctx, row0 + tm != n_tok)
    e_ref[0:8] = jnp.where(has_prev, prev_ref[...], 0.0)
    e_ref[8:8 + tm] = x_ref[...]
    e_ref[8 + tm:16 + tm] = jnp.where(has_next, next_ref[...], 0.0)
    acc = jnp.zeros(x_ref.shape, F32) + b_ref[...]
    for k in range(SSM_CONV):
        acc = acc + w_ref[k:k + 1, :] * e_ref[pl.ds(8 - pad + k, tm), :]
    xo_ref[...] = _silu(acc)
    d = dt_ref[...] + dtb_ref[...]
    dto_ref[...] = jnp.maximum(d, 0.0) + jnp.log1p(jnp.exp(-jnp.abs(d)))


def _ssm_prep(u, conv_w, conv_b, dt_bias, n_ctx):
    t = u.shape[0]
    tm = ROW_TILE
    nt = t // tm
    cb = U_XBC // SSM_XBC
    const = lambda i: (0, 0)
    return pl.pallas_call(
        functools.partial(_ssm_prep_kernel, n_ctx=n_ctx, n_tok=t, tm=tm),
        out_shape=(jax.ShapeDtypeStruct((t, SSM_XBC), F32), jax.ShapeDtypeStruct((t, LANES), F32)),
        grid=(nt,),
        in_specs=[pl.BlockSpec((tm, SSM_XBC), lambda i: (i, cb)),
                  pl.BlockSpec((8, SSM_XBC), lambda i: (jnp.maximum(i * (tm // 8) - 1, 0), cb)),
                  pl.BlockSpec((8, SSM_XBC), lambda i: (jnp.minimum((i + 1) * (tm // 8), t // 8 - 1), cb)),
                  pl.BlockSpec((tm, LANES), lambda i: (i, U_DT // LANES)),
                  pl.BlockSpec((8, SSM_XBC), const),
                  pl.BlockSpec((1, SSM_XBC), const),
                  pl.BlockSpec((1, LANES), const)],
        out_specs=(pl.BlockSpec((tm, SSM_XBC), lambda i: (i, 0)),
                   pl.BlockSpec((tm, LANES), lambda i: (i, 0))),
        scratch_shapes=[pltpu.VMEM((tm + 16, SSM_XBC), F32)],
        compiler_params=_params("parallel"),
        name="ssm_prep",
    )(u, u, u, u, conv_w, conv_b, dt_bias)


def _ssd_kernel(x_ref, dt_ref, alog_ref, o_ref, h_ref, *, direction):
    lc = SSM_CHUNK

    @pl.when(pl.program_id(0) == 0)
    def _():
        h_ref[...] = jnp.zeros(h_ref.shape, F32)

    xbc = x_ref[...]
    dt = dt_ref[...]
    dta = dt * (-jnp.exp(alog_ref[...]))
    r = lax.broadcasted_iota(jnp.int32, (lc, lc), 0)
    c = lax.broadcasted_iota(jnp.int32, (lc, lc), 1)
    keep = (r >= c) if direction == 0 else (r <= c)
    tri = jnp.where(keep, 1.0, 0.0).astype(F32)
    cum = jnp.dot(tri, dta, preferred_element_type=F32, precision=lax.Precision.HIGHEST)
    cum_t = cum.T
    total = jnp.sum(dta, axis=0, keepdims=True)
    to_end = jnp.exp(total - cum)
    from_start = jnp.exp(cum)
    chunk_decay = jnp.exp(total)
    x_t = xbc[:, :SSM_WIDTH].T
    w_t = (dt * to_end).T
    outs = []
    for g in range(SSM_GROUPS):
        b_g = xbc[:, SSM_WIDTH + SSM_STATE * g:SSM_WIDTH + SSM_STATE * (g + 1)].astype(BF16)
        c_g = xbc[:, SSM_WIDTH + SSM_STATE * (SSM_GROUPS + g):
                  SSM_WIDTH + SSM_STATE * (SSM_GROUPS + g + 1)].astype(BF16)
        cb = _dot_nt(c_g, b_g)
        for hh in range(SSM_HEADS // SSM_GROUPS):
            h = g * (SSM_HEADS // SSM_GROUPS) + hh
            col = direction * SSM_HEADS + h
            seg = jnp.exp(jnp.where(keep, cum[:, col:col + 1] - cum_t[col:col + 1, :], -jnp.inf))
            xd = xbc[:, SSM_HEAD_DIM * h:SSM_HEAD_DIM * (h + 1)] * dt[:, col:col + 1]
            y = _dot((cb * seg).astype(BF16), xd.astype(BF16))
            state = h_ref[h]
            y_off = _dot_nt(c_g, state.astype(BF16))
            outs.append(y + y_off * from_start[:, col:col + 1])
            xw_t = x_t[SSM_HEAD_DIM * h:SSM_HEAD_DIM * (h + 1), :] * w_t[col:col + 1, :]
            upd = _dot(xw_t.astype(BF16), b_g)
            h_ref[h] = state * chunk_decay[:, col:col + 1] + upd
    o_ref[...] = jnp.concatenate(outs, axis=1)


def _ssd(xbc, dt, a_log, n_ctx, direction):
    t = xbc.shape[0]
    lc = SSM_CHUNK
    nc = t // lc
    ncc = n_ctx // lc
    if direction == 0:
        order = lambda s: s
    else:
        order = lambda s: jnp.where(s < ncc, ncc - 1 - s, nc - 1 - (s - ncc))
    return pl.pallas_call(
        functools.partial(_ssd_kernel, direction=direction),
        out_shape=jax.ShapeDtypeStruct((t, SSM_WIDTH), F32),
        grid=(nc,),
        in_specs=[pl.BlockSpec((lc, SSM_XBC), lambda s: (order(s), 0)),
                  pl.BlockSpec((lc, LANES), lambda s: (order(s), 0)),
                  pl.BlockSpec((1, LANES), lambda s: (0, 0))],
        out_specs=pl.BlockSpec((lc, SSM_WIDTH), lambda s: (order(s), 0)),
        scratch_shapes=[pltpu.VMEM((SSM_HEADS, SSM_HEAD_DIM, SSM_STATE), F32)],
        compiler_params=_params("arbitrary"),
        name="ssd_fwd" if direction == 0 else "ssd_bwd",
    )(xbc, dt, a_log)


def _merge_kernel(x_ref, gate_ref, ya_ref, yb_ref, yf_ref, yr_ref, xs_ref, z_ref,
                  bg_ref, dskip_ref, gssm_ref, wa_ref, wb_ref, wc_ref, wo_ref,
                  ml_ref, mc_ref, g2_ref, wr_ref, br_ref,
                  xo_ref, f_ref, lg_ref, *, n_ctx, tm):
    y = (yf_ref[...] + yr_ref[...] + dskip_ref[...] * xs_ref[...]) * _silu(z_ref[...])
    gw = SSM_WIDTH // SSM_GROUPS
    yc = jnp.concatenate([_rms_rows(y[:, gw * g:gw * (g + 1)], gw) for g in range(SSM_GROUPS)], axis=1)
    yc = yc * gssm_ref[...]
    gate = _sigmoid(gate_ref[...] + bg_ref[...])
    m = (gate[:, :D_MODEL] * _dot(ya_ref[...].astype(BF16), wa_ref[...])
         + gate[:, D_MODEL:2 * D_MODEL] * _dot(yb_ref[...].astype(BF16), wb_ref[...])
         + gate[:, 2 * D_MODEL:] * _dot(yc.astype(BF16), wc_ref[...]))
    out = _dot(m.astype(BF16), wo_ref[...])
    row0 = pl.program_id(0) * tm
    is_ctx = (row0 + lax.broadcasted_iota(jnp.int32, (tm, 1), 0)) < n_ctx
    gt1 = jnp.where(is_ctx, mc_ref[2:3, :], ml_ref[2:3, :])
    x_new = x_ref[...] + gt1 * out
    xo_ref[...] = x_new
    f = _modulated_norm(x_new, g2_ref[...], ml_ref[3:5, :], mc_ref[3:5, :], row0, n_ctx)
    f_ref[...] = f.astype(BF16)
    lg_ref[...] = jnp.dot(f, wr_ref[...], preferred_element_type=F32,
                          precision=lax.Precision.HIGHEST) + br_ref[...]


def _merge(x_all, u, ya, yb, yf, yr, xbc, b_gate, dskip, g_ssm, wa, wb, wc, wo, mod_lat, mod_ctx,
           g2, w_router, b_router, n_ctx):
    t, d = x_all.shape
    tm = ROW_TILE
    const = lambda i: (0, 0)
    rows = lambda i: (i, 0)
    full = lambda a: pl.BlockSpec(a.shape, const)
    return pl.pallas_call(
        functools.partial(_merge_kernel, n_ctx=n_ctx, tm=tm),
        out_shape=(jax.ShapeDtypeStruct((t, d), F32), jax.ShapeDtypeStruct((t, d), BF16),
                   jax.ShapeDtypeStruct((t, LANES), F32)),
        grid=(t // tm,),
        in_specs=[pl.BlockSpec((tm, d), rows),
                  pl.BlockSpec((tm, GATE_COLS), lambda i: (i, U_GATE // GATE_COLS)),
                  pl.BlockSpec((tm, MLA_WIDTH), rows),
                  pl.BlockSpec((tm, DIF_WIDTH), rows),
                  pl.BlockSpec((tm, SSM_WIDTH), rows),
                  pl.BlockSpec((tm, SSM_WIDTH), rows),
                  pl.BlockSpec((tm, SSM_WIDTH), rows),
                  pl.BlockSpec((tm, SSM_WIDTH), lambda i: (i, U_Z // SSM_WIDTH)),
                  full(b_gate), full(dskip), full(g_ssm), full(wa), full(wb), full(wc), full(wo),
                  full(mod_lat), full(mod_ctx), full(g2), full(w_router), full(b_router)],
        out_specs=(pl.BlockSpec((tm, d), rows), pl.BlockSpec((tm, d), rows),
                   pl.BlockSpec((tm, LANES), rows)),
        compiler_params=_params("parallel"),
        name="merge",
    )(x_all, u, ya, yb, yf, yr, xbc, u, b_gate, dskip, g_ssm, wa, wb, wc, wo, mod_lat, mod_ctx,
      g2, w_router, b_router)


def _moe_kernel(be_ref, nb_ref, x_ref, wgu_ref, bgu_ref, wd_ref, bd_ref, o_ref, wgu_s, wd_s):
    b = pl.program_id(0)
    prev = be_ref[jnp.maximum(b - 1, 0)]
    fresh = jnp.logical_or(b == 0, be_ref[b] != prev)

    @pl.when(fresh)
    def _():
        wgu_s[...] = wgu_ref[...].astype(BF16)
        wd_s[...] = wd_ref[...].astype(BF16)

    @pl.when(b < nb_ref[0])
    def _():
        gu = _dot(x_ref[...], wgu_s[...]) + bgu_ref[...]
        glu = jnp.minimum(gu[:, :D_FF], SWIGLU_LIMIT)
        lin = jnp.clip(gu[:, D_FF:], -SWIGLU_LIMIT, SWIGLU_LIMIT)
        act = glu * _sigmoid(SWIGLU_ALPHA * glu) * (lin + 1.0)
        o_ref[...] = _dot(act.astype(BF16), wd_s[...]) + bd_ref[...]

    @pl.when(b >= nb_ref[0])
    def _():
        o_ref[...] = jnp.zeros(o_ref.shape, F32)


def _moe_experts(block_e, n_used, x_sorted, w_gu, b_gu, w_down, b_down, layer):
    n_slots, d = x_sorted.shape
    n_blocks = n_slots // MOE_BLOCK
    grid_spec = pltpu.PrefetchScalarGridSpec(
        num_scalar_prefetch=2,
        grid=(n_blocks,),
        in_specs=[pl.BlockSpec((MOE_BLOCK, d), lambda b, be, nb: (b, 0)),
                  pl.BlockSpec((None, None, d, 2 * D_FF), lambda b, be, nb: (layer, be[b], 0, 0)),
                  pl.BlockSpec((None, None, 1, 2 * D_FF), lambda b, be, nb: (layer, be[b], 0, 0)),
                  pl.BlockSpec((None, None, D_FF, d), lambda b, be, nb: (layer, be[b], 0, 0)),
                  pl.BlockSpec((None, None, 1, d), lambda b, be, nb: (layer, be[b], 0, 0))],
        out_specs=pl.BlockSpec((MOE_BLOCK, d), lambda b, be, nb: (b, 0)),
        scratch_shapes=[pltpu.VMEM((d, 2 * D_FF), BF16), pltpu.VMEM((D_FF, d), BF16)],
    )
    return pl.pallas_call(
        _moe_kernel,
        out_shape=jax.ShapeDtypeStruct((n_slots, d), F32),
        grid_spec=grid_spec,
        compiler_params=_params("arbitrary"),
        name="moe_experts",
    )(block_e, n_used, x_sorted, w_gu, b_gu, w_down, b_down)


def _route(logits, n_tok):
    top_val, top_idx = lax.top_k(logits, TOP_K)
    gates = jax.nn.softmax(top_val, axis=-1)
    n_assign = n_tok * TOP_K
    flat_e = top_idx.reshape(-1).astype(jnp.int32)
    counts = jnp.zeros((N_EXPERTS,), jnp.int32).at[flat_e].add(1)
    padded = (counts + MOE_BLOCK - 1) // MOE_BLOCK * MOE_BLOCK
    pad_end = jnp.cumsum(padded)
    pad_start = pad_end - padded
    raw_start = jnp.cumsum(counts) - counts
    order = jnp.argsort(flat_e)
    sorted_e = flat_e[order]
    dest = pad_start[sorted_e] + jnp.arange(n_assign, dtype=jnp.int32) - raw_start[sorted_e]
    n_blocks = -(-(n_assign + N_EXPERTS * (MOE_BLOCK - 1)) // MOE_BLOCK)
    n_slots = n_blocks * MOE_BLOCK
    slot_tok = jnp.zeros((n_slots,), jnp.int32).at[dest].set(order // TOP_K)
    slot_of = jnp.zeros((n_assign,), jnp.int32).at[order].set(dest).reshape(n_tok, TOP_K)
    block_start = jnp.arange(n_blocks, dtype=jnp.int32) * MOE_BLOCK
    block_e = jnp.minimum(jnp.searchsorted(pad_end, block_start, side='right'), N_EXPERTS - 1)
    n_used = (pad_end[-1] // MOE_BLOCK).reshape(1)
    return gates, slot_tok, slot_of, block_e.astype(jnp.int32), n_used.astype(jnp.int32)


def kernel(x, c, ctx, c_ctx, w_mod, b_mod, g_norm1, g_norm2, w_in, b_gate, mla_g_q, mla_w_uq, mla_g_kv, mla_w_ukv, mla_g_qn, mla_g_kn, dif_g_qn, dif_g_kn, dif_lambda, dif_g_sub, ssm_conv_w, ssm_conv_b, ssm_dt_bias, ssm_a_log, ssm_d, ssm_g_norm, w_up_mla, w_up_dif, w_up_ssm, w_out, moe_w_router, moe_b_router, moe_w_gu, moe_b_gu, moe_w_down, moe_b_down):
    assert x.shape[0] == 1 and ctx.shape[0] == 1
    depth = w_in.shape[0]
    seq = x.shape[1]
    n_ctx = ctx.shape[1]
    n_tok = n_ctx + seq
    d = D_MODEL
    assert n_ctx == KEY_CHUNK and n_tok % ROW_TILE == 0 and seq % GRID_W == 0

    x_all = jnp.concatenate([ctx[0], x[0]], axis=0)
    cc = jnp.zeros((8, d), F32).at[0].set(c[0]).at[1].set(c_ctx)
    mod = _mod_vectors(cc, w_mod, b_mod)
    mod = mod[:, :2].reshape(depth, 2, 6, d)

    rope_mla = _rope_tables(seq, n_ctx, MLA_ROPE, MLA_NOPE)
    rope_dif = _rope_tables(seq, n_ctx, DIF_HEAD_DIM, 0)
    in_cols = _in_proj_columns()
    uq_cols = _head_columns(MLA_HEADS, MLA_QK, 0, MLA_QK, LANES)
    uk_cols = _head_columns(MLA_HEADS, MLA_NOPE + MLA_V, 0, MLA_NOPE, LANES)
    uv_cols = _head_columns(MLA_HEADS, MLA_NOPE + MLA_V, MLA_NOPE, MLA_V, MLA_V)

    for i in range(depth):
        lam_init = 0.8 - 0.6 * math.exp(-0.3 * i)
        mod_lat, mod_ctx = mod[i, 0], mod[i, 1]
        w_in_i = _take_columns(w_in[i], in_cols).astype(BF16)
        u = _in_proj(x_all, g_norm1[i][None], mod_lat[0:2], mod_ctx[0:2], w_in_i, n_ctx)

        qt, k, vt = _mla_prep(
            u, mla_g_q[i][None], _take_columns(mla_w_uq[i], uq_cols).astype(BF16),
            mla_g_kv[i][None], _take_columns(mla_w_ukv[i], uk_cols).astype(BF16),
            _take_columns(mla_w_ukv[i], uv_cols).astype(BF16),
            _pad_lanes(mla_g_qn[i][None]), _pad_lanes(mla_g_kn[i][None]), rope_mla)
        ya = _attention(qt, k, vt)

        qt, k, vt = _dif_prep(u, _pad_lanes(dif_g_qn[i][None]), _pad_lanes(dif_g_kn[i][None]), rope_dif)
        lam_rows = jnp.zeros((8, LANES), F32).at[:4, :DIF_HEAD_DIM].set(dif_lambda[i])
        yb = _attention(qt, k, vt, lam=lam_rows, gsub=dif_g_sub[i][None], lam_init=lam_init)

        conv_w = jnp.zeros((8, SSM_XBC), F32).at[:SSM_CONV].set(ssm_conv_w[i])
        xbc, dt = _ssm_prep(u, conv_w, ssm_conv_b[i][None], _pad_lanes(ssm_dt_bias[i].reshape(1, -1)), n_ctx)
        a_log = _pad_lanes(ssm_a_log[i].reshape(1, -1))
        yf = _ssd(xbc, dt, a_log, n_ctx, 0)
        yr = _ssd(xbc, dt, a_log, n_ctx, 1)

        dskip = jnp.repeat(ssm_d[i, 0] + ssm_d[i, 1], SSM_HEAD_DIM)[None]
        w_router = jnp.zeros((d, LANES), F32).at[:, :N_EXPERTS].set(moe_w_router[i])
        b_router = jnp.zeros((1, LANES), F32).at[0, :N_EXPERTS].set(moe_b_router[i])
        x_all, f, logits = _merge(
            x_all, u, ya, yb, yf, yr, xbc, b_gate[i][None], dskip, ssm_g_norm[i][None],
            w_up_mla[i].astype(BF16), w_up_dif[i].astype(BF16), w_up_ssm[i].astype(BF16),
            w_out[i].astype(BF16), mod_lat[0:5], mod_ctx[0:5], g_norm2[i][None],
            w_router, b_router, n_ctx)

        gates, slot_tok, slot_of, block_e, n_used = _route(logits[:, :N_EXPERTS], n_tok)
        y_slots = _moe_experts(block_e, n_used, f[slot_tok], moe_w_gu,
                               moe_b_gu.reshape(depth, N_EXPERTS, 1, 2 * D_FF), moe_w_down,
                               moe_b_down.reshape(depth, N_EXPERTS, 1, d), i)
        y = jnp.sum(y_slots[slot_of] * gates[..., None], axis=1)
        gt2 = jnp.concatenate([jnp.broadcast_to(mod_ctx[5], (n_ctx, d)),
                               jnp.broadcast_to(mod_lat[5], (seq, d))], axis=0)
        x_all = x_all + gt2 * y
    return x_all[n_ctx:][None]
```

```python
import functools
import math

import numpy as np
import jax
import jax.numpy as jnp
from jax import lax
from jax.experimental import pallas as pl
from jax.experimental.pallas import tpu as pltpu

F32 = jnp.float32
BF16 = jnp.bfloat16
LANES = 128
VMEM_LIMIT = 52 * 1024 * 1024

D_MODEL = 1024
EPS = 1e-6
ROPE_THETA = 10000.0
GRID_W = 64
N_BRANCH = 3

MLA_HEADS = 8
MLA_Q_RANK = 256
MLA_KV_RANK = 128
MLA_NOPE = 64
MLA_ROPE = 32
MLA_V = 64
MLA_QK = MLA_NOPE + MLA_ROPE
MLA_WIDTH = MLA_HEADS * MLA_V

DIF_HEADS = 4
DIF_HEAD_DIM = 64
DIF_WIDTH = DIF_HEADS * 2 * DIF_HEAD_DIM

SSM_HEADS = 8
SSM_HEAD_DIM = 64
SSM_WIDTH = SSM_HEADS * SSM_HEAD_DIM
SSM_GROUPS = 2
SSM_STATE = 128
SSM_CONV = 5
SSM_CHUNK = 128
SSM_XBC = SSM_WIDTH + 2 * SSM_GROUPS * SSM_STATE

N_EXPERTS = 32
TOP_K = 4
D_FF = 1024
SWIGLU_LIMIT = 7.0
SWIGLU_ALPHA = 1.702
MOE_BLOCK = 256

MLA_COLS = MLA_Q_RANK + MLA_KV_RANK + MLA_ROPE
DIF_COLS = 3 * DIF_WIDTH
SSM_COLS = SSM_WIDTH + SSM_XBC + 2 * SSM_HEADS
GATE_COLS = N_BRANCH * D_MODEL

ROW_TILE = 256
KEY_CHUNK = 256
LOG2E = 1.4426950408889634

U_GATE, U_DQ, U_DK, U_XBC, U_MLA, U_DV, U_Z, U_DT = 0, 3072, 4096, 5120, 6144, 6656, 7168, 7680
U_COLS = 8192
U_TILE_N = 1024


def _in_proj_columns():
    src = np.full((U_COLS,), -1, np.int64)
    dif0 = MLA_COLS
    ssm0 = MLA_COLS + DIF_COLS
    gate0 = ssm0 + SSM_COLS
    src[U_GATE:U_GATE + GATE_COLS] = gate0 + np.arange(GATE_COLS)
    for a in range(2 * DIF_HEADS):
        src[U_DQ + LANES * a:U_DQ + LANES * a + DIF_HEAD_DIM] = dif0 + DIF_HEAD_DIM * a + np.arange(DIF_HEAD_DIM)
        src[U_DK + LANES * a:U_DK + LANES * a + DIF_HEAD_DIM] = (dif0 + DIF_WIDTH + DIF_HEAD_DIM * a
                                                                  + np.arange(DIF_HEAD_DIM))
    src[U_DV:U_DV + DIF_WIDTH] = dif0 + 2 * DIF_WIDTH + np.arange(DIF_WIDTH)
    src[U_MLA:U_MLA + MLA_Q_RANK + MLA_KV_RANK] = np.arange(MLA_Q_RANK + MLA_KV_RANK)
    pe0 = U_MLA + MLA_Q_RANK + MLA_KV_RANK + MLA_NOPE
    src[pe0:pe0 + MLA_ROPE] = MLA_Q_RANK + MLA_KV_RANK + np.arange(MLA_ROPE)
    src[U_Z:U_Z + SSM_WIDTH] = ssm0 + np.arange(SSM_WIDTH)
    src[U_XBC:U_XBC + SSM_XBC] = ssm0 + SSM_WIDTH + np.arange(SSM_XBC)
    src[U_DT:U_DT + 2 * SSM_HEADS] = ssm0 + SSM_WIDTH + SSM_XBC + np.arange(2 * SSM_HEADS)
    return src


def _take_columns(w, src):
    cols = jnp.take(w, jnp.asarray(np.maximum(src, 0)), axis=-1)
    return jnp.where(jnp.asarray(src >= 0), cols, 0.0)


def _head_columns(n_heads, src_stride, src_off, width, dst_stride):
    src = np.full((n_heads * dst_stride,), -1, np.int64)
    for h in range(n_heads):
        src[h * dst_stride:h * dst_stride + width] = h * src_stride + src_off + np.arange(width)
    return src


def _pad_lanes(v, n=LANES):
    return jnp.pad(v, [(0, 0)] * (v.ndim - 1) + [(0, n - v.shape[-1])])


def _row_tile(n, cap):
    best = 8
    for t in range(8, cap + 1, 8):
        if n % t == 0:
            best = t
    return best


def _dot(a, b):
    return jnp.dot(a, b, preferred_element_type=F32)


def _dot_nt(a, b):
    return lax.dot_general(a, b, (((1,), (1,)), ((), ())), preferred_element_type=F32)


def _sigmoid(x):
    return 1.0 / (1.0 + jnp.exp(-x))


def _silu(x):
    return x * _sigmoid(x)


def _rms_rows(x, n):
    return x * lax.rsqrt(jnp.sum(x * x, axis=-1, keepdims=True) * (1.0 / n) + EPS)


def _params(*sem):
    return pltpu.CompilerParams(dimension_semantics=sem, vmem_limit_bytes=VMEM_LIMIT)


def _mod_kernel(a_ref, w_ref, b_ref, o_ref):
    a = _silu(a_ref[...]).astype(BF16)
    o_ref[0] = _dot(a, w_ref[0].astype(BF16)) + b_ref[0]


def _mod_vectors(cc, w_mod, b_mod):
    depth, d, n = w_mod.shape
    tn = 1536
    return pl.pallas_call(
        _mod_kernel,
        out_shape=jax.ShapeDtypeStruct((depth, 8, n), F32),
        grid=(depth, n // tn),
        in_specs=[pl.BlockSpec((8, d), lambda l, j: (0, 0)),
                  pl.BlockSpec((1, d, tn), lambda l, j: (l, 0, j)),
                  pl.BlockSpec((1, 1, tn), lambda l, j: (l, 0, j))],
        out_specs=pl.BlockSpec((1, 8, tn), lambda l, j: (l, 0, j)),
        compiler_params=_params("parallel", "parallel"),
        name="mod_vectors",
    )(cc, w_mod, b_mod.reshape(depth, 1, n))


def _modulated_norm(x, g, mod_lat, mod_ctx, row0, n_ctx):
    rows = x.shape[0]
    is_ctx = (row0 + lax.broadcasted_iota(jnp.int32, (rows, 1), 0)) < n_ctx
    shift = jnp.where(is_ctx, mod_ctx[0:1, :], mod_lat[0:1, :])
    scale = jnp.where(is_ctx, mod_ctx[1:2, :], mod_lat[1:2, :])
    return _rms_rows(x, x.shape[1]) * g * (1.0 + scale) + shift


def _in_proj_kernel(x_ref, g_ref, ml_ref, mc_ref, w_ref, o_ref, h_ref, *, n_ctx, tm):
    @pl.when(pl.program_id(1) == 0)
    def _():
        h = _modulated_norm(x_ref[...], g_ref[...], ml_ref[...], mc_ref[...], pl.program_id(0) * tm, n_ctx)
        h_ref[...] = h.astype(BF16)

    o_ref[...] = _dot(h_ref[...], w_ref[...])


def _in_proj(x_all, g, mod_lat, mod_ctx, w, n_ctx):
    t, d = x_all.shape
    n = w.shape[1]
    tm = _row_tile(t, 1280)
    return pl.pallas_call(
        functools.partial(_in_proj_kernel, n_ctx=n_ctx, tm=tm),
        out_shape=jax.ShapeDtypeStruct((t, n), F32),
        grid=(t // tm, n // U_TILE_N),
        in_specs=[pl.BlockSpec((tm, d), lambda i, j: (i, 0)),
                  pl.BlockSpec((1, d), lambda i, j: (0, 0)),
                  pl.BlockSpec((2, d), lambda i, j: (0, 0)),
                  pl.BlockSpec((2, d), lambda i, j: (0, 0)),
                  pl.BlockSpec((d, U_TILE_N), lambda i, j: (0, j))],
        out_specs=pl.BlockSpec((tm, U_TILE_N), lambda i, j: (i, j)),
        scratch_shapes=[pltpu.VMEM((tm, d), BF16)],
        compiler_params=_params("parallel", "arbitrary"),
        name="in_proj",
    )(x_all, g, mod_lat, mod_ctx, w)


def _rope_tables(seq_len, n_ctx, rot_dim, lane0):
    n_rows = seq_len // GRID_W
    row = jnp.repeat(jnp.arange(n_rows), GRID_W).astype(F32)
    col = jnp.tile(jnp.arange(GRID_W), n_rows).astype(F32)
    axis_dim = rot_dim // 2
    half = axis_dim // 2
    inv = ROPE_THETA ** (-jnp.arange(0, axis_dim, 2, dtype=F32) / axis_dim)
    ang_r = row[:, None] * inv
    ang_c = col[:, None] * inv
    zeros = jnp.zeros((seq_len, half), F32)
    cos = jnp.concatenate([jnp.cos(ang_r), jnp.cos(ang_r), jnp.cos(ang_c), jnp.cos(ang_c)], axis=1)
    s1 = jnp.concatenate([zeros, jnp.sin(ang_r), zeros, jnp.sin(ang_c)], axis=1)
    s2 = jnp.concatenate([-jnp.sin(ang_r), zeros, -jnp.sin(ang_c), zeros], axis=1)

    def place(tab, fill):
        full = jnp.full((seq_len, LANES), fill, F32).at[:, lane0:lane0 + rot_dim].set(tab)
        ctx = jnp.full((n_ctx, LANES), fill, F32)
        return jnp.concatenate([ctx, full], axis=0)

    return place(cos, 1.0), place(s1, 0.0), place(s2, 0.0)


def _rope(x, cos, s1, s2, half):
    return x * cos + pltpu.roll(x, half, 1) * s1 + pltpu.roll(x, LANES - half, 1) * s2


def _mla_prep_kernel(u_ref, gq_ref, wuq_ref, gkv_ref, wk_ref, wv_ref, gqn_ref, gkn_ref,
                     cos_ref, s1_ref, s2_ref, qt_ref, k_ref, vt_ref):
    u = u_ref[...]
    cq = u[:, :MLA_Q_RANK]
    ckv = u[:, MLA_Q_RANK:MLA_Q_RANK + MLA_KV_RANK]
    pe = u[:, MLA_Q_RANK + MLA_KV_RANK:]
    q = _dot((_rms_rows(cq, MLA_Q_RANK) * gq_ref[...]).astype(BF16), wuq_ref[...])
    kv_in = (_rms_rows(ckv, MLA_KV_RANK) * gkv_ref[...]).astype(BF16)
    kn = _dot(kv_in, wk_ref[...])
    v = _dot(kv_in, wv_ref[...])
    cos, s1, s2 = cos_ref[...], s1_ref[...], s2_ref[...]
    half = MLA_ROPE // 4
    q_scale = MLA_QK ** -0.5 * LOG2E
    for h in range(MLA_HEADS):
        qh = _rms_rows(q[:, LANES * h:LANES * (h + 1)], MLA_QK) * gqn_ref[...]
        qh = _rope(qh, cos, s1, s2, half) * q_scale
        qt_ref[h] = qh.T.astype(BF16)
        kh = _rms_rows(kn[:, LANES * h:LANES * (h + 1)] + pe, MLA_QK) * gkn_ref[...]
        k_ref[h, 0] = _rope(kh, cos, s1, s2, half).astype(BF16)
    tm = v.shape[0]
    vt_ref[:, 0] = v.T.reshape(MLA_HEADS, MLA_V, tm).astype(BF16)


def _mla_prep(u, gq, wuq, gkv, wk, wv, gqn, gkn, tabs):
    t = u.shape[0]
    tm = ROW_TILE
    nt = t // tm
    const = lambda i: (0, 0)
    rows = lambda i: (i, 0)
    return pl.pallas_call(
        _mla_prep_kernel,
        out_shape=(jax.ShapeDtypeStruct((MLA_HEADS, LANES, t), BF16),
                   jax.ShapeDtypeStruct((MLA_HEADS, nt, tm, LANES), BF16),
                   jax.ShapeDtypeStruct((MLA_HEADS, nt, MLA_V, tm), BF16)),
        grid=(nt,),
        in_specs=[pl.BlockSpec((tm, 512), lambda i: (i, U_MLA // 512)),
                  pl.BlockSpec((1, MLA_Q_RANK), const),
                  pl.BlockSpec(wuq.shape, const),
                  pl.BlockSpec((1, MLA_KV_RANK), const),
                  pl.BlockSpec(wk.shape, const),
                  pl.BlockSpec(wv.shape, const),
                  pl.BlockSpec((1, LANES), const),
                  pl.BlockSpec((1, LANES), const),
                  pl.BlockSpec((tm, LANES), rows),
                  pl.BlockSpec((tm, LANES), rows),
                  pl.BlockSpec((tm, LANES), rows)],
        out_specs=(pl.BlockSpec((MLA_HEADS, LANES, tm), lambda i: (0, 0, i)),
                   pl.BlockSpec((MLA_HEADS, 1, tm, LANES), lambda i: (0, i, 0, 0)),
                   pl.BlockSpec((MLA_HEADS, 1, MLA_V, tm), lambda i: (0, i, 0, 0))),
        compiler_params=_params("parallel"),
        name="mla_prep",
    )(u, gq, wuq, gkv, wk, wv, gqn, gkn, *tabs)


def _dif_prep_kernel(q_ref, k_ref, v_ref, gq_ref, gk_ref, cos_ref, s1_ref, s2_ref,
                     qt_out, k_out, vt_out):
    cos, s1, s2 = cos_ref[...], s1_ref[...], s2_ref[...]
    half = DIF_HEAD_DIM // 4
    q_scale = DIF_HEAD_DIM ** -0.5 * LOG2E
    for a in range(2 * DIF_HEADS):
        qa = _rms_rows(q_ref[:, LANES * a:LANES * (a + 1)], DIF_HEAD_DIM) * gq_ref[...]
        qt_out[a] = (_rope(qa, cos, s1, s2, half) * q_scale).T.astype(BF16)
        ka = _rms_rows(k_ref[:, LANES * a:LANES * (a + 1)], DIF_HEAD_DIM) * gk_ref[...]
        k_out[a, 0] = _rope(ka, cos, s1, s2, half).astype(BF16)
    v = v_ref[...]
    vt_out[:, 0] = v.T.reshape(DIF_HEADS, 2 * DIF_HEAD_DIM, v.shape[0]).astype(BF16)


def _dif_prep(u, gq, gk, tabs):
    t = u.shape[0]
    tm = ROW_TILE
    nt = t // tm
    nsub = 2 * DIF_HEADS
    const = lambda i: (0, 0)
    rows = lambda i: (i, 0)
    return pl.pallas_call(
        _dif_prep_kernel,
        out_shape=(jax.ShapeDtypeStruct((nsub, LANES, t), BF16),
                   jax.ShapeDtypeStruct((nsub, nt, tm, LANES), BF16),
                   jax.ShapeDtypeStruct((DIF_HEADS, nt, 2 * DIF_HEAD_DIM, tm), BF16)),
        grid=(nt,),
        in_specs=[pl.BlockSpec((tm, 1024), lambda i: (i, U_DQ // 1024)),
                  pl.BlockSpec((tm, 1024), lambda i: (i, U_DK // 1024)),
                  pl.BlockSpec((tm, 512), lambda i: (i, U_DV // 512)),
                  pl.BlockSpec((1, LANES), const),
                  pl.BlockSpec((1, LANES), const),
                  pl.BlockSpec((tm, LANES), rows),
                  pl.BlockSpec((tm, LANES), rows),
                  pl.BlockSpec((tm, LANES), rows)],
        out_specs=(pl.BlockSpec((nsub, LANES, tm), lambda i: (0, 0, i)),
                   pl.BlockSpec((nsub, 1, tm, LANES), lambda i: (0, i, 0, 0)),
                   pl.BlockSpec((DIF_HEADS, 1, 2 * DIF_HEAD_DIM, tm), lambda i: (0, i, 0, 0))),
        compiler_params=_params("parallel"),
        name="dif_prep",
    )(u, u, u, gq, gk, *tabs)


ATTN_GROUP = 4


def _attn_group(qt_ref, k_ref, vt_ref, v_of_sub, s_buf, p_buf, acc_ref, n_chunks):
    tq = qt_ref.shape[2]
    last = n_chunks - 1

    def scores(c):
        out = []
        for a in range(ATTN_GROUP):
            s = _dot(k_ref[a, c], qt_ref[a])
            out.append((s, jnp.max(s, axis=0, keepdims=True)))
        return out

    def stash(sc, slot):
        for a in range(ATTN_GROUP):
            s_buf[slot, a] = sc[a][0]
        return tuple(mx for (_, mx) in sc)

    def softmax(s_of, mx, slot, ml):
        new_ml, alphas = [], []
        for a in range(ATTN_GROUP):
            m, l = ml[a]
            m_new = jnp.maximum(m, mx[a])
            alpha = jnp.exp2(m - m_new)
            p = jnp.exp2(s_of(a) - m_new)
            new_ml.append((m_new, alpha * l + jnp.sum(p, axis=0, keepdims=True)))
            p_buf[slot, a] = p.astype(BF16)
            alphas.append(alpha)
        return tuple(new_ml), tuple(alphas)

    def values(c, slot, alphas):
        for a in range(ATTN_GROUP):
            acc_ref[a] = alphas[a] * acc_ref[a] + _dot(vt_ref[v_of_sub[a], c], p_buf[slot, a])

    def four_chunks(t, state):
        ml, alphas, mx0 = state
        c = 4 * t + 1
        for half in range(2):
            cur, nxt = half, 1 - half
            even = scores(c + 1)
            values(c - 1, 0, alphas)
            ml, alphas = softmax(lambda a: s_buf[cur, a], mx0, 1, ml)
            mx0 = stash(scores(jnp.minimum(c + 2, last)), nxt)
            values(c, 1, alphas)
            ml, alphas = softmax(lambda a: even[a][0], [mx for (_, mx) in even], 0, ml)
            c = c + 2
        return ml, alphas, mx0

    acc_ref[...] = jnp.zeros(acc_ref.shape, F32)
    ml = tuple((jnp.full((1, tq), -jnp.inf, F32), jnp.zeros((1, tq), F32)) for _ in range(ATTN_GROUP))
    first = scores(0)
    ml, alphas = softmax(lambda a: first[a][0], [mx for (_, mx) in first], 0, ml)
    mx0 = stash(scores(1), 0)
    n_trips = jnp.where(pl.program_id(1) > 0, last // 4, 0)
    ml, alphas, _ = lax.fori_loop(0, n_trips, four_chunks, (ml, alphas, mx0))
    values(4 * n_trips, 0, alphas)
    return [l for (_, l) in ml]


def _mla_attn_kernel(qt_ref, k_ref, vt_ref, o_ref, s_buf, p_buf, acc_ref, *, n_chunks):
    ls = _attn_group(qt_ref, k_ref, vt_ref, tuple(range(ATTN_GROUP)), s_buf, p_buf, acc_ref, n_chunks)
    o = jnp.concatenate([acc_ref[a] * (1.0 / ls[a]) for a in range(ATTN_GROUP)], axis=0)
    o_ref[...] = o.T


def _dif_attn_kernel(lam_ref, gsub_ref, qt_ref, k_ref, vt_ref, o_ref, s_buf, p_buf, acc_ref,
                     *, n_chunks, lam_init):
    ls = _attn_group(qt_ref, k_ref, vt_ref, tuple(a // 2 for a in range(ATTN_GROUP)), s_buf, p_buf,
                     acc_ref, n_chunks)
    lp = lam_ref[...]
    lam = (jnp.exp(jnp.sum(lp[0:1] * lp[1:2], axis=-1, keepdims=True))
           - jnp.exp(jnp.sum(lp[2:3] * lp[3:4], axis=-1, keepdims=True)) + lam_init)
    outs = []
    for h in range(ATTN_GROUP // 2):
        o = acc_ref[2 * h] * (1.0 / ls[2 * h]) - lam * (acc_ref[2 * h + 1] * (1.0 / ls[2 * h + 1]))
        outs.append(o * lax.rsqrt(jnp.mean(o * o, axis=0, keepdims=True) + EPS))
    gsub = gsub_ref[...] * (1.0 - lam_init)
    o_ref[...] = jnp.concatenate(outs, axis=0).T * jnp.concatenate([gsub] * len(outs), axis=1)


def _attention(qt, k, vt, *, lam=None, gsub=None, lam_init=None):
    nsub, _, t = qt.shape
    n_chunks = k.shape[1]
    assert (n_chunks - 1) % 4 == 0 and nsub % ATTN_GROUP == 0
    tq = KEY_CHUNK
    dv = vt.shape[2]
    groups = nsub // ATTN_GROUP
    n_v = vt.shape[0] // groups
    out_w = n_v * dv
    resident = dict(pipeline_mode=pl.Buffered(1))
    specs = [pl.BlockSpec((ATTN_GROUP, LANES, tq), lambda g, i: (g, 0, i)),
             pl.BlockSpec((ATTN_GROUP, n_chunks, KEY_CHUNK, LANES), lambda g, i: (g, 0, 0, 0), **resident),
             pl.BlockSpec((n_v, n_chunks, dv, KEY_CHUNK), lambda g, i: (g, 0, 0, 0), **resident)]
    if lam is None:
        body = functools.partial(_mla_attn_kernel, n_chunks=n_chunks)
        args = (qt, k, vt)
    else:
        body = functools.partial(_dif_attn_kernel, n_chunks=n_chunks, lam_init=lam_init)
        specs = [pl.BlockSpec((8, LANES), lambda g, i: (0, 0)),
                 pl.BlockSpec((1, LANES), lambda g, i: (0, 0))] + specs
        args = (lam, gsub, qt, k, vt)
    return pl.pallas_call(
        body,
        out_shape=jax.ShapeDtypeStruct((t, groups * out_w), F32),
        grid=(groups, t // tq),
        in_specs=specs,
        out_specs=pl.BlockSpec((tq, out_w), lambda g, i: (i, g)),
        scratch_shapes=[pltpu.VMEM((2, ATTN_GROUP, KEY_CHUNK, tq), F32),
                        pltpu.VMEM((2, ATTN_GROUP, KEY_CHUNK, tq), BF16),
                        pltpu.VMEM((ATTN_GROUP, dv, tq), F32)],
        compiler_params=_params("parallel", "arbitrary"),
        name="mla_attention" if lam is None else "dif_attention",
    )(*args)


def _ssm_prep_kernel(x_ref, prev_ref, next_ref, dt_ref, w_ref, b_ref, dtb_ref, xo_ref, dto_ref, e_ref,
                     *, n_ctx, n_tok, tm):
    row0 = pl.program_id(0) * tm
    pad = SSM_CONV // 2
    has_prev = jnp.logical_and(row0 != 0, row0 != n_ctx)
    has_next = jnp.logical_and(row0 + tm != n_ctx, row0 + tm != n_tok)
    e_ref[0:8] = jnp.where(has_prev, prev_ref[...], 0.0)
    e_ref[8:8 + tm] = x_ref[...]
    e_ref[8 + tm:16 + tm] = jnp.where(has_next, next_ref[...], 0.0)
    acc = jnp.zeros(x_ref.shape, F32) + b_ref[...]
    for k in range(SSM_CONV):
        acc = acc + w_ref[k:k + 1, :] * e_ref[pl.ds(8 - pad + k, tm), :]
    xo_ref[...] = _silu(acc)
    d = dt_ref[...] + dtb_ref[...]
    dto_ref[...] = jnp.maximum(d, 0.0) + jnp.log1p(jnp.exp(-jnp.abs(d)))


def _ssm_prep(u, conv_w, conv_b, dt_bias, n_ctx):
    t = u.shape[0]
    tm = ROW_TILE
    nt = t // tm
    cb = U_XBC // SSM_XBC
    const = lambda i: (0, 0)
    return pl.pallas_call(
        functools.partial(_ssm_prep_kernel, n_ctx=n_ctx, n_tok=t, tm=tm),
        out_shape=(jax.ShapeDtypeStruct((t, SSM_XBC), F32), jax.ShapeDtypeStruct((t, LANES), F32)),
        grid=(nt,),
        in_specs=[pl.BlockSpec((tm, SSM_XBC), lambda i: (i, cb)),
                  pl.BlockSpec((8, SSM_XBC), lambda i: (jnp.maximum(i * (tm // 8) - 1, 0), cb)),
                  pl.BlockSpec((8, SSM_XBC), lambda i: (jnp.minimum((i + 1) * (tm // 8), t // 8 - 1), cb)),
                  pl.BlockSpec((tm, LANES), lambda i: (i, U_DT // LANES)),
                  pl.BlockSpec((8, SSM_XBC), const),
                  pl.BlockSpec((1, SSM_XBC), const),
                  pl.BlockSpec((1, LANES), const)],
        out_specs=(pl.BlockSpec((tm, SSM_XBC), lambda i: (i, 0)),
                   pl.BlockSpec((tm, LANES), lambda i: (i, 0))),
        scratch_shapes=[pltpu.VMEM((tm + 16, SSM_XBC), F32)],
        compiler_params=_params("parallel"),
        name="ssm_prep",
    )(u, u, u, u, conv_w, conv_b, dt_bias)


def _ssd_kernel(x_ref, dt_ref, alog_ref, o_ref, h_ref, *, direction):
    lc = SSM_CHUNK

    @pl.when(pl.program_id(0) == 0)
    def _():
        h_ref[...] = jnp.zeros(h_ref.shape, F32)

    xbc = x_ref[...]
    dt = dt_ref[...]
    dta = dt * (-jnp.exp(alog_ref[...]))
    r = lax.broadcasted_iota(jnp.int32, (lc, lc), 0)
    c = lax.broadcasted_iota(jnp.int32, (lc, lc), 1)
    keep = (r >= c) if direction == 0 else (r <= c)
    tri = jnp.where(keep, 1.0, 0.0).astype(F32)
    cum = jnp.dot(tri, dta, preferred_element_type=F32, precision=lax.Precision.HIGHEST)
    cum_t = cum.T
    total = jnp.sum(dta, axis=0, keepdims=True)
    to_end = jnp.exp(total - cum)
    from_start = jnp.exp(cum)
    chunk_decay = jnp.exp(total)
    x_t = xbc[:, :SSM_WIDTH].T
    w_t = (dt * to_end).T
    outs = []
    for g in range(SSM_GROUPS):
        b_g = xbc[:, SSM_WIDTH + SSM_STATE * g:SSM_WIDTH + SSM_STATE * (g + 1)].astype(BF16)
        c_g = xbc[:, SSM_WIDTH + SSM_STATE * (SSM_GROUPS + g):
                  SSM_WIDTH + SSM_STATE * (SSM_GROUPS + g + 1)].astype(BF16)
        cb = _dot_nt(c_g, b_g)
        for hh in range(SSM_HEADS // SSM_GROUPS):
            h = g * (SSM_HEADS // SSM_GROUPS) + hh
            col = direction * SSM_HEADS + h
            seg = jnp.exp(jnp.where(keep, cum[:, col:col + 1] - cum_t[col:col + 1, :], -jnp.inf))
            xd = xbc[:, SSM_HEAD_DIM * h:SSM_HEAD_DIM * (h + 1)] * dt[:, col:col + 1]
            y = _dot((cb * seg).astype(BF16), xd.astype(BF16))
            state = h_ref[h]
            y_off = _dot_nt(c_g, state.astype(BF16))
            outs.append(y + y_off * from_start[:, col:col + 1])
            xw_t = x_t[SSM_HEAD_DIM * h:SSM_HEAD_DIM * (h + 1), :] * w_t[col:col + 1, :]
            upd = _dot(xw_t.astype(BF16), b_g)
            h_ref[h] = state * chunk_decay[:, col:col + 1] + upd
    o_ref[...] = jnp.concatenate(outs, axis=1)


def _ssd(xbc, dt, a_log, n_ctx, direction):
    t = xbc.shape[0]
    lc = SSM_CHUNK
    nc = t // lc
    ncc = n_ctx // lc
    if direction == 0:
        order = lambda s: s
    else:
        order = lambda s: jnp.where(s < ncc, ncc - 1 - s, nc - 1 - (s - ncc))
    return pl.pallas_call(
        functools.partial(_ssd_kernel, direction=direction),
        out_shape=jax.ShapeDtypeStruct((t, SSM_WIDTH), F32),
        grid=(nc,),
        in_specs=[pl.BlockSpec((lc, SSM_XBC), lambda s: (order(s), 0)),
                  pl.BlockSpec((lc, LANES), lambda s: (order(s), 0)),
                  pl.BlockSpec((1, LANES), lambda s: (0, 0))],
        out_specs=pl.BlockSpec((lc, SSM_WIDTH), lambda s: (order(s), 0)),
        scratch_shapes=[pltpu.VMEM((SSM_HEADS, SSM_HEAD_DIM, SSM_STATE), F32)],
        compiler_params=_params("arbitrary"),
        name="ssd_fwd" if direction == 0 else "ssd_bwd",
    )(xbc, dt, a_log)


def _merge_kernel(x_ref, gate_ref, ya_ref, yb_ref, yf_ref, yr_ref, xs_ref, z_ref,
                  bg_ref, dskip_ref, gssm_ref, wa_ref, wb_ref, wc_ref, wo_ref,
                  ml_ref, mc_ref, g2_ref, wr_ref, br_ref,
                  xo_ref, f_ref, lg_ref, *, n_ctx, tm):
    y = (yf_ref[...] + yr_ref[...] + dskip_ref[...] * xs_ref[...]) * _silu(z_ref[...])
    gw = SSM_WIDTH // SSM_GROUPS
    yc = jnp.concatenate([_rms_rows(y[:, gw * g:gw * (g + 1)], gw) for g in range(SSM_GROUPS)], axis=1)
    yc = yc * gssm_ref[...]
    gate = _sigmoid(gate_ref[...] + bg_ref[...])
    m = (gate[:, :D_MODEL] * _dot(ya_ref[...].astype(BF16), wa_ref[...])
         + gate[:, D_MODEL:2 * D_MODEL] * _dot(yb_ref[...].astype(BF16), wb_ref[...])
         + gate[:, 2 * D_MODEL:] * _dot(yc.astype(BF16), wc_ref[...]))
    out = _dot(m.astype(BF16), wo_ref[...])
    row0 = pl.program_id(0) * tm
    is_ctx = (row0 + lax.broadcasted_iota(jnp.int32, (tm, 1), 0)) < n_ctx
    gt1 = jnp.where(is_ctx, mc_ref[2:3, :], ml_ref[2:3, :])
    x_new = x_ref[...] + gt1 * out
    xo_ref[...] = x_new
    f = _modulated_norm(x_new, g2_ref[...], ml_ref[3:5, :], mc_ref[3:5, :], row0, n_ctx)
    f_ref[...] = f.astype(BF16)
    lg_ref[...] = jnp.dot(f, wr_ref[...], preferred_element_type=F32,
                          precision=lax.Precision.HIGHEST) + br_ref[...]


def _merge(x_all, u, ya, yb, yf, yr, xbc, b_gate, dskip, g_ssm, wa, wb, wc, wo, mod_lat, mod_ctx,
           g2, w_router, b_router, n_ctx):
    t, d = x_all.shape
    tm = ROW_TILE
    const = lambda i: (0, 0)
    rows = lambda i: (i, 0)
    full = lambda a: pl.BlockSpec(a.shape, const)
    return pl.pallas_call(
        functools.partial(_merge_kernel, n_ctx=n_ctx, tm=tm),
        out_shape=(jax.ShapeDtypeStruct((t, d), F32), jax.ShapeDtypeStruct((t, d), BF16),
                   jax.ShapeDtypeStruct((t, LANES), F32)),
        grid=(t // tm,),
        in_specs=[pl.BlockSpec((tm, d), rows),
                  pl.BlockSpec((tm, GATE_COLS), lambda i: (i, U_GATE // GATE_COLS)),
                  pl.BlockSpec((tm, MLA_WIDTH), rows),
                  pl.BlockSpec((tm, DIF_WIDTH), rows),
                  pl.BlockSpec((tm, SSM_WIDTH), rows),
                  pl.BlockSpec((tm, SSM_WIDTH), rows),
                  pl.BlockSpec((tm, SSM_WIDTH), rows),
                  pl.BlockSpec((tm, SSM_WIDTH), lambda i: (i, U_Z // SSM_WIDTH)),
                  full(b_gate), full(dskip), full(g_ssm), full(wa), full(wb), full(wc), full(wo),
                  full(mod_lat), full(mod_ctx), full(g2), full(w_router), full(b_router)],
        out_specs=(pl.BlockSpec((tm, d), rows), pl.BlockSpec((tm, d), rows),
                   pl.BlockSpec((tm, LANES), rows)),
        compiler_params=_params("parallel"),
        name="merge",
    )(x_all, u, ya, yb, yf, yr, xbc, u, b_gate, dskip, g_ssm, wa, wb, wc, wo, mod_lat, mod_ctx,
      g2, w_router, b_router)


def _moe_kernel(be_ref, nb_ref, x_ref, wgu_ref, bgu_ref, wd_ref, bd_ref, o_ref, wgu_s, wd_s):
    b = pl.program_id(0)
    prev = be_ref[jnp.maximum(b - 1, 0)]
    fresh = jnp.logical_or(b == 0, be_ref[b] != prev)

    @pl.when(fresh)
    def _():
        wgu_s[...] = wgu_ref[...].astype(BF16)
        wd_s[...] = wd_ref[...].astype(BF16)

    @pl.when(b < nb_ref[0])
    def _():
        gu = _dot(x_ref[...], wgu_s[...]) + bgu_ref[...]
        glu = jnp.minimum(gu[:, :D_FF], SWIGLU_LIMIT)
        lin = jnp.clip(gu[:, D_FF:], -SWIGLU_LIMIT, SWIGLU_LIMIT)
        act = glu * _sigmoid(SWIGLU_ALPHA * glu) * (lin + 1.0)
        o_ref[...] = _dot(act.astype(BF16), wd_s[...]) + bd_ref[...]

    @pl.when(b >= nb_ref[0])
    def _():
        o_ref[...] = jnp.zeros(o_ref.shape, F32)


def _moe_experts(block_e, n_used, x_sorted, w_gu, b_gu, w_down, b_down, layer):
    n_slots, d = x_sorted.shape
    n_blocks = n_slots // MOE_BLOCK
    grid_spec = pltpu.PrefetchScalarGridSpec(
        num_scalar_prefetch=2,
        grid=(n_blocks,),
        in_specs=[pl.BlockSpec((MOE_BLOCK, d), lambda b, be, nb: (b, 0)),
                  pl.BlockSpec((None, None, d, 2 * D_FF), lambda b, be, nb: (layer, be[b], 0, 0)),
                  pl.BlockSpec((None, None, 1, 2 * D_FF), lambda b, be, nb: (layer, be[b], 0, 0)),
                  pl.BlockSpec((None, None, D_FF, d), lambda b, be, nb: (layer, be[b], 0, 0)),
                  pl.BlockSpec((None, None, 1, d), lambda b, be, nb: (layer, be[b], 0, 0))],
        out_specs=pl.BlockSpec((MOE_BLOCK, d), lambda b, be, nb: (b, 0)),
        scratch_shapes=[pltpu.VMEM((d, 2 * D_FF), BF16), pltpu.VMEM((D_FF, d), BF16)],
    )
    return pl.pallas_call(
        _moe_kernel,
        out_shape=jax.ShapeDtypeStruct((n_slots, d), F32),
        grid_spec=grid_spec,
        compiler_params=_params("arbitrary"),
        name="moe_experts",
    )(block_e, n_used, x_sorted, w_gu, b_gu, w_down, b_down)


def _route(logits, n_tok):
    top_val, top_idx = lax.top_k(logits, TOP_K)
    gates = jax.nn.softmax(top_val, axis=-1)
    n_assign = n_tok * TOP_K
    flat_e = top_idx.reshape(-1).astype(jnp.int32)
    counts = jnp.zeros((N_EXPERTS,), jnp.int32).at[flat_e].add(1)
    padded = (counts + MOE_BLOCK - 1) // MOE_BLOCK * MOE_BLOCK
    pad_end = jnp.cumsum(padded)
    pad_start = pad_end - padded
    raw_start = jnp.cumsum(counts) - counts
    order = jnp.argsort(flat_e)
    sorted_e = flat_e[order]
    dest = pad_start[sorted_e] + jnp.arange(n_assign, dtype=jnp.int32) - raw_start[sorted_e]
    n_blocks = -(-(n_assign + N_EXPERTS * (MOE_BLOCK - 1)) // MOE_BLOCK)
    n_slots = n_blocks * MOE_BLOCK
    slot_tok = jnp.zeros((n_slots,), jnp.int32).at[dest].set(order // TOP_K)
    slot_of = jnp.zeros((n_assign,), jnp.int32).at[order].set(dest).reshape(n_tok, TOP_K)
    block_start = jnp.arange(n_blocks, dtype=jnp.int32) * MOE_BLOCK
    block_e = jnp.minimum(jnp.searchsorted(pad_end, block_start, side='right'), N_EXPERTS - 1)
    n_used = (pad_end[-1] // MOE_BLOCK).reshape(1)
    return gates, slot_tok, slot_of, block_e.astype(jnp.int32), n_used.astype(jnp.int32)


def kernel(x, c, ctx, c_ctx, w_mod, b_mod, g_norm1, g_norm2, w_in, b_gate, mla_g_q, mla_w_uq, mla_g_kv, mla_w_ukv, mla_g_qn, mla_g_kn, dif_g_qn, dif_g_kn, dif_lambda, dif_g_sub, ssm_conv_w, ssm_conv_b, ssm_dt_bias, ssm_a_log, ssm_d, ssm_g_norm, w_up_mla, w_up_dif, w_up_ssm, w_out, moe_w_router, moe_b_router, moe_w_gu, moe_b_gu, moe_w_down, moe_b_down):
    assert x.shape[0] == 1 and ctx.shape[0] == 1
    depth = w_in.shape[0]
    seq = x.shape[1]
    n_ctx = ctx.shape[1]
    n_tok = n_ctx + seq
    d = D_MODEL
    assert n_ctx == KEY_CHUNK and n_tok % ROW_TILE == 0 and seq % GRID_W == 0

    x_all = jnp.concatenate([ctx[0], x[0]], axis=0)
    cc = jnp.zeros((8, d), F32).at[0].set(c[0]).at[1].set(c_ctx)
    mod = _mod_vectors(cc, w_mod, b_mod)
    mod = mod[:, :2].reshape(depth, 2, 6, d)

    rope_mla = _rope_tables(seq, n_ctx, MLA_ROPE, MLA_NOPE)
    rope_dif = _rope_tables(seq, n_ctx, DIF_HEAD_DIM, 0)
    in_cols = _in_proj_columns()
    uq_cols = _head_columns(MLA_HEADS, MLA_QK, 0, MLA_QK, LANES)
    uk_cols = _head_columns(MLA_HEADS, MLA_NOPE + MLA_V, 0, MLA_NOPE, LANES)
    uv_cols = _head_columns(MLA_HEADS, MLA_NOPE + MLA_V, MLA_NOPE, MLA_V, MLA_V)

    for i in range(depth):
        lam_init = 0.8 - 0.6 * math.exp(-0.3 * i)
        mod_lat, mod_ctx = mod[i, 0], mod[i, 1]
        w_in_i = _take_columns(w_in[i], in_cols).astype(BF16)
        u = _in_proj(x_all, g_norm1[i][None], mod_lat[0:2], mod_ctx[0:2], w_in_i, n_ctx)

        qt, k, vt = _mla_prep(
            u, mla_g_q[i][None], _take_columns(mla_w_uq[i], uq_cols).astype(BF16),
            mla_g_kv[i][None], _take_columns(mla_w_ukv[i], uk_cols).astype(BF16),
            _take_columns(mla_w_ukv[i], uv_cols).astype(BF16),
            _pad_lanes(mla_g_qn[i][None]), _pad_lanes(mla_g_kn[i][None]), rope_mla)
        ya = _attention(qt, k, vt)

        qt, k, vt = _dif_prep(u, _pad_lanes(dif_g_qn[i][None]), _pad_lanes(dif_g_kn[i][None]), rope_dif)
        lam_rows = jnp.zeros((8, LANES), F32).at[:4, :DIF_HEAD_DIM].set(dif_lambda[i])
        yb = _attention(qt, k, vt, lam=lam_rows, gsub=dif_g_sub[i][None], lam_init=lam_init)

        conv_w = jnp.zeros((8, SSM_XBC), F32).at[:SSM_CONV].set(ssm_conv_w[i])
        xbc, dt = _ssm_prep(u, conv_w, ssm_conv_b[i][None], _pad_lanes(ssm_dt_bias[i].reshape(1, -1)), n_ctx)
        a_log = _pad_lanes(ssm_a_log[i].reshape(1, -1))
        yf = _ssd(xbc, dt, a_log, n_ctx, 0)
        yr = _ssd(xbc, dt, a_log, n_ctx, 1)

        dskip = jnp.repeat(ssm_d[i, 0] + ssm_d[i, 1], SSM_HEAD_DIM)[None]
        w_router = jnp.zeros((d, LANES), F32).at[:, :N_EXPERTS].set(moe_w_router[i])
        b_router = jnp.zeros((1, LANES), F32).at[0, :N_EXPERTS].set(moe_b_router[i])
        x_all, f, logits = _merge(
            x_all, u, ya, yb, yf, yr, xbc, b_gate[i][None], dskip, ssm_g_norm[i][None],
            w_up_mla[i].astype(BF16), w_up_dif[i].astype(BF16), w_up_ssm[i].astype(BF16),
            w_out[i].astype(BF16), mod_lat[0:5], mod_ctx[0:5], g_norm2[i][None],
            w_router, b_router, n_ctx)

        gates, slot_tok, slot_of, block_e, n_used = _route(logits[:, :N_EXPERTS], n_tok)
        y_slots = _moe_experts(block_e, n_used, f[slot_tok], moe_w_gu,
                               moe_b_gu.reshape(depth, N_EXPERTS, 1, 2 * D_FF), moe_w_down,
                               moe_b_down.reshape(depth, N_EXPERTS, 1, d), i)
        y = jnp.sum(y_slots[slot_of] * gates[..., None], axis=1)
        gt2 = jnp.concatenate([jnp.broadcast_to(mod_ctx[5], (n_ctx, d)),
                               jnp.broadcast_to(mod_lat[5], (seq, d))], axis=0)
        x_all = x_all + gt2 * y
    return x_all[n_ctx:][None]
```

```python
import functools
import math

import numpy as np
import jax
import jax.numpy as jnp
from jax import lax
from jax.experimental import pallas as pl
from jax.experimental.pallas import tpu as pltpu

F32 = jnp.float32
BF16 = jnp.bfloat16
LANES = 128
VMEM_LIMIT = 52 * 1024 * 1024

D_MODEL = 1024
EPS = 1e-6
ROPE_THETA = 10000.0
GRID_W = 64
N_BRANCH = 3

MLA_HEADS = 8
MLA_Q_RANK = 256
MLA_KV_RANK = 128
MLA_NOPE = 64
MLA_ROPE = 32
MLA_V = 64
MLA_QK = MLA_NOPE + MLA_ROPE
MLA_WIDTH = MLA_HEADS * MLA_V

DIF_HEADS = 4
DIF_HEAD_DIM = 64
DIF_WIDTH = DIF_HEADS * 2 * DIF_HEAD_DIM

SSM_HEADS = 8
SSM_HEAD_DIM = 64
SSM_WIDTH = SSM_HEADS * SSM_HEAD_DIM
SSM_GROUPS = 2
SSM_STATE = 128
SSM_CONV = 5
SSM_CHUNK = 128
SSM_XBC = SSM_WIDTH + 2 * SSM_GROUPS * SSM_STATE

N_EXPERTS = 32
TOP_K = 4
D_FF = 1024
SWIGLU_LIMIT = 7.0
SWIGLU_ALPHA = 1.702
MOE_BLOCK = 256

MLA_COLS = MLA_Q_RANK + MLA_KV_RANK + MLA_ROPE
DIF_COLS = 3 * DIF_WIDTH
SSM_COLS = SSM_WIDTH + SSM_XBC + 2 * SSM_HEADS
GATE_COLS = N_BRANCH * D_MODEL

ROW_TILE = 256
KEY_CHUNK = 256
LOG2E = 1.4426950408889634

U_GATE, U_DQ, U_DK, U_XBC, U_MLA, U_DV, U_Z, U_DT = 0, 3072, 4096, 5120, 6144, 6656, 7168, 7680
U_COLS = 8192
U_TILE_N = 1024


def _in_proj_columns():
    src = np.full((U_COLS,), -1, np.int64)
    dif0 = MLA_COLS
    ssm0 = MLA_COLS + DIF_COLS
    gate0 = ssm0 + SSM_COLS
    src[U_GATE:U_GATE + GATE_COLS] = gate0 + np.arange(GATE_COLS)
    for a in range(2 * DIF_HEADS):
        src[U_DQ + LANES * a:U_DQ + LANES * a + DIF_HEAD_DIM] = dif0 + DIF_HEAD_DIM * a + np.arange(DIF_HEAD_DIM)
        src[U_DK + LANES * a:U_DK + LANES * a + DIF_HEAD_DIM] = (dif0 + DIF_WIDTH + DIF_HEAD_DIM * a
                                                                  + np.arange(DIF_HEAD_DIM))
    src[U_DV:U_DV + DIF_WIDTH] = dif0 + 2 * DIF_WIDTH + np.arange(DIF_WIDTH)
    src[U_MLA:U_MLA + MLA_Q_RANK + MLA_KV_RANK] = np.arange(MLA_Q_RANK + MLA_KV_RANK)
    pe0 = U_MLA + MLA_Q_RANK + MLA_KV_RANK + MLA_NOPE
    src[pe0:pe0 + MLA_ROPE] = MLA_Q_RANK + MLA_KV_RANK + np.arange(MLA_ROPE)
    src[U_Z:U_Z + SSM_WIDTH] = ssm0 + np.arange(SSM_WIDTH)
    src[U_XBC:U_XBC + SSM_XBC] = ssm0 + SSM_WIDTH + np.arange(SSM_XBC)
    src[U_DT:U_DT + 2 * SSM_HEADS] = ssm0 + SSM_WIDTH + SSM_XBC + np.arange(2 * SSM_HEADS)
    return src


def _take_columns(w, src):
    cols = jnp.take(w, jnp.asarray(np.maximum(src, 0)), axis=-1)
    return jnp.where(jnp.asarray(src >= 0), cols, 0.0)


def _head_columns(n_heads, src_stride, src_off, width, dst_stride):
    src = np.full((n_heads * dst_stride,), -1, np.int64)
    for h in range(n_heads):
        src[h * dst_stride:h * dst_stride + width] = h * src_stride + src_off + np.arange(width)
    return src


def _pad_lanes(v, n=LANES):
    return jnp.pad(v, [(0, 0)] * (v.ndim - 1) + [(0, n - v.shape[-1])])


def _row_tile(n, cap):
    best = 8
    for t in range(8, cap + 1, 8):
        if n % t == 0:
            best = t
    return best


def _dot(a, b):
    return jnp.dot(a, b, preferred_element_type=F32)


def _dot_nt(a, b):
    return lax.dot_general(a, b, (((1,), (1,)), ((), ())), preferred_element_type=F32)


def _sigmoid(x):
    return 1.0 / (1.0 + jnp.exp(-x))


def _silu(x):
    return x * _sigmoid(x)


def _rms_rows(x, n):
    return x * lax.rsqrt(jnp.sum(x * x, axis=-1, keepdims=True) * (1.0 / n) + EPS)


def _params(*sem):
    return pltpu.CompilerParams(dimension_semantics=sem, vmem_limit_bytes=VMEM_LIMIT)


def _mod_kernel(a_ref, w_ref, b_ref, o_ref):
    a = _silu(a_ref[...]).astype(BF16)
    o_ref[0] = _dot(a, w_ref[0].astype(BF16)) + b_ref[0]


def _mod_vectors(cc, w_mod, b_mod):
    depth, d, n = w_mod.shape
    tn = 1536
    return pl.pallas_call(
        _mod_kernel,
        out_shape=jax.ShapeDtypeStruct((depth, 8, n), F32),
        grid=(depth, n // tn),
        in_specs=[pl.BlockSpec((8, d), lambda l, j: (0, 0)),
                  pl.BlockSpec((1, d, tn), lambda l, j: (l, 0, j)),
                  pl.BlockSpec((1, 1, tn), lambda l, j: (l, 0, j))],
        out_specs=pl.BlockSpec((1, 8, tn), lambda l, j: (l, 0, j)),
        compiler_params=_params("parallel", "parallel"),
        name="mod_vectors",
    )(cc, w_mod, b_mod.reshape(depth, 1, n))


def _modulated_norm(x, g, mod_lat, mod_ctx, row0, n_ctx):
    rows = x.shape[0]
    is_ctx = (row0 + lax.broadcasted_iota(jnp.int32, (rows, 1), 0)) < n_ctx
    shift = jnp.where(is_ctx, mod_ctx[0:1, :], mod_lat[0:1, :])
    scale = jnp.where(is_ctx, mod_ctx[1:2, :], mod_lat[1:2, :])
    return _rms_rows(x, x.shape[1]) * g * (1.0 + scale) + shift


def _in_proj_kernel(x_ref, g_ref, ml_ref, mc_ref, w_ref, o_ref, h_ref, *, n_ctx, tm):
    @pl.when(pl.program_id(1) == 0)
    def _():
        h = _modulated_norm(x_ref[...], g_ref[...], ml_ref[...], mc_ref[...], pl.program_id(0) * tm, n_ctx)
        h_ref[...] = h.astype(BF16)

    o_ref[...] = _dot(h_ref[...], w_ref[...])


def _in_proj(x_all, g, mod_lat, mod_ctx, w, n_ctx):
    t, d = x_all.shape
    n = w.shape[1]
    tm = _row_tile(t, 1280)
    return pl.pallas_call(
        functools.partial(_in_proj_kernel, n_ctx=n_ctx, tm=tm),
        out_shape=jax.ShapeDtypeStruct((t, n), F32),
        grid=(t // tm, n // U_TILE_N),
        in_specs=[pl.BlockSpec((tm, d), lambda i, j: (i, 0)),
                  pl.BlockSpec((1, d), lambda i, j: (0, 0)),
                  pl.BlockSpec((2, d), lambda i, j: (0, 0)),
                  pl.BlockSpec((2, d), lambda i, j: (0, 0)),
                  pl.BlockSpec((d, U_TILE_N), lambda i, j: (0, j))],
        out_specs=pl.BlockSpec((tm, U_TILE_N), lambda i, j: (i, j)),
        scratch_shapes=[pltpu.VMEM((tm, d), BF16)],
        compiler_params=_params("parallel", "arbitrary"),
        name="in_proj",
    )(x_all, g, mod_lat, mod_ctx, w)


def _rope_tables(seq_len, n_ctx, rot_dim, lane0):
    n_rows = seq_len // GRID_W
    row = jnp.repeat(jnp.arange(n_rows), GRID_W).astype(F32)
    col = jnp.tile(jnp.arange(GRID_W), n_rows).astype(F32)
    axis_dim = rot_dim // 2
    half = axis_dim // 2
    inv = ROPE_THETA ** (-jnp.arange(0, axis_dim, 2, dtype=F32) / axis_dim)
    ang_r = row[:, None] * inv
    ang_c = col[:, None] * inv
    zeros = jnp.zeros((seq_len, half), F32)
    cos = jnp.concatenate([jnp.cos(ang_r), jnp.cos(ang_r), jnp.cos(ang_c), jnp.cos(ang_c)], axis=1)
    s1 = jnp.concatenate([zeros, jnp.sin(ang_r), zeros, jnp.sin(ang_c)], axis=1)
    s2 = jnp.concatenate([-jnp.sin(ang_r), zeros, -jnp.sin(ang_c), zeros], axis=1)

    def place(tab, fill):
        full = jnp.full((seq_len, LANES), fill, F32).at[:, lane0:lane0 + rot_dim].set(tab)
        ctx = jnp.full((n_ctx, LANES), fill, F32)
        return jnp.concatenate([ctx, full], axis=0)

    return place(cos, 1.0), place(s1, 0.0), place(s2, 0.0)


def _rope(x, cos, s1, s2, half):
    return x * cos + pltpu.roll(x, half, 1) * s1 + pltpu.roll(x, LANES - half, 1) * s2


def _mla_prep_kernel(u_ref, gq_ref, wuq_ref, gkv_ref, wk_ref, wv_ref, gqn_ref, gkn_ref,
                     cos_ref, s1_ref, s2_ref, qt_ref, k_ref, vt_ref):
    u = u_ref[...]
    cq = u[:, :MLA_Q_RANK]
    ckv = u[:, MLA_Q_RANK:MLA_Q_RANK + MLA_KV_RANK]
    pe = u[:, MLA_Q_RANK + MLA_KV_RANK:]
    q = _dot((_rms_rows(cq, MLA_Q_RANK) * gq_ref[...]).astype(BF16), wuq_ref[...])
    kv_in = (_rms_rows(ckv, MLA_KV_RANK) * gkv_ref[...]).astype(BF16)
    kn = _dot(kv_in, wk_ref[...])
    v = _dot(kv_in, wv_ref[...])
    cos, s1, s2 = cos_ref[...], s1_ref[...], s2_ref[...]
    half = MLA_ROPE // 4
    q_scale = MLA_QK ** -0.5 * LOG2E
    for h in range(MLA_HEADS):
        qh = _rms_rows(q[:, LANES * h:LANES * (h + 1)], MLA_QK) * gqn_ref[...]
        qh = _rope(qh, cos, s1, s2, half) * q_scale
        qt_ref[h] = qh.T.astype(BF16)
        kh = _rms_rows(kn[:, LANES * h:LANES * (h + 1)] + pe, MLA_QK) * gkn_ref[...]
        k_ref[h, 0] = _rope(kh, cos, s1, s2, half).astype(BF16)
    tm = v.shape[0]
    vt_ref[:, 0] = v.T.reshape(MLA_HEADS, MLA_V, tm).astype(BF16)


def _mla_prep(u, gq, wuq, gkv, wk, wv, gqn, gkn, tabs):
    t = u.shape[0]
    tm = ROW_TILE
    nt = t // tm
    const = lambda i: (0, 0)
    rows = lambda i: (i, 0)
    return pl.pallas_call(
        _mla_prep_kernel,
        out_shape=(jax.ShapeDtypeStruct((MLA_HEADS, LANES, t), BF16),
                   jax.ShapeDtypeStruct((MLA_HEADS, nt, tm, LANES), BF16),
                   jax.ShapeDtypeStruct((MLA_HEADS, nt, MLA_V, tm), BF16)),
        grid=(nt,),
        in_specs=[pl.BlockSpec((tm, 512), lambda i: (i, U_MLA // 512)),
                  pl.BlockSpec((1, MLA_Q_RANK), const),
                  pl.BlockSpec(wuq.shape, const),
                  pl.BlockSpec((1, MLA_KV_RANK), const),
                  pl.BlockSpec(wk.shape, const),
                  pl.BlockSpec(wv.shape, const),
                  pl.BlockSpec((1, LANES), const),
                  pl.BlockSpec((1, LANES), const),
                  pl.BlockSpec((tm, LANES), rows),
                  pl.BlockSpec((tm, LANES), rows),
                  pl.BlockSpec((tm, LANES), rows)],
        out_specs=(pl.BlockSpec((MLA_HEADS, LANES, tm), lambda i: (0, 0, i)),
                   pl.BlockSpec((MLA_HEADS, 1, tm, LANES), lambda i: (0, i, 0, 0)),
                   pl.BlockSpec((MLA_HEADS, 1, MLA_V, tm), lambda i: (0, i, 0, 0))),
        compiler_params=_params("parallel"),
        name="mla_prep",
    )(u, gq, wuq, gkv, wk, wv, gqn, gkn, *tabs)


def _dif_prep_kernel(q_ref, k_ref, v_ref, gq_ref, gk_ref, cos_ref, s1_ref, s2_ref,
                     qt_out, k_out, vt_out):
    cos, s1, s2 = cos_ref[...], s1_ref[...], s2_ref[...]
    half = DIF_HEAD_DIM // 4
    q_scale = DIF_HEAD_DIM ** -0.5 * LOG2E
    for a in range(2 * DIF_HEADS):
        qa = _rms_rows(q_ref[:, LANES * a:LANES * (a + 1)], DIF_HEAD_DIM) * gq_ref[...]
        qt_out[a] = (_rope(qa, cos, s1, s2, half) * q_scale).T.astype(BF16)
        ka = _rms_rows(k_ref[:, LANES * a:LANES * (a + 1)], DIF_HEAD_DIM) * gk_ref[...]
        k_out[a, 0] = _rope(ka, cos, s1, s2, half).astype(BF16)
    v = v_ref[...]
    vt_out[:, 0] = v.T.reshape(DIF_HEADS, 2 * DIF_HEAD_DIM, v.shape[0]).astype(BF16)


def _dif_prep(u, gq, gk, tabs):
    t = u.shape[0]
    tm = ROW_TILE
    nt = t // tm
    nsub = 2 * DIF_HEADS
    const = lambda i: (0, 0)
    rows = lambda i: (i, 0)
    return pl.pallas_call(
        _dif_prep_kernel,
        out_shape=(jax.ShapeDtypeStruct((nsub, LANES, t), BF16),
                   jax.ShapeDtypeStruct((nsub, nt, tm, LANES), BF16),
                   jax.ShapeDtypeStruct((DIF_HEADS, nt, 2 * DIF_HEAD_DIM, tm), BF16)),
        grid=(nt,),
        in_specs=[pl.BlockSpec((tm, 1024), lambda i: (i, U_DQ // 1024)),
                  pl.BlockSpec((tm, 1024), lambda i: (i, U_DK // 1024)),
                  pl.BlockSpec((tm, 512), lambda i: (i, U_DV // 512)),
                  pl.BlockSpec((1, LANES), const),
                  pl.BlockSpec((1, LANES), const),
                  pl.BlockSpec((tm, LANES), rows),
                  pl.BlockSpec((tm, LANES), rows),
                  pl.BlockSpec((tm, LANES), rows)],
        out_specs=(pl.BlockSpec((nsub, LANES, tm), lambda i: (0, 0, i)),
                   pl.BlockSpec((nsub, 1, tm, LANES), lambda i: (0, i, 0, 0)),
                   pl.BlockSpec((DIF_HEADS, 1, 2 * DIF_HEAD_DIM, tm), lambda i: (0, i, 0, 0))),
        compiler_params=_params("parallel"),
        name="dif_prep",
    )(u, u, u, gq, gk, *tabs)


ATTN_GROUP = 4


def _attn_group(qt_ref, k_ref, vt_ref, v_of_sub, s_buf, p_buf, acc_ref, n_chunks):
    tq = qt_ref.shape[2]
    last = n_chunks - 1

    def scores(c):
        out = []
        for a in range(ATTN_GROUP):
            s = _dot(k_ref[a, c], qt_ref[a])
            out.append((s, jnp.max(s, axis=0, keepdims=True)))
        return out

    def stash(sc, slot):
        for a in range(ATTN_GROUP):
            s_buf[slot, a] = sc[a][0]
        return tuple(mx for (_, mx) in sc)

    def softmax(s_of, mx, slot, ml):
        new_ml, alphas = [], []
        for a in range(ATTN_GROUP):
            m, l = ml[a]
            m_new = jnp.maximum(m, mx[a])
            alpha = jnp.exp2(m - m_new)
            p = jnp.exp2(s_of(a) - m_new)
            new_ml.append((m_new, alpha * l + jnp.sum(p, axis=0, keepdims=True)))
            p_buf[slot, a] = p.astype(BF16)
            alphas.append(alpha)
        return tuple(new_ml), tuple(alphas)

    def values(c, slot, alphas):
        for a in range(ATTN_GROUP):
            acc_ref[a] = alphas[a] * acc_ref[a] + _dot(vt_ref[v_of_sub[a], c], p_buf[slot, a])

    def four_chunks(t, state):
        ml, alphas, mx0 = state
        c = 4 * t + 1
        for half in range(2):
            cur, nxt = half, 1 - half
            even = scores(c + 1)
            values(c - 1, 0, alphas)
            ml, alphas = softmax(lambda a: s_buf[cur, a], mx0, 1, ml)
            mx0 = stash(scores(jnp.minimum(c + 2, last)), nxt)
            values(c, 1, alphas)
            ml, alphas = softmax(lambda a: even[a][0], [mx for (_, mx) in even], 0, ml)
            c = c + 2
        return ml, alphas, mx0

    acc_ref[...] = jnp.zeros(acc_ref.shape, F32)
    ml = tuple((jnp.full((1, tq), -jnp.inf, F32), jnp.zeros((1, tq), F32)) for _ in range(ATTN_GROUP))
    first = scores(0)
    ml, alphas = softmax(lambda a: first[a][0], [mx for (_, mx) in first], 0, ml)
    mx0 = stash(scores(1), 0)
    n_trips = jnp.where(pl.program_id(1) > 0, last // 4, 0)
    ml, alphas, _ = lax.fori_loop(0, n_trips, four_chunks, (ml, alphas, mx0))
    values(4 * n_trips, 0, alphas)
    return [l for (_, l) in ml]


def _mla_attn_kernel(qt_ref, k_ref, vt_ref, o_ref, s_buf, p_buf, acc_ref, *, n_chunks):
    ls = _attn_group(qt_ref, k_ref, vt_ref, tuple(range(ATTN_GROUP)), s_buf, p_buf, acc_ref, n_chunks)
    o = jnp.concatenate([acc_ref[a] * (1.0 / ls[a]) for a in range(ATTN_GROUP)], axis=0)
    o_ref[...] = o.T


def _dif_attn_kernel(lam_ref, gsub_ref, qt_ref, k_ref, vt_ref, o_ref, s_buf, p_buf, acc_ref,
                     *, n_chunks, lam_init):
    ls = _attn_group(qt_ref, k_ref, vt_ref, tuple(a // 2 for a in range(ATTN_GROUP)), s_buf, p_buf,
                     acc_ref, n_chunks)
    lp = lam_ref[...]
    lam = (jnp.exp(jnp.sum(lp[0:1] * lp[1:2], axis=-1, keepdims=True))
           - jnp.exp(jnp.sum(lp[2:3] * lp[3:4], axis=-1, keepdims=True)) + lam_init)
    outs = []
    for h in range(ATTN_GROUP // 2):
        o = acc_ref[2 * h] * (1.0 / ls[2 * h]) - lam * (acc_ref[2 * h + 1] * (1.0 / ls[2 * h + 1]))
        outs.append(o * lax.rsqrt(jnp.mean(o * o, axis=0, keepdims=True) + EPS))
    gsub = gsub_ref[...] * (1.0 - lam_init)
    o_ref[...] = jnp.concatenate(outs, axis=0).T * jnp.concatenate([gsub] * len(outs), axis=1)


def _attention(qt, k, vt, *, lam=None, gsub=None, lam_init=None):
    nsub, _, t = qt.shape
    n_chunks = k.shape[1]
    assert (n_chunks - 1) % 4 == 0 and nsub % ATTN_GROUP == 0
    tq = KEY_CHUNK
    dv = vt.shape[2]
    groups = nsub // ATTN_GROUP
    n_v = vt.shape[0] // groups
    out_w = n_v * dv
    resident = dict(pipeline_mode=pl.Buffered(1))
    specs = [pl.BlockSpec((ATTN_GROUP, LANES, tq), lambda g, i: (g, 0, i)),
             pl.BlockSpec((ATTN_GROUP, n_chunks, KEY_CHUNK, LANES), lambda g, i: (g, 0, 0, 0), **resident),
             pl.BlockSpec((n_v, n_chunks, dv, KEY_CHUNK), lambda g, i: (g, 0, 0, 0), **resident)]
    if lam is None:
        body = functools.partial(_mla_attn_kernel, n_chunks=n_chunks)
        args = (qt, k, vt)
    else:
        body = functools.partial(_dif_attn_kernel, n_chunks=n_chunks, lam_init=lam_init)
        specs = [pl.BlockSpec((8, LANES), lambda g, i: (0, 0)),
                 pl.BlockSpec((1, LANES), lambda g, i: (0, 0))] + specs
        args = (lam, gsub, qt, k, vt)
    return pl.pallas_call(
        body,
        out_shape=jax.ShapeDtypeStruct((t, groups * out_w), F32),
        grid=(groups, t // tq),
        in_specs=specs,
        out_specs=pl.BlockSpec((tq, out_w), lambda g, i: (i, g)),
        scratch_shapes=[pltpu.VMEM((2, ATTN_GROUP, KEY_CHUNK, tq), F32),
                        pltpu.VMEM((2, ATTN_GROUP, KEY_CHUNK, tq), BF16),
                        pltpu.VMEM((ATTN_GROUP, dv, tq), F32)],
        compiler_params=_params("parallel", "arbitrary"),
        name="mla_attention" if lam is None else "dif_attention",
    )(*args)


def _ssm_prep_kernel(x_ref, prev_ref, next_ref, dt_ref, w_ref, b_ref, dtb_ref, xo_ref, dto_ref, e_ref,
                     *, n_ctx, n_tok, tm):
    row0 = pl.program_id(0) * tm
    pad = SSM_CONV // 2
    has_prev = jnp.logical_and(row0 != 0, row0 != n_ctx)
    has_next = jnp.logical_and(row0 + tm != n_ctx, row0 + tm != n_tok)
    e_ref[0:8] = jnp.where(has_prev, prev_ref[...], 0.0)
    e_ref[8:8 + tm] = x_ref[...]
    e_ref[8 + tm:16 + tm] = jnp.where(has_next, next_ref[...], 0.0)
    acc = jnp.zeros(x_ref.shape, F32) + b_ref[...]
    for k in range(SSM_CONV):
        acc = acc + w_ref[k:k + 1, :] * e_ref[pl.ds(8 - pad + k, tm), :]
    xo_ref[...] = _silu(acc)
    d = dt_ref[...] + dtb_ref[...]
    dto_ref[...] = jnp.maximum(d, 0.0) + jnp.log1p(jnp.exp(-jnp.abs(d)))


def _ssm_prep(u, conv_w, conv_b, dt_bias, n_ctx):
    t = u.shape[0]
    tm = ROW_TILE
    nt = t // tm
    cb = U_XBC // SSM_XBC
    const = lambda i: (0, 0)
    return pl.pallas_call(
        functools.partial(_ssm_prep_kernel, n_ctx=n_ctx, n_tok=t, tm=tm),
        out_shape=(jax.ShapeDtypeStruct((t, SSM_XBC), F32), jax.ShapeDtypeStruct((t, LANES), F32)),
        grid=(nt,),
        in_specs=[pl.BlockSpec((tm, SSM_XBC), lambda i: (i, cb)),
                  pl.BlockSpec((8, SSM_XBC), lambda i: (jnp.maximum(i * (tm // 8) - 1, 0), cb)),
                  pl.BlockSpec((8, SSM_XBC), lambda i: (jnp.minimum((i + 1) * (tm // 8), t // 8 - 1), cb)),
                  pl.BlockSpec((tm, LANES), lambda i: (i, U_DT // LANES)),
                  pl.BlockSpec((8, SSM_XBC), const),
                  pl.BlockSpec((1, SSM_XBC), const),
                  pl.BlockSpec((1, LANES), const)],
        out_specs=(pl.BlockSpec((tm, SSM_XBC), lambda i: (i, 0)),
                   pl.BlockSpec((tm, LANES), lambda i: (i, 0))),
        scratch_shapes=[pltpu.VMEM((tm + 16, SSM_XBC), F32)],
        compiler_params=_params("parallel"),
        name="ssm_prep",
    )(u, u, u, u, conv_w, conv_b, dt_bias)


def _ssd_kernel(x_ref, dt_ref, alog_ref, o_ref, h_ref, *, direction):
    lc = SSM_CHUNK

    @pl.when(pl.program_id(0) == 0)
    def _():
        h_ref[...] = jnp.zeros(h_ref.shape, F32)

    xbc = x_ref[...]
    dt = dt_ref[...]
    dta = dt * (-jnp.exp(alog_ref[...]))
    r = lax.broadcasted_iota(jnp.int32, (lc, lc), 0)
    c = lax.broadcasted_iota(jnp.int32, (lc, lc), 1)
    keep = (r >= c) if direction == 0 else (r <= c)
    tri = jnp.where(keep, 1.0, 0.0).astype(F32)
    cum = jnp.dot(tri, dta, preferred_element_type=F32, precision=lax.Precision.HIGHEST)
    cum_t = cum.T
    total = jnp.sum(dta, axis=0, keepdims=True)
    to_end = jnp.exp(total - cum)
    from_start = jnp.exp(cum)
    chunk_decay = jnp.exp(total)
    x_t = xbc[:, :SSM_WIDTH].T
    w_t = (dt * to_end).T
    outs = []
    for g in range(SSM_GROUPS):
        b_g = xbc[:, SSM_WIDTH + SSM_STATE * g:SSM_WIDTH + SSM_STATE * (g + 1)].astype(BF16)
        c_g = xbc[:, SSM_WIDTH + SSM_STATE * (SSM_GROUPS + g):
                  SSM_WIDTH + SSM_STATE * (SSM_GROUPS + g + 1)].astype(BF16)
        cb = _dot_nt(c_g, b_g)
        for hh in range(SSM_HEADS // SSM_GROUPS):
            h = g * (SSM_HEADS // SSM_GROUPS) + hh
            col = direction * SSM_HEADS + h
            seg = jnp.exp(jnp.where(keep, cum[:, col:col + 1] - cum_t[col:col + 1, :], -jnp.inf))
            xd = xbc[:, SSM_HEAD_DIM * h:SSM_HEAD_DIM * (h + 1)] * dt[:, col:col + 1]
            y = _dot((cb * seg).astype(BF16), xd.astype(BF16))
            state = h_ref[h]
            y_off = _dot_nt(c_g, state.astype(BF16))
            outs.append(y + y_off * from_start[:, col:col + 1])
            xw_t = x_t[SSM_HEAD_DIM * h:SSM_HEAD_DIM * (h + 1), :] * w_t[col:col + 1, :]
            upd = _dot(xw_t.astype(BF16), b_g)
            h_ref[h] = state * chunk_decay[:, col:col + 1] + upd
    o_ref[...] = jnp.concatenate(outs, axis=1)


def _ssd(xbc, dt, a_log, n_ctx, direction):
    t = xbc.shape[0]
    lc = SSM_CHUNK
    nc = t // lc
    ncc = n_ctx // lc
    if direction == 0:
        order = lambda s: s
    else:
        order = lambda s: jnp.where(s < ncc, ncc - 1 - s, nc - 1 - (s - ncc))
    return pl.pallas_call(
        functools.partial(_ssd_kernel, direction=direction),
        out_shape=jax.ShapeDtypeStruct((t, SSM_WIDTH), F32),
        grid=(nc,),
        in_specs=[pl.BlockSpec((lc, SSM_XBC), lambda s: (order(s), 0)),
                  pl.BlockSpec((lc, LANES), lambda s: (order(s), 0)),
                  pl.BlockSpec((1, LANES), lambda s: (0, 0))],
        out_specs=pl.BlockSpec((lc, SSM_WIDTH), lambda s: (order(s), 0)),
        scratch_shapes=[pltpu.VMEM((SSM_HEADS, SSM_HEAD_DIM, SSM_STATE), F32)],
        compiler_params=_params("arbitrary"),
        name="ssd_fwd" if direction == 0 else "ssd_bwd",
    )(xbc, dt, a_log)


def _merge_kernel(x_ref, gate_ref, ya_ref, yb_ref, yf_ref, yr_ref, xs_ref, z_ref,
                  bg_ref, dskip_ref, gssm_ref, wa_ref, wb_ref, wc_ref, wo_ref,
                  ml_ref, mc_ref, g2_ref, wr_ref, br_ref,
                  xo_ref, f_ref, lg_ref, *, n_ctx, tm):
    y = (yf_ref[...] + yr_ref[...] + dskip_ref[...] * xs_ref[...]) * _silu(z_ref[...])
    gw = SSM_WIDTH // SSM_GROUPS
    yc = jnp.concatenate([_rms_rows(y[:, gw * g:gw * (g + 1)], gw) for g in range(SSM_GROUPS)], axis=1)
    yc = yc * gssm_ref[...]
    gate = _sigmoid(gate_ref[...] + bg_ref[...])
    m = (gate[:, :D_MODEL] * _dot(ya_ref[...].astype(BF16), wa_ref[...])
         + gate[:, D_MODEL:2 * D_MODEL] * _dot(yb_ref[...].astype(BF16), wb_ref[...])
         + gate[:, 2 * D_MODEL:] * _dot(yc.astype(BF16), wc_ref[...]))
    out = _dot(m.astype(BF16), wo_ref[...])
    row0 = pl.program_id(0) * tm
    is_ctx = (row0 + lax.broadcasted_iota(jnp.int32, (tm, 1), 0)) < n_ctx
    gt1 = jnp.where(is_ctx, mc_ref[2:3, :], ml_ref[2:3, :])
    x_new = x_ref[...] + gt1 * out
    xo_ref[...] = x_new
    f = _modulated_norm(x_new, g2_ref[...], ml_ref[3:5, :], mc_ref[3:5, :], row0, n_ctx)
    f_ref[...] = f.astype(BF16)
    lg_ref[...] = jnp.dot(f, wr_ref[...], preferred_element_type=F32,
                          precision=lax.Precision.HIGHEST) + br_ref[...]


def _merge(x_all, u, ya, yb, yf, yr, xbc, b_gate, dskip, g_ssm, wa, wb, wc, wo, mod_lat, mod_ctx,
           g2, w_router, b_router, n_ctx):
    t, d = x_all.shape
    tm = ROW_TILE
    const = lambda i: (0, 0)
    rows = lambda i: (i, 0)
    full = lambda a: pl.BlockSpec(a.shape, const)
    return pl.pallas_call(
        functools.partial(_merge_kernel, n_ctx=n_ctx, tm=tm),
        out_shape=(jax.ShapeDtypeStruct((t, d), F32), jax.ShapeDtypeStruct((t, d), BF16),
                   jax.ShapeDtypeStruct((t, LANES), F32)),
        grid=(t // tm,),
        in_specs=[pl.BlockSpec((tm, d), rows),
                  pl.BlockSpec((tm, GATE_COLS), lambda i: (i, U_GATE // GATE_COLS)),
                  pl.BlockSpec((tm, MLA_WIDTH), rows),
                  pl.BlockSpec((tm, DIF_WIDTH), rows),
                  pl.BlockSpec((tm, SSM_WIDTH), rows),
                  pl.BlockSpec((tm, SSM_WIDTH), rows),
                  pl.BlockSpec((tm, SSM_WIDTH), rows),
                  pl.BlockSpec((tm, SSM_WIDTH), lambda i: (i, U_Z // SSM_WIDTH)),
                  full(b_gate), full(dskip), full(g_ssm), full(wa), full(wb), full(wc), full(wo),
                  full(mod_lat), full(mod_ctx), full(g2), full(w_router), full(b_router)],
        out_specs=(pl.BlockSpec((tm, d), rows), pl.BlockSpec((tm, d), rows),
                   pl.BlockSpec((tm, LANES), rows)),
        compiler_params=_params("parallel"),
        name="merge",
    )(x_all, u, ya, yb, yf, yr, xbc, u, b_gate, dskip, g_ssm, wa, wb, wc, wo, mod_lat, mod_ctx,
      g2, w_router, b_router)


def _moe_kernel(be_ref, nb_ref, x_ref, wgu_ref, bgu_ref, wd_ref, bd_ref, o_ref, wgu_s, wd_s):
    b = pl.program_id(0)
    prev = be_ref[jnp.maximum(b - 1, 0)]
    fresh = jnp.logical_or(b == 0, be_ref[b] != prev)

    @pl.when(fresh)
    def _():
        wgu_s[...] = wgu_ref[...].astype(BF16)
        wd_s[...] = wd_ref[...].astype(BF16)

    @pl.when(b < nb_ref[0])
    def _():
        gu = _dot(x_ref[...], wgu_s[...]) + bgu_ref[...]
        glu = jnp.minimum(gu[:, :D_FF], SWIGLU_LIMIT)
        lin = jnp.clip(gu[:, D_FF:], -SWIGLU_LIMIT, SWIGLU_LIMIT)
        act = glu * _sigmoid(SWIGLU_ALPHA * glu) * (lin + 1.0)
        o_ref[...] = _dot(act.astype(BF16), wd_s[...]) + bd_ref[...]

    @pl.when(b >= nb_ref[0])
    def _():
        o_ref[...] = jnp.zeros(o_ref.shape, F32)


def _moe_experts(block_e, n_used, x_sorted, w_gu, b_gu, w_down, b_down, layer):
    n_slots, d = x_sorted.shape
    n_blocks = n_slots // MOE_BLOCK
    grid_spec = pltpu.PrefetchScalarGridSpec(
        num_scalar_prefetch=2,
        grid=(n_blocks,),
        in_specs=[pl.BlockSpec((MOE_BLOCK, d), lambda b, be, nb: (b, 0)),
                  pl.BlockSpec((None, None, d, 2 * D_FF), lambda b, be, nb: (layer, be[b], 0, 0)),
                  pl.BlockSpec((None, None, 1, 2 * D_FF), lambda b, be, nb: (layer, be[b], 0, 0)),
                  pl.BlockSpec((None, None, D_FF, d), lambda b, be, nb: (layer, be[b], 0, 0)),
                  pl.BlockSpec((None, None, 1, d), lambda b, be, nb: (layer, be[b], 0, 0))],
        out_specs=pl.BlockSpec((MOE_BLOCK, d), lambda b, be, nb: (b, 0)),
        scratch_shapes=[pltpu.VMEM((d, 2 * D_FF), BF16), pltpu.VMEM((D_FF, d), BF16)],
    )
    return pl.pallas_call(
        _moe_kernel,
        out_shape=jax.ShapeDtypeStruct((n_slots, d), F32),
        grid_spec=grid_spec,
        compiler_params=_params("arbitrary"),
        name="moe_experts",
    )(block_e, n_used, x_sorted, w_gu, b_gu, w_down, b_down)


def _combine_kernel(x_ref, y_ref, g_ref, ml_ref, mc_ref, o_ref, *, n_ctx, tm):
    d = x_ref.shape[1]
    is_ctx = (pl.program_id(0) * tm + lax.broadcasted_iota(jnp.int32, (tm, 1), 0)) < n_ctx
    gate2 = jnp.where(is_ctx, mc_ref[...], ml_ref[...])
    g = g_ref[...]
    y = g[:, 0:1] * y_ref[:, 0:d]
    for k in range(1, TOP_K):
        y = y + g[:, k:k + 1] * y_ref[:, k * d:(k + 1) * d]
    o_ref[...] = x_ref[...] + gate2 * y


def _combine(x_all, y_rows, gates, gate2_lat, gate2_ctx, n_ctx):
    t, d = x_all.shape
    tm = ROW_TILE
    rows = lambda i: (i, 0)
    const = lambda i: (0, 0)
    return pl.pallas_call(
        functools.partial(_combine_kernel, n_ctx=n_ctx, tm=tm),
        out_shape=jax.ShapeDtypeStruct((t, d), F32),
        grid=(t // tm,),
        in_specs=[pl.BlockSpec((tm, d), rows), pl.BlockSpec((tm, TOP_K * d), rows),
                  pl.BlockSpec((tm, LANES), rows), pl.BlockSpec((1, d), const), pl.BlockSpec((1, d), const)],
        out_specs=pl.BlockSpec((tm, d), rows),
        compiler_params=_params("parallel"),
        name="moe_combine",
    )(x_all, y_rows, gates, gate2_lat, gate2_ctx)


def _route(logits, n_tok):
    top_val, top_idx = lax.top_k(logits, TOP_K)
    gates = jax.nn.softmax(top_val, axis=-1)
    n_assign = n_tok * TOP_K
    chosen = jnp.sum((top_idx[..., None] == jnp.arange(N_EXPERTS)).astype(jnp.int32), axis=1)
    before = jnp.cumsum(chosen, axis=0) - chosen
    counts = before[-1] + chosen[-1]
    padded = (counts + MOE_BLOCK - 1) // MOE_BLOCK * MOE_BLOCK
    pad_end = jnp.cumsum(padded)
    pad_start = pad_end - padded
    slot_of = jnp.take_along_axis(before + pad_start[None, :], top_idx, axis=1)
    n_blocks = -(-(n_assign + N_EXPERTS * (MOE_BLOCK - 1)) // MOE_BLOCK)
    n_slots = n_blocks * MOE_BLOCK
    token = jnp.arange(n_assign, dtype=jnp.int32) // TOP_K
    slot_tok = jnp.zeros((n_slots,), jnp.int32).at[slot_of.reshape(-1)].set(token, unique_indices=True)
    block_start = jnp.arange(n_blocks, dtype=jnp.int32) * MOE_BLOCK
    block_e = jnp.minimum(jnp.searchsorted(pad_end, block_start, side='right'), N_EXPERTS - 1)
    n_used = (pad_end[-1] // MOE_BLOCK).reshape(1)
    return gates, slot_tok, slot_of, block_e.astype(jnp.int32), n_used.astype(jnp.int32)


def kernel(x, c, ctx, c_ctx, w_mod, b_mod, g_norm1, g_norm2, w_in, b_gate, mla_g_q, mla_w_uq, mla_g_kv, mla_w_ukv, mla_g_qn, mla_g_kn, dif_g_qn, dif_g_kn, dif_lambda, dif_g_sub, ssm_conv_w, ssm_conv_b, ssm_dt_bias, ssm_a_log, ssm_d, ssm_g_norm, w_up_mla, w_up_dif, w_up_ssm, w_out, moe_w_router, moe_b_router, moe_w_gu, moe_b_gu, moe_w_down, moe_b_down):
    assert x.shape[0] == 1 and ctx.shape[0] == 1
    depth = w_in.shape[0]
    seq = x.shape[1]
    n_ctx = ctx.shape[1]
    n_tok = n_ctx + seq
    d = D_MODEL
    assert n_ctx == KEY_CHUNK and n_tok % ROW_TILE == 0 and seq % GRID_W == 0

    x_all = jnp.concatenate([ctx[0], x[0]], axis=0)
    cc = jnp.zeros((8, d), F32).at[0].set(c[0]).at[1].set(c_ctx)
    mod = _mod_vectors(cc, w_mod, b_mod)
    mod = mod[:, :2].reshape(depth, 2, 6, d)

    rope_mla = _rope_tables(seq, n_ctx, MLA_ROPE, MLA_NOPE)
    rope_dif = _rope_tables(seq, n_ctx, DIF_HEAD_DIM, 0)
    in_cols = _in_proj_columns()
    uq_cols = _head_columns(MLA_HEADS, MLA_QK, 0, MLA_QK, LANES)
    uk_cols = _head_columns(MLA_HEADS, MLA_NOPE + MLA_V, 0, MLA_NOPE, LANES)
    uv_cols = _head_columns(MLA_HEADS, MLA_NOPE + MLA_V, MLA_NOPE, MLA_V, MLA_V)

    for i in range(depth):
        lam_init = 0.8 - 0.6 * math.exp(-0.3 * i)
        mod_lat, mod_ctx = mod[i, 0], mod[i, 1]
        w_in_i = _take_columns(w_in[i], in_cols).astype(BF16)
        u = _in_proj(x_all, g_norm1[i][None], mod_lat[0:2], mod_ctx[0:2], w_in_i, n_ctx)

        qt, k, vt = _mla_prep(
            u, mla_g_q[i][None], _take_columns(mla_w_uq[i], uq_cols).astype(BF16),
            mla_g_kv[i][None], _take_columns(mla_w_ukv[i], uk_cols).astype(BF16),
            _take_columns(mla_w_ukv[i], uv_cols).astype(BF16),
            _pad_lanes(mla_g_qn[i][None]), _pad_lanes(mla_g_kn[i][None]), rope_mla)
        ya = _attention(qt, k, vt)

        qt, k, vt = _dif_prep(u, _pad_lanes(dif_g_qn[i][None]), _pad_lanes(dif_g_kn[i][None]), rope_dif)
        lam_rows = jnp.zeros((8, LANES), F32).at[:4, :DIF_HEAD_DIM].set(dif_lambda[i])
        yb = _attention(qt, k, vt, lam=lam_rows, gsub=dif_g_sub[i][None], lam_init=lam_init)

        conv_w = jnp.zeros((8, SSM_XBC), F32).at[:SSM_CONV].set(ssm_conv_w[i])
        xbc, dt = _ssm_prep(u, conv_w, ssm_conv_b[i][None], _pad_lanes(ssm_dt_bias[i].reshape(1, -1)), n_ctx)
        a_log = _pad_lanes(ssm_a_log[i].reshape(1, -1))
        yf = _ssd(xbc, dt, a_log, n_ctx, 0)
        yr = _ssd(xbc, dt, a_log, n_ctx, 1)

        dskip = jnp.repeat(ssm_d[i, 0] + ssm_d[i, 1], SSM_HEAD_DIM)[None]
        w_router = jnp.zeros((d, LANES), F32).at[:, :N_EXPERTS].set(moe_w_router[i])
        b_router = jnp.zeros((1, LANES), F32).at[0, :N_EXPERTS].set(moe_b_router[i])
        x_all, f, logits = _merge(
            x_all, u, ya, yb, yf, yr, xbc, b_gate[i][None], dskip, ssm_g_norm[i][None],
            w_up_mla[i].astype(BF16), w_up_dif[i].astype(BF16), w_up_ssm[i].astype(BF16),
            w_out[i].astype(BF16), mod_lat[0:5], mod_ctx[0:5], g_norm2[i][None],
            w_router, b_router, n_ctx)

        gates, slot_tok, slot_of, block_e, n_used = _route(logits[:, :N_EXPERTS], n_tok)
        y_slots = _moe_experts(block_e, n_used, f[slot_tok], moe_w_gu,
                               moe_b_gu.reshape(depth, N_EXPERTS, 1, 2 * D_FF), moe_w_down,
                               moe_b_down.reshape(depth, N_EXPERTS, 1, d), i)
        y_rows = y_slots[slot_of.reshape(-1)].reshape(n_tok, TOP_K * d)
        x_all = _combine(x_all, y_rows, _pad_lanes(gates), mod_lat[5:6], mod_ctx[5:6], n_ctx)
    return x_all[n_ctx:][None]
```

```python
import functools
import math

import numpy as np
import jax
import jax.numpy as jnp
from jax import lax
from jax.experimental import pallas as pl
from jax.experimental.pallas import tpu as pltpu

F32 = jnp.float32
BF16 = jnp.bfloat16
LANES = 128
VMEM_LIMIT = 52 * 1024 * 1024

D_MODEL = 1024
EPS = 1e-6
ROPE_THETA = 10000.0
GRID_W = 64
N_BRANCH = 3

MLA_HEADS = 8
MLA_Q_RANK = 256
MLA_KV_RANK = 128
MLA_NOPE = 64
MLA_ROPE = 32
MLA_V = 64
MLA_QK = MLA_NOPE + MLA_ROPE
MLA_WIDTH = MLA_HEADS * MLA_V

DIF_HEADS = 4
DIF_HEAD_DIM = 64
DIF_WIDTH = DIF_HEADS * 2 * DIF_HEAD_DIM

SSM_HEADS = 8
SSM_HEAD_DIM = 64
SSM_WIDTH = SSM_HEADS * SSM_HEAD_DIM
SSM_GROUPS = 2
SSM_STATE = 128
SSM_CONV = 5
SSM_CHUNK = 128
SSM_XBC = SSM_WIDTH + 2 * SSM_GROUPS * SSM_STATE

N_EXPERTS = 32
TOP_K = 4
D_FF = 1024
SWIGLU_LIMIT = 7.0
SWIGLU_ALPHA = 1.702
MOE_BLOCK = 256

MLA_COLS = MLA_Q_RANK + MLA_KV_RANK + MLA_ROPE
DIF_COLS = 3 * DIF_WIDTH
SSM_COLS = SSM_WIDTH + SSM_XBC + 2 * SSM_HEADS
GATE_COLS = N_BRANCH * D_MODEL

ROW_TILE = 256
KEY_CHUNK = 256
LOG2E = 1.4426950408889634

U_GATE, U_DQ, U_DK, U_XBC, U_MLA, U_DV, U_Z, U_DT = 0, 3072, 4096, 5120, 6144, 6656, 7168, 7680
U_COLS = 8192
U_TILE_N = 1024


def _in_proj_columns():
    src = np.full((U_COLS,), -1, np.int64)
    dif0 = MLA_COLS
    ssm0 = MLA_COLS + DIF_COLS
    gate0 = ssm0 + SSM_COLS
    src[U_GATE:U_GATE + GATE_COLS] = gate0 + np.arange(GATE_COLS)
    for a in range(2 * DIF_HEADS):
        src[U_DQ + LANES * a:U_DQ + LANES * a + DIF_HEAD_DIM] = dif0 + DIF_HEAD_DIM * a + np.arange(DIF_HEAD_DIM)
        src[U_DK + LANES * a:U_DK + LANES * a + DIF_HEAD_DIM] = (dif0 + DIF_WIDTH + DIF_HEAD_DIM * a
                                                                  + np.arange(DIF_HEAD_DIM))
    src[U_DV:U_DV + DIF_WIDTH] = dif0 + 2 * DIF_WIDTH + np.arange(DIF_WIDTH)
    src[U_MLA:U_MLA + MLA_Q_RANK + MLA_KV_RANK] = np.arange(MLA_Q_RANK + MLA_KV_RANK)
    pe0 = U_MLA + MLA_Q_RANK + MLA_KV_RANK + MLA_NOPE
    src[pe0:pe0 + MLA_ROPE] = MLA_Q_RANK + MLA_KV_RANK + np.arange(MLA_ROPE)
    src[U_Z:U_Z + SSM_WIDTH] = ssm0 + np.arange(SSM_WIDTH)
    src[U_XBC:U_XBC + SSM_XBC] = ssm0 + SSM_WIDTH + np.arange(SSM_XBC)
    src[U_DT:U_DT + 2 * SSM_HEADS] = ssm0 + SSM_WIDTH + SSM_XBC + np.arange(2 * SSM_HEADS)
    return src


def _take_columns(w, src):
    cols = jnp.take(w, jnp.asarray(np.maximum(src, 0)), axis=-1)
    return jnp.where(jnp.asarray(src >= 0), cols, 0.0)


def _head_columns(n_heads, src_stride, src_off, width, dst_stride):
    src = np.full((n_heads * dst_stride,), -1, np.int64)
    for h in range(n_heads):
        src[h * dst_stride:h * dst_stride + width] = h * src_stride + src_off + np.arange(width)
    return src


def _pad_lanes(v, n=LANES):
    return jnp.pad(v, [(0, 0)] * (v.ndim - 1) + [(0, n - v.shape[-1])])


def _row_tile(n, cap):
    best = 8
    for t in range(8, cap + 1, 8):
        if n % t == 0:
            best = t
    return best


def _dot(a, b):
    return jnp.dot(a, b, preferred_element_type=F32)


def _dot_nt(a, b):
    return lax.dot_general(a, b, (((1,), (1,)), ((), ())), preferred_element_type=F32)


def _sigmoid(x):
    return 1.0 / (1.0 + jnp.exp(-x))


def _silu(x):
    return x * _sigmoid(x)


def _rms_rows(x, n):
    return x * lax.rsqrt(jnp.sum(x * x, axis=-1, keepdims=True) * (1.0 / n) + EPS)


def _params(*sem):
    return pltpu.CompilerParams(dimension_semantics=sem, vmem_limit_bytes=VMEM_LIMIT)


def _mod_kernel(a_ref, w_ref, b_ref, o_ref):
    a = _silu(a_ref[...]).astype(BF16)
    o_ref[0] = _dot(a, w_ref[0].astype(BF16)) + b_ref[0]


def _mod_vectors(cc, w_mod, b_mod):
    depth, d, n = w_mod.shape
    tn = 1536
    return pl.pallas_call(
        _mod_kernel,
        out_shape=jax.ShapeDtypeStruct((depth, 8, n), F32),
        grid=(depth, n // tn),
        in_specs=[pl.BlockSpec((8, d), lambda l, j: (0, 0)),
                  pl.BlockSpec((1, d, tn), lambda l, j: (l, 0, j)),
                  pl.BlockSpec((1, 1, tn), lambda l, j: (l, 0, j))],
        out_specs=pl.BlockSpec((1, 8, tn), lambda l, j: (l, 0, j)),
        compiler_params=_params("parallel", "parallel"),
        name="mod_vectors",
    )(cc, w_mod, b_mod.reshape(depth, 1, n))


def _modulated_norm(x, g, mod_lat, mod_ctx, row0, n_ctx):
    rows = x.shape[0]
    is_ctx = (row0 + lax.broadcasted_iota(jnp.int32, (rows, 1), 0)) < n_ctx
    shift = jnp.where(is_ctx, mod_ctx[0:1, :], mod_lat[0:1, :])
    scale = jnp.where(is_ctx, mod_ctx[1:2, :], mod_lat[1:2, :])
    return _rms_rows(x, x.shape[1]) * g * (1.0 + scale) + shift


def _in_proj_kernel(x_ref, g_ref, ml_ref, mc_ref, w_ref, o_ref, h_ref, *, n_ctx, tm):
    @pl.when(pl.program_id(1) == 0)
    def _():
        h = _modulated_norm(x_ref[...], g_ref[...], ml_ref[...], mc_ref[...], pl.program_id(0) * tm, n_ctx)
        h_ref[...] = h.astype(BF16)

    o_ref[...] = _dot(h_ref[...], w_ref[...])


def _in_proj(x_all, g, mod_lat, mod_ctx, w, n_ctx):
    t, d = x_all.shape
    n = w.shape[1]
    tm = _row_tile(t, 1280)
    return pl.pallas_call(
        functools.partial(_in_proj_kernel, n_ctx=n_ctx, tm=tm),
        out_shape=jax.ShapeDtypeStruct((t, n), F32),
        grid=(t // tm, n // U_TILE_N),
        in_specs=[pl.BlockSpec((tm, d), lambda i, j: (i, 0)),
                  pl.BlockSpec((1, d), lambda i, j: (0, 0)),
                  pl.BlockSpec((2, d), lambda i, j: (0, 0)),
                  pl.BlockSpec((2, d), lambda i, j: (0, 0)),
                  pl.BlockSpec((d, U_TILE_N), lambda i, j: (0, j))],
        out_specs=pl.BlockSpec((tm, U_TILE_N), lambda i, j: (i, j)),
        scratch_shapes=[pltpu.VMEM((tm, d), BF16)],
        compiler_params=_params("parallel", "arbitrary"),
        name="in_proj",
    )(x_all, g, mod_lat, mod_ctx, w)


def _rope_tables(seq_len, n_ctx, rot_dim, lane0):
    n_rows = seq_len // GRID_W
    row = jnp.repeat(jnp.arange(n_rows), GRID_W).astype(F32)
    col = jnp.tile(jnp.arange(GRID_W), n_rows).astype(F32)
    axis_dim = rot_dim // 2
    half = axis_dim // 2
    inv = ROPE_THETA ** (-jnp.arange(0, axis_dim, 2, dtype=F32) / axis_dim)
    ang_r = row[:, None] * inv
    ang_c = col[:, None] * inv
    zeros = jnp.zeros((seq_len, half), F32)
    cos = jnp.concatenate([jnp.cos(ang_r), jnp.cos(ang_r), jnp.cos(ang_c), jnp.cos(ang_c)], axis=1)
    s1 = jnp.concatenate([zeros, jnp.sin(ang_r), zeros, jnp.sin(ang_c)], axis=1)
    s2 = jnp.concatenate([-jnp.sin(ang_r), zeros, -jnp.sin(ang_c), zeros], axis=1)

    def place(tab, fill):
        full = jnp.full((seq_len, LANES), fill, F32).at[:, lane0:lane0 + rot_dim].set(tab)
        ctx = jnp.full((n_ctx, LANES), fill, F32)
        return jnp.concatenate([ctx, full], axis=0)

    return place(cos, 1.0), place(s1, 0.0), place(s2, 0.0)


def _rope(x, cos, s1, s2, half):
    return x * cos + pltpu.roll(x, half, 1) * s1 + pltpu.roll(x, LANES - half, 1) * s2


def _mla_prep_kernel(u_ref, gq_ref, wuq_ref, gkv_ref, wk_ref, wv_ref, gqn_ref, gkn_ref,
                     cos_ref, s1_ref, s2_ref, qt_ref, k_ref, vt_ref):
    u = u_ref[...]
    cq = u[:, :MLA_Q_RANK]
    ckv = u[:, MLA_Q_RANK:MLA_Q_RANK + MLA_KV_RANK]
    pe = u[:, MLA_Q_RANK + MLA_KV_RANK:]
    q = _dot((_rms_rows(cq, MLA_Q_RANK) * gq_ref[...]).astype(BF16), wuq_ref[...])
    kv_in = (_rms_rows(ckv, MLA_KV_RANK) * gkv_ref[...]).astype(BF16)
    kn = _dot(kv_in, wk_ref[...])
    v = _dot(kv_in, wv_ref[...])
    cos, s1, s2 = cos_ref[...], s1_ref[...], s2_ref[...]
    half = MLA_ROPE // 4
    q_scale = MLA_QK ** -0.5 * LOG2E
    for h in range(MLA_HEADS):
        qh = _rms_rows(q[:, LANES * h:LANES * (h + 1)], MLA_QK) * gqn_ref[...]
        qh = _rope(qh, cos, s1, s2, half) * q_scale
        qt_ref[h] = qh.T.astype(BF16)
        kh = _rms_rows(kn[:, LANES * h:LANES * (h + 1)] + pe, MLA_QK) * gkn_ref[...]
        k_ref[h, 0] = _rope(kh, cos, s1, s2, half).astype(BF16)
    tm = v.shape[0]
    vt_ref[:, 0] = _with_sum_rows(v.T.reshape(MLA_HEADS, MLA_V, tm)).astype(BF16)


def _mla_prep(u, gq, wuq, gkv, wk, wv, gqn, gkn, tabs):
    t = u.shape[0]
    tm = ROW_TILE
    nt = t // tm
    const = lambda i: (0, 0)
    rows = lambda i: (i, 0)
    return pl.pallas_call(
        _mla_prep_kernel,
        out_shape=(jax.ShapeDtypeStruct((MLA_HEADS, LANES, t), BF16),
                   jax.ShapeDtypeStruct((MLA_HEADS, nt, tm, LANES), BF16),
                   jax.ShapeDtypeStruct((MLA_HEADS, nt, MLA_V + SUM_ROWS, tm), BF16)),
        grid=(nt,),
        in_specs=[pl.BlockSpec((tm, 512), lambda i: (i, U_MLA // 512)),
                  pl.BlockSpec((1, MLA_Q_RANK), const),
                  pl.BlockSpec(wuq.shape, const),
                  pl.BlockSpec((1, MLA_KV_RANK), const),
                  pl.BlockSpec(wk.shape, const),
                  pl.BlockSpec(wv.shape, const),
                  pl.BlockSpec((1, LANES), const),
                  pl.BlockSpec((1, LANES), const),
                  pl.BlockSpec((tm, LANES), rows),
                  pl.BlockSpec((tm, LANES), rows),
                  pl.BlockSpec((tm, LANES), rows)],
        out_specs=(pl.BlockSpec((MLA_HEADS, LANES, tm), lambda i: (0, 0, i)),
                   pl.BlockSpec((MLA_HEADS, 1, tm, LANES), lambda i: (0, i, 0, 0)),
                   pl.BlockSpec((MLA_HEADS, 1, MLA_V + SUM_ROWS, tm), lambda i: (0, i, 0, 0))),
        compiler_params=_params("parallel"),
        name="mla_prep",
    )(u, gq, wuq, gkv, wk, wv, gqn, gkn, *tabs)


def _dif_prep_kernel(q_ref, k_ref, v_ref, gq_ref, gk_ref, cos_ref, s1_ref, s2_ref,
                     qt_out, k_out, vt_out):
    cos, s1, s2 = cos_ref[...], s1_ref[...], s2_ref[...]
    half = DIF_HEAD_DIM // 4
    q_scale = DIF_HEAD_DIM ** -0.5 * LOG2E
    for a in range(2 * DIF_HEADS):
        qa = _rms_rows(q_ref[:, LANES * a:LANES * (a + 1)], DIF_HEAD_DIM) * gq_ref[...]
        qt_out[a] = (_rope(qa, cos, s1, s2, half) * q_scale).T.astype(BF16)
        ka = _rms_rows(k_ref[:, LANES * a:LANES * (a + 1)], DIF_HEAD_DIM) * gk_ref[...]
        k_out[a, 0] = _rope(ka, cos, s1, s2, half).astype(BF16)
    v = v_ref[...]
    vt_out[:, 0] = _with_sum_rows(v.T.reshape(DIF_HEADS, 2 * DIF_HEAD_DIM, v.shape[0])).astype(BF16)


def _dif_prep(u, gq, gk, tabs):
    t = u.shape[0]
    tm = ROW_TILE
    nt = t // tm
    nsub = 2 * DIF_HEADS
    const = lambda i: (0, 0)
    rows = lambda i: (i, 0)
    return pl.pallas_call(
        _dif_prep_kernel,
        out_shape=(jax.ShapeDtypeStruct((nsub, LANES, t), BF16),
                   jax.ShapeDtypeStruct((nsub, nt, tm, LANES), BF16),
                   jax.ShapeDtypeStruct((DIF_HEADS, nt, 2 * DIF_HEAD_DIM + SUM_ROWS, tm), BF16)),
        grid=(nt,),
        in_specs=[pl.BlockSpec((tm, 1024), lambda i: (i, U_DQ // 1024)),
                  pl.BlockSpec((tm, 1024), lambda i: (i, U_DK // 1024)),
                  pl.BlockSpec((tm, 512), lambda i: (i, U_DV // 512)),
                  pl.BlockSpec((1, LANES), const),
                  pl.BlockSpec((1, LANES), const),
                  pl.BlockSpec((tm, LANES), rows),
                  pl.BlockSpec((tm, LANES), rows),
                  pl.BlockSpec((tm, LANES), rows)],
        out_specs=(pl.BlockSpec((nsub, LANES, tm), lambda i: (0, 0, i)),
                   pl.BlockSpec((nsub, 1, tm, LANES), lambda i: (0, i, 0, 0)),
                   pl.BlockSpec((DIF_HEADS, 1, 2 * DIF_HEAD_DIM + SUM_ROWS, tm), lambda i: (0, i, 0, 0))),
        compiler_params=_params("parallel"),
        name="dif_prep",
    )(u, u, u, gq, gk, *tabs)


ATTN_GROUP = 4
SUM_ROWS = 16


def _with_sum_rows(vt):
    heads, _, keys = vt.shape
    row = lax.broadcasted_iota(jnp.int32, (heads, SUM_ROWS, keys), 1)
    return jnp.concatenate([vt, jnp.where(row == 0, 1.0, 0.0).astype(vt.dtype)], axis=1)


def _attn_group(qt_ref, k_ref, vt_ref, v_of_sub, s_buf, p_buf, acc_ref, n_chunks):
    tq = qt_ref.shape[2]
    last = n_chunks - 1

    def scores(c):
        out = []
        for a in range(ATTN_GROUP):
            s = _dot(k_ref[a, c], qt_ref[a])
            out.append((s, jnp.max(s, axis=0, keepdims=True)))
        return out

    def stash(sc, slot):
        for a in range(ATTN_GROUP):
            s_buf[slot, a] = sc[a][0]
        return tuple(mx for (_, mx) in sc)

    def softmax(s_of, mx, slot, ms):
        new_ms, alphas = [], []
        for a in range(ATTN_GROUP):
            m_new = jnp.maximum(ms[a], mx[a])
            alphas.append(jnp.exp2(ms[a] - m_new))
            p_buf[slot, a] = jnp.exp2(s_of(a) - m_new).astype(BF16)
            new_ms.append(m_new)
        return tuple(new_ms), tuple(alphas)

    def values(c, slot, alphas):
        for a in range(ATTN_GROUP):
            acc_ref[a] = alphas[a] * acc_ref[a] + _dot(vt_ref[v_of_sub[a], c], p_buf[slot, a])

    def four_chunks(t, state):
        ms, alphas, mx0 = state
        c = 4 * t + 1
        for half in range(2):
            cur, nxt = half, 1 - half
            even = scores(c + 1)
            values(c - 1, 0, alphas)
            ms, alphas = softmax(lambda a: s_buf[cur, a], mx0, 1, ms)
            mx0 = stash(scores(jnp.minimum(c + 2, last)), nxt)
            values(c, 1, alphas)
            ms, alphas = softmax(lambda a: even[a][0], [mx for (_, mx) in even], 0, ms)
            c = c + 2
        return ms, alphas, mx0

    acc_ref[...] = jnp.zeros(acc_ref.shape, F32)
    ms = tuple(jnp.full((1, tq), -jnp.inf, F32) for _ in range(ATTN_GROUP))
    first = scores(0)
    ms, alphas = softmax(lambda a: first[a][0], [mx for (_, mx) in first], 0, ms)
    mx0 = stash(scores(1), 0)
    n_trips = jnp.where(pl.program_id(1) > 0, last // 4, 0)
    ms, alphas, _ = lax.fori_loop(0, n_trips, four_chunks, (ms, alphas, mx0))
    values(4 * n_trips, 0, alphas)


def _normalised(acc_ref, a):
    dv = acc_ref.shape[1] - SUM_ROWS
    return acc_ref[a, :dv] * (1.0 / acc_ref[a, dv:dv + 1])


def _mla_attn_kernel(qt_ref, k_ref, vt_ref, o_ref, s_buf, p_buf, acc_ref, *, n_chunks):
    _attn_group(qt_ref, k_ref, vt_ref, tuple(range(ATTN_GROUP)), s_buf, p_buf, acc_ref, n_chunks)
    o = jnp.concatenate([_normalised(acc_ref, a) for a in range(ATTN_GROUP)], axis=0)
    o_ref[...] = o.T


def _dif_attn_kernel(lam_ref, gsub_ref, qt_ref, k_ref, vt_ref, o_ref, s_buf, p_buf, acc_ref,
                     *, n_chunks, lam_init):
    _attn_group(qt_ref, k_ref, vt_ref, tuple(a // 2 for a in range(ATTN_GROUP)), s_buf, p_buf,
                acc_ref, n_chunks)
    lp = lam_ref[...]
    lam = (jnp.exp(jnp.sum(lp[0:1] * lp[1:2], axis=-1, keepdims=True))
           - jnp.exp(jnp.sum(lp[2:3] * lp[3:4], axis=-1, keepdims=True)) + lam_init)
    outs = []
    for h in range(ATTN_GROUP // 2):
        o = _normalised(acc_ref, 2 * h) - lam * _normalised(acc_ref, 2 * h + 1)
        outs.append(o * lax.rsqrt(jnp.mean(o * o, axis=0, keepdims=True) + EPS))
    gsub = gsub_ref[...] * (1.0 - lam_init)
    o_ref[...] = jnp.concatenate(outs, axis=0).T * jnp.concatenate([gsub] * len(outs), axis=1)


def _attention(qt, k, vt, *, lam=None, gsub=None, lam_init=None):
    nsub, _, t = qt.shape
    n_chunks = k.shape[1]
    assert (n_chunks - 1) % 4 == 0 and nsub % ATTN_GROUP == 0
    tq = KEY_CHUNK
    dv = vt.shape[2] - SUM_ROWS
    groups = nsub // ATTN_GROUP
    n_v = vt.shape[0] // groups
    out_w = n_v * dv
    resident = dict(pipeline_mode=pl.Buffered(1))
    specs = [pl.BlockSpec((ATTN_GROUP, LANES, tq), lambda g, i: (g, 0, i)),
             pl.BlockSpec((ATTN_GROUP, n_chunks, KEY_CHUNK, LANES), lambda g, i: (g, 0, 0, 0), **resident),
             pl.BlockSpec((n_v, n_chunks, dv + SUM_ROWS, KEY_CHUNK), lambda g, i: (g, 0, 0, 0), **resident)]
    if lam is None:
        body = functools.partial(_mla_attn_kernel, n_chunks=n_chunks)
        args = (qt, k, vt)
    else:
        body = functools.partial(_dif_attn_kernel, n_chunks=n_chunks, lam_init=lam_init)
        specs = [pl.BlockSpec((8, LANES), lambda g, i: (0, 0)),
                 pl.BlockSpec((1, LANES), lambda g, i: (0, 0))] + specs
        args = (lam, gsub, qt, k, vt)
    return pl.pallas_call(
        body,
        out_shape=jax.ShapeDtypeStruct((t, groups * out_w), F32),
        grid=(groups, t // tq),
        in_specs=specs,
        out_specs=pl.BlockSpec((tq, out_w), lambda g, i: (i, g)),
        scratch_shapes=[pltpu.VMEM((2, ATTN_GROUP, KEY_CHUNK, tq), F32),
                        pltpu.VMEM((2, ATTN_GROUP, KEY_CHUNK, tq), BF16),
                        pltpu.VMEM((ATTN_GROUP, dv + SUM_ROWS, tq), F32)],
        compiler_params=_params("parallel", "arbitrary"),
        name="mla_attention" if lam is None else "dif_attention",
    )(*args)


def _ssm_prep_kernel(x_ref, prev_ref, next_ref, dt_ref, w_ref, b_ref, dtb_ref, xo_ref, dto_ref, e_ref,
                     *, n_ctx, n_tok, tm):
    row0 = pl.program_id(0) * tm
    pad = SSM_CONV // 2
    has_prev = jnp.logical_and(row0 != 0, row0 != n_ctx)
    has_next = jnp.logical_and(row0 + tm != n_ctx, row0 + tm != n_tok)
    e_ref[0:8] = jnp.where(has_prev, prev_ref[...], 0.0)
    e_ref[8:8 + tm] = x_ref[...]
    e_ref[8 + tm:16 + tm] = jnp.where(has_next, next_ref[...], 0.0)
    acc = jnp.zeros(x_ref.shape, F32) + b_ref[...]
    for k in range(SSM_CONV):
        acc = acc + w_ref[k:k + 1, :] * e_ref[pl.ds(8 - pad + k, tm), :]
    xo_ref[...] = _silu(acc)
    d = dt_ref[...] + dtb_ref[...]
    dto_ref[...] = jnp.maximum(d, 0.0) + jnp.log1p(jnp.exp(-jnp.abs(d)))


def _ssm_prep(u, conv_w, conv_b, dt_bias, n_ctx):
    t = u.shape[0]
    tm = ROW_TILE
    nt = t // tm
    cb = U_XBC // SSM_XBC
    const = lambda i: (0, 0)
    return pl.pallas_call(
        functools.partial(_ssm_prep_kernel, n_ctx=n_ctx, n_tok=t, tm=tm),
        out_shape=(jax.ShapeDtypeStruct((t, SSM_XBC), F32), jax.ShapeDtypeStruct((t, LANES), F32)),
        grid=(nt,),
        in_specs=[pl.BlockSpec((tm, SSM_XBC), lambda i: (i, cb)),
                  pl.BlockSpec((8, SSM_XBC), lambda i: (jnp.maximum(i * (tm // 8) - 1, 0), cb)),
                  pl.BlockSpec((8, SSM_XBC), lambda i: (jnp.minimum((i + 1) * (tm // 8), t // 8 - 1), cb)),
                  pl.BlockSpec((tm, LANES), lambda i: (i, U_DT // LANES)),
                  pl.BlockSpec((8, SSM_XBC), const),
                  pl.BlockSpec((1, SSM_XBC), const),
                  pl.BlockSpec((1, LANES), const)],
        out_specs=(pl.BlockSpec((tm, SSM_XBC), lambda i: (i, 0)),
                   pl.BlockSpec((tm, LANES), lambda i: (i, 0))),
        scratch_shapes=[pltpu.VMEM((tm + 16, SSM_XBC), F32)],
        compiler_params=_params("parallel"),
        name="ssm_prep",
    )(u, u, u, u, conv_w, conv_b, dt_bias)


def _ssd_kernel(x_ref, dt_ref, alog_ref, o_ref, h_ref, *, direction):
    lc = SSM_CHUNK

    @pl.when(pl.program_id(0) == 0)
    def _():
        h_ref[...] = jnp.zeros(h_ref.shape, F32)

    xbc = x_ref[...]
    dt = dt_ref[...]
    dta = dt * (-jnp.exp(alog_ref[...]))
    r = lax.broadcasted_iota(jnp.int32, (lc, lc), 0)
    c = lax.broadcasted_iota(jnp.int32, (lc, lc), 1)
    keep = (r >= c) if direction == 0 else (r <= c)
    tri = jnp.where(keep, 1.0, 0.0).astype(F32)
    cum = jnp.dot(tri, dta, preferred_element_type=F32, precision=lax.Precision.HIGHEST)
    cum_t = cum.T
    total = jnp.sum(dta, axis=0, keepdims=True)
    to_end = jnp.exp(total - cum)
    from_start = jnp.exp(cum)
    chunk_decay = jnp.exp(total)
    x_t = xbc[:, :SSM_WIDTH].T
    w_t = (dt * to_end).T
    outs = []
    for g in range(SSM_GROUPS):
        b_g = xbc[:, SSM_WIDTH + SSM_STATE * g:SSM_WIDTH + SSM_STATE * (g + 1)].astype(BF16)
        c_g = xbc[:, SSM_WIDTH + SSM_STATE * (SSM_GROUPS + g):
                  SSM_WIDTH + SSM_STATE * (SSM_GROUPS + g + 1)].astype(BF16)
        cb = _dot_nt(c_g, b_g)
        for hh in range(SSM_HEADS // SSM_GROUPS):
            h = g * (SSM_HEADS // SSM_GROUPS) + hh
            col = direction * SSM_HEADS + h
            seg = jnp.exp(jnp.where(keep, cum[:, col:col + 1] - cum_t[col:col + 1, :], -jnp.inf))
            xd = xbc[:, SSM_HEAD_DIM * h:SSM_HEAD_DIM * (h + 1)] * dt[:, col:col + 1]
            y = _dot((cb * seg).astype(BF16), xd.astype(BF16))
            state = h_ref[h]
            y_off = _dot_nt(c_g, state.astype(BF16))
            outs.append(y + y_off * from_start[:, col:col + 1])
            xw_t = x_t[SSM_HEAD_DIM * h:SSM_HEAD_DIM * (h + 1), :] * w_t[col:col + 1, :]
            upd = _dot(xw_t.astype(BF16), b_g)
            h_ref[h] = state * chunk_decay[:, col:col + 1] + upd
    o_ref[...] = jnp.concatenate(outs, axis=1)


def _ssd(xbc, dt, a_log, n_ctx, direction):
    t = xbc.shape[0]
    lc = SSM_CHUNK
    nc = t // lc
    ncc = n_ctx // lc
    if direction == 0:
        order = lambda s: s
    else:
        order = lambda s: jnp.where(s < ncc, ncc - 1 - s, nc - 1 - (s - ncc))
    return pl.pallas_call(
        functools.partial(_ssd_kernel, direction=direction),
        out_shape=jax.ShapeDtypeStruct((t, SSM_WIDTH), F32),
        grid=(nc,),
        in_specs=[pl.BlockSpec((lc, SSM_XBC), lambda s: (order(s), 0)),
                  pl.BlockSpec((lc, LANES), lambda s: (order(s), 0)),
                  pl.BlockSpec((1, LANES), lambda s: (0, 0))],
        out_specs=pl.BlockSpec((lc, SSM_WIDTH), lambda s: (order(s), 0)),
        scratch_shapes=[pltpu.VMEM((SSM_HEADS, SSM_HEAD_DIM, SSM_STATE), F32)],
        compiler_params=_params("arbitrary"),
        name="ssd_fwd" if direction == 0 else "ssd_bwd",
    )(xbc, dt, a_log)


def _merge_kernel(x_ref, gate_ref, ya_ref, yb_ref, yf_ref, yr_ref, xs_ref, z_ref,
                  bg_ref, dskip_ref, gssm_ref, wa_ref, wb_ref, wc_ref, wo_ref,
                  ml_ref, mc_ref, g2_ref, wr_ref, br_ref,
                  xo_ref, f_ref, lg_ref, *, n_ctx, tm):
    y = (yf_ref[...] + yr_ref[...] + dskip_ref[...] * xs_ref[...]) * _silu(z_ref[...])
    gw = SSM_WIDTH // SSM_GROUPS
    yc = jnp.concatenate([_rms_rows(y[:, gw * g:gw * (g + 1)], gw) for g in range(SSM_GROUPS)], axis=1)
    yc = yc * gssm_ref[...]
    gate = _sigmoid(gate_ref[...] + bg_ref[...])
    m = (gate[:, :D_MODEL] * _dot(ya_ref[...].astype(BF16), wa_ref[...])
         + gate[:, D_MODEL:2 * D_MODEL] * _dot(yb_ref[...].astype(BF16), wb_ref[...])
         + gate[:, 2 * D_MODEL:] * _dot(yc.astype(BF16), wc_ref[...]))
    out = _dot(m.astype(BF16), wo_ref[...])
    row0 = pl.program_id(0) * tm
    is_ctx = (row0 + lax.broadcasted_iota(jnp.int32, (tm, 1), 0)) < n_ctx
    gt1 = jnp.where(is_ctx, mc_ref[2:3, :], ml_ref[2:3, :])
    x_new = x_ref[...] + gt1 * out
    xo_ref[...] = x_new
    f = _modulated_norm(x_new, g2_ref[...], ml_ref[3:5, :], mc_ref[3:5, :], row0, n_ctx)
    f_ref[...] = f.astype(BF16)
    lg_ref[...] = jnp.dot(f, wr_ref[...], preferred_element_type=F32,
                          precision=lax.Precision.HIGHEST) + br_ref[...]


def _merge(x_all, u, ya, yb, yf, yr, xbc, b_gate, dskip, g_ssm, wa, wb, wc, wo, mod_lat, mod_ctx,
           g2, w_router, b_router, n_ctx):
    t, d = x_all.shape
    tm = ROW_TILE
    const = lambda i: (0, 0)
    rows = lambda i: (i, 0)
    full = lambda a: pl.BlockSpec(a.shape, const)
    return pl.pallas_call(
        functools.partial(_merge_kernel, n_ctx=n_ctx, tm=tm),
        out_shape=(jax.ShapeDtypeStruct((t, d), F32), jax.ShapeDtypeStruct((t, d), BF16),
                   jax.ShapeDtypeStruct((t, LANES), F32)),
        grid=(t // tm,),
        in_specs=[pl.BlockSpec((tm, d), rows),
                  pl.BlockSpec((tm, GATE_COLS), lambda i: (i, U_GATE // GATE_COLS)),
                  pl.BlockSpec((tm, MLA_WIDTH), rows),
                  pl.BlockSpec((tm, DIF_WIDTH), rows),
                  pl.BlockSpec((tm, SSM_WIDTH), rows),
                  pl.BlockSpec((tm, SSM_WIDTH), rows),
                  pl.BlockSpec((tm, SSM_WIDTH), rows),
                  pl.BlockSpec((tm, SSM_WIDTH), lambda i: (i, U_Z // SSM_WIDTH)),
                  full(b_gate), full(dskip), full(g_ssm), full(wa), full(wb), full(wc), full(wo),
                  full(mod_lat), full(mod_ctx), full(g2), full(w_router), full(b_router)],
        out_specs=(pl.BlockSpec((tm, d), rows), pl.BlockSpec((tm, d), rows),
                   pl.BlockSpec((tm, LANES), rows)),
        compiler_params=_params("parallel"),
        name="merge",
    )(x_all, u, ya, yb, yf, yr, xbc, u, b_gate, dskip, g_ssm, wa, wb, wc, wo, mod_lat, mod_ctx,
      g2, w_router, b_router)


def _moe_kernel(be_ref, nb_ref, x_ref, wgu_ref, bgu_ref, wd_ref, bd_ref, o_ref, wgu_s, wd_s):
    b = pl.program_id(0)
    prev = be_ref[jnp.maximum(b - 1, 0)]
    fresh = jnp.logical_or(b == 0, be_ref[b] != prev)

    @pl.when(fresh)
    def _():
        wgu_s[...] = wgu_ref[...].astype(BF16)
        wd_s[...] = wd_ref[...].astype(BF16)

    @pl.when(b < nb_ref[0])
    def _():
        gu = _dot(x_ref[...], wgu_s[...]) + bgu_ref[...]
        glu = jnp.minimum(gu[:, :D_FF], SWIGLU_LIMIT)
        lin = jnp.clip(gu[:, D_FF:], -SWIGLU_LIMIT, SWIGLU_LIMIT)
        act = glu * _sigmoid(SWIGLU_ALPHA * glu) * (lin + 1.0)
        o_ref[...] = _dot(act.astype(BF16), wd_s[...]) + bd_ref[...]

    @pl.when(b >= nb_ref[0])
    def _():
        o_ref[...] = jnp.zeros(o_ref.shape, F32)


def _moe_experts(block_e, n_used, x_sorted, w_gu, b_gu, w_down, b_down, layer):
    n_slots, d = x_sorted.shape
    n_blocks = n_slots // MOE_BLOCK
    grid_spec = pltpu.PrefetchScalarGridSpec(
        num_scalar_prefetch=2,
        grid=(n_blocks,),
        in_specs=[pl.BlockSpec((MOE_BLOCK, d), lambda b, be, nb: (b, 0)),
                  pl.BlockSpec((None, None, d, 2 * D_FF), lambda b, be, nb: (layer, be[b], 0, 0)),
                  pl.BlockSpec((None, None, 1, 2 * D_FF), lambda b, be, nb: (layer, be[b], 0, 0)),
                  pl.BlockSpec((None, None, D_FF, d), lambda b, be, nb: (layer, be[b], 0, 0)),
                  pl.BlockSpec((None, None, 1, d), lambda b, be, nb: (layer, be[b], 0, 0))],
        out_specs=pl.BlockSpec((MOE_BLOCK, d), lambda b, be, nb: (b, 0)),
        scratch_shapes=[pltpu.VMEM((d, 2 * D_FF), BF16), pltpu.VMEM((D_FF, d), BF16)],
    )
    return pl.pallas_call(
        _moe_kernel,
        out_shape=jax.ShapeDtypeStruct((n_slots, d), F32),
        grid_spec=grid_spec,
        compiler_params=_params("arbitrary"),
        name="moe_experts",
    )(block_e, n_used, x_sorted, w_gu, b_gu, w_down, b_down)


def _combine_kernel(x_ref, g_ref, ml_ref, mc_ref, *rest, n_ctx, tm):
    y_refs, o_ref = rest[:TOP_K], rest[TOP_K]
    is_ctx = (pl.program_id(0) * tm + lax.broadcasted_iota(jnp.int32, (tm, 1), 0)) < n_ctx
    gate2 = jnp.where(is_ctx, mc_ref[...], ml_ref[...])
    g = g_ref[...]
    y = g[:, 0:1] * y_refs[0][...]
    for k in range(1, TOP_K):
        y = y + g[:, k:k + 1] * y_refs[k][...]
    o_ref[...] = x_ref[...] + gate2 * y


def _combine(x_all, y_rows, gates, gate2_lat, gate2_ctx, n_ctx):
    t, d = x_all.shape
    tm = ROW_TILE
    rows = lambda i: (i, 0)
    const = lambda i: (0, 0)
    return pl.pallas_call(
        functools.partial(_combine_kernel, n_ctx=n_ctx, tm=tm),
        out_shape=jax.ShapeDtypeStruct((t, d), F32),
        grid=(t // tm,),
        in_specs=[pl.BlockSpec((tm, d), rows), pl.BlockSpec((tm, LANES), rows),
                  pl.BlockSpec((1, d), const), pl.BlockSpec((1, d), const)]
                 + [pl.BlockSpec((tm, d), rows)] * TOP_K,
        out_specs=pl.BlockSpec((tm, d), rows),
        compiler_params=_params("parallel"),
        name="moe_combine",
    )(x_all, gates, gate2_lat, gate2_ctx, *y_rows)


def _route(logits, n_tok):
    top_val, top_idx = lax.top_k(logits, TOP_K)
    gates = jax.nn.softmax(top_val, axis=-1)
    n_assign = n_tok * TOP_K
    chosen = jnp.sum((top_idx[..., None] == jnp.arange(N_EXPERTS)).astype(jnp.int32), axis=1)
    before = jnp.cumsum(chosen, axis=0) - chosen
    counts = before[-1] + chosen[-1]
    padded = (counts + MOE_BLOCK - 1) // MOE_BLOCK * MOE_BLOCK
    pad_end = jnp.cumsum(padded)
    pad_start = pad_end - padded
    slot_of = jnp.take_along_axis(before + pad_start[None, :], top_idx, axis=1)
    n_blocks = -(-(n_assign + N_EXPERTS * (MOE_BLOCK - 1)) // MOE_BLOCK)
    n_slots = n_blocks * MOE_BLOCK
    token = jnp.arange(n_assign, dtype=jnp.int32) // TOP_K
    slot_tok = jnp.zeros((n_slots,), jnp.int32).at[slot_of.reshape(-1)].set(token, unique_indices=True)
    block_start = jnp.arange(n_blocks, dtype=jnp.int32) * MOE_BLOCK
    block_e = jnp.minimum(jnp.searchsorted(pad_end, block_start, side='right'), N_EXPERTS - 1)
    n_used = (pad_end[-1] // MOE_BLOCK).reshape(1)
    return gates, slot_tok, slot_of, block_e.astype(jnp.int32), n_used.astype(jnp.int32)


def kernel(x, c, ctx, c_ctx, w_mod, b_mod, g_norm1, g_norm2, w_in, b_gate, mla_g_q, mla_w_uq, mla_g_kv, mla_w_ukv, mla_g_qn, mla_g_kn, dif_g_qn, dif_g_kn, dif_lambda, dif_g_sub, ssm_conv_w, ssm_conv_b, ssm_dt_bias, ssm_a_log, ssm_d, ssm_g_norm, w_up_mla, w_up_dif, w_up_ssm, w_out, moe_w_router, moe_b_router, moe_w_gu, moe_b_gu, moe_w_down, moe_b_down):
    assert x.shape[0] == 1 and ctx.shape[0] == 1
    depth = w_in.shape[0]
    seq = x.shape[1]
    n_ctx = ctx.shape[1]
    n_tok = n_ctx + seq
    d = D_MODEL
    assert n_ctx == KEY_CHUNK and n_tok % ROW_TILE == 0 and seq % GRID_W == 0

    x_all = jnp.concatenate([ctx[0], x[0]], axis=0)
    cc = jnp.zeros((8, d), F32).at[0].set(c[0]).at[1].set(c_ctx)
    mod = _mod_vectors(cc, w_mod, b_mod)
    mod = mod[:, :2].reshape(depth, 2, 6, d)

    rope_mla = _rope_tables(seq, n_ctx, MLA_ROPE, MLA_NOPE)
    rope_dif = _rope_tables(seq, n_ctx, DIF_HEAD_DIM, 0)
    in_cols = _in_proj_columns()
    uq_cols = _head_columns(MLA_HEADS, MLA_QK, 0, MLA_QK, LANES)
    uk_cols = _head_columns(MLA_HEADS, MLA_NOPE + MLA_V, 0, MLA_NOPE, LANES)
    uv_cols = _head_columns(MLA_HEADS, MLA_NOPE + MLA_V, MLA_NOPE, MLA_V, MLA_V)

    for i in range(depth):
        lam_init = 0.8 - 0.6 * math.exp(-0.3 * i)
        mod_lat, mod_ctx = mod[i, 0], mod[i, 1]
        w_in_i = _take_columns(w_in[i], in_cols).astype(BF16)
        u = _in_proj(x_all, g_norm1[i][None], mod_lat[0:2], mod_ctx[0:2], w_in_i, n_ctx)

        qt, k, vt = _mla_prep(
            u, mla_g_q[i][None], _take_columns(mla_w_uq[i], uq_cols).astype(BF16),
            mla_g_kv[i][None], _take_columns(mla_w_ukv[i], uk_cols).astype(BF16),
            _take_columns(mla_w_ukv[i], uv_cols).astype(BF16),
            _pad_lanes(mla_g_qn[i][None]), _pad_lanes(mla_g_kn[i][None]), rope_mla)
        ya = _attention(qt, k, vt)

        qt, k, vt = _dif_prep(u, _pad_lanes(dif_g_qn[i][None]), _pad_lanes(dif_g_kn[i][None]), rope_dif)
        lam_rows = jnp.zeros((8, LANES), F32).at[:4, :DIF_HEAD_DIM].set(dif_lambda[i])
        yb = _attention(qt, k, vt, lam=lam_rows, gsub=dif_g_sub[i][None], lam_init=lam_init)

        conv_w = jnp.zeros((8, SSM_XBC), F32).at[:SSM_CONV].set(ssm_conv_w[i])
        xbc, dt = _ssm_prep(u, conv_w, ssm_conv_b[i][None], _pad_lanes(ssm_dt_bias[i].reshape(1, -1)), n_ctx)
        a_log = _pad_lanes(ssm_a_log[i].reshape(1, -1))
        yf = _ssd(xbc, dt, a_log, n_ctx, 0)
        yr = _ssd(xbc, dt, a_log, n_ctx, 1)

        dskip = jnp.repeat(ssm_d[i, 0] + ssm_d[i, 1], SSM_HEAD_DIM)[None]
        w_router = jnp.zeros((d, LANES), F32).at[:, :N_EXPERTS].set(moe_w_router[i])
        b_router = jnp.zeros((1, LANES), F32).at[0, :N_EXPERTS].set(moe_b_router[i])
        x_all, f, logits = _merge(
            x_all, u, ya, yb, yf, yr, xbc, b_gate[i][None], dskip, ssm_g_norm[i][None],
            w_up_mla[i].astype(BF16), w_up_dif[i].astype(BF16), w_up_ssm[i].astype(BF16),
            w_out[i].astype(BF16), mod_lat[0:5], mod_ctx[0:5], g_norm2[i][None],
            w_router, b_router, n_ctx)

        gates, slot_tok, slot_of, block_e, n_used = _route(logits[:, :N_EXPERTS], n_tok)
        y_slots = _moe_experts(block_e, n_used, f[slot_tok], moe_w_gu,
                               moe_b_gu.reshape(depth, N_EXPERTS, 1, 2 * D_FF), moe_w_down,
                               moe_b_down.reshape(depth, N_EXPERTS, 1, d), i)
        y_rows = [y_slots[slot_of[:, k]] for k in range(TOP_K)]
        x_all = _combine(x_all, y_rows, _pad_lanes(gates), mod_lat[5:6], mod_ctx[5:6], n_ctx)
    return x_all[n_ctx:][None]
```

```python
import functools
import math

import numpy as np
import jax
import jax.numpy as jnp
from jax import lax
from jax.experimental import pallas as pl
from jax.experimental.pallas import tpu as pltpu

F32 = jnp.float32
BF16 = jnp.bfloat16
LANES = 128
VMEM_LIMIT = 52 * 1024 * 1024

D_MODEL = 1024
EPS = 1e-6
ROPE_THETA = 10000.0
GRID_W = 64
N_BRANCH = 3

MLA_HEADS = 8
MLA_Q_RANK = 256
MLA_KV_RANK = 128
MLA_NOPE = 64
MLA_ROPE = 32
MLA_V = 64
MLA_QK = MLA_NOPE + MLA_ROPE
MLA_WIDTH = MLA_HEADS * MLA_V

DIF_HEADS = 4
DIF_HEAD_DIM = 64
DIF_WIDTH = DIF_HEADS * 2 * DIF_HEAD_DIM

SSM_HEADS = 8
SSM_HEAD_DIM = 64
SSM_WIDTH = SSM_HEADS * SSM_HEAD_DIM
SSM_GROUPS = 2
SSM_STATE = 128
SSM_CONV = 5
SSM_CHUNK = 128
SSM_XBC = SSM_WIDTH + 2 * SSM_GROUPS * SSM_STATE

N_EXPERTS = 32
TOP_K = 4
D_FF = 1024
SWIGLU_LIMIT = 7.0
SWIGLU_ALPHA = 1.702
MOE_BLOCK = 256

MLA_COLS = MLA_Q_RANK + MLA_KV_RANK + MLA_ROPE
DIF_COLS = 3 * DIF_WIDTH
SSM_COLS = SSM_WIDTH + SSM_XBC + 2 * SSM_HEADS
GATE_COLS = N_BRANCH * D_MODEL

ROW_TILE = 256
KEY_CHUNK = 256
LOG2E = 1.4426950408889634

U_GATE, U_DQ, U_DK, U_XBC, U_MLA, U_DV, U_Z, U_DT = 0, 3072, 4096, 5120, 6144, 6656, 7168, 7680
U_COLS = 8192
U_TILE_N = 1024


def _in_proj_columns():
    src = np.full((U_COLS,), -1, np.int64)
    dif0 = MLA_COLS
    ssm0 = MLA_COLS + DIF_COLS
    gate0 = ssm0 + SSM_COLS
    src[U_GATE:U_GATE + GATE_COLS] = gate0 + np.arange(GATE_COLS)
    for a in range(2 * DIF_HEADS):
        src[U_DQ + LANES * a:U_DQ + LANES * a + DIF_HEAD_DIM] = dif0 + DIF_HEAD_DIM * a + np.arange(DIF_HEAD_DIM)
        src[U_DK + LANES * a:U_DK + LANES * a + DIF_HEAD_DIM] = (dif0 + DIF_WIDTH + DIF_HEAD_DIM * a
                                                                  + np.arange(DIF_HEAD_DIM))
    src[U_DV:U_DV + DIF_WIDTH] = dif0 + 2 * DIF_WIDTH + np.arange(DIF_WIDTH)
    src[U_MLA:U_MLA + MLA_Q_RANK + MLA_KV_RANK] = np.arange(MLA_Q_RANK + MLA_KV_RANK)
    pe0 = U_MLA + MLA_Q_RANK + MLA_KV_RANK + MLA_NOPE
    src[pe0:pe0 + MLA_ROPE] = MLA_Q_RANK + MLA_KV_RANK + np.arange(MLA_ROPE)
    src[U_Z:U_Z + SSM_WIDTH] = ssm0 + np.arange(SSM_WIDTH)
    src[U_XBC:U_XBC + SSM_XBC] = ssm0 + SSM_WIDTH + np.arange(SSM_XBC)
    src[U_DT:U_DT + 2 * SSM_HEADS] = ssm0 + SSM_WIDTH + SSM_XBC + np.arange(2 * SSM_HEADS)
    return src


def _take_columns(w, src):
    cols = jnp.take(w, jnp.asarray(np.maximum(src, 0)), axis=-1)
    return jnp.where(jnp.asarray(src >= 0), cols, 0.0)


def _head_columns(n_heads, src_stride, src_off, width, dst_stride):
    src = np.full((n_heads * dst_stride,), -1, np.int64)
    for h in range(n_heads):
        src[h * dst_stride:h * dst_stride + width] = h * src_stride + src_off + np.arange(width)
    return src


def _pad_lanes(v, n=LANES):
    return jnp.pad(v, [(0, 0)] * (v.ndim - 1) + [(0, n - v.shape[-1])])


def _row_tile(n, cap):
    best = 8
    for t in range(8, cap + 1, 8):
        if n % t == 0:
            best = t
    return best


def _dot(a, b):
    return jnp.dot(a, b, preferred_element_type=F32)


def _dot_nt(a, b):
    return lax.dot_general(a, b, (((1,), (1,)), ((), ())), preferred_element_type=F32)


def _sigmoid(x):
    return 1.0 / (1.0 + jnp.exp(-x))


def _silu(x):
    return x * _sigmoid(x)


def _rms_rows(x, n):
    return x * lax.rsqrt(jnp.sum(x * x, axis=-1, keepdims=True) * (1.0 / n) + EPS)


def _params(*sem):
    return pltpu.CompilerParams(dimension_semantics=sem, vmem_limit_bytes=VMEM_LIMIT)


def _mod_kernel(a_ref, w_ref, b_ref, o_ref):
    a = _silu(a_ref[...]).astype(BF16)
    o_ref[0] = _dot(a, w_ref[0].astype(BF16)) + b_ref[0]


def _mod_vectors(cc, w_mod, b_mod):
    depth, d, n = w_mod.shape
    tn = 1536
    return pl.pallas_call(
        _mod_kernel,
        out_shape=jax.ShapeDtypeStruct((depth, 8, n), F32),
        grid=(depth, n // tn),
        in_specs=[pl.BlockSpec((8, d), lambda l, j: (0, 0)),
                  pl.BlockSpec((1, d, tn), lambda l, j: (l, 0, j)),
                  pl.BlockSpec((1, 1, tn), lambda l, j: (l, 0, j))],
        out_specs=pl.BlockSpec((1, 8, tn), lambda l, j: (l, 0, j)),
        compiler_params=_params("parallel", "parallel"),
        name="mod_vectors",
    )(cc, w_mod, b_mod.reshape(depth, 1, n))


def _modulated_norm(x, g, mod_lat, mod_ctx, row0, n_ctx):
    rows = x.shape[0]
    is_ctx = (row0 + lax.broadcasted_iota(jnp.int32, (rows, 1), 0)) < n_ctx
    shift = jnp.where(is_ctx, mod_ctx[0:1, :], mod_lat[0:1, :])
    scale = jnp.where(is_ctx, mod_ctx[1:2, :], mod_lat[1:2, :])
    return _rms_rows(x, x.shape[1]) * g * (1.0 + scale) + shift


def _in_proj_kernel(x_ref, g_ref, ml_ref, mc_ref, w_ref, o_ref, h_ref, *, n_ctx, tm):
    @pl.when(pl.program_id(1) == 0)
    def _():
        h = _modulated_norm(x_ref[...], g_ref[...], ml_ref[...], mc_ref[...], pl.program_id(0) * tm, n_ctx)
        h_ref[...] = h.astype(BF16)

    o_ref[...] = _dot(h_ref[...], w_ref[...])


def _in_proj(x_all, g, mod_lat, mod_ctx, w, n_ctx):
    t, d = x_all.shape
    n = w.shape[1]
    tm = _row_tile(t, 1280)
    return pl.pallas_call(
        functools.partial(_in_proj_kernel, n_ctx=n_ctx, tm=tm),
        out_shape=jax.ShapeDtypeStruct((t, n), F32),
        grid=(t // tm, n // U_TILE_N),
        in_specs=[pl.BlockSpec((tm, d), lambda i, j: (i, 0)),
                  pl.BlockSpec((1, d), lambda i, j: (0, 0)),
                  pl.BlockSpec((2, d), lambda i, j: (0, 0)),
                  pl.BlockSpec((2, d), lambda i, j: (0, 0)),
                  pl.BlockSpec((d, U_TILE_N), lambda i, j: (0, j))],
        out_specs=pl.BlockSpec((tm, U_TILE_N), lambda i, j: (i, j)),
        scratch_shapes=[pltpu.VMEM((tm, d), BF16)],
        compiler_params=_params("parallel", "arbitrary"),
        name="in_proj",
    )(x_all, g, mod_lat, mod_ctx, w)


def _rope_tables(seq_len, n_ctx, rot_dim, lane0):
    n_rows = seq_len // GRID_W
    row = jnp.repeat(jnp.arange(n_rows), GRID_W).astype(F32)
    col = jnp.tile(jnp.arange(GRID_W), n_rows).astype(F32)
    axis_dim = rot_dim // 2
    half = axis_dim // 2
    inv = ROPE_THETA ** (-jnp.arange(0, axis_dim, 2, dtype=F32) / axis_dim)
    ang_r = row[:, None] * inv
    ang_c = col[:, None] * inv
    zeros = jnp.zeros((seq_len, half), F32)
    cos = jnp.concatenate([jnp.cos(ang_r), jnp.cos(ang_r), jnp.cos(ang_c), jnp.cos(ang_c)], axis=1)
    s1 = jnp.concatenate([zeros, jnp.sin(ang_r), zeros, jnp.sin(ang_c)], axis=1)
    s2 = jnp.concatenate([-jnp.sin(ang_r), zeros, -jnp.sin(ang_c), zeros], axis=1)

    def place(tab, fill):
        full = jnp.full((seq_len, LANES), fill, F32).at[:, lane0:lane0 + rot_dim].set(tab)
        ctx = jnp.full((n_ctx, LANES), fill, F32)
        return jnp.concatenate([ctx, full], axis=0)

    return place(cos, 1.0), place(s1, 0.0), place(s2, 0.0)


def _rope(x, cos, s1, s2, half):
    return x * cos + pltpu.roll(x, half, 1) * s1 + pltpu.roll(x, LANES - half, 1) * s2


def _mla_prep_kernel(u_ref, gq_ref, wuq_ref, gkv_ref, wk_ref, wv_ref, gqn_ref, gkn_ref,
                     cos_ref, s1_ref, s2_ref, qt_ref, k_ref, vt_ref):
    u = u_ref[...]
    cq = u[:, :MLA_Q_RANK]
    ckv = u[:, MLA_Q_RANK:MLA_Q_RANK + MLA_KV_RANK]
    pe = u[:, MLA_Q_RANK + MLA_KV_RANK:]
    q = _dot((_rms_rows(cq, MLA_Q_RANK) * gq_ref[...]).astype(BF16), wuq_ref[...])
    kv_in = (_rms_rows(ckv, MLA_KV_RANK) * gkv_ref[...]).astype(BF16)
    kn = _dot(kv_in, wk_ref[...])
    v = _dot(kv_in, wv_ref[...])
    cos, s1, s2 = cos_ref[...], s1_ref[...], s2_ref[...]
    half = MLA_ROPE // 4
    q_scale = MLA_QK ** -0.5 * LOG2E
    for h in range(MLA_HEADS):
        qh = _rms_rows(q[:, LANES * h:LANES * (h + 1)], MLA_QK) * gqn_ref[...]
        qh = _rope(qh, cos, s1, s2, half) * q_scale
        qt_ref[h] = qh.T.astype(BF16)
        kh = _rms_rows(kn[:, LANES * h:LANES * (h + 1)] + pe, MLA_QK) * gkn_ref[...]
        k_ref[h, 0] = _rope(kh, cos, s1, s2, half).astype(BF16)
    tm = v.shape[0]
    vt_ref[:, 0] = _with_sum_rows(v.T.reshape(MLA_HEADS, MLA_V, tm)).astype(BF16)


def _mla_prep(u, gq, wuq, gkv, wk, wv, gqn, gkn, tabs):
    t = u.shape[0]
    tm = ROW_TILE
    nt = t // tm
    const = lambda i: (0, 0)
    rows = lambda i: (i, 0)
    return pl.pallas_call(
        _mla_prep_kernel,
        out_shape=(jax.ShapeDtypeStruct((MLA_HEADS, LANES, t), BF16),
                   jax.ShapeDtypeStruct((MLA_HEADS, nt, tm, LANES), BF16),
                   jax.ShapeDtypeStruct((MLA_HEADS, nt, MLA_V + SUM_ROWS, tm), BF16)),
        grid=(nt,),
        in_specs=[pl.BlockSpec((tm, 512), lambda i: (i, U_MLA // 512)),
                  pl.BlockSpec((1, MLA_Q_RANK), const),
                  pl.BlockSpec(wuq.shape, const),
                  pl.BlockSpec((1, MLA_KV_RANK), const),
                  pl.BlockSpec(wk.shape, const),
                  pl.BlockSpec(wv.shape, const),
                  pl.BlockSpec((1, LANES), const),
                  pl.BlockSpec((1, LANES), const),
                  pl.BlockSpec((tm, LANES), rows),
                  pl.BlockSpec((tm, LANES), rows),
                  pl.BlockSpec((tm, LANES), rows)],
        out_specs=(pl.BlockSpec((MLA_HEADS, LANES, tm), lambda i: (0, 0, i)),
                   pl.BlockSpec((MLA_HEADS, 1, tm, LANES), lambda i: (0, i, 0, 0)),
                   pl.BlockSpec((MLA_HEADS, 1, MLA_V + SUM_ROWS, tm), lambda i: (0, i, 0, 0))),
        compiler_params=_params("parallel"),
        name="mla_prep",
    )(u, gq, wuq, gkv, wk, wv, gqn, gkn, *tabs)


def _dif_prep_kernel(q_ref, k_ref, v_ref, gq_ref, gk_ref, cos_ref, s1_ref, s2_ref,
                     qt_out, k_out, vt_out):
    cos, s1, s2 = cos_ref[...], s1_ref[...], s2_ref[...]
    half = DIF_HEAD_DIM // 4
    q_scale = DIF_HEAD_DIM ** -0.5 * LOG2E
    for a in range(2 * DIF_HEADS):
        qa = _rms_rows(q_ref[:, LANES * a:LANES * (a + 1)], DIF_HEAD_DIM) * gq_ref[...]
        qt_out[a] = (_rope(qa, cos, s1, s2, half) * q_scale).T.astype(BF16)
        ka = _rms_rows(k_ref[:, LANES * a:LANES * (a + 1)], DIF_HEAD_DIM) * gk_ref[...]
        k_out[a, 0] = _rope(ka, cos, s1, s2, half).astype(BF16)
    v = v_ref[...]
    vt_out[:, 0] = _with_sum_rows(v.T.reshape(DIF_HEADS, 2 * DIF_HEAD_DIM, v.shape[0])).astype(BF16)


def _dif_prep(u, gq, gk, tabs):
    t = u.shape[0]
    tm = ROW_TILE
    nt = t // tm
    nsub = 2 * DIF_HEADS
    const = lambda i: (0, 0)
    rows = lambda i: (i, 0)
    return pl.pallas_call(
        _dif_prep_kernel,
        out_shape=(jax.ShapeDtypeStruct((nsub, LANES, t), BF16),
                   jax.ShapeDtypeStruct((nsub, nt, tm, LANES), BF16),
                   jax.ShapeDtypeStruct((DIF_HEADS, nt, 2 * DIF_HEAD_DIM + SUM_ROWS, tm), BF16)),
        grid=(nt,),
        in_specs=[pl.BlockSpec((tm, 1024), lambda i: (i, U_DQ // 1024)),
                  pl.BlockSpec((tm, 1024), lambda i: (i, U_DK // 1024)),
                  pl.BlockSpec((tm, 512), lambda i: (i, U_DV // 512)),
                  pl.BlockSpec((1, LANES), const),
                  pl.BlockSpec((1, LANES), const),
                  pl.BlockSpec((tm, LANES), rows),
                  pl.BlockSpec((tm, LANES), rows),
                  pl.BlockSpec((tm, LANES), rows)],
        out_specs=(pl.BlockSpec((nsub, LANES, tm), lambda i: (0, 0, i)),
                   pl.BlockSpec((nsub, 1, tm, LANES), lambda i: (0, i, 0, 0)),
                   pl.BlockSpec((DIF_HEADS, 1, 2 * DIF_HEAD_DIM + SUM_ROWS, tm), lambda i: (0, i, 0, 0))),
        compiler_params=_params("parallel"),
        name="dif_prep",
    )(u, u, u, gq, gk, *tabs)


ATTN_GROUP = 4
SUM_ROWS = 16


def _with_sum_rows(vt):
    heads, _, keys = vt.shape
    row = lax.broadcasted_iota(jnp.int32, (heads, SUM_ROWS, keys), 1)
    return jnp.concatenate([vt, jnp.where(row == 0, 1.0, 0.0).astype(vt.dtype)], axis=1)


def _attn_group(qt_ref, k_ref, vt_ref, v_of_sub, s_buf, p_buf, acc_ref, n_chunks):
    tq = qt_ref.shape[2]
    last = n_chunks - 1

    def scores(c):
        out = []
        for a in range(ATTN_GROUP):
            s = _dot(k_ref[a, c], qt_ref[a])
            out.append((s, jnp.max(s, axis=0, keepdims=True)))
        return out

    def stash(sc, slot):
        for a in range(ATTN_GROUP):
            s_buf[slot, a] = sc[a][0]
        return tuple(mx for (_, mx) in sc)

    def softmax(s_of, mx, slot, ms):
        new_ms, alphas = [], []
        for a in range(ATTN_GROUP):
            m_new = jnp.maximum(ms[a], mx[a])
            alphas.append(jnp.exp2(ms[a] - m_new))
            p_buf[slot, a] = jnp.exp2(s_of(a) - m_new).astype(BF16)
            new_ms.append(m_new)
        return tuple(new_ms), tuple(alphas)

    def values(c, slot, alphas):
        for a in range(ATTN_GROUP):
            acc_ref[a] = alphas[a] * acc_ref[a] + _dot(vt_ref[v_of_sub[a], c], p_buf[slot, a])

    def four_chunks(t, state):
        ms, alphas, mx0 = state
        c = 4 * t + 1
        for half in range(2):
            cur, nxt = half, 1 - half
            even = scores(c + 1)
            values(c - 1, 0, alphas)
            ms, alphas = softmax(lambda a: s_buf[cur, a], mx0, 1, ms)
            mx0 = stash(scores(jnp.minimum(c + 2, last)), nxt)
            values(c, 1, alphas)
            ms, alphas = softmax(lambda a: even[a][0], [mx for (_, mx) in even], 0, ms)
            c = c + 2
        return ms, alphas, mx0

    acc_ref[...] = jnp.zeros(acc_ref.shape, F32)
    ms = tuple(jnp.full((1, tq), -jnp.inf, F32) for _ in range(ATTN_GROUP))
    first = scores(0)
    ms, alphas = softmax(lambda a: first[a][0], [mx for (_, mx) in first], 0, ms)
    mx0 = stash(scores(1), 0)
    n_trips = jnp.where(pl.program_id(1) > 0, last // 4, 0)
    ms, alphas, _ = lax.fori_loop(0, n_trips, four_chunks, (ms, alphas, mx0))
    values(4 * n_trips, 0, alphas)


def _normalised(acc_ref, a):
    dv = acc_ref.shape[1] - SUM_ROWS
    return acc_ref[a, :dv] * (1.0 / acc_ref[a, dv:dv + 1])


def _mla_attn_kernel(qt_ref, k_ref, vt_ref, o_ref, s_buf, p_buf, acc_ref, *, n_chunks):
    _attn_group(qt_ref, k_ref, vt_ref, tuple(range(ATTN_GROUP)), s_buf, p_buf, acc_ref, n_chunks)
    o = jnp.concatenate([_normalised(acc_ref, a) for a in range(ATTN_GROUP)], axis=0)
    o_ref[...] = o.T


def _dif_attn_kernel(lam_ref, gsub_ref, qt_ref, k_ref, vt_ref, o_ref, s_buf, p_buf, acc_ref,
                     *, n_chunks, lam_init):
    _attn_group(qt_ref, k_ref, vt_ref, tuple(a // 2 for a in range(ATTN_GROUP)), s_buf, p_buf,
                acc_ref, n_chunks)
    lp = lam_ref[...]
    lam = (jnp.exp(jnp.sum(lp[0:1] * lp[1:2], axis=-1, keepdims=True))
           - jnp.exp(jnp.sum(lp[2:3] * lp[3:4], axis=-1, keepdims=True)) + lam_init)
    outs = []
    for h in range(ATTN_GROUP // 2):
        o = _normalised(acc_ref, 2 * h) - lam * _normalised(acc_ref, 2 * h + 1)
        outs.append(o * lax.rsqrt(jnp.mean(o * o, axis=0, keepdims=True) + EPS))
    gsub = gsub_ref[...] * (1.0 - lam_init)
    o_ref[...] = jnp.concatenate(outs, axis=0).T * jnp.concatenate([gsub] * len(outs), axis=1)


def _attention(qt, k, vt, *, lam=None, gsub=None, lam_init=None):
    nsub, _, t = qt.shape
    n_chunks = k.shape[1]
    assert (n_chunks - 1) % 4 == 0 and nsub % ATTN_GROUP == 0
    tq = KEY_CHUNK
    dv = vt.shape[2] - SUM_ROWS
    groups = nsub // ATTN_GROUP
    n_v = vt.shape[0] // groups
    out_w = n_v * dv
    resident = dict(pipeline_mode=pl.Buffered(1))
    specs = [pl.BlockSpec((ATTN_GROUP, LANES, tq), lambda g, i: (g, 0, i)),
             pl.BlockSpec((ATTN_GROUP, n_chunks, KEY_CHUNK, LANES), lambda g, i: (g, 0, 0, 0), **resident),
             pl.BlockSpec((n_v, n_chunks, dv + SUM_ROWS, KEY_CHUNK), lambda g, i: (g, 0, 0, 0), **resident)]
    if lam is None:
        body = functools.partial(_mla_attn_kernel, n_chunks=n_chunks)
        args = (qt, k, vt)
    else:
        body = functools.partial(_dif_attn_kernel, n_chunks=n_chunks, lam_init=lam_init)
        specs = [pl.BlockSpec((8, LANES), lambda g, i: (0, 0)),
                 pl.BlockSpec((1, LANES), lambda g, i: (0, 0))] + specs
        args = (lam, gsub, qt, k, vt)
    return pl.pallas_call(
        body,
        out_shape=jax.ShapeDtypeStruct((t, groups * out_w), F32),
        grid=(groups, t // tq),
        in_specs=specs,
        out_specs=pl.BlockSpec((tq, out_w), lambda g, i: (i, g)),
        scratch_shapes=[pltpu.VMEM((2, ATTN_GROUP, KEY_CHUNK, tq), F32),
                        pltpu.VMEM((2, ATTN_GROUP, KEY_CHUNK, tq), BF16),
                        pltpu.VMEM((ATTN_GROUP, dv + SUM_ROWS, tq), F32)],
        compiler_params=_params("parallel", "arbitrary"),
        name="mla_attention" if lam is None else "dif_attention",
    )(*args)


def _ssm_prep_kernel(x_ref, prev_ref, next_ref, dt_ref, w_ref, b_ref, dtb_ref, xo_ref, dto_ref, e_ref,
                     *, n_ctx, n_tok, tm):
    row0 = pl.program_id(0) * tm
    pad = SSM_CONV // 2
    has_prev = jnp.logical_and(row0 != 0, row0 != n_ctx)
    has_next = jnp.logical_and(row0 + tm != n_ctx, row0 + tm != n_tok)
    e_ref[0:8] = jnp.where(has_prev, prev_ref[...], 0.0)
    e_ref[8:8 + tm] = x_ref[...]
    e_ref[8 + tm:16 + tm] = jnp.where(has_next, next_ref[...], 0.0)
    acc = jnp.zeros(x_ref.shape, F32) + b_ref[...]
    for k in range(SSM_CONV):
        acc = acc + w_ref[k:k + 1, :] * e_ref[pl.ds(8 - pad + k, tm), :]
    xo_ref[...] = _silu(acc)
    d = dt_ref[...] + dtb_ref[...]
    dto_ref[...] = jnp.maximum(d, 0.0) + jnp.log1p(jnp.exp(-jnp.abs(d)))


def _ssm_prep(u, conv_w, conv_b, dt_bias, n_ctx):
    t = u.shape[0]
    tm = ROW_TILE
    nt = t // tm
    cb = U_XBC // SSM_XBC
    const = lambda i: (0, 0)
    return pl.pallas_call(
        functools.partial(_ssm_prep_kernel, n_ctx=n_ctx, n_tok=t, tm=tm),
        out_shape=(jax.ShapeDtypeStruct((t, SSM_XBC), F32), jax.ShapeDtypeStruct((t, LANES), F32)),
        grid=(nt,),
        in_specs=[pl.BlockSpec((tm, SSM_XBC), lambda i: (i, cb)),
                  pl.BlockSpec((8, SSM_XBC), lambda i: (jnp.maximum(i * (tm // 8) - 1, 0), cb)),
                  pl.BlockSpec((8, SSM_XBC), lambda i: (jnp.minimum((i + 1) * (tm // 8), t // 8 - 1), cb)),
                  pl.BlockSpec((tm, LANES), lambda i: (i, U_DT // LANES)),
                  pl.BlockSpec((8, SSM_XBC), const),
                  pl.BlockSpec((1, SSM_XBC), const),
                  pl.BlockSpec((1, LANES), const)],
        out_specs=(pl.BlockSpec((tm, SSM_XBC), lambda i: (i, 0)),
                   pl.BlockSpec((tm, LANES), lambda i: (i, 0))),
        scratch_shapes=[pltpu.VMEM((tm + 16, SSM_XBC), F32)],
        compiler_params=_params("parallel"),
        name="ssm_prep",
    )(u, u, u, u, conv_w, conv_b, dt_bias)


def _ssd_kernel(x_ref, dt_ref, alog_ref, o_ref, h_ref, *, direction):
    lc = SSM_CHUNK

    @pl.when(pl.program_id(0) == 0)
    def _():
        h_ref[...] = jnp.zeros(h_ref.shape, F32)

    xbc = x_ref[...]
    dt = dt_ref[...]
    dta = dt * (-jnp.exp(alog_ref[...]))
    r = lax.broadcasted_iota(jnp.int32, (lc, lc), 0)
    c = lax.broadcasted_iota(jnp.int32, (lc, lc), 1)
    keep = (r >= c) if direction == 0 else (r <= c)
    tri = jnp.where(keep, 1.0, 0.0).astype(F32)
    cum = jnp.dot(tri, dta, preferred_element_type=F32, precision=lax.Precision.HIGHEST)
    cum_t = cum.T
    total = jnp.sum(dta, axis=0, keepdims=True)
    to_end = jnp.exp(total - cum)
    from_start = jnp.exp(cum)
    chunk_decay = jnp.exp(total)
    x_t = xbc[:, :SSM_WIDTH].T
    w_t = (dt * to_end).T
    outs = []
    for g in range(SSM_GROUPS):
        b_g = xbc[:, SSM_WIDTH + SSM_STATE * g:SSM_WIDTH + SSM_STATE * (g + 1)].astype(BF16)
        c_g = xbc[:, SSM_WIDTH + SSM_STATE * (SSM_GROUPS + g):
                  SSM_WIDTH + SSM_STATE * (SSM_GROUPS + g + 1)].astype(BF16)
        cb = _dot_nt(c_g, b_g)
        for hh in range(SSM_HEADS // SSM_GROUPS):
            h = g * (SSM_HEADS // SSM_GROUPS) + hh
            col = direction * SSM_HEADS + h
            seg = jnp.exp(jnp.where(keep, cum[:, col:col + 1] - cum_t[col:col + 1, :], -jnp.inf))
            xd = xbc[:, SSM_HEAD_DIM * h:SSM_HEAD_DIM * (h + 1)] * dt[:, col:col + 1]
            y = _dot((cb * seg).astype(BF16), xd.astype(BF16))
            state = h_ref[h]
            y_off = _dot_nt(c_g, state.astype(BF16))
            outs.append(y + y_off * from_start[:, col:col + 1])
            xw_t = x_t[SSM_HEAD_DIM * h:SSM_HEAD_DIM * (h + 1), :] * w_t[col:col + 1, :]
            upd = _dot(xw_t.astype(BF16), b_g)
            h_ref[h] = state * chunk_decay[:, col:col + 1] + upd
    o_ref[...] = jnp.concatenate(outs, axis=1)


def _ssd(xbc, dt, a_log, n_ctx, direction):
    t = xbc.shape[0]
    lc = SSM_CHUNK
    nc = t // lc
    ncc = n_ctx // lc
    if direction == 0:
        order = lambda s: s
    else:
        order = lambda s: jnp.where(s < ncc, ncc - 1 - s, nc - 1 - (s - ncc))
    return pl.pallas_call(
        functools.partial(_ssd_kernel, direction=direction),
        out_shape=jax.ShapeDtypeStruct((t, SSM_WIDTH), F32),
        grid=(nc,),
        in_specs=[pl.BlockSpec((lc, SSM_XBC), lambda s: (order(s), 0)),
                  pl.BlockSpec((lc, LANES), lambda s: (order(s), 0)),
                  pl.BlockSpec((1, LANES), lambda s: (0, 0))],
        out_specs=pl.BlockSpec((lc, SSM_WIDTH), lambda s: (order(s), 0)),
        scratch_shapes=[pltpu.VMEM((SSM_HEADS, SSM_HEAD_DIM, SSM_STATE), F32)],
        compiler_params=_params("arbitrary"),
        name="ssd_fwd" if direction == 0 else "ssd_bwd",
    )(xbc, dt, a_log)


def _merge_kernel(x_ref, gate_ref, ya_ref, yb_ref, yf_ref, yr_ref, xs_ref, z_ref,
                  bg_ref, dskip_ref, gssm_ref, wa_ref, wb_ref, wc_ref, wo_ref,
                  ml_ref, mc_ref, g2_ref, wr_ref, br_ref,
                  xo_ref, f_ref, lg_ref, *, n_ctx, tm):
    y = (yf_ref[...] + yr_ref[...] + dskip_ref[...] * xs_ref[...]) * _silu(z_ref[...])
    gw = SSM_WIDTH // SSM_GROUPS
    yc = jnp.concatenate([_rms_rows(y[:, gw * g:gw * (g + 1)], gw) for g in range(SSM_GROUPS)], axis=1)
    yc = yc * gssm_ref[...]
    gate = _sigmoid(gate_ref[...] + bg_ref[...])
    m = (gate[:, :D_MODEL] * _dot(ya_ref[...].astype(BF16), wa_ref[...])
         + gate[:, D_MODEL:2 * D_MODEL] * _dot(yb_ref[...].astype(BF16), wb_ref[...])
         + gate[:, 2 * D_MODEL:] * _dot(yc.astype(BF16), wc_ref[...]))
    out = _dot(m.astype(BF16), wo_ref[...])
    row0 = pl.program_id(0) * tm
    is_ctx = (row0 + lax.broadcasted_iota(jnp.int32, (tm, 1), 0)) < n_ctx
    gt1 = jnp.where(is_ctx, mc_ref[2:3, :], ml_ref[2:3, :])
    x_new = x_ref[...] + gt1 * out
    xo_ref[...] = x_new
    f = _modulated_norm(x_new, g2_ref[...], ml_ref[3:5, :], mc_ref[3:5, :], row0, n_ctx)
    f_ref[...] = f
    lg_ref[...] = jnp.dot(f, wr_ref[...], preferred_element_type=F32,
                          precision=lax.Precision.HIGHEST) + br_ref[...]


def _merge(x_all, u, ya, yb, yf, yr, xbc, b_gate, dskip, g_ssm, wa, wb, wc, wo, mod_lat, mod_ctx,
           g2, w_router, b_router, n_ctx):
    t, d = x_all.shape
    tm = ROW_TILE
    const = lambda i: (0, 0)
    rows = lambda i: (i, 0)
    full = lambda a: pl.BlockSpec(a.shape, const)
    return pl.pallas_call(
        functools.partial(_merge_kernel, n_ctx=n_ctx, tm=tm),
        out_shape=(jax.ShapeDtypeStruct((t, d), F32), jax.ShapeDtypeStruct((t, d), F32),
                   jax.ShapeDtypeStruct((t, LANES), F32)),
        grid=(t // tm,),
        in_specs=[pl.BlockSpec((tm, d), rows),
                  pl.BlockSpec((tm, GATE_COLS), lambda i: (i, U_GATE // GATE_COLS)),
                  pl.BlockSpec((tm, MLA_WIDTH), rows),
                  pl.BlockSpec((tm, DIF_WIDTH), rows),
                  pl.BlockSpec((tm, SSM_WIDTH), rows),
                  pl.BlockSpec((tm, SSM_WIDTH), rows),
                  pl.BlockSpec((tm, SSM_WIDTH), rows),
                  pl.BlockSpec((tm, SSM_WIDTH), lambda i: (i, U_Z // SSM_WIDTH)),
                  full(b_gate), full(dskip), full(g_ssm), full(wa), full(wb), full(wc), full(wo),
                  full(mod_lat), full(mod_ctx), full(g2), full(w_router), full(b_router)],
        out_specs=(pl.BlockSpec((tm, d), rows), pl.BlockSpec((tm, d), rows),
                   pl.BlockSpec((tm, LANES), rows)),
        compiler_params=_params("parallel"),
        name="merge",
    )(x_all, u, ya, yb, yf, yr, xbc, u, b_gate, dskip, g_ssm, wa, wb, wc, wo, mod_lat, mod_ctx,
      g2, w_router, b_router)


def _moe_kernel(be_ref, nb_ref, x_ref, wgu_ref, bgu_ref, wd_ref, bd_ref, o_ref, wgu_s, wd_s):
    b = pl.program_id(0)
    prev = be_ref[jnp.maximum(b - 1, 0)]
    fresh = jnp.logical_or(b == 0, be_ref[b] != prev)

    @pl.when(fresh)
    def _():
        wgu_s[...] = wgu_ref[...].astype(BF16)
        wd_s[...] = wd_ref[...].astype(BF16)

    @pl.when(b < nb_ref[0])
    def _():
        gu = _dot(x_ref[...].astype(BF16), wgu_s[...]) + bgu_ref[...]
        glu = jnp.minimum(gu[:, :D_FF], SWIGLU_LIMIT)
        lin = jnp.clip(gu[:, D_FF:], -SWIGLU_LIMIT, SWIGLU_LIMIT)
        act = glu * _sigmoid(SWIGLU_ALPHA * glu) * (lin + 1.0)
        o_ref[...] = _dot(act.astype(BF16), wd_s[...]) + bd_ref[...]

    @pl.when(b >= nb_ref[0])
    def _():
        o_ref[...] = jnp.zeros(o_ref.shape, F32)


def _moe_experts(block_e, n_used, x_sorted, w_gu, b_gu, w_down, b_down, layer):
    n_slots, d = x_sorted.shape
    n_blocks = n_slots // MOE_BLOCK
    grid_spec = pltpu.PrefetchScalarGridSpec(
        num_scalar_prefetch=2,
        grid=(n_blocks,),
        in_specs=[pl.BlockSpec((MOE_BLOCK, d), lambda b, be, nb: (b, 0)),
                  pl.BlockSpec((None, None, d, 2 * D_FF), lambda b, be, nb: (layer, be[b], 0, 0)),
                  pl.BlockSpec((None, None, 1, 2 * D_FF), lambda b, be, nb: (layer, be[b], 0, 0)),
                  pl.BlockSpec((None, None, D_FF, d), lambda b, be, nb: (layer, be[b], 0, 0)),
                  pl.BlockSpec((None, None, 1, d), lambda b, be, nb: (layer, be[b], 0, 0))],
        out_specs=pl.BlockSpec((MOE_BLOCK, d), lambda b, be, nb: (b, 0)),
        scratch_shapes=[pltpu.VMEM((d, 2 * D_FF), BF16), pltpu.VMEM((D_FF, d), BF16)],
    )
    return pl.pallas_call(
        _moe_kernel,
        out_shape=jax.ShapeDtypeStruct((n_slots, d), F32),
        grid_spec=grid_spec,
        compiler_params=_params("arbitrary"),
        name="moe_experts",
    )(block_e, n_used, x_sorted, w_gu, b_gu, w_down, b_down)


def _combine_kernel(x_ref, g_ref, ml_ref, mc_ref, *rest, n_ctx, tm):
    y_refs, o_ref = rest[:TOP_K], rest[TOP_K]
    is_ctx = (pl.program_id(0) * tm + lax.broadcasted_iota(jnp.int32, (tm, 1), 0)) < n_ctx
    gate2 = jnp.where(is_ctx, mc_ref[...], ml_ref[...])
    g = g_ref[...]
    y = g[:, 0:1] * y_refs[0][...]
    for k in range(1, TOP_K):
        y = y + g[:, k:k + 1] * y_refs[k][...]
    o_ref[...] = x_ref[...] + gate2 * y


def _combine(x_all, y_rows, gates, gate2_lat, gate2_ctx, n_ctx):
    t, d = x_all.shape
    tm = ROW_TILE
    rows = lambda i: (i, 0)
    const = lambda i: (0, 0)
    return pl.pallas_call(
        functools.partial(_combine_kernel, n_ctx=n_ctx, tm=tm),
        out_shape=jax.ShapeDtypeStruct((t, d), F32),
        grid=(t // tm,),
        in_specs=[pl.BlockSpec((tm, d), rows), pl.BlockSpec((tm, LANES), rows),
                  pl.BlockSpec((1, d), const), pl.BlockSpec((1, d), const)]
                 + [pl.BlockSpec((tm, d), rows)] * TOP_K,
        out_specs=pl.BlockSpec((tm, d), rows),
        compiler_params=_params("parallel"),
        name="moe_combine",
    )(x_all, gates, gate2_lat, gate2_ctx, *y_rows)


def _route(logits, n_tok):
    top_val, top_idx = lax.top_k(logits, TOP_K)
    gates = jax.nn.softmax(top_val, axis=-1)
    n_assign = n_tok * TOP_K
    chosen = jnp.sum((top_idx[..., None] == jnp.arange(N_EXPERTS)).astype(jnp.int32), axis=1)
    before = jnp.cumsum(chosen, axis=0) - chosen
    counts = before[-1] + chosen[-1]
    padded = (counts + MOE_BLOCK - 1) // MOE_BLOCK * MOE_BLOCK
    pad_end = jnp.cumsum(padded)
    pad_start = pad_end - padded
    slot_of = jnp.take_along_axis(before + pad_start[None, :], top_idx, axis=1)
    n_blocks = -(-(n_assign + N_EXPERTS * (MOE_BLOCK - 1)) // MOE_BLOCK)
    n_slots = n_blocks * MOE_BLOCK
    token = jnp.arange(n_assign, dtype=jnp.int32) // TOP_K
    slot_tok = jnp.zeros((n_slots,), jnp.int32).at[slot_of.reshape(-1)].set(token, unique_indices=True)
    block_start = jnp.arange(n_blocks, dtype=jnp.int32) * MOE_BLOCK
    block_e = jnp.minimum(jnp.searchsorted(pad_end, block_start, side='right'), N_EXPERTS - 1)
    n_used = (pad_end[-1] // MOE_BLOCK).reshape(1)
    return gates, slot_tok, slot_of, block_e.astype(jnp.int32), n_used.astype(jnp.int32)


def kernel(x, c, ctx, c_ctx, w_mod, b_mod, g_norm1, g_norm2, w_in, b_gate, mla_g_q, mla_w_uq, mla_g_kv, mla_w_ukv, mla_g_qn, mla_g_kn, dif_g_qn, dif_g_kn, dif_lambda, dif_g_sub, ssm_conv_w, ssm_conv_b, ssm_dt_bias, ssm_a_log, ssm_d, ssm_g_norm, w_up_mla, w_up_dif, w_up_ssm, w_out, moe_w_router, moe_b_router, moe_w_gu, moe_b_gu, moe_w_down, moe_b_down):
    assert x.shape[0] == 1 and ctx.shape[0] == 1
    depth = w_in.shape[0]
    seq = x.shape[1]
    n_ctx = ctx.shape[1]
    n_tok = n_ctx + seq
    d = D_MODEL
    assert n_ctx == KEY_CHUNK and n_tok % ROW_TILE == 0 and seq % GRID_W == 0

    x_all = jnp.concatenate([ctx[0], x[0]], axis=0)
    cc = jnp.zeros((8, d), F32).at[0].set(c[0]).at[1].set(c_ctx)
    mod = _mod_vectors(cc, w_mod, b_mod)
    mod = mod[:, :2].reshape(depth, 2, 6, d)

    rope_mla = _rope_tables(seq, n_ctx, MLA_ROPE, MLA_NOPE)
    rope_dif = _rope_tables(seq, n_ctx, DIF_HEAD_DIM, 0)
    in_cols = _in_proj_columns()
    uq_cols = _head_columns(MLA_HEADS, MLA_QK, 0, MLA_QK, LANES)
    uk_cols = _head_columns(MLA_HEADS, MLA_NOPE + MLA_V, 0, MLA_NOPE, LANES)
    uv_cols = _head_columns(MLA_HEADS, MLA_NOPE + MLA_V, MLA_NOPE, MLA_V, MLA_V)

    for i in range(depth):
        lam_init = 0.8 - 0.6 * math.exp(-0.3 * i)
        mod_lat, mod_ctx = mod[i, 0], mod[i, 1]
        w_in_i = _take_columns(w_in[i], in_cols).astype(BF16)
        u = _in_proj(x_all, g_norm1[i][None], mod_lat[0:2], mod_ctx[0:2], w_in_i, n_ctx)

        qt, k, vt = _mla_prep(
            u, mla_g_q[i][None], _take_columns(mla_w_uq[i], uq_cols).astype(BF16),
            mla_g_kv[i][None], _take_columns(mla_w_ukv[i], uk_cols).astype(BF16),
            _take_columns(mla_w_ukv[i], uv_cols).astype(BF16),
            _pad_lanes(mla_g_qn[i][None]), _pad_lanes(mla_g_kn[i][None]), rope_mla)
        ya = _attention(qt, k, vt)

        qt, k, vt = _dif_prep(u, _pad_lanes(dif_g_qn[i][None]), _pad_lanes(dif_g_kn[i][None]), rope_dif)
        lam_rows = jnp.zeros((8, LANES), F32).at[:4, :DIF_HEAD_DIM].set(dif_lambda[i])
        yb = _attention(qt, k, vt, lam=lam_rows, gsub=dif_g_sub[i][None], lam_init=lam_init)

        conv_w = jnp.zeros((8, SSM_XBC), F32).at[:SSM_CONV].set(ssm_conv_w[i])
        xbc, dt = _ssm_prep(u, conv_w, ssm_conv_b[i][None], _pad_lanes(ssm_dt_bias[i].reshape(1, -1)), n_ctx)
        a_log = _pad_lanes(ssm_a_log[i].reshape(1, -1))
        yf = _ssd(xbc, dt, a_log, n_ctx, 0)
        yr = _ssd(xbc, dt, a_log, n_ctx, 1)

        dskip = jnp.repeat(ssm_d[i, 0] + ssm_d[i, 1], SSM_HEAD_DIM)[None]
        w_router = jnp.zeros((d, LANES), F32).at[:, :N_EXPERTS].set(moe_w_router[i])
        b_router = jnp.zeros((1, LANES), F32).at[0, :N_EXPERTS].set(moe_b_router[i])
        x_all, f, logits = _merge(
            x_all, u, ya, yb, yf, yr, xbc, b_gate[i][None], dskip, ssm_g_norm[i][None],
            w_up_mla[i].astype(BF16), w_up_dif[i].astype(BF16), w_up_ssm[i].astype(BF16),
            w_out[i].astype(BF16), mod_lat[0:5], mod_ctx[0:5], g_norm2[i][None],
            w_router, b_router, n_ctx)

        gates, slot_tok, slot_of, block_e, n_used = _route(logits[:, :N_EXPERTS], n_tok)
        y_slots = _moe_experts(block_e, n_used, f[slot_tok], moe_w_gu,
                               moe_b_gu.reshape(depth, N_EXPERTS, 1, 2 * D_FF), moe_w_down,
                               moe_b_down.reshape(depth, N_EXPERTS, 1, d), i)
        y_rows = [y_slots[slot_of[:, k]] for k in range(TOP_K)]
        x_all = _combine(x_all, y_rows, _pad_lanes(gates), mod_lat[5:6], mod_ctx[5:6], n_ctx)
    return x_all[n_ctx:][None]
```

```python
import functools
import math

import numpy as np
import jax
import jax.numpy as jnp
from jax import lax
from jax.experimental import pallas as pl
from jax.experimental.pallas import tpu as pltpu

F32 = jnp.float32
BF16 = jnp.bfloat16
LANES = 128
VMEM_LIMIT = 52 * 1024 * 1024

D_MODEL = 1024
EPS = 1e-6
ROPE_THETA = 10000.0
GRID_W = 64
N_BRANCH = 3

MLA_HEADS = 8
MLA_Q_RANK = 256
MLA_KV_RANK = 128
MLA_NOPE = 64
MLA_ROPE = 32
MLA_V = 64
MLA_QK = MLA_NOPE + MLA_ROPE
MLA_WIDTH = MLA_HEADS * MLA_V

DIF_HEADS = 4
DIF_HEAD_DIM = 64
DIF_WIDTH = DIF_HEADS * 2 * DIF_HEAD_DIM

SSM_HEADS = 8
SSM_HEAD_DIM = 64
SSM_WIDTH = SSM_HEADS * SSM_HEAD_DIM
SSM_GROUPS = 2
SSM_STATE = 128
SSM_CONV = 5
SSM_CHUNK = 128
SSM_XBC = SSM_WIDTH + 2 * SSM_GROUPS * SSM_STATE

N_EXPERTS = 32
TOP_K = 4
D_FF = 1024
SWIGLU_LIMIT = 7.0
SWIGLU_ALPHA = 1.702
MOE_BLOCK = 256

MLA_COLS = MLA_Q_RANK + MLA_KV_RANK + MLA_ROPE
DIF_COLS = 3 * DIF_WIDTH
SSM_COLS = SSM_WIDTH + SSM_XBC + 2 * SSM_HEADS
GATE_COLS = N_BRANCH * D_MODEL

ROW_TILE = 256
KEY_CHUNK = 256
LOG2E = 1.4426950408889634

U_GATE, U_DQ, U_DK, U_XBC, U_MLA, U_DV, U_Z, U_DT = 0, 3072, 4096, 5120, 6144, 6656, 7168, 7680
U_COLS = 8192
U_TILE_N = 1024


def _in_proj_columns():
    src = np.full((U_COLS,), -1, np.int64)
    dif0 = MLA_COLS
    ssm0 = MLA_COLS + DIF_COLS
    gate0 = ssm0 + SSM_COLS
    src[U_GATE:U_GATE + GATE_COLS] = gate0 + np.arange(GATE_COLS)
    for a in range(2 * DIF_HEADS):
        src[U_DQ + LANES * a:U_DQ + LANES * a + DIF_HEAD_DIM] = dif0 + DIF_HEAD_DIM * a + np.arange(DIF_HEAD_DIM)
        src[U_DK + LANES * a:U_DK + LANES * a + DIF_HEAD_DIM] = (dif0 + DIF_WIDTH + DIF_HEAD_DIM * a
                                                                  + np.arange(DIF_HEAD_DIM))
    src[U_DV:U_DV + DIF_WIDTH] = dif0 + 2 * DIF_WIDTH + np.arange(DIF_WIDTH)
    src[U_MLA:U_MLA + MLA_Q_RANK + MLA_KV_RANK] = np.arange(MLA_Q_RANK + MLA_KV_RANK)
    pe0 = U_MLA + MLA_Q_RANK + MLA_KV_RANK + MLA_NOPE
    src[pe0:pe0 + MLA_ROPE] = MLA_Q_RANK + MLA_KV_RANK + np.arange(MLA_ROPE)
    src[U_Z:U_Z + SSM_WIDTH] = ssm0 + np.arange(SSM_WIDTH)
    src[U_XBC:U_XBC + SSM_XBC] = ssm0 + SSM_WIDTH + np.arange(SSM_XBC)
    src[U_DT:U_DT + 2 * SSM_HEADS] = ssm0 + SSM_WIDTH + SSM_XBC + np.arange(2 * SSM_HEADS)
    return src


def _take_columns(w, src):
    cols = jnp.take(w, jnp.asarray(np.maximum(src, 0)), axis=-1)
    return jnp.where(jnp.asarray(src >= 0), cols, 0.0)


def _head_columns(n_heads, src_stride, src_off, width, dst_stride):
    src = np.full((n_heads * dst_stride,), -1, np.int64)
    for h in range(n_heads):
        src[h * dst_stride:h * dst_stride + width] = h * src_stride + src_off + np.arange(width)
    return src


def _pad_lanes(v, n=LANES):
    return jnp.pad(v, [(0, 0)] * (v.ndim - 1) + [(0, n - v.shape[-1])])


def _row_tile(n, cap):
    best = 8
    for t in range(8, cap + 1, 8):
        if n % t == 0:
            best = t
    return best


def _dot(a, b):
    return jnp.dot(a, b, preferred_element_type=F32)


def _dot_nt(a, b):
    return lax.dot_general(a, b, (((1,), (1,)), ((), ())), preferred_element_type=F32)


def _sigmoid(x):
    return 1.0 / (1.0 + jnp.exp(-x))


def _silu(x):
    return x * _sigmoid(x)


def _rms_rows(x, n):
    return x * lax.rsqrt(jnp.sum(x * x, axis=-1, keepdims=True) * (1.0 / n) + EPS)


def _params(*sem):
    return pltpu.CompilerParams(dimension_semantics=sem, vmem_limit_bytes=VMEM_LIMIT)


def _mod_kernel(a_ref, w_ref, b_ref, o_ref):
    a = _silu(a_ref[...]).astype(BF16)
    o_ref[0] = _dot(a, w_ref[0].astype(BF16)) + b_ref[0]


def _mod_vectors(cc, w_mod, b_mod):
    depth, d, n = w_mod.shape
    tn = 1536
    return pl.pallas_call(
        _mod_kernel,
        out_shape=jax.ShapeDtypeStruct((depth, 8, n), F32),
        grid=(depth, n // tn),
        in_specs=[pl.BlockSpec((8, d), lambda l, j: (0, 0)),
                  pl.BlockSpec((1, d, tn), lambda l, j: (l, 0, j)),
                  pl.BlockSpec((1, 1, tn), lambda l, j: (l, 0, j))],
        out_specs=pl.BlockSpec((1, 8, tn), lambda l, j: (l, 0, j)),
        compiler_params=_params("parallel", "parallel"),
        name="mod_vectors",
    )(cc, w_mod, b_mod.reshape(depth, 1, n))


def _modulated_norm(x, g, mod_lat, mod_ctx, row0, n_ctx):
    rows = x.shape[0]
    is_ctx = (row0 + lax.broadcasted_iota(jnp.int32, (rows, 1), 0)) < n_ctx
    shift = jnp.where(is_ctx, mod_ctx[0:1, :], mod_lat[0:1, :])
    scale = jnp.where(is_ctx, mod_ctx[1:2, :], mod_lat[1:2, :])
    return _rms_rows(x, x.shape[1]) * g * (1.0 + scale) + shift


def _in_proj_kernel(x_ref, g_ref, ml_ref, mc_ref, w_ref, o_ref, h_ref, *, n_ctx, tm):
    @pl.when(pl.program_id(1) == 0)
    def _():
        h = _modulated_norm(x_ref[...], g_ref[...], ml_ref[...], mc_ref[...], pl.program_id(0) * tm, n_ctx)
        h_ref[...] = h.astype(BF16)

    o_ref[...] = _dot(h_ref[...], w_ref[...])


def _in_proj(x_all, g, mod_lat, mod_ctx, w, layer, n_ctx):
    t, d = x_all.shape
    n = w.shape[2]
    tm = _row_tile(t, 1280)
    return pl.pallas_call(
        functools.partial(_in_proj_kernel, n_ctx=n_ctx, tm=tm),
        out_shape=jax.ShapeDtypeStruct((t, n), F32),
        grid=(t // tm, n // U_TILE_N),
        in_specs=[pl.BlockSpec((tm, d), lambda i, j: (i, 0)),
                  pl.BlockSpec((1, d), lambda i, j: (0, 0)),
                  pl.BlockSpec((2, d), lambda i, j: (0, 0)),
                  pl.BlockSpec((2, d), lambda i, j: (0, 0)),
                  pl.BlockSpec((None, d, U_TILE_N), lambda i, j: (layer, 0, j))],
        out_specs=pl.BlockSpec((tm, U_TILE_N), lambda i, j: (i, j)),
        scratch_shapes=[pltpu.VMEM((tm, d), BF16)],
        compiler_params=_params("parallel", "arbitrary"),
        name="in_proj",
    )(x_all, g, mod_lat, mod_ctx, w)


def _rope_tables(seq_len, n_ctx, rot_dim, lane0):
    n_rows = seq_len // GRID_W
    row = jnp.repeat(jnp.arange(n_rows), GRID_W).astype(F32)
    col = jnp.tile(jnp.arange(GRID_W), n_rows).astype(F32)
    axis_dim = rot_dim // 2
    half = axis_dim // 2
    inv = ROPE_THETA ** (-jnp.arange(0, axis_dim, 2, dtype=F32) / axis_dim)
    ang_r = row[:, None] * inv
    ang_c = col[:, None] * inv
    zeros = jnp.zeros((seq_len, half), F32)
    cos = jnp.concatenate([jnp.cos(ang_r), jnp.cos(ang_r), jnp.cos(ang_c), jnp.cos(ang_c)], axis=1)
    s1 = jnp.concatenate([zeros, jnp.sin(ang_r), zeros, jnp.sin(ang_c)], axis=1)
    s2 = jnp.concatenate([-jnp.sin(ang_r), zeros, -jnp.sin(ang_c), zeros], axis=1)

    def place(tab, fill):
        full = jnp.full((seq_len, LANES), fill, F32).at[:, lane0:lane0 + rot_dim].set(tab)
        ctx = jnp.full((n_ctx, LANES), fill, F32)
        return jnp.concatenate([ctx, full], axis=0)

    return place(cos, 1.0), place(s1, 0.0), place(s2, 0.0)


def _rope(x, cos, s1, s2, half):
    return x * cos + pltpu.roll(x, half, 1) * s1 + pltpu.roll(x, LANES - half, 1) * s2


def _mla_prep_kernel(u_ref, gq_ref, wuq_ref, gkv_ref, wk_ref, wv_ref, gqn_ref, gkn_ref,
                     cos_ref, s1_ref, s2_ref, qt_ref, k_ref, vt_ref):
    u = u_ref[...]
    cq = u[:, :MLA_Q_RANK]
    ckv = u[:, MLA_Q_RANK:MLA_Q_RANK + MLA_KV_RANK]
    pe = u[:, MLA_Q_RANK + MLA_KV_RANK:]
    q = _dot((_rms_rows(cq, MLA_Q_RANK) * gq_ref[...]).astype(BF16), wuq_ref[...])
    kv_in = (_rms_rows(ckv, MLA_KV_RANK) * gkv_ref[...]).astype(BF16)
    kn = _dot(kv_in, wk_ref[...])
    v = _dot(kv_in, wv_ref[...])
    cos, s1, s2 = cos_ref[...], s1_ref[...], s2_ref[...]
    half = MLA_ROPE // 4
    q_scale = MLA_QK ** -0.5 * LOG2E
    for h in range(MLA_HEADS):
        qh = _rms_rows(q[:, LANES * h:LANES * (h + 1)], MLA_QK) * gqn_ref[...]
        qh = _rope(qh, cos, s1, s2, half) * q_scale
        qt_ref[h] = qh.T.astype(BF16)
        kh = _rms_rows(kn[:, LANES * h:LANES * (h + 1)] + pe, MLA_QK) * gkn_ref[...]
        k_ref[h, 0] = _rope(kh, cos, s1, s2, half).astype(BF16)
    tm = v.shape[0]
    vt_ref[:, 0] = _with_sum_rows(v.T.reshape(MLA_HEADS, MLA_V, tm)).astype(BF16)


def _mla_prep(u, gq, wuq, gkv, wk, wv, gqn, gkn, tabs):
    t = u.shape[0]
    tm = ROW_TILE
    nt = t // tm
    const = lambda i: (0, 0)
    rows = lambda i: (i, 0)
    return pl.pallas_call(
        _mla_prep_kernel,
        out_shape=(jax.ShapeDtypeStruct((MLA_HEADS, LANES, t), BF16),
                   jax.ShapeDtypeStruct((MLA_HEADS, nt, tm, LANES), BF16),
                   jax.ShapeDtypeStruct((MLA_HEADS, nt, MLA_V + SUM_ROWS, tm), BF16)),
        grid=(nt,),
        in_specs=[pl.BlockSpec((tm, 512), lambda i: (i, U_MLA // 512)),
                  pl.BlockSpec((1, MLA_Q_RANK), const),
                  pl.BlockSpec(wuq.shape, const),
                  pl.BlockSpec((1, MLA_KV_RANK), const),
                  pl.BlockSpec(wk.shape, const),
                  pl.BlockSpec(wv.shape, const),
                  pl.BlockSpec((1, LANES), const),
                  pl.BlockSpec((1, LANES), const),
                  pl.BlockSpec((tm, LANES), rows),
                  pl.BlockSpec((tm, LANES), rows),
                  pl.BlockSpec((tm, LANES), rows)],
        out_specs=(pl.BlockSpec((MLA_HEADS, LANES, tm), lambda i: (0, 0, i)),
                   pl.BlockSpec((MLA_HEADS, 1, tm, LANES), lambda i: (0, i, 0, 0)),
                   pl.BlockSpec((MLA_HEADS, 1, MLA_V + SUM_ROWS, tm), lambda i: (0, i, 0, 0))),
        compiler_params=_params("parallel"),
        name="mla_prep",
    )(u, gq, wuq, gkv, wk, wv, gqn, gkn, *tabs)


def _dif_prep_kernel(q_ref, k_ref, v_ref, gq_ref, gk_ref, cos_ref, s1_ref, s2_ref,
                     qt_out, k_out, vt_out):
    cos, s1, s2 = cos_ref[...], s1_ref[...], s2_ref[...]
    half = DIF_HEAD_DIM // 4
    q_scale = DIF_HEAD_DIM ** -0.5 * LOG2E
    for a in range(2 * DIF_HEADS):
        qa = _rms_rows(q_ref[:, LANES * a:LANES * (a + 1)], DIF_HEAD_DIM) * gq_ref[...]
        qt_out[a] = (_rope(qa, cos, s1, s2, half) * q_scale).T.astype(BF16)
        ka = _rms_rows(k_ref[:, LANES * a:LANES * (a + 1)], DIF_HEAD_DIM) * gk_ref[...]
        k_out[a, 0] = _rope(ka, cos, s1, s2, half).astype(BF16)
    v = v_ref[...]
    vt_out[:, 0] = _with_sum_rows(v.T.reshape(DIF_HEADS, 2 * DIF_HEAD_DIM, v.shape[0])).astype(BF16)


def _dif_prep(u, gq, gk, tabs):
    t = u.shape[0]
    tm = ROW_TILE
    nt = t // tm
    nsub = 2 * DIF_HEADS
    const = lambda i: (0, 0)
    rows = lambda i: (i, 0)
    return pl.pallas_call(
        _dif_prep_kernel,
        out_shape=(jax.ShapeDtypeStruct((nsub, LANES, t), BF16),
                   jax.ShapeDtypeStruct((nsub, nt, tm, LANES), BF16),
                   jax.ShapeDtypeStruct((DIF_HEADS, nt, 2 * DIF_HEAD_DIM + SUM_ROWS, tm), BF16)),
        grid=(nt,),
        in_specs=[pl.BlockSpec((tm, 1024), lambda i: (i, U_DQ // 1024)),
                  pl.BlockSpec((tm, 1024), lambda i: (i, U_DK // 1024)),
                  pl.BlockSpec((tm, 512), lambda i: (i, U_DV // 512)),
                  pl.BlockSpec((1, LANES), const),
                  pl.BlockSpec((1, LANES), const),
                  pl.BlockSpec((tm, LANES), rows),
                  pl.BlockSpec((tm, LANES), rows),
                  pl.BlockSpec((tm, LANES), rows)],
        out_specs=(pl.BlockSpec((nsub, LANES, tm), lambda i: (0, 0, i)),
                   pl.BlockSpec((nsub, 1, tm, LANES), lambda i: (0, i, 0, 0)),
                   pl.BlockSpec((DIF_HEADS, 1, 2 * DIF_HEAD_DIM + SUM_ROWS, tm), lambda i: (0, i, 0, 0))),
        compiler_params=_params("parallel"),
        name="dif_prep",
    )(u, u, u, gq, gk, *tabs)


ATTN_GROUP = 4
SUM_ROWS = 16


def _with_sum_rows(vt):
    heads, _, keys = vt.shape
    row = lax.broadcasted_iota(jnp.int32, (heads, SUM_ROWS, keys), 1)
    return jnp.concatenate([vt, jnp.where(row == 0, 1.0, 0.0).astype(vt.dtype)], axis=1)


def _attn_group(qt_ref, k_ref, vt_ref, v_of_sub, s_buf, p_buf, acc_ref, n_chunks):
    tq = qt_ref.shape[2]
    last = n_chunks - 1

    def scores(c):
        out = []
        for a in range(ATTN_GROUP):
            s = _dot(k_ref[a, c], qt_ref[a])
            out.append((s, jnp.max(s, axis=0, keepdims=True)))
        return out

    def stash(sc, slot):
        for a in range(ATTN_GROUP):
            s_buf[slot, a] = sc[a][0]
        return tuple(mx for (_, mx) in sc)

    def softmax(s_of, mx, slot, ms):
        new_ms, alphas = [], []
        for a in range(ATTN_GROUP):
            m_new = jnp.maximum(ms[a], mx[a])
            alphas.append(jnp.exp2(ms[a] - m_new))
            p_buf[slot, a] = jnp.exp2(s_of(a) - m_new).astype(BF16)
            new_ms.append(m_new)
        return tuple(new_ms), tuple(alphas)

    def values(c, slot, alphas):
        for a in range(ATTN_GROUP):
            acc_ref[a] = alphas[a] * acc_ref[a] + _dot(vt_ref[v_of_sub[a], c], p_buf[slot, a])

    per_trip = 8 if last % 8 == 0 else 4

    def trip(t, state):
        ms, alphas, mx0 = state
        c = per_trip * t + 1
        for pair in range(per_trip // 2):
            cur, nxt = pair % 2, 1 - pair % 2
            even = scores(c + 1)
            values(c - 1, 0, alphas)
            ms, alphas = softmax(lambda a: s_buf[cur, a], mx0, 1, ms)
            mx0 = stash(scores(jnp.minimum(c + 2, last)), nxt)
            values(c, 1, alphas)
            ms, alphas = softmax(lambda a: even[a][0], [mx for (_, mx) in even], 0, ms)
            c = c + 2
        return ms, alphas, mx0

    acc_ref[...] = jnp.zeros(acc_ref.shape, F32)
    ms = tuple(jnp.full((1, tq), -jnp.inf, F32) for _ in range(ATTN_GROUP))
    first = scores(0)
    ms, alphas = softmax(lambda a: first[a][0], [mx for (_, mx) in first], 0, ms)
    mx0 = stash(scores(1), 0)
    n_trips = jnp.where(pl.program_id(1) > 0, last // per_trip, 0)
    ms, alphas, _ = lax.fori_loop(0, n_trips, trip, (ms, alphas, mx0))
    values(per_trip * n_trips, 0, alphas)


def _normalised(acc_ref, a):
    dv = acc_ref.shape[1] - SUM_ROWS
    return acc_ref[a, :dv] * (1.0 / acc_ref[a, dv:dv + 1])


def _mla_attn_kernel(qt_ref, k_ref, vt_ref, o_ref, s_buf, p_buf, acc_ref, *, n_chunks):
    _attn_group(qt_ref, k_ref, vt_ref, tuple(range(ATTN_GROUP)), s_buf, p_buf, acc_ref, n_chunks)
    o = jnp.concatenate([_normalised(acc_ref, a) for a in range(ATTN_GROUP)], axis=0)
    o_ref[...] = o.T


def _dif_attn_kernel(lam_ref, gsub_ref, qt_ref, k_ref, vt_ref, o_ref, s_buf, p_buf, acc_ref,
                     *, n_chunks, lam_init):
    _attn_group(qt_ref, k_ref, vt_ref, tuple(a // 2 for a in range(ATTN_GROUP)), s_buf, p_buf,
                acc_ref, n_chunks)
    lp = lam_ref[...]
    lam = (jnp.exp(jnp.sum(lp[0:1] * lp[1:2], axis=-1, keepdims=True))
           - jnp.exp(jnp.sum(lp[2:3] * lp[3:4], axis=-1, keepdims=True)) + lam_init)
    outs = []
    for h in range(ATTN_GROUP // 2):
        o = _normalised(acc_ref, 2 * h) - lam * _normalised(acc_ref, 2 * h + 1)
        outs.append(o * lax.rsqrt(jnp.mean(o * o, axis=0, keepdims=True) + EPS))
    gsub = gsub_ref[...] * (1.0 - lam_init)
    o_ref[...] = jnp.concatenate(outs, axis=0).T * jnp.concatenate([gsub] * len(outs), axis=1)


def _attention(qt, k, vt, *, lam=None, gsub=None, lam_init=None):
    nsub, _, t = qt.shape
    n_chunks = k.shape[1]
    assert (n_chunks - 1) % 4 == 0 and nsub % ATTN_GROUP == 0
    tq = KEY_CHUNK
    dv = vt.shape[2] - SUM_ROWS
    groups = nsub // ATTN_GROUP
    n_v = vt.shape[0] // groups
    out_w = n_v * dv
    resident = dict(pipeline_mode=pl.Buffered(1))
    specs = [pl.BlockSpec((ATTN_GROUP, LANES, tq), lambda g, i: (g, 0, i)),
             pl.BlockSpec((ATTN_GROUP, n_chunks, KEY_CHUNK, LANES), lambda g, i: (g, 0, 0, 0), **resident),
             pl.BlockSpec((n_v, n_chunks, dv + SUM_ROWS, KEY_CHUNK), lambda g, i: (g, 0, 0, 0), **resident)]
    if lam is None:
        body = functools.partial(_mla_attn_kernel, n_chunks=n_chunks)
        args = (qt, k, vt)
    else:
        body = functools.partial(_dif_attn_kernel, n_chunks=n_chunks, lam_init=lam_init)
        specs = [pl.BlockSpec((8, LANES), lambda g, i: (0, 0)),
                 pl.BlockSpec((1, LANES), lambda g, i: (0, 0))] + specs
        args = (lam, gsub, qt, k, vt)
    return pl.pallas_call(
        body,
        out_shape=jax.ShapeDtypeStruct((t, groups * out_w), F32),
        grid=(groups, t // tq),
        in_specs=specs,
        out_specs=pl.BlockSpec((tq, out_w), lambda g, i: (i, g)),
        scratch_shapes=[pltpu.VMEM((2, ATTN_GROUP, KEY_CHUNK, tq), F32),
                        pltpu.VMEM((2, ATTN_GROUP, KEY_CHUNK, tq), BF16),
                        pltpu.VMEM((ATTN_GROUP, dv + SUM_ROWS, tq), F32)],
        compiler_params=_params("parallel", "arbitrary"),
        name="mla_attention" if lam is None else "dif_attention",
    )(*args)


def _ssm_prep_kernel(x_ref, prev_ref, next_ref, dt_ref, w_ref, b_ref, dtb_ref, xo_ref, dto_ref, e_ref,
                     *, n_ctx, n_tok, tm):
    row0 = pl.program_id(0) * tm
    pad = SSM_CONV // 2
    has_prev = jnp.logical_and(row0 != 0, row0 != n_ctx)
    has_next = jnp.logical_and(row0 + tm != n_ctx, row0 + tm != n_tok)
    e_ref[0:8] = jnp.where(has_prev, prev_ref[...], 0.0)
    e_ref[8:8 + tm] = x_ref[...]
    e_ref[8 + tm:16 + tm] = jnp.where(has_next, next_ref[...], 0.0)
    acc = jnp.zeros(x_ref.shape, F32) + b_ref[...]
    for k in range(SSM_CONV):
        acc = acc + w_ref[k:k + 1, :] * e_ref[pl.ds(8 - pad + k, tm), :]
    xo_ref[...] = _silu(acc)
    d = dt_ref[...] + dtb_ref[...]
    dto_ref[...] = jnp.maximum(d, 0.0) + jnp.log1p(jnp.exp(-jnp.abs(d)))


def _ssm_prep(u, conv_w, conv_b, dt_bias, n_ctx):
    t = u.shape[0]
    tm = ROW_TILE
    nt = t // tm
    cb = U_XBC // SSM_XBC
    const = lambda i: (0, 0)
    return pl.pallas_call(
        functools.partial(_ssm_prep_kernel, n_ctx=n_ctx, n_tok=t, tm=tm),
        out_shape=(jax.ShapeDtypeStruct((t, SSM_XBC), F32), jax.ShapeDtypeStruct((t, LANES), F32)),
        grid=(nt,),
        in_specs=[pl.BlockSpec((tm, SSM_XBC), lambda i: (i, cb)),
                  pl.BlockSpec((8, SSM_XBC), lambda i: (jnp.maximum(i * (tm // 8) - 1, 0), cb)),
                  pl.BlockSpec((8, SSM_XBC), lambda i: (jnp.minimum((i + 1) * (tm // 8), t // 8 - 1), cb)),
                  pl.BlockSpec((tm, LANES), lambda i: (i, U_DT // LANES)),
                  pl.BlockSpec((8, SSM_XBC), const),
                  pl.BlockSpec((1, SSM_XBC), const),
                  pl.BlockSpec((1, LANES), const)],
        out_specs=(pl.BlockSpec((tm, SSM_XBC), lambda i: (i, 0)),
                   pl.BlockSpec((tm, LANES), lambda i: (i, 0))),
        scratch_shapes=[pltpu.VMEM((tm + 16, SSM_XBC), F32)],
        compiler_params=_params("parallel"),
        name="ssm_prep",
    )(u, u, u, u, conv_w, conv_b, dt_bias)


def _ssd_kernel(xf_ref, dtf_ref, xr_ref, dtr_ref, alog_ref, of_ref, or_ref, hf_ref, hr_ref):
    @pl.when(pl.program_id(0) == 0)
    def _():
        hf_ref[...] = jnp.zeros(hf_ref.shape, F32)
        hr_ref[...] = jnp.zeros(hr_ref.shape, F32)

    _ssd_chunk(xf_ref, dtf_ref, alog_ref, of_ref, hf_ref, 0)
    _ssd_chunk(xr_ref, dtr_ref, alog_ref, or_ref, hr_ref, 1)


def _ssd_chunk(x_ref, dt_ref, alog_ref, o_ref, h_ref, direction):
    lc = SSM_CHUNK
    xbc = x_ref[...]
    dt = dt_ref[...]
    dta = dt * (-jnp.exp(alog_ref[...]))
    r = lax.broadcasted_iota(jnp.int32, (lc, lc), 0)
    c = lax.broadcasted_iota(jnp.int32, (lc, lc), 1)
    keep = (r >= c) if direction == 0 else (r <= c)
    tri = jnp.where(keep, 1.0, 0.0).astype(F32)
    cum = jnp.dot(tri, dta, preferred_element_type=F32, precision=lax.Precision.HIGHEST)
    cum_t = cum.T
    total = jnp.sum(dta, axis=0, keepdims=True)
    to_end = jnp.exp(total - cum)
    from_start = jnp.exp(cum)
    chunk_decay = jnp.exp(total)
    x_t = xbc[:, :SSM_WIDTH].T
    w_t = (dt * to_end).T
    outs = []
    for g in range(SSM_GROUPS):
        b_g = xbc[:, SSM_WIDTH + SSM_STATE * g:SSM_WIDTH + SSM_STATE * (g + 1)].astype(BF16)
        c_g = xbc[:, SSM_WIDTH + SSM_STATE * (SSM_GROUPS + g):
                  SSM_WIDTH + SSM_STATE * (SSM_GROUPS + g + 1)].astype(BF16)
        cb = _dot_nt(c_g, b_g)
        for hh in range(SSM_HEADS // SSM_GROUPS):
            h = g * (SSM_HEADS // SSM_GROUPS) + hh
            col = direction * SSM_HEADS + h
            seg = jnp.exp(jnp.where(keep, cum[:, col:col + 1] - cum_t[col:col + 1, :], -jnp.inf))
            xd = xbc[:, SSM_HEAD_DIM * h:SSM_HEAD_DIM * (h + 1)] * dt[:, col:col + 1]
            y = _dot((cb * seg).astype(BF16), xd.astype(BF16))
            state = h_ref[h]
            y_off = _dot_nt(c_g, state.astype(BF16))
            outs.append(y + y_off * from_start[:, col:col + 1])
            xw_t = x_t[SSM_HEAD_DIM * h:SSM_HEAD_DIM * (h + 1), :] * w_t[col:col + 1, :]
            upd = _dot(xw_t.astype(BF16), b_g)
            h_ref[h] = state * chunk_decay[:, col:col + 1] + upd
    o_ref[...] = jnp.concatenate(outs, axis=1)


def _ssd(xbc, dt, a_log, n_ctx):
    t = xbc.shape[0]
    lc = SSM_CHUNK
    nc = t // lc
    ncc = n_ctx // lc
    fwd = lambda s: (s, 0)
    bwd = lambda s: (jnp.where(s < ncc, ncc - 1 - s, nc - 1 - (s - ncc)), 0)
    state = pltpu.VMEM((SSM_HEADS, SSM_HEAD_DIM, SSM_STATE), F32)
    out = jax.ShapeDtypeStruct((t, SSM_WIDTH), F32)
    return pl.pallas_call(
        _ssd_kernel,
        out_shape=(out, out),
        grid=(nc,),
        in_specs=[pl.BlockSpec((lc, SSM_XBC), fwd), pl.BlockSpec((lc, LANES), fwd),
                  pl.BlockSpec((lc, SSM_XBC), bwd), pl.BlockSpec((lc, LANES), bwd),
                  pl.BlockSpec((1, LANES), lambda s: (0, 0))],
        out_specs=(pl.BlockSpec((lc, SSM_WIDTH), fwd), pl.BlockSpec((lc, SSM_WIDTH), bwd)),
        scratch_shapes=[state, state],
        compiler_params=_params("arbitrary"),
        name="ssd_scan",
    )(xbc, dt, xbc, dt, a_log)


def _merge_kernel(x_ref, gate_ref, ya_ref, yb_ref, yf_ref, yr_ref, xs_ref, z_ref,
                  bg_ref, dskip_ref, gssm_ref, wa_ref, wb_ref, wc_ref, wo_ref,
                  ml_ref, mc_ref, g2_ref, wr_ref, br_ref,
                  xo_ref, f_ref, lg_ref, *, n_ctx, tm):
    y = (yf_ref[...] + yr_ref[...] + dskip_ref[...] * xs_ref[...]) * _silu(z_ref[...])
    gw = SSM_WIDTH // SSM_GROUPS
    yc = jnp.concatenate([_rms_rows(y[:, gw * g:gw * (g + 1)], gw) for g in range(SSM_GROUPS)], axis=1)
    yc = yc * gssm_ref[...]
    gate = _sigmoid(gate_ref[...] + bg_ref[...])
    m = (gate[:, :D_MODEL] * _dot(ya_ref[...].astype(BF16), wa_ref[...])
         + gate[:, D_MODEL:2 * D_MODEL] * _dot(yb_ref[...].astype(BF16), wb_ref[...])
         + gate[:, 2 * D_MODEL:] * _dot(yc.astype(BF16), wc_ref[...]))
    out = _dot(m.astype(BF16), wo_ref[...])
    row0 = pl.program_id(0) * tm
    is_ctx = (row0 + lax.broadcasted_iota(jnp.int32, (tm, 1), 0)) < n_ctx
    gt1 = jnp.where(is_ctx, mc_ref[2:3, :], ml_ref[2:3, :])
    x_new = x_ref[...] + gt1 * out
    xo_ref[...] = x_new
    f = _modulated_norm(x_new, g2_ref[...], ml_ref[3:5, :], mc_ref[3:5, :], row0, n_ctx)
    f_ref[...] = f
    lg_ref[...] = jnp.dot(f, wr_ref[...], preferred_element_type=F32,
                          precision=lax.Precision.HIGHEST) + br_ref[...]


def _merge(x_all, u, ya, yb, yf, yr, xbc, b_gate, dskip, g_ssm, wa, wb, wc, wo, mod_lat, mod_ctx,
           g2, w_router, b_router, n_ctx):
    t, d = x_all.shape
    tm = ROW_TILE
    const = lambda i: (0, 0)
    rows = lambda i: (i, 0)
    full = lambda a: pl.BlockSpec(a.shape, const)
    return pl.pallas_call(
        functools.partial(_merge_kernel, n_ctx=n_ctx, tm=tm),
        out_shape=(jax.ShapeDtypeStruct((t, d), F32), jax.ShapeDtypeStruct((t, d), F32),
                   jax.ShapeDtypeStruct((t, LANES), F32)),
        grid=(t // tm,),
        in_specs=[pl.BlockSpec((tm, d), rows),
                  pl.BlockSpec((tm, GATE_COLS), lambda i: (i, U_GATE // GATE_COLS)),
                  pl.BlockSpec((tm, MLA_WIDTH), rows),
                  pl.BlockSpec((tm, DIF_WIDTH), rows),
                  pl.BlockSpec((tm, SSM_WIDTH), rows),
                  pl.BlockSpec((tm, SSM_WIDTH), rows),
                  pl.BlockSpec((tm, SSM_WIDTH), rows),
                  pl.BlockSpec((tm, SSM_WIDTH), lambda i: (i, U_Z // SSM_WIDTH)),
                  full(b_gate), full(dskip), full(g_ssm), full(wa), full(wb), full(wc), full(wo),
                  full(mod_lat), full(mod_ctx), full(g2), full(w_router), full(b_router)],
        out_specs=(pl.BlockSpec((tm, d), rows), pl.BlockSpec((tm, d), rows),
                   pl.BlockSpec((tm, LANES), rows)),
        compiler_params=_params("parallel"),
        name="merge",
    )(x_all, u, ya, yb, yf, yr, xbc, u, b_gate, dskip, g_ssm, wa, wb, wc, wo, mod_lat, mod_ctx,
      g2, w_router, b_router)


def _moe_kernel(be_ref, nb_ref, x_ref, wgu_ref, bgu_ref, wd_ref, bd_ref, o_ref, wgu_s, wd_s):
    b = pl.program_id(0)
    prev = be_ref[jnp.maximum(b - 1, 0)]
    fresh = jnp.logical_or(b == 0, be_ref[b] != prev)

    @pl.when(fresh)
    def _():
        wgu_s[...] = wgu_ref[...].astype(BF16)
        wd_s[...] = wd_ref[...].astype(BF16)

    @pl.when(b < nb_ref[0])
    def _():
        gu = _dot(x_ref[...].astype(BF16), wgu_s[...]) + bgu_ref[...]
        glu = jnp.minimum(gu[:, :D_FF], SWIGLU_LIMIT)
        lin = jnp.clip(gu[:, D_FF:], -SWIGLU_LIMIT, SWIGLU_LIMIT)
        act = glu * _sigmoid(SWIGLU_ALPHA * glu) * (lin + 1.0)
        o_ref[...] = _dot(act.astype(BF16), wd_s[...]) + bd_ref[...]

    @pl.when(b >= nb_ref[0])
    def _():
        o_ref[...] = jnp.zeros(o_ref.shape, F32)


def _moe_experts(block_e, n_used, x_sorted, w_gu, b_gu, w_down, b_down, layer):
    n_slots, d = x_sorted.shape
    n_blocks = n_slots // MOE_BLOCK
    grid_spec = pltpu.PrefetchScalarGridSpec(
        num_scalar_prefetch=2,
        grid=(n_blocks,),
        in_specs=[pl.BlockSpec((MOE_BLOCK, d), lambda b, be, nb: (b, 0)),
                  pl.BlockSpec((None, None, d, 2 * D_FF), lambda b, be, nb: (layer, be[b], 0, 0)),
                  pl.BlockSpec((None, None, 1, 2 * D_FF), lambda b, be, nb: (layer, be[b], 0, 0)),
                  pl.BlockSpec((None, None, D_FF, d), lambda b, be, nb: (layer, be[b], 0, 0)),
                  pl.BlockSpec((None, None, 1, d), lambda b, be, nb: (layer, be[b], 0, 0))],
        out_specs=pl.BlockSpec((MOE_BLOCK, d), lambda b, be, nb: (b, 0)),
        scratch_shapes=[pltpu.VMEM((d, 2 * D_FF), BF16), pltpu.VMEM((D_FF, d), BF16)],
    )
    return pl.pallas_call(
        _moe_kernel,
        out_shape=jax.ShapeDtypeStruct((n_slots, d), F32),
        grid_spec=grid_spec,
        compiler_params=_params("arbitrary"),
        name="moe_experts",
    )(block_e, n_used, x_sorted, w_gu, b_gu, w_down, b_down)


def _combine_kernel(x_ref, g_ref, ml_ref, mc_ref, *rest, n_ctx, tm):
    y_refs, o_ref = rest[:TOP_K], rest[TOP_K]
    is_ctx = (pl.program_id(0) * tm + lax.broadcasted_iota(jnp.int32, (tm, 1), 0)) < n_ctx
    gate2 = jnp.where(is_ctx, mc_ref[...], ml_ref[...])
    g = g_ref[...]
    y = g[:, 0:1] * y_refs[0][...]
    for k in range(1, TOP_K):
        y = y + g[:, k:k + 1] * y_refs[k][...]
    o_ref[...] = x_ref[...] + gate2 * y


def _combine(x_all, y_rows, gates, gate2_lat, gate2_ctx, n_ctx):
    t, d = x_all.shape
    tm = ROW_TILE
    rows = lambda i: (i, 0)
    const = lambda i: (0, 0)
    return pl.pallas_call(
        functools.partial(_combine_kernel, n_ctx=n_ctx, tm=tm),
        out_shape=jax.ShapeDtypeStruct((t, d), F32),
        grid=(t // tm,),
        in_specs=[pl.BlockSpec((tm, d), rows), pl.BlockSpec((tm, LANES), rows),
                  pl.BlockSpec((1, d), const), pl.BlockSpec((1, d), const)]
                 + [pl.BlockSpec((tm, d), rows)] * TOP_K,
        out_specs=pl.BlockSpec((tm, d), rows),
        compiler_params=_params("parallel"),
        name="moe_combine",
    )(x_all, gates, gate2_lat, gate2_ctx, *y_rows)


ROUTE_IDX, ROUTE_GATE, ROUTE_RANK = 0, TOP_K, 2 * TOP_K


def _router_kernel(lg_ref, o_ref, cnt_ref, run_ref, *, tm):
    @pl.when(pl.program_id(0) == 0)
    def _():
        run_ref[...] = jnp.zeros(run_ref.shape, F32)

    lane = lax.broadcasted_iota(jnp.int32, (tm, LANES), 1)
    lane_f = lane.astype(F32)
    lg = jnp.where(lane < N_EXPERTS, lg_ref[...], -jnp.inf)
    hots, vals = [], []
    for _ in range(TOP_K):
        mx = jnp.max(lg, axis=-1, keepdims=True)
        idx = jnp.min(jnp.where(lg == mx, lane_f, float(LANES)), axis=-1, keepdims=True)
        hot = lane_f == idx
        lg = jnp.where(hot, -jnp.inf, lg)
        hots.append((hot, idx))
        vals.append(mx)
    exps = [jnp.exp(v - vals[0]) for v in vals]
    inv = 1.0 / sum(exps[1:], exps[0])
    chosen = jnp.zeros((tm, LANES), F32)
    for hot, _ in hots:
        chosen = jnp.where(hot, 1.0, chosen)
    r = lax.broadcasted_iota(jnp.int32, (tm, tm), 0)
    c = lax.broadcasted_iota(jnp.int32, (tm, tm), 1)
    earlier = jnp.where(r > c, 1.0, 0.0).astype(BF16)
    before = _dot(earlier, chosen.astype(BF16)) + run_ref[0:1, :]
    out = jnp.zeros((tm, LANES), F32)
    for k, (hot, idx) in enumerate(hots):
        rank = jnp.sum(jnp.where(hot, before, 0.0), axis=-1, keepdims=True)
        out = jnp.where(lane == ROUTE_IDX + k, idx, out)
        out = jnp.where(lane == ROUTE_GATE + k, exps[k] * inv, out)
        out = jnp.where(lane == ROUTE_RANK + k, rank, out)
    o_ref[...] = out
    run_ref[...] = run_ref[...] + jnp.sum(chosen, axis=0, keepdims=True)
    cnt_ref[...] = run_ref[...]


def _router(logits):
    t = logits.shape[0]
    tm = ROW_TILE
    return pl.pallas_call(
        functools.partial(_router_kernel, tm=tm),
        out_shape=(jax.ShapeDtypeStruct((t, LANES), F32), jax.ShapeDtypeStruct((8, LANES), F32)),
        grid=(t // tm,),
        in_specs=[pl.BlockSpec((tm, LANES), lambda i: (i, 0))],
        out_specs=(pl.BlockSpec((tm, LANES), lambda i: (i, 0)), pl.BlockSpec((8, LANES), lambda i: (0, 0))),
        scratch_shapes=[pltpu.VMEM((8, LANES), F32)],
        compiler_params=_params("arbitrary"),
        name="moe_router",
    )(logits)


def _route(logits, n_tok):
    routed, counts = _router(logits)
    top_idx = routed[:, ROUTE_IDX:ROUTE_IDX + TOP_K].astype(jnp.int32)
    gates = routed[:, ROUTE_GATE:ROUTE_GATE + TOP_K]
    rank = routed[:, ROUTE_RANK:ROUTE_RANK + TOP_K].astype(jnp.int32)
    n_assign = n_tok * TOP_K
    counts = counts[0, :N_EXPERTS].astype(jnp.int32)
    padded = (counts + MOE_BLOCK - 1) // MOE_BLOCK * MOE_BLOCK
    pad_end = jnp.cumsum(padded)
    pad_start = pad_end - padded
    slot_of = pad_start[top_idx] + rank
    n_blocks = -(-(n_assign + N_EXPERTS * (MOE_BLOCK - 1)) // MOE_BLOCK)
    n_slots = n_blocks * MOE_BLOCK
    token = jnp.arange(n_assign, dtype=jnp.int32) // TOP_K
    slot_tok = jnp.zeros((n_slots,), jnp.int32).at[slot_of.reshape(-1)].set(token, unique_indices=True)
    block_start = jnp.arange(n_blocks, dtype=jnp.int32) * MOE_BLOCK
    block_e = jnp.minimum(jnp.searchsorted(pad_end, block_start, side='right'), N_EXPERTS - 1)
    n_used = (pad_end[-1] // MOE_BLOCK).reshape(1)
    return gates, slot_tok, slot_of, block_e.astype(jnp.int32), n_used.astype(jnp.int32)


def kernel(x, c, ctx, c_ctx, w_mod, b_mod, g_norm1, g_norm2, w_in, b_gate, mla_g_q, mla_w_uq, mla_g_kv, mla_w_ukv, mla_g_qn, mla_g_kn, dif_g_qn, dif_g_kn, dif_lambda, dif_g_sub, ssm_conv_w, ssm_conv_b, ssm_dt_bias, ssm_a_log, ssm_d, ssm_g_norm, w_up_mla, w_up_dif, w_up_ssm, w_out, moe_w_router, moe_b_router, moe_w_gu, moe_b_gu, moe_w_down, moe_b_down):
    assert x.shape[0] == 1 and ctx.shape[0] == 1
    depth = w_in.shape[0]
    seq = x.shape[1]
    n_ctx = ctx.shape[1]
    n_tok = n_ctx + seq
    d = D_MODEL
    assert n_ctx == KEY_CHUNK and n_tok % ROW_TILE == 0 and seq % GRID_W == 0

    x_all = jnp.concatenate([ctx[0], x[0]], axis=0)
    cc = jnp.zeros((8, d), F32).at[0].set(c[0]).at[1].set(c_ctx)
    mod = _mod_vectors(cc, w_mod, b_mod)
    mod = mod[:, :2].reshape(depth, 2, 6, d)

    rope_mla = _rope_tables(seq, n_ctx, MLA_ROPE, MLA_NOPE)
    rope_dif = _rope_tables(seq, n_ctx, DIF_HEAD_DIM, 0)
    w_in_all = _take_columns(w_in, _in_proj_columns()).astype(BF16)
    w_uq_all = _take_columns(mla_w_uq, _head_columns(MLA_HEADS, MLA_QK, 0, MLA_QK, LANES)).astype(BF16)
    w_uk_all = _take_columns(mla_w_ukv, _head_columns(MLA_HEADS, MLA_NOPE + MLA_V, 0, MLA_NOPE, LANES)).astype(BF16)
    w_uv_all = _take_columns(mla_w_ukv, _head_columns(MLA_HEADS, MLA_NOPE + MLA_V, MLA_NOPE, MLA_V, MLA_V)).astype(BF16)
    w_up_all = [w.astype(BF16) for w in (w_up_mla, w_up_dif, w_up_ssm, w_out)]

    for i in range(depth):
        lam_init = 0.8 - 0.6 * math.exp(-0.3 * i)
        mod_lat, mod_ctx = mod[i, 0], mod[i, 1]
        u = _in_proj(x_all, g_norm1[i][None], mod_lat[0:2], mod_ctx[0:2], w_in_all, i, n_ctx)

        qt, k, vt = _mla_prep(
            u, mla_g_q[i][None], w_uq_all[i], mla_g_kv[i][None], w_uk_all[i], w_uv_all[i],
            _pad_lanes(mla_g_qn[i][None]), _pad_lanes(mla_g_kn[i][None]), rope_mla)
        ya = _attention(qt, k, vt)

        qt, k, vt = _dif_prep(u, _pad_lanes(dif_g_qn[i][None]), _pad_lanes(dif_g_kn[i][None]), rope_dif)
        lam_rows = jnp.zeros((8, LANES), F32).at[:4, :DIF_HEAD_DIM].set(dif_lambda[i])
        yb = _attention(qt, k, vt, lam=lam_rows, gsub=dif_g_sub[i][None], lam_init=lam_init)

        conv_w = jnp.zeros((8, SSM_XBC), F32).at[:SSM_CONV].set(ssm_conv_w[i])
        xbc, dt = _ssm_prep(u, conv_w, ssm_conv_b[i][None], _pad_lanes(ssm_dt_bias[i].reshape(1, -1)), n_ctx)
        a_log = _pad_lanes(ssm_a_log[i].reshape(1, -1))
        yf, yr = _ssd(xbc, dt, a_log, n_ctx)

        dskip = jnp.repeat(ssm_d[i, 0] + ssm_d[i, 1], SSM_HEAD_DIM)[None]
        w_router = jnp.zeros((d, LANES), F32).at[:, :N_EXPERTS].set(moe_w_router[i])
        b_router = jnp.zeros((1, LANES), F32).at[0, :N_EXPERTS].set(moe_b_router[i])
        x_all, f, logits = _merge(
            x_all, u, ya, yb, yf, yr, xbc, b_gate[i][None], dskip, ssm_g_norm[i][None],
            w_up_all[0][i], w_up_all[1][i], w_up_all[2][i], w_up_all[3][i],
            mod_lat[0:5], mod_ctx[0:5], g_norm2[i][None],
            w_router, b_router, n_ctx)

        gates, slot_tok, slot_of, block_e, n_used = _route(logits, n_tok)
        y_slots = _moe_experts(block_e, n_used, f[slot_tok], moe_w_gu,
                               moe_b_gu.reshape(depth, N_EXPERTS, 1, 2 * D_FF), moe_w_down,
                               moe_b_down.reshape(depth, N_EXPERTS, 1, d), i)
        y_rows = [y_slots[slot_of[:, k]] for k in range(TOP_K)]
        x_all = _combine(x_all, y_rows, _pad_lanes(gates), mod_lat[5:6], mod_ctx[5:6], n_ctx)
    return x_all[n_ctx:][None]
```

```python
import functools
import math

import numpy as np
import jax
import jax.numpy as jnp
from jax import lax
from jax.experimental import pallas as pl
from jax.experimental.pallas import tpu as pltpu

F32 = jnp.float32
BF16 = jnp.bfloat16
LANES = 128
VMEM_LIMIT = 52 * 1024 * 1024

D_MODEL = 1024
EPS = 1e-6
ROPE_THETA = 10000.0
GRID_W = 64
N_BRANCH = 3

MLA_HEADS = 8
MLA_Q_RANK = 256
MLA_KV_RANK = 128
MLA_NOPE = 64
MLA_ROPE = 32
MLA_V = 64
MLA_QK = MLA_NOPE + MLA_ROPE
MLA_WIDTH = MLA_HEADS * MLA_V

DIF_HEADS = 4
DIF_HEAD_DIM = 64
DIF_WIDTH = DIF_HEADS * 2 * DIF_HEAD_DIM

SSM_HEADS = 8
SSM_HEAD_DIM = 64
SSM_WIDTH = SSM_HEADS * SSM_HEAD_DIM
SSM_GROUPS = 2
SSM_STATE = 128
SSM_CONV = 5
SSM_CHUNK = 128
SSM_XBC = SSM_WIDTH + 2 * SSM_GROUPS * SSM_STATE

N_EXPERTS = 32
TOP_K = 4
D_FF = 1024
SWIGLU_LIMIT = 7.0
SWIGLU_ALPHA = 1.702
MOE_BLOCK = 256

MLA_COLS = MLA_Q_RANK + MLA_KV_RANK + MLA_ROPE
DIF_COLS = 3 * DIF_WIDTH
SSM_COLS = SSM_WIDTH + SSM_XBC + 2 * SSM_HEADS
GATE_COLS = N_BRANCH * D_MODEL

ROW_TILE = 256
KEY_CHUNK = 256
LOG2E = 1.4426950408889634

U_GATE, U_DQ, U_DK, U_XBC, U_MLA, U_DV, U_Z, U_DT = 0, 3072, 4096, 5120, 6144, 6656, 7168, 7680
U_COLS = 8192
U_TILE_N = 1024


def _in_proj_columns():
    src = np.full((U_COLS,), -1, np.int64)
    dif0 = MLA_COLS
    ssm0 = MLA_COLS + DIF_COLS
    gate0 = ssm0 + SSM_COLS
    src[U_GATE:U_GATE + GATE_COLS] = gate0 + np.arange(GATE_COLS)
    for a in range(2 * DIF_HEADS):
        src[U_DQ + LANES * a:U_DQ + LANES * a + DIF_HEAD_DIM] = dif0 + DIF_HEAD_DIM * a + np.arange(DIF_HEAD_DIM)
        src[U_DK + LANES * a:U_DK + LANES * a + DIF_HEAD_DIM] = (dif0 + DIF_WIDTH + DIF_HEAD_DIM * a
                                                                  + np.arange(DIF_HEAD_DIM))
    src[U_DV:U_DV + DIF_WIDTH] = dif0 + 2 * DIF_WIDTH + np.arange(DIF_WIDTH)
    src[U_MLA:U_MLA + MLA_Q_RANK + MLA_KV_RANK] = np.arange(MLA_Q_RANK + MLA_KV_RANK)
    pe0 = U_MLA + MLA_Q_RANK + MLA_KV_RANK + MLA_NOPE
    src[pe0:pe0 + MLA_ROPE] = MLA_Q_RANK + MLA_KV_RANK + np.arange(MLA_ROPE)
    src[U_Z:U_Z + SSM_WIDTH] = ssm0 + np.arange(SSM_WIDTH)
    src[U_XBC:U_XBC + SSM_XBC] = ssm0 + SSM_WIDTH + np.arange(SSM_XBC)
    src[U_DT:U_DT + 2 * SSM_HEADS] = ssm0 + SSM_WIDTH + SSM_XBC + np.arange(2 * SSM_HEADS)
    return src


def _take_columns(w, src):
    cols = jnp.take(w, jnp.asarray(np.maximum(src, 0)), axis=-1)
    return jnp.where(jnp.asarray(src >= 0), cols, 0.0)


def _head_columns(n_heads, src_stride, src_off, width, dst_stride):
    src = np.full((n_heads * dst_stride,), -1, np.int64)
    for h in range(n_heads):
        src[h * dst_stride:h * dst_stride + width] = h * src_stride + src_off + np.arange(width)
    return src


def _pad_lanes(v, n=LANES):
    return jnp.pad(v, [(0, 0)] * (v.ndim - 1) + [(0, n - v.shape[-1])])


def _row_tile(n, cap):
    best = 8
    for t in range(8, cap + 1, 8):
        if n % t == 0:
            best = t
    return best


def _dot(a, b):
    return jnp.dot(a, b, preferred_element_type=F32)


def _dot_nt(a, b):
    return lax.dot_general(a, b, (((1,), (1,)), ((), ())), preferred_element_type=F32)


def _sigmoid(x):
    return 1.0 / (1.0 + jnp.exp(-x))


def _silu(x):
    return x * _sigmoid(x)


def _rms_rows(x, n):
    return x * lax.rsqrt(jnp.sum(x * x, axis=-1, keepdims=True) * (1.0 / n) + EPS)


def _params(*sem):
    return pltpu.CompilerParams(dimension_semantics=sem, vmem_limit_bytes=VMEM_LIMIT)


def _mod_kernel(a_ref, w_ref, b_ref, o_ref):
    a = _silu(a_ref[...]).astype(BF16)
    o_ref[0] = _dot(a, w_ref[0].astype(BF16)) + b_ref[0]


def _mod_vectors(cc, w_mod, b_mod):
    depth, d, n = w_mod.shape
    tn = 1536
    return pl.pallas_call(
        _mod_kernel,
        out_shape=jax.ShapeDtypeStruct((depth, 8, n), F32),
        grid=(depth, n // tn),
        in_specs=[pl.BlockSpec((8, d), lambda l, j: (0, 0)),
                  pl.BlockSpec((1, d, tn), lambda l, j: (l, 0, j)),
                  pl.BlockSpec((1, 1, tn), lambda l, j: (l, 0, j))],
        out_specs=pl.BlockSpec((1, 8, tn), lambda l, j: (l, 0, j)),
        compiler_params=_params("parallel", "parallel"),
        name="mod_vectors",
    )(cc, w_mod, b_mod.reshape(depth, 1, n))


def _modulated_norm(x, g, mod_lat, mod_ctx, row0, n_ctx):
    rows = x.shape[0]
    is_ctx = (row0 + lax.broadcasted_iota(jnp.int32, (rows, 1), 0)) < n_ctx
    shift = jnp.where(is_ctx, mod_ctx[0:1, :], mod_lat[0:1, :])
    scale = jnp.where(is_ctx, mod_ctx[1:2, :], mod_lat[1:2, :])
    return _rms_rows(x, x.shape[1]) * g * (1.0 + scale) + shift


def _in_proj_kernel(x_ref, g_ref, ml_ref, mc_ref, w_ref, o_ref, h_ref, *, n_ctx, tm):
    @pl.when(pl.program_id(1) == 0)
    def _():
        h = _modulated_norm(x_ref[...], g_ref[...], ml_ref[...], mc_ref[...], pl.program_id(0) * tm, n_ctx)
        h_ref[...] = h.astype(BF16)

    o_ref[...] = _dot(h_ref[...], w_ref[...])


def _in_proj(x_all, g, mod_lat, mod_ctx, w, layer, n_ctx):
    t, d = x_all.shape
    n = w.shape[2]
    tm = _row_tile(t, 1280)
    return pl.pallas_call(
        functools.partial(_in_proj_kernel, n_ctx=n_ctx, tm=tm),
        out_shape=jax.ShapeDtypeStruct((t, n), F32),
        grid=(t // tm, n // U_TILE_N),
        in_specs=[pl.BlockSpec((tm, d), lambda i, j: (i, 0)),
                  pl.BlockSpec((1, d), lambda i, j: (0, 0)),
                  pl.BlockSpec((2, d), lambda i, j: (0, 0)),
                  pl.BlockSpec((2, d), lambda i, j: (0, 0)),
                  pl.BlockSpec((None, d, U_TILE_N), lambda i, j: (layer, 0, j))],
        out_specs=pl.BlockSpec((tm, U_TILE_N), lambda i, j: (i, j)),
        scratch_shapes=[pltpu.VMEM((tm, d), BF16)],
        compiler_params=_params("parallel", "arbitrary"),
        name="in_proj",
    )(x_all, g, mod_lat, mod_ctx, w)


def _rope_tables(seq_len, n_ctx, rot_dim, lane0):
    n_rows = seq_len // GRID_W
    row = jnp.repeat(jnp.arange(n_rows), GRID_W).astype(F32)
    col = jnp.tile(jnp.arange(GRID_W), n_rows).astype(F32)
    axis_dim = rot_dim // 2
    half = axis_dim // 2
    inv = ROPE_THETA ** (-jnp.arange(0, axis_dim, 2, dtype=F32) / axis_dim)
    ang_r = row[:, None] * inv
    ang_c = col[:, None] * inv
    zeros = jnp.zeros((seq_len, half), F32)
    cos = jnp.concatenate([jnp.cos(ang_r), jnp.cos(ang_r), jnp.cos(ang_c), jnp.cos(ang_c)], axis=1)
    s1 = jnp.concatenate([zeros, jnp.sin(ang_r), zeros, jnp.sin(ang_c)], axis=1)
    s2 = jnp.concatenate([-jnp.sin(ang_r), zeros, -jnp.sin(ang_c), zeros], axis=1)

    def place(tab, fill):
        full = jnp.full((seq_len, LANES), fill, F32).at[:, lane0:lane0 + rot_dim].set(tab)
        ctx = jnp.full((n_ctx, LANES), fill, F32)
        return jnp.concatenate([ctx, full], axis=0)

    return place(cos, 1.0), place(s1, 0.0), place(s2, 0.0)


def _rope(x, cos, s1, s2, half):
    return x * cos + pltpu.roll(x, half, 1) * s1 + pltpu.roll(x, LANES - half, 1) * s2


def _mla_prep_kernel(u_ref, gq_ref, wuq_ref, gkv_ref, wk_ref, wv_ref, gqn_ref, gkn_ref,
                     cos_ref, s1_ref, s2_ref, qt_ref, k_ref, vt_ref):
    u = u_ref[...]
    cq = u[:, :MLA_Q_RANK]
    ckv = u[:, MLA_Q_RANK:MLA_Q_RANK + MLA_KV_RANK]
    pe = u[:, MLA_Q_RANK + MLA_KV_RANK:]
    q = _dot((_rms_rows(cq, MLA_Q_RANK) * gq_ref[...]).astype(BF16), wuq_ref[...])
    kv_in = (_rms_rows(ckv, MLA_KV_RANK) * gkv_ref[...]).astype(BF16)
    kn = _dot(kv_in, wk_ref[...])
    v = _dot(kv_in, wv_ref[...])
    cos, s1, s2 = cos_ref[...], s1_ref[...], s2_ref[...]
    half = MLA_ROPE // 4
    q_scale = MLA_QK ** -0.5 * LOG2E
    for h in range(MLA_HEADS):
        qh = _rms_rows(q[:, LANES * h:LANES * (h + 1)], MLA_QK) * gqn_ref[...]
        qh = _rope(qh, cos, s1, s2, half) * q_scale
        qt_ref[h] = qh.T.astype(BF16)
        kh = _rms_rows(kn[:, LANES * h:LANES * (h + 1)] + pe, MLA_QK) * gkn_ref[...]
        k_ref[h, 0] = _rope(kh, cos, s1, s2, half).astype(BF16)
    tm = v.shape[0]
    vt_ref[:, 0] = _with_sum_rows(v.T.reshape(MLA_HEADS, MLA_V, tm)).astype(BF16)


def _mla_prep(u, gq, wuq, gkv, wk, wv, gqn, gkn, tabs):
    t = u.shape[0]
    tm = ROW_TILE
    nt = t // tm
    const = lambda i: (0, 0)
    rows = lambda i: (i, 0)
    return pl.pallas_call(
        _mla_prep_kernel,
        out_shape=(jax.ShapeDtypeStruct((MLA_HEADS, LANES, t), BF16),
                   jax.ShapeDtypeStruct((MLA_HEADS, nt, tm, LANES), BF16),
                   jax.ShapeDtypeStruct((MLA_HEADS, nt, MLA_V + SUM_ROWS, tm), BF16)),
        grid=(nt,),
        in_specs=[pl.BlockSpec((tm, 512), lambda i: (i, U_MLA // 512)),
                  pl.BlockSpec((1, MLA_Q_RANK), const),
                  pl.BlockSpec(wuq.shape, const),
                  pl.BlockSpec((1, MLA_KV_RANK), const),
                  pl.BlockSpec(wk.shape, const),
                  pl.BlockSpec(wv.shape, const),
                  pl.BlockSpec((1, LANES), const),
                  pl.BlockSpec((1, LANES), const),
                  pl.BlockSpec((tm, LANES), rows),
                  pl.BlockSpec((tm, LANES), rows),
                  pl.BlockSpec((tm, LANES), rows)],
        out_specs=(pl.BlockSpec((MLA_HEADS, LANES, tm), lambda i: (0, 0, i)),
                   pl.BlockSpec((MLA_HEADS, 1, tm, LANES), lambda i: (0, i, 0, 0)),
                   pl.BlockSpec((MLA_HEADS, 1, MLA_V + SUM_ROWS, tm), lambda i: (0, i, 0, 0))),
        compiler_params=_params("parallel"),
        name="mla_prep",
    )(u, gq, wuq, gkv, wk, wv, gqn, gkn, *tabs)


def _dif_prep_kernel(q_ref, k_ref, v_ref, gq_ref, gk_ref, cos_ref, s1_ref, s2_ref,
                     qt_out, k_out, vt_out):
    cos, s1, s2 = cos_ref[...], s1_ref[...], s2_ref[...]
    half = DIF_HEAD_DIM // 4
    q_scale = DIF_HEAD_DIM ** -0.5 * LOG2E
    for a in range(2 * DIF_HEADS):
        qa = _rms_rows(q_ref[:, LANES * a:LANES * (a + 1)], DIF_HEAD_DIM) * gq_ref[...]
        qt_out[a] = (_rope(qa, cos, s1, s2, half) * q_scale).T.astype(BF16)
        ka = _rms_rows(k_ref[:, LANES * a:LANES * (a + 1)], DIF_HEAD_DIM) * gk_ref[...]
        k_out[a, 0] = _rope(ka, cos, s1, s2, half).astype(BF16)
    v = v_ref[...]
    vt_out[:, 0] = _with_sum_rows(v.T.reshape(DIF_HEADS, 2 * DIF_HEAD_DIM, v.shape[0])).astype(BF16)


def _dif_prep(u, gq, gk, tabs):
    t = u.shape[0]
    tm = ROW_TILE
    nt = t // tm
    nsub = 2 * DIF_HEADS
    const = lambda i: (0, 0)
    rows = lambda i: (i, 0)
    return pl.pallas_call(
        _dif_prep_kernel,
        out_shape=(jax.ShapeDtypeStruct((nsub, LANES, t), BF16),
                   jax.ShapeDtypeStruct((nsub, nt, tm, LANES), BF16),
                   jax.ShapeDtypeStruct((DIF_HEADS, nt, 2 * DIF_HEAD_DIM + SUM_ROWS, tm), BF16)),
        grid=(nt,),
        in_specs=[pl.BlockSpec((tm, 1024), lambda i: (i, U_DQ // 1024)),
                  pl.BlockSpec((tm, 1024), lambda i: (i, U_DK // 1024)),
                  pl.BlockSpec((tm, 512), lambda i: (i, U_DV // 512)),
                  pl.BlockSpec((1, LANES), const),
                  pl.BlockSpec((1, LANES), const),
                  pl.BlockSpec((tm, LANES), rows),
                  pl.BlockSpec((tm, LANES), rows),
                  pl.BlockSpec((tm, LANES), rows)],
        out_specs=(pl.BlockSpec((nsub, LANES, tm), lambda i: (0, 0, i)),
                   pl.BlockSpec((nsub, 1, tm, LANES), lambda i: (0, i, 0, 0)),
                   pl.BlockSpec((DIF_HEADS, 1, 2 * DIF_HEAD_DIM + SUM_ROWS, tm), lambda i: (0, i, 0, 0))),
        compiler_params=_params("parallel"),
        name="dif_prep",
    )(u, u, u, gq, gk, *tabs)


ATTN_GROUP = 2
ATTN_TRIP = 64
SUM_ROWS = 16


def _with_sum_rows(vt):
    heads, _, keys = vt.shape
    row = lax.broadcasted_iota(jnp.int32, (heads, SUM_ROWS, keys), 1)
    return jnp.concatenate([vt, jnp.where(row == 0, 1.0, 0.0).astype(vt.dtype)], axis=1)


def _attn_group(qt_ref, k_ref, vt_ref, v_of_sub, s_buf, p_buf, acc_ref, n_chunks):
    tq = qt_ref.shape[2]
    last = n_chunks - 1

    def scores(c):
        out = []
        for a in range(ATTN_GROUP):
            s = _dot(k_ref[a, c], qt_ref[a])
            out.append((s, jnp.max(s, axis=0, keepdims=True)))
        return out

    def stash(sc, slot):
        for a in range(ATTN_GROUP):
            s_buf[slot, a] = sc[a][0]
        return tuple(mx for (_, mx) in sc)

    def softmax(s_of, mx, slot, ms):
        new_ms, alphas = [], []
        for a in range(ATTN_GROUP):
            m_new = jnp.maximum(ms[a], mx[a])
            alphas.append(jnp.exp2(ms[a] - m_new))
            p_buf[slot, a] = jnp.exp2(s_of(a) - m_new).astype(BF16)
            new_ms.append(m_new)
        return tuple(new_ms), tuple(alphas)

    def values(c, slot, alphas):
        for a in range(ATTN_GROUP):
            acc_ref[a] = alphas[a] * acc_ref[a] + _dot(vt_ref[v_of_sub[a], c], p_buf[slot, a])

    per_trip = math.gcd(last, ATTN_TRIP)

    def trip(t, state):
        ms, alphas, mx0 = state
        c = per_trip * t + 1
        for pair in range(per_trip // 2):
            cur, nxt = pair % 2, 1 - pair % 2
            even = scores(c + 1)
            values(c - 1, 0, alphas)
            ms, alphas = softmax(lambda a: s_buf[cur, a], mx0, 1, ms)
            mx0 = stash(scores(jnp.minimum(c + 2, last)), nxt)
            values(c, 1, alphas)
            ms, alphas = softmax(lambda a: even[a][0], [mx for (_, mx) in even], 0, ms)
            c = c + 2
        return ms, alphas, mx0

    acc_ref[...] = jnp.zeros(acc_ref.shape, F32)
    ms = tuple(jnp.full((1, tq), -jnp.inf, F32) for _ in range(ATTN_GROUP))
    first = scores(0)
    ms, alphas = softmax(lambda a: first[a][0], [mx for (_, mx) in first], 0, ms)
    mx0 = stash(scores(1), 0)
    n_trips = jnp.where(pl.program_id(1) > 0, last // per_trip, 0)
    ms, alphas, _ = lax.fori_loop(0, n_trips, trip, (ms, alphas, mx0))
    values(per_trip * n_trips, 0, alphas)


def _normalised(acc_ref, a):
    dv = acc_ref.shape[1] - SUM_ROWS
    return acc_ref[a, :dv] * (1.0 / acc_ref[a, dv:dv + 1])


def _mla_attn_kernel(qt_ref, k_ref, vt_ref, o_ref, s_buf, p_buf, acc_ref, *, n_chunks):
    _attn_group(qt_ref, k_ref, vt_ref, tuple(range(ATTN_GROUP)), s_buf, p_buf, acc_ref, n_chunks)
    o = jnp.concatenate([_normalised(acc_ref, a) for a in range(ATTN_GROUP)], axis=0)
    o_ref[...] = o.T


def _dif_attn_kernel(lam_ref, gsub_ref, qt_ref, k_ref, vt_ref, o_ref, s_buf, p_buf, acc_ref,
                     *, n_chunks, lam_init):
    _attn_group(qt_ref, k_ref, vt_ref, tuple(a // 2 for a in range(ATTN_GROUP)), s_buf, p_buf,
                acc_ref, n_chunks)
    lp = lam_ref[...]
    lam = (jnp.exp(jnp.sum(lp[0:1] * lp[1:2], axis=-1, keepdims=True))
           - jnp.exp(jnp.sum(lp[2:3] * lp[3:4], axis=-1, keepdims=True)) + lam_init)
    outs = []
    for h in range(ATTN_GROUP // 2):
        o = _normalised(acc_ref, 2 * h) - lam * _normalised(acc_ref, 2 * h + 1)
        outs.append(o * lax.rsqrt(jnp.mean(o * o, axis=0, keepdims=True) + EPS))
    gsub = gsub_ref[...] * (1.0 - lam_init)
    o_ref[...] = jnp.concatenate(outs, axis=0).T * jnp.concatenate([gsub] * len(outs), axis=1)


def _attention(qt, k, vt, *, lam=None, gsub=None, lam_init=None):
    nsub, _, t = qt.shape
    n_chunks = k.shape[1]
    assert (n_chunks - 1) % 4 == 0 and nsub % ATTN_GROUP == 0
    tq = KEY_CHUNK
    dv = vt.shape[2] - SUM_ROWS
    groups = nsub // ATTN_GROUP
    n_v = vt.shape[0] // groups
    out_w = n_v * dv
    resident = dict(pipeline_mode=pl.Buffered(1))
    specs = [pl.BlockSpec((ATTN_GROUP, LANES, tq), lambda g, i: (g, 0, i)),
             pl.BlockSpec((ATTN_GROUP, n_chunks, KEY_CHUNK, LANES), lambda g, i: (g, 0, 0, 0), **resident),
             pl.BlockSpec((n_v, n_chunks, dv + SUM_ROWS, KEY_CHUNK), lambda g, i: (g, 0, 0, 0), **resident)]
    if lam is None:
        body = functools.partial(_mla_attn_kernel, n_chunks=n_chunks)
        args = (qt, k, vt)
    else:
        body = functools.partial(_dif_attn_kernel, n_chunks=n_chunks, lam_init=lam_init)
        specs = [pl.BlockSpec((8, LANES), lambda g, i: (0, 0)),
                 pl.BlockSpec((1, LANES), lambda g, i: (0, 0))] + specs
        args = (lam, gsub, qt, k, vt)
    return pl.pallas_call(
        body,
        out_shape=jax.ShapeDtypeStruct((t, groups * out_w), F32),
        grid=(groups, t // tq),
        in_specs=specs,
        out_specs=pl.BlockSpec((tq, out_w), lambda g, i: (i, g)),
        scratch_shapes=[pltpu.VMEM((2, ATTN_GROUP, KEY_CHUNK, tq), F32),
                        pltpu.VMEM((2, ATTN_GROUP, KEY_CHUNK, tq), BF16),
                        pltpu.VMEM((ATTN_GROUP, dv + SUM_ROWS, tq), F32)],
        compiler_params=_params("parallel", "arbitrary"),
        name="mla_attention" if lam is None else "dif_attention",
    )(*args)


def _ssm_prep_kernel(x_ref, prev_ref, next_ref, dt_ref, w_ref, b_ref, dtb_ref, xo_ref, dto_ref, e_ref,
                     *, n_ctx, n_tok, tm):
    row0 = pl.program_id(0) * tm
    pad = SSM_CONV // 2
    has_prev = jnp.logical_and(row0 != 0, row0 != n_ctx)
    has_next = jnp.logical_and(row0 + tm != n_ctx, row0 + tm != n_tok)
    e_ref[0:8] = jnp.where(has_prev, prev_ref[...], 0.0)
    e_ref[8:8 + tm] = x_ref[...]
    e_ref[8 + tm:16 + tm] = jnp.where(has_next, next_ref[...], 0.0)
    acc = jnp.zeros(x_ref.shape, F32) + b_ref[...]
    for k in range(SSM_CONV):
        acc = acc + w_ref[k:k + 1, :] * e_ref[pl.ds(8 - pad + k, tm), :]
    xo_ref[...] = _silu(acc)
    d = dt_ref[...] + dtb_ref[...]
    dto_ref[...] = jnp.maximum(d, 0.0) + jnp.log1p(jnp.exp(-jnp.abs(d)))


def _ssm_prep(u, conv_w, conv_b, dt_bias, n_ctx):
    t = u.shape[0]
    tm = ROW_TILE
    nt = t // tm
    cb = U_XBC // SSM_XBC
    const = lambda i: (0, 0)
    return pl.pallas_call(
        functools.partial(_ssm_prep_kernel, n_ctx=n_ctx, n_tok=t, tm=tm),
        out_shape=(jax.ShapeDtypeStruct((t, SSM_XBC), F32), jax.ShapeDtypeStruct((t, LANES), F32)),
        grid=(nt,),
        in_specs=[pl.BlockSpec((tm, SSM_XBC), lambda i: (i, cb)),
                  pl.BlockSpec((8, SSM_XBC), lambda i: (jnp.maximum(i * (tm // 8) - 1, 0), cb)),
                  pl.BlockSpec((8, SSM_XBC), lambda i: (jnp.minimum((i + 1) * (tm // 8), t // 8 - 1), cb)),
                  pl.BlockSpec((tm, LANES), lambda i: (i, U_DT // LANES)),
                  pl.BlockSpec((8, SSM_XBC), const),
                  pl.BlockSpec((1, SSM_XBC), const),
                  pl.BlockSpec((1, LANES), const)],
        out_specs=(pl.BlockSpec((tm, SSM_XBC), lambda i: (i, 0)),
                   pl.BlockSpec((tm, LANES), lambda i: (i, 0))),
        scratch_shapes=[pltpu.VMEM((tm + 16, SSM_XBC), F32)],
        compiler_params=_params("parallel"),
        name="ssm_prep",
    )(u, u, u, u, conv_w, conv_b, dt_bias)


def _ssd_kernel(xf_ref, dtf_ref, xr_ref, dtr_ref, alog_ref, of_ref, or_ref, hf_ref, hr_ref):
    @pl.when(pl.program_id(0) == 0)
    def _():
        hf_ref[...] = jnp.zeros(hf_ref.shape, F32)
        hr_ref[...] = jnp.zeros(hr_ref.shape, F32)

    _ssd_chunk(xf_ref, dtf_ref, alog_ref, of_ref, hf_ref, 0)
    _ssd_chunk(xr_ref, dtr_ref, alog_ref, or_ref, hr_ref, 1)


def _ssd_chunk(x_ref, dt_ref, alog_ref, o_ref, h_ref, direction):
    lc = SSM_CHUNK
    xbc = x_ref[...]
    dt = dt_ref[...]
    dta = dt * (-jnp.exp(alog_ref[...]))
    r = lax.broadcasted_iota(jnp.int32, (lc, lc), 0)
    c = lax.broadcasted_iota(jnp.int32, (lc, lc), 1)
    keep = (r >= c) if direction == 0 else (r <= c)
    tri = jnp.where(keep, 1.0, 0.0).astype(F32)
    cum = jnp.dot(tri, dta, preferred_element_type=F32, precision=lax.Precision.HIGHEST)
    cum_t = cum.T
    total = jnp.sum(dta, axis=0, keepdims=True)
    to_end = jnp.exp(total - cum)
    from_start = jnp.exp(cum)
    chunk_decay = jnp.exp(total)
    x_t = xbc[:, :SSM_WIDTH].T
    w_t = (dt * to_end).T
    outs = []
    for g in range(SSM_GROUPS):
        b_g = xbc[:, SSM_WIDTH + SSM_STATE * g:SSM_WIDTH + SSM_STATE * (g + 1)].astype(BF16)
        c_g = xbc[:, SSM_WIDTH + SSM_STATE * (SSM_GROUPS + g):
                  SSM_WIDTH + SSM_STATE * (SSM_GROUPS + g + 1)].astype(BF16)
        cb = _dot_nt(c_g, b_g)
        for hh in range(SSM_HEADS // SSM_GROUPS):
            h = g * (SSM_HEADS // SSM_GROUPS) + hh
            col = direction * SSM_HEADS + h
            seg = jnp.exp(jnp.where(keep, cum[:, col:col + 1] - cum_t[col:col + 1, :], -jnp.inf))
            xd = xbc[:, SSM_HEAD_DIM * h:SSM_HEAD_DIM * (h + 1)] * dt[:, col:col + 1]
            y = _dot((cb * seg).astype(BF16), xd.astype(BF16))
            state = h_ref[h]
            y_off = _dot_nt(c_g, state.astype(BF16))
            outs.append(y + y_off * from_start[:, col:col + 1])
            xw_t = x_t[SSM_HEAD_DIM * h:SSM_HEAD_DIM * (h + 1), :] * w_t[col:col + 1, :]
            upd = _dot(xw_t.astype(BF16), b_g)
            h_ref[h] = state * chunk_decay[:, col:col + 1] + upd
    o_ref[...] = jnp.concatenate(outs, axis=1)


def _ssd(xbc, dt, a_log, n_ctx):
    t = xbc.shape[0]
    lc = SSM_CHUNK
    nc = t // lc
    ncc = n_ctx // lc
    fwd = lambda s: (s, 0)
    bwd = lambda s: (jnp.where(s < ncc, ncc - 1 - s, nc - 1 - (s - ncc)), 0)
    state = pltpu.VMEM((SSM_HEADS, SSM_HEAD_DIM, SSM_STATE), F32)
    out = jax.ShapeDtypeStruct((t, SSM_WIDTH), F32)
    return pl.pallas_call(
        _ssd_kernel,
        out_shape=(out, out),
        grid=(nc,),
        in_specs=[pl.BlockSpec((lc, SSM_XBC), fwd), pl.BlockSpec((lc, LANES), fwd),
                  pl.BlockSpec((lc, SSM_XBC), bwd), pl.BlockSpec((lc, LANES), bwd),
                  pl.BlockSpec((1, LANES), lambda s: (0, 0))],
        out_specs=(pl.BlockSpec((lc, SSM_WIDTH), fwd), pl.BlockSpec((lc, SSM_WIDTH), bwd)),
        scratch_shapes=[state, state],
        compiler_params=_params("arbitrary"),
        name="ssd_scan",
    )(xbc, dt, xbc, dt, a_log)


def _merge_kernel(x_ref, gate_ref, ya_ref, yb_ref, yf_ref, yr_ref, xs_ref, z_ref,
                  bg_ref, dskip_ref, gssm_ref, wa_ref, wb_ref, wc_ref, wo_ref,
                  ml_ref, mc_ref, g2_ref, wr_ref, br_ref,
                  xo_ref, f_ref, lg_ref, *, n_ctx, tm):
    y = (yf_ref[...] + yr_ref[...] + dskip_ref[...] * xs_ref[...]) * _silu(z_ref[...])
    gw = SSM_WIDTH // SSM_GROUPS
    yc = jnp.concatenate([_rms_rows(y[:, gw * g:gw * (g + 1)], gw) for g in range(SSM_GROUPS)], axis=1)
    yc = yc * gssm_ref[...]
    gate = _sigmoid(gate_ref[...] + bg_ref[...])
    m = (gate[:, :D_MODEL] * _dot(ya_ref[...].astype(BF16), wa_ref[...])
         + gate[:, D_MODEL:2 * D_MODEL] * _dot(yb_ref[...].astype(BF16), wb_ref[...])
         + gate[:, 2 * D_MODEL:] * _dot(yc.astype(BF16), wc_ref[...]))
    out = _dot(m.astype(BF16), wo_ref[...])
    row0 = pl.program_id(0) * tm
    is_ctx = (row0 + lax.broadcasted_iota(jnp.int32, (tm, 1), 0)) < n_ctx
    gt1 = jnp.where(is_ctx, mc_ref[2:3, :], ml_ref[2:3, :])
    x_new = x_ref[...] + gt1 * out
    xo_ref[...] = x_new
    f = _modulated_norm(x_new, g2_ref[...], ml_ref[3:5, :], mc_ref[3:5, :], row0, n_ctx)
    f_ref[...] = f
    lg_ref[...] = jnp.dot(f, wr_ref[...], preferred_element_type=F32,
                          precision=lax.Precision.HIGHEST) + br_ref[...]


def _merge(x_all, u, ya, yb, yf, yr, xbc, b_gate, dskip, g_ssm, wa, wb, wc, wo, mod_lat, mod_ctx,
           g2, w_router, b_router, n_ctx):
    t, d = x_all.shape
    tm = ROW_TILE
    const = lambda i: (0, 0)
    rows = lambda i: (i, 0)
    full = lambda a: pl.BlockSpec(a.shape, const)
    return pl.pallas_call(
        functools.partial(_merge_kernel, n_ctx=n_ctx, tm=tm),
        out_shape=(jax.ShapeDtypeStruct((t, d), F32), jax.ShapeDtypeStruct((t, d), F32),
                   jax.ShapeDtypeStruct((t, LANES), F32)),
        grid=(t // tm,),
        in_specs=[pl.BlockSpec((tm, d), rows),
                  pl.BlockSpec((tm, GATE_COLS), lambda i: (i, U_GATE // GATE_COLS)),
                  pl.BlockSpec((tm, MLA_WIDTH), rows),
                  pl.BlockSpec((tm, DIF_WIDTH), rows),
                  pl.BlockSpec((tm, SSM_WIDTH), rows),
                  pl.BlockSpec((tm, SSM_WIDTH), rows),
                  pl.BlockSpec((tm, SSM_WIDTH), rows),
                  pl.BlockSpec((tm, SSM_WIDTH), lambda i: (i, U_Z // SSM_WIDTH)),
                  full(b_gate), full(dskip), full(g_ssm), full(wa), full(wb), full(wc), full(wo),
                  full(mod_lat), full(mod_ctx), full(g2), full(w_router), full(b_router)],
        out_specs=(pl.BlockSpec((tm, d), rows), pl.BlockSpec((tm, d), rows),
                   pl.BlockSpec((tm, LANES), rows)),
        compiler_params=_params("parallel"),
        name="merge",
    )(x_all, u, ya, yb, yf, yr, xbc, u, b_gate, dskip, g_ssm, wa, wb, wc, wo, mod_lat, mod_ctx,
      g2, w_router, b_router)


def _moe_kernel(be_ref, nb_ref, x_ref, wgu_ref, bgu_ref, wd_ref, bd_ref, o_ref, wgu_s, wd_s):
    b = pl.program_id(0)
    prev = be_ref[jnp.maximum(b - 1, 0)]
    fresh = jnp.logical_or(b == 0, be_ref[b] != prev)

    @pl.when(fresh)
    def _():
        wgu_s[...] = wgu_ref[...].astype(BF16)
        wd_s[...] = wd_ref[...].astype(BF16)

    @pl.when(b < nb_ref[0])
    def _():
        gu = _dot(x_ref[...].astype(BF16), wgu_s[...]) + bgu_ref[...]
        glu = jnp.minimum(gu[:, :D_FF], SWIGLU_LIMIT)
        lin = jnp.clip(gu[:, D_FF:], -SWIGLU_LIMIT, SWIGLU_LIMIT)
        act = glu * _sigmoid(SWIGLU_ALPHA * glu) * (lin + 1.0)
        o_ref[...] = _dot(act.astype(BF16), wd_s[...]) + bd_ref[...]

    @pl.when(b >= nb_ref[0])
    def _():
        o_ref[...] = jnp.zeros(o_ref.shape, F32)


def _moe_experts(block_e, n_used, x_sorted, w_gu, b_gu, w_down, b_down, layer):
    n_slots, d = x_sorted.shape
    n_blocks = n_slots // MOE_BLOCK
    grid_spec = pltpu.PrefetchScalarGridSpec(
        num_scalar_prefetch=2,
        grid=(n_blocks,),
        in_specs=[pl.BlockSpec((MOE_BLOCK, d), lambda b, be, nb: (b, 0)),
                  pl.BlockSpec((None, None, d, 2 * D_FF), lambda b, be, nb: (layer, be[b], 0, 0)),
                  pl.BlockSpec((None, None, 1, 2 * D_FF), lambda b, be, nb: (layer, be[b], 0, 0)),
                  pl.BlockSpec((None, None, D_FF, d), lambda b, be, nb: (layer, be[b], 0, 0)),
                  pl.BlockSpec((None, None, 1, d), lambda b, be, nb: (layer, be[b], 0, 0))],
        out_specs=pl.BlockSpec((MOE_BLOCK, d), lambda b, be, nb: (b, 0)),
        scratch_shapes=[pltpu.VMEM((d, 2 * D_FF), BF16), pltpu.VMEM((D_FF, d), BF16)],
    )
    return pl.pallas_call(
        _moe_kernel,
        out_shape=jax.ShapeDtypeStruct((n_slots, d), F32),
        grid_spec=grid_spec,
        compiler_params=_params("arbitrary"),
        name="moe_experts",
    )(block_e, n_used, x_sorted, w_gu, b_gu, w_down, b_down)


def _combine_kernel(x_ref, g_ref, ml_ref, mc_ref, *rest, n_ctx, tm):
    y_refs, o_ref = rest[:TOP_K], rest[TOP_K]
    is_ctx = (pl.program_id(0) * tm + lax.broadcasted_iota(jnp.int32, (tm, 1), 0)) < n_ctx
    gate2 = jnp.where(is_ctx, mc_ref[...], ml_ref[...])
    g = g_ref[...]
    y = g[:, 0:1] * y_refs[0][...]
    for k in range(1, TOP_K):
        y = y + g[:, k:k + 1] * y_refs[k][...]
    o_ref[...] = x_ref[...] + gate2 * y


def _combine(x_all, y_rows, gates, gate2_lat, gate2_ctx, n_ctx):
    t, d = x_all.shape
    tm = ROW_TILE
    rows = lambda i: (i, 0)
    const = lambda i: (0, 0)
    return pl.pallas_call(
        functools.partial(_combine_kernel, n_ctx=n_ctx, tm=tm),
        out_shape=jax.ShapeDtypeStruct((t, d), F32),
        grid=(t // tm,),
        in_specs=[pl.BlockSpec((tm, d), rows), pl.BlockSpec((tm, LANES), rows),
                  pl.BlockSpec((1, d), const), pl.BlockSpec((1, d), const)]
                 + [pl.BlockSpec((tm, d), rows)] * TOP_K,
        out_specs=pl.BlockSpec((tm, d), rows),
        compiler_params=_params("parallel"),
        name="moe_combine",
    )(x_all, gates, gate2_lat, gate2_ctx, *y_rows)


ROUTE_IDX, ROUTE_GATE, ROUTE_RANK = 0, TOP_K, 2 * TOP_K


def _router_kernel(lg_ref, o_ref, cnt_ref, run_ref, *, tm):
    @pl.when(pl.program_id(0) == 0)
    def _():
        run_ref[...] = jnp.zeros(run_ref.shape, F32)

    lane = lax.broadcasted_iota(jnp.int32, (tm, LANES), 1)
    lane_f = lane.astype(F32)
    lg = jnp.where(lane < N_EXPERTS, lg_ref[...], -jnp.inf)
    hots, vals = [], []
    for _ in range(TOP_K):
        mx = jnp.max(lg, axis=-1, keepdims=True)
        idx = jnp.min(jnp.where(lg == mx, lane_f, float(LANES)), axis=-1, keepdims=True)
        hot = lane_f == idx
        lg = jnp.where(hot, -jnp.inf, lg)
        hots.append((hot, idx))
        vals.append(mx)
    exps = [jnp.exp(v - vals[0]) for v in vals]
    inv = 1.0 / sum(exps[1:], exps[0])
    chosen = jnp.zeros((tm, LANES), F32)
    for hot, _ in hots:
        chosen = jnp.where(hot, 1.0, chosen)
    r = lax.broadcasted_iota(jnp.int32, (tm, tm), 0)
    c = lax.broadcasted_iota(jnp.int32, (tm, tm), 1)
    earlier = jnp.where(r > c, 1.0, 0.0).astype(BF16)
    before = _dot(earlier, chosen.astype(BF16)) + run_ref[0:1, :]
    out = jnp.zeros((tm, LANES), F32)
    for k, (hot, idx) in enumerate(hots):
        rank = jnp.sum(jnp.where(hot, before, 0.0), axis=-1, keepdims=True)
        out = jnp.where(lane == ROUTE_IDX + k, idx, out)
        out = jnp.where(lane == ROUTE_GATE + k, exps[k] * inv, out)
        out = jnp.where(lane == ROUTE_RANK + k, rank, out)
    o_ref[...] = out
    run_ref[...] = run_ref[...] + jnp.sum(chosen, axis=0, keepdims=True)
    cnt_ref[...] = run_ref[...]


def _router(logits):
    t = logits.shape[0]
    tm = ROW_TILE
    return pl.pallas_call(
        functools.partial(_router_kernel, tm=tm),
        out_shape=(jax.ShapeDtypeStruct((t, LANES), F32), jax.ShapeDtypeStruct((8, LANES), F32)),
        grid=(t // tm,),
        in_specs=[pl.BlockSpec((tm, LANES), lambda i: (i, 0))],
        out_specs=(pl.BlockSpec((tm, LANES), lambda i: (i, 0)), pl.BlockSpec((8, LANES), lambda i: (0, 0))),
        scratch_shapes=[pltpu.VMEM((8, LANES), F32)],
        compiler_params=_params("arbitrary"),
        name="moe_router",
    )(logits)


def _route(logits, n_tok):
    routed, counts = _router(logits)
    top_idx = routed[:, ROUTE_IDX:ROUTE_IDX + TOP_K].astype(jnp.int32)
    gates = routed[:, ROUTE_GATE:ROUTE_GATE + TOP_K]
    rank = routed[:, ROUTE_RANK:ROUTE_RANK + TOP_K].astype(jnp.int32)
    n_assign = n_tok * TOP_K
    counts = counts[0, :N_EXPERTS].astype(jnp.int32)
    padded = (counts + MOE_BLOCK - 1) // MOE_BLOCK * MOE_BLOCK
    pad_end = jnp.cumsum(padded)
    pad_start = pad_end - padded
    slot_of = pad_start[top_idx] + rank
    n_blocks = -(-(n_assign + N_EXPERTS * (MOE_BLOCK - 1)) // MOE_BLOCK)
    n_slots = n_blocks * MOE_BLOCK
    token = jnp.arange(n_assign, dtype=jnp.int32) // TOP_K
    slot_tok = jnp.zeros((n_slots,), jnp.int32).at[slot_of.reshape(-1)].set(token, unique_indices=True)
    block_start = jnp.arange(n_blocks, dtype=jnp.int32) * MOE_BLOCK
    block_e = jnp.minimum(jnp.searchsorted(pad_end, block_start, side='right'), N_EXPERTS - 1)
    n_used = (pad_end[-1] // MOE_BLOCK).reshape(1)
    return gates, slot_tok, slot_of, block_e.astype(jnp.int32), n_used.astype(jnp.int32)


def kernel(x, c, ctx, c_ctx, w_mod, b_mod, g_norm1, g_norm2, w_in, b_gate, mla_g_q, mla_w_uq, mla_g_kv, mla_w_ukv, mla_g_qn, mla_g_kn, dif_g_qn, dif_g_kn, dif_lambda, dif_g_sub, ssm_conv_w, ssm_conv_b, ssm_dt_bias, ssm_a_log, ssm_d, ssm_g_norm, w_up_mla, w_up_dif, w_up_ssm, w_out, moe_w_router, moe_b_router, moe_w_gu, moe_b_gu, moe_w_down, moe_b_down):
    assert x.shape[0] == 1 and ctx.shape[0] == 1
    depth = w_in.shape[0]
    seq = x.shape[1]
    n_ctx = ctx.shape[1]
    n_tok = n_ctx + seq
    d = D_MODEL
    assert n_ctx == KEY_CHUNK and n_tok % ROW_TILE == 0 and seq % GRID_W == 0

    x_all = jnp.concatenate([ctx[0], x[0]], axis=0)
    cc = jnp.zeros((8, d), F32).at[0].set(c[0]).at[1].set(c_ctx)
    mod = _mod_vectors(cc, w_mod, b_mod)
    mod = mod[:, :2].reshape(depth, 2, 6, d)

    rope_mla = _rope_tables(seq, n_ctx, MLA_ROPE, MLA_NOPE)
    rope_dif = _rope_tables(seq, n_ctx, DIF_HEAD_DIM, 0)
    w_in_all = _take_columns(w_in, _in_proj_columns()).astype(BF16)
    w_uq_all = _take_columns(mla_w_uq, _head_columns(MLA_HEADS, MLA_QK, 0, MLA_QK, LANES)).astype(BF16)
    w_uk_all = _take_columns(mla_w_ukv, _head_columns(MLA_HEADS, MLA_NOPE + MLA_V, 0, MLA_NOPE, LANES)).astype(BF16)
    w_uv_all = _take_columns(mla_w_ukv, _head_columns(MLA_HEADS, MLA_NOPE + MLA_V, MLA_NOPE, MLA_V, MLA_V)).astype(BF16)
    w_up_all = [w.astype(BF16) for w in (w_up_mla, w_up_dif, w_up_ssm, w_out)]

    for i in range(depth):
        lam_init = 0.8 - 0.6 * math.exp(-0.3 * i)
        mod_lat, mod_ctx = mod[i, 0], mod[i, 1]
        u = _in_proj(x_all, g_norm1[i][None], mod_lat[0:2], mod_ctx[0:2], w_in_all, i, n_ctx)

        qt, k, vt = _mla_prep(
            u, mla_g_q[i][None], w_uq_all[i], mla_g_kv[i][None], w_uk_all[i], w_uv_all[i],
            _pad_lanes(mla_g_qn[i][None]), _pad_lanes(mla_g_kn[i][None]), rope_mla)
        ya = _attention(qt, k, vt)

        qt, k, vt = _dif_prep(u, _pad_lanes(dif_g_qn[i][None]), _pad_lanes(dif_g_kn[i][None]), rope_dif)
        lam_rows = jnp.zeros((8, LANES), F32).at[:4, :DIF_HEAD_DIM].set(dif_lambda[i])
        yb = _attention(qt, k, vt, lam=lam_rows, gsub=dif_g_sub[i][None], lam_init=lam_init)

        conv_w = jnp.zeros((8, SSM_XBC), F32).at[:SSM_CONV].set(ssm_conv_w[i])
        xbc, dt = _ssm_prep(u, conv_w, ssm_conv_b[i][None], _pad_lanes(ssm_dt_bias[i].reshape(1, -1)), n_ctx)
        a_log = _pad_lanes(ssm_a_log[i].reshape(1, -1))
        yf, yr = _ssd(xbc, dt, a_log, n_ctx)

        dskip = jnp.repeat(ssm_d[i, 0] + ssm_d[i, 1], SSM_HEAD_DIM)[None]
        w_router = jnp.zeros((d, LANES), F32).at[:, :N_EXPERTS].set(moe_w_router[i])
        b_router = jnp.zeros((1, LANES), F32).at[0, :N_EXPERTS].set(moe_b_router[i])
        x_all, f, logits = _merge(
            x_all, u, ya, yb, yf, yr, xbc, b_gate[i][None], dskip, ssm_g_norm[i][None],
            w_up_all[0][i], w_up_all[1][i], w_up_all[2][i], w_up_all[3][i],
            mod_lat[0:5], mod_ctx[0:5], g_norm2[i][None],
            w_router, b_router, n_ctx)

        gates, slot_tok, slot_of, block_e, n_used = _route(logits, n_tok)
        y_slots = _moe_experts(block_e, n_used, f[slot_tok], moe_w_gu,
                               moe_b_gu.reshape(depth, N_EXPERTS, 1, 2 * D_FF), moe_w_down,
                               moe_b_down.reshape(depth, N_EXPERTS, 1, d), i)
        y_rows = [y_slots[slot_of[:, k]] for k in range(TOP_K)]
        x_all = _combine(x_all, y_rows, _pad_lanes(gates), mod_lat[5:6], mod_ctx[5:6], n_ctx)
    return x_all[n_ctx:][None]
```

```python
import functools
import math

import numpy as np
import jax
import jax.numpy as jnp
from jax import lax
from jax.experimental import pallas as pl
from jax.experimental.pallas import tpu as pltpu

F32 = jnp.float32
BF16 = jnp.bfloat16
LANES = 128
VMEM_LIMIT = 52 * 1024 * 1024

D_MODEL = 1024
EPS = 1e-6
ROPE_THETA = 10000.0
GRID_W = 64
N_BRANCH = 3

MLA_HEADS = 8
MLA_Q_RANK = 256
MLA_KV_RANK = 128
MLA_NOPE = 64
MLA_ROPE = 32
MLA_V = 64
MLA_QK = MLA_NOPE + MLA_ROPE
MLA_WIDTH = MLA_HEADS * MLA_V

DIF_HEADS = 4
DIF_HEAD_DIM = 64
DIF_WIDTH = DIF_HEADS * 2 * DIF_HEAD_DIM

SSM_HEADS = 8
SSM_HEAD_DIM = 64
SSM_WIDTH = SSM_HEADS * SSM_HEAD_DIM
SSM_GROUPS = 2
SSM_STATE = 128
SSM_CONV = 5
SSM_CHUNK = 128
SSM_XBC = SSM_WIDTH + 2 * SSM_GROUPS * SSM_STATE

N_EXPERTS = 32
TOP_K = 4
D_FF = 1024
SWIGLU_LIMIT = 7.0
SWIGLU_ALPHA = 1.702
MOE_BLOCK = 256

MLA_COLS = MLA_Q_RANK + MLA_KV_RANK + MLA_ROPE
DIF_COLS = 3 * DIF_WIDTH
SSM_COLS = SSM_WIDTH + SSM_XBC + 2 * SSM_HEADS
GATE_COLS = N_BRANCH * D_MODEL

ROW_TILE = 256
KEY_CHUNK = 256
LOG2E = 1.4426950408889634

U_GATE, U_DQ, U_DK, U_XBC, U_MLA, U_DV, U_Z, U_DT = 0, 3072, 4096, 5120, 6144, 6656, 7168, 7680
U_COLS = 8192
U_TILE_N = 1024


def _in_proj_columns():
    src = np.full((U_COLS,), -1, np.int64)
    dif0 = MLA_COLS
    ssm0 = MLA_COLS + DIF_COLS
    gate0 = ssm0 + SSM_COLS
    src[U_GATE:U_GATE + GATE_COLS] = gate0 + np.arange(GATE_COLS)
    for a in range(2 * DIF_HEADS):
        src[U_DQ + LANES * a:U_DQ + LANES * a + DIF_HEAD_DIM] = dif0 + DIF_HEAD_DIM * a + np.arange(DIF_HEAD_DIM)
        src[U_DK + LANES * a:U_DK + LANES * a + DIF_HEAD_DIM] = (dif0 + DIF_WIDTH + DIF_HEAD_DIM * a
                                                                  + np.arange(DIF_HEAD_DIM))
    src[U_DV:U_DV + DIF_WIDTH] = dif0 + 2 * DIF_WIDTH + np.arange(DIF_WIDTH)
    src[U_MLA:U_MLA + MLA_Q_RANK + MLA_KV_RANK] = np.arange(MLA_Q_RANK + MLA_KV_RANK)
    pe0 = U_MLA + MLA_Q_RANK + MLA_KV_RANK + MLA_NOPE
    src[pe0:pe0 + MLA_ROPE] = MLA_Q_RANK + MLA_KV_RANK + np.arange(MLA_ROPE)
    src[U_Z:U_Z + SSM_WIDTH] = ssm0 + np.arange(SSM_WIDTH)
    src[U_XBC:U_XBC + SSM_XBC] = ssm0 + SSM_WIDTH + np.arange(SSM_XBC)
    src[U_DT:U_DT + 2 * SSM_HEADS] = ssm0 + SSM_WIDTH + SSM_XBC + np.arange(2 * SSM_HEADS)
    return src


def _take_columns(w, src):
    cols = jnp.take(w, jnp.asarray(np.maximum(src, 0)), axis=-1)
    return jnp.where(jnp.asarray(src >= 0), cols, 0.0)


def _head_columns(n_heads, src_stride, src_off, width, dst_stride):
    src = np.full((n_heads * dst_stride,), -1, np.int64)
    for h in range(n_heads):
        src[h * dst_stride:h * dst_stride + width] = h * src_stride + src_off + np.arange(width)
    return src


def _pad_lanes(v, n=LANES):
    return jnp.pad(v, [(0, 0)] * (v.ndim - 1) + [(0, n - v.shape[-1])])


def _row_tile(n, cap):
    best = 8
    for t in range(8, cap + 1, 8):
        if n % t == 0:
            best = t
    return best


def _dot(a, b):
    return jnp.dot(a, b, preferred_element_type=F32)


def _dot_nt(a, b):
    return lax.dot_general(a, b, (((1,), (1,)), ((), ())), preferred_element_type=F32)


def _sigmoid(x):
    return 1.0 / (1.0 + jnp.exp(-x))


def _silu(x):
    return x * _sigmoid(x)


def _rms_rows(x, n):
    return x * lax.rsqrt(jnp.sum(x * x, axis=-1, keepdims=True) * (1.0 / n) + EPS)


def _params(*sem):
    return pltpu.CompilerParams(dimension_semantics=sem, vmem_limit_bytes=VMEM_LIMIT)


def _mod_kernel(a_ref, w_ref, b_ref, o_ref):
    a = _silu(a_ref[...]).astype(BF16)
    o_ref[0] = _dot(a, w_ref[0].astype(BF16)) + b_ref[0]


def _mod_vectors(cc, w_mod, b_mod):
    depth, d, n = w_mod.shape
    tn = 1536
    return pl.pallas_call(
        _mod_kernel,
        out_shape=jax.ShapeDtypeStruct((depth, 8, n), F32),
        grid=(depth, n // tn),
        in_specs=[pl.BlockSpec((8, d), lambda l, j: (0, 0)),
                  pl.BlockSpec((1, d, tn), lambda l, j: (l, 0, j)),
                  pl.BlockSpec((1, 1, tn), lambda l, j: (l, 0, j))],
        out_specs=pl.BlockSpec((1, 8, tn), lambda l, j: (l, 0, j)),
        compiler_params=_params("parallel", "parallel"),
        name="mod_vectors",
    )(cc, w_mod, b_mod.reshape(depth, 1, n))


def _modulated_norm(x, g, mod_lat, mod_ctx, row0, n_ctx):
    rows = x.shape[0]
    is_ctx = (row0 + lax.broadcasted_iota(jnp.int32, (rows, 1), 0)) < n_ctx
    shift = jnp.where(is_ctx, mod_ctx[0:1, :], mod_lat[0:1, :])
    scale = jnp.where(is_ctx, mod_ctx[1:2, :], mod_lat[1:2, :])
    return _rms_rows(x, x.shape[1]) * g * (1.0 + scale) + shift


def _in_proj_kernel(x_ref, g_ref, ml_ref, mc_ref, w_ref, o_ref, h_ref, *, n_ctx, tm):
    @pl.when(pl.program_id(1) == 0)
    def _():
        h = _modulated_norm(x_ref[...], g_ref[...], ml_ref[...], mc_ref[...], pl.program_id(0) * tm, n_ctx)
        h_ref[...] = h.astype(BF16)

    o_ref[...] = _dot(h_ref[...], w_ref[...])


def _in_proj(x_all, g, mod_lat, mod_ctx, w, layer, n_ctx):
    t, d = x_all.shape
    n = w.shape[2]
    tm = _row_tile(t, 1280)
    return pl.pallas_call(
        functools.partial(_in_proj_kernel, n_ctx=n_ctx, tm=tm),
        out_shape=jax.ShapeDtypeStruct((t, n), F32),
        grid=(t // tm, n // U_TILE_N),
        in_specs=[pl.BlockSpec((tm, d), lambda i, j: (i, 0)),
                  pl.BlockSpec((1, d), lambda i, j: (0, 0)),
                  pl.BlockSpec((2, d), lambda i, j: (0, 0)),
                  pl.BlockSpec((2, d), lambda i, j: (0, 0)),
                  pl.BlockSpec((None, d, U_TILE_N), lambda i, j: (layer, 0, j))],
        out_specs=pl.BlockSpec((tm, U_TILE_N), lambda i, j: (i, j)),
        scratch_shapes=[pltpu.VMEM((tm, d), BF16)],
        compiler_params=_params("parallel", "arbitrary"),
        name="in_proj",
    )(x_all, g, mod_lat, mod_ctx, w)


def _rope_tables(seq_len, n_ctx, rot_dim, lane0):
    n_rows = seq_len // GRID_W
    row = jnp.repeat(jnp.arange(n_rows), GRID_W).astype(F32)
    col = jnp.tile(jnp.arange(GRID_W), n_rows).astype(F32)
    axis_dim = rot_dim // 2
    half = axis_dim // 2
    inv = ROPE_THETA ** (-jnp.arange(0, axis_dim, 2, dtype=F32) / axis_dim)
    ang_r = row[:, None] * inv
    ang_c = col[:, None] * inv
    zeros = jnp.zeros((seq_len, half), F32)
    cos = jnp.concatenate([jnp.cos(ang_r), jnp.cos(ang_r), jnp.cos(ang_c), jnp.cos(ang_c)], axis=1)
    s1 = jnp.concatenate([zeros, jnp.sin(ang_r), zeros, jnp.sin(ang_c)], axis=1)
    s2 = jnp.concatenate([-jnp.sin(ang_r), zeros, -jnp.sin(ang_c), zeros], axis=1)

    def place(tab, fill):
        full = jnp.full((seq_len, LANES), fill, F32).at[:, lane0:lane0 + rot_dim].set(tab)
        ctx = jnp.full((n_ctx, LANES), fill, F32)
        return jnp.concatenate([ctx, full], axis=0)

    return place(cos, 1.0), place(s1, 0.0), place(s2, 0.0)


def _norm_rope_blocks(blocks, gains, n, cos, s1, s2, half):
    sums = [jnp.sum(x * x, axis=-1, keepdims=True) for x in blocks]
    inv = [lax.rsqrt(s * (1.0 / n) + EPS) for s in sums]
    normed = [x * r * g for x, r, g in zip(blocks, inv, gains)]
    fwd = [pltpu.roll(y, half, 1) for y in normed]
    bwd = [pltpu.roll(y, LANES - half, 1) for y in normed]
    return [y * cos + a * s1 + b * s2 for y, a, b in zip(normed, fwd, bwd)]


def _mla_prep_kernel(u_ref, gq_ref, wuq_ref, gkv_ref, wk_ref, wv_ref, gqn_ref, gkn_ref,
                     cos_ref, s1_ref, s2_ref, qt_ref, k_ref, vt_ref):
    u = u_ref[...]
    cq = u[:, :MLA_Q_RANK]
    ckv = u[:, MLA_Q_RANK:MLA_Q_RANK + MLA_KV_RANK]
    pe = u[:, MLA_Q_RANK + MLA_KV_RANK:]
    q = _dot((_rms_rows(cq, MLA_Q_RANK) * gq_ref[...]).astype(BF16), wuq_ref[...])
    kv_in = (_rms_rows(ckv, MLA_KV_RANK) * gkv_ref[...]).astype(BF16)
    kn = _dot(kv_in, wk_ref[...])
    v = _dot(kv_in, wv_ref[...])
    cos, s1, s2 = cos_ref[...], s1_ref[...], s2_ref[...]
    half = MLA_ROPE // 4
    q_scale = MLA_QK ** -0.5 * LOG2E
    n = MLA_HEADS
    blocks = ([q[:, LANES * h:LANES * (h + 1)] for h in range(n)]
              + [kn[:, LANES * h:LANES * (h + 1)] + pe for h in range(n)])
    roped = _norm_rope_blocks(blocks, [gqn_ref[...]] * n + [gkn_ref[...]] * n, MLA_QK, cos, s1, s2, half)
    for h in range(n):
        qt_ref[h] = (roped[h] * q_scale).T.astype(BF16)
        k_ref[h, 0] = roped[n + h].astype(BF16)
    tm = v.shape[0]
    vt_ref[:, 0] = _with_sum_rows(v.T.reshape(MLA_HEADS, MLA_V, tm)).astype(BF16)


def _mla_prep(u, gq, wuq, gkv, wk, wv, gqn, gkn, tabs):
    t = u.shape[0]
    tm = ROW_TILE
    nt = t // tm
    const = lambda i: (0, 0)
    rows = lambda i: (i, 0)
    return pl.pallas_call(
        _mla_prep_kernel,
        out_shape=(jax.ShapeDtypeStruct((MLA_HEADS, LANES, t), BF16),
                   jax.ShapeDtypeStruct((MLA_HEADS, nt, tm, LANES), BF16),
                   jax.ShapeDtypeStruct((MLA_HEADS, nt, MLA_V + SUM_ROWS, tm), BF16)),
        grid=(nt,),
        in_specs=[pl.BlockSpec((tm, 512), lambda i: (i, U_MLA // 512)),
                  pl.BlockSpec((1, MLA_Q_RANK), const),
                  pl.BlockSpec(wuq.shape, const),
                  pl.BlockSpec((1, MLA_KV_RANK), const),
                  pl.BlockSpec(wk.shape, const),
                  pl.BlockSpec(wv.shape, const),
                  pl.BlockSpec((1, LANES), const),
                  pl.BlockSpec((1, LANES), const),
                  pl.BlockSpec((tm, LANES), rows),
                  pl.BlockSpec((tm, LANES), rows),
                  pl.BlockSpec((tm, LANES), rows)],
        out_specs=(pl.BlockSpec((MLA_HEADS, LANES, tm), lambda i: (0, 0, i)),
                   pl.BlockSpec((MLA_HEADS, 1, tm, LANES), lambda i: (0, i, 0, 0)),
                   pl.BlockSpec((MLA_HEADS, 1, MLA_V + SUM_ROWS, tm), lambda i: (0, i, 0, 0))),
        compiler_params=_params("parallel"),
        name="mla_prep",
    )(u, gq, wuq, gkv, wk, wv, gqn, gkn, *tabs)


def _dif_prep_kernel(q_ref, k_ref, v_ref, gq_ref, gk_ref, cos_ref, s1_ref, s2_ref,
                     qt_out, k_out, vt_out):
    cos, s1, s2 = cos_ref[...], s1_ref[...], s2_ref[...]
    half = DIF_HEAD_DIM // 4
    q_scale = DIF_HEAD_DIM ** -0.5 * LOG2E
    n = 2 * DIF_HEADS
    blocks = ([q_ref[:, LANES * a:LANES * (a + 1)] for a in range(n)]
              + [k_ref[:, LANES * a:LANES * (a + 1)] for a in range(n)])
    roped = _norm_rope_blocks(blocks, [gq_ref[...]] * n + [gk_ref[...]] * n, DIF_HEAD_DIM, cos, s1, s2, half)
    for a in range(n):
        qt_out[a] = (roped[a] * q_scale).T.astype(BF16)
        k_out[a, 0] = roped[n + a].astype(BF16)
    v = v_ref[...]
    vt_out[:, 0] = _with_sum_rows(v.T.reshape(DIF_HEADS, 2 * DIF_HEAD_DIM, v.shape[0])).astype(BF16)


def _dif_prep(u, gq, gk, tabs):
    t = u.shape[0]
    tm = ROW_TILE
    nt = t // tm
    nsub = 2 * DIF_HEADS
    const = lambda i: (0, 0)
    rows = lambda i: (i, 0)
    return pl.pallas_call(
        _dif_prep_kernel,
        out_shape=(jax.ShapeDtypeStruct((nsub, LANES, t), BF16),
                   jax.ShapeDtypeStruct((nsub, nt, tm, LANES), BF16),
                   jax.ShapeDtypeStruct((DIF_HEADS, nt, 2 * DIF_HEAD_DIM + SUM_ROWS, tm), BF16)),
        grid=(nt,),
        in_specs=[pl.BlockSpec((tm, 1024), lambda i: (i, U_DQ // 1024)),
                  pl.BlockSpec((tm, 1024), lambda i: (i, U_DK // 1024)),
                  pl.BlockSpec((tm, 512), lambda i: (i, U_DV // 512)),
                  pl.BlockSpec((1, LANES), const),
                  pl.BlockSpec((1, LANES), const),
                  pl.BlockSpec((tm, LANES), rows),
                  pl.BlockSpec((tm, LANES), rows),
                  pl.BlockSpec((tm, LANES), rows)],
        out_specs=(pl.BlockSpec((nsub, LANES, tm), lambda i: (0, 0, i)),
                   pl.BlockSpec((nsub, 1, tm, LANES), lambda i: (0, i, 0, 0)),
                   pl.BlockSpec((DIF_HEADS, 1, 2 * DIF_HEAD_DIM + SUM_ROWS, tm), lambda i: (0, i, 0, 0))),
        compiler_params=_params("parallel"),
        name="dif_prep",
    )(u, u, u, gq, gk, *tabs)


ATTN_GROUP = 2
ATTN_TRIP = 64
SUM_ROWS = 16


def _with_sum_rows(vt):
    heads, _, keys = vt.shape
    row = lax.broadcasted_iota(jnp.int32, (heads, SUM_ROWS, keys), 1)
    return jnp.concatenate([vt, jnp.where(row == 0, 1.0, 0.0).astype(vt.dtype)], axis=1)


def _attn_group(qt_ref, k_ref, vt_ref, v_of_sub, s_buf, p_buf, acc_ref, n_chunks):
    tq = qt_ref.shape[2]
    last = n_chunks - 1

    def scores(c):
        out = []
        for a in range(ATTN_GROUP):
            s = _dot(k_ref[a, c], qt_ref[a])
            out.append((s, jnp.max(s, axis=0, keepdims=True)))
        return out

    def stash(sc, slot):
        for a in range(ATTN_GROUP):
            s_buf[slot, a] = sc[a][0]
        return tuple(mx for (_, mx) in sc)

    def softmax(s_of, mx, slot, ms):
        new_ms, alphas = [], []
        for a in range(ATTN_GROUP):
            m_new = jnp.maximum(ms[a], mx[a])
            alphas.append(jnp.exp2(ms[a] - m_new))
            p_buf[slot, a] = jnp.exp2(s_of(a) - m_new).astype(BF16)
            new_ms.append(m_new)
        return tuple(new_ms), tuple(alphas)

    def values(c, slot, alphas):
        for a in range(ATTN_GROUP):
            acc_ref[a] = alphas[a] * acc_ref[a] + _dot(vt_ref[v_of_sub[a], c], p_buf[slot, a])

    per_trip = math.gcd(last, ATTN_TRIP)

    def trip(t, state):
        ms, alphas, mx0 = state
        c = per_trip * t + 1
        for pair in range(per_trip // 2):
            cur, nxt = pair % 2, 1 - pair % 2
            even = scores(c + 1)
            values(c - 1, 0, alphas)
            ms, alphas = softmax(lambda a: s_buf[cur, a], mx0, 1, ms)
            mx0 = stash(scores(jnp.minimum(c + 2, last)), nxt)
            values(c, 1, alphas)
            ms, alphas = softmax(lambda a: even[a][0], [mx for (_, mx) in even], 0, ms)
            c = c + 2
        return ms, alphas, mx0

    acc_ref[...] = jnp.zeros(acc_ref.shape, F32)
    ms = tuple(jnp.full((1, tq), -jnp.inf, F32) for _ in range(ATTN_GROUP))
    first = scores(0)
    ms, alphas = softmax(lambda a: first[a][0], [mx for (_, mx) in first], 0, ms)
    mx0 = stash(scores(1), 0)
    n_trips = jnp.where(pl.program_id(1) > 0, last // per_trip, 0)
    ms, alphas, _ = lax.fori_loop(0, n_trips, trip, (ms, alphas, mx0))
    values(per_trip * n_trips, 0, alphas)


def _normalised(acc_ref, a):
    dv = acc_ref.shape[1] - SUM_ROWS
    return acc_ref[a, :dv] * (1.0 / acc_ref[a, dv:dv + 1])


def _mla_attn_kernel(qt_ref, k_ref, vt_ref, o_ref, s_buf, p_buf, acc_ref, *, n_chunks):
    _attn_group(qt_ref, k_ref, vt_ref, tuple(range(ATTN_GROUP)), s_buf, p_buf, acc_ref, n_chunks)
    o = jnp.concatenate([_normalised(acc_ref, a) for a in range(ATTN_GROUP)], axis=0)
    o_ref[...] = o.T


def _dif_attn_kernel(lam_ref, gsub_ref, qt_ref, k_ref, vt_ref, o_ref, s_buf, p_buf, acc_ref,
                     *, n_chunks, lam_init):
    _attn_group(qt_ref, k_ref, vt_ref, tuple(a // 2 for a in range(ATTN_GROUP)), s_buf, p_buf,
                acc_ref, n_chunks)
    lp = lam_ref[...]
    lam = (jnp.exp(jnp.sum(lp[0:1] * lp[1:2], axis=-1, keepdims=True))
           - jnp.exp(jnp.sum(lp[2:3] * lp[3:4], axis=-1, keepdims=True)) + lam_init)
    outs = []
    for h in range(ATTN_GROUP // 2):
        o = _normalised(acc_ref, 2 * h) - lam * _normalised(acc_ref, 2 * h + 1)
        outs.append(o * lax.rsqrt(jnp.mean(o * o, axis=0, keepdims=True) + EPS))
    gsub = gsub_ref[...] * (1.0 - lam_init)
    o_ref[...] = jnp.concatenate(outs, axis=0).T * jnp.concatenate([gsub] * len(outs), axis=1)


def _attention(qt, k, vt, *, lam=None, gsub=None, lam_init=None):
    nsub, _, t = qt.shape
    n_chunks = k.shape[1]
    assert (n_chunks - 1) % 4 == 0 and nsub % ATTN_GROUP == 0
    tq = KEY_CHUNK
    dv = vt.shape[2] - SUM_ROWS
    groups = nsub // ATTN_GROUP
    n_v = vt.shape[0] // groups
    out_w = n_v * dv
    resident = dict(pipeline_mode=pl.Buffered(1))
    specs = [pl.BlockSpec((ATTN_GROUP, LANES, tq), lambda g, i: (g, 0, i)),
             pl.BlockSpec((ATTN_GROUP, n_chunks, KEY_CHUNK, LANES), lambda g, i: (g, 0, 0, 0), **resident),
             pl.BlockSpec((n_v, n_chunks, dv + SUM_ROWS, KEY_CHUNK), lambda g, i: (g, 0, 0, 0), **resident)]
    if lam is None:
        body = functools.partial(_mla_attn_kernel, n_chunks=n_chunks)
        args = (qt, k, vt)
    else:
        body = functools.partial(_dif_attn_kernel, n_chunks=n_chunks, lam_init=lam_init)
        specs = [pl.BlockSpec((8, LANES), lambda g, i: (0, 0)),
                 pl.BlockSpec((1, LANES), lambda g, i: (0, 0))] + specs
        args = (lam, gsub, qt, k, vt)
    return pl.pallas_call(
        body,
        out_shape=jax.ShapeDtypeStruct((t, groups * out_w), F32),
        grid=(groups, t // tq),
        in_specs=specs,
        out_specs=pl.BlockSpec((tq, out_w), lambda g, i: (i, g)),
        scratch_shapes=[pltpu.VMEM((2, ATTN_GROUP, KEY_CHUNK, tq), F32),
                        pltpu.VMEM((2, ATTN_GROUP, KEY_CHUNK, tq), BF16),
                        pltpu.VMEM((ATTN_GROUP, dv + SUM_ROWS, tq), F32)],
        compiler_params=_params("parallel", "arbitrary"),
        name="mla_attention" if lam is None else "dif_attention",
    )(*args)


def _ssm_prep_kernel(x_ref, prev_ref, next_ref, dt_ref, w_ref, b_ref, dtb_ref, xo_ref, dto_ref, e_ref,
                     *, n_ctx, n_tok, tm):
    row0 = pl.program_id(0) * tm
    pad = SSM_CONV // 2
    has_prev = jnp.logical_and(row0 != 0, row0 != n_ctx)
    has_next = jnp.logical_and(row0 + tm != n_ctx, row0 + tm != n_tok)
    e_ref[0:8] = jnp.where(has_prev, prev_ref[...], 0.0)
    e_ref[8:8 + tm] = x_ref[...]
    e_ref[8 + tm:16 + tm] = jnp.where(has_next, next_ref[...], 0.0)
    acc = jnp.zeros(x_ref.shape, F32) + b_ref[...]
    for k in range(SSM_CONV):
        acc = acc + w_ref[k:k + 1, :] * e_ref[pl.ds(8 - pad + k, tm), :]
    xo_ref[...] = _silu(acc)
    d = dt_ref[...] + dtb_ref[...]
    dto_ref[...] = jnp.maximum(d, 0.0) + jnp.log1p(jnp.exp(-jnp.abs(d)))


def _ssm_prep(u, conv_w, conv_b, dt_bias, n_ctx):
    t = u.shape[0]
    tm = ROW_TILE
    nt = t // tm
    cb = U_XBC // SSM_XBC
    const = lambda i: (0, 0)
    return pl.pallas_call(
        functools.partial(_ssm_prep_kernel, n_ctx=n_ctx, n_tok=t, tm=tm),
        out_shape=(jax.ShapeDtypeStruct((t, SSM_XBC), F32), jax.ShapeDtypeStruct((t, LANES), F32)),
        grid=(nt,),
        in_specs=[pl.BlockSpec((tm, SSM_XBC), lambda i: (i, cb)),
                  pl.BlockSpec((8, SSM_XBC), lambda i: (jnp.maximum(i * (tm // 8) - 1, 0), cb)),
                  pl.BlockSpec((8, SSM_XBC), lambda i: (jnp.minimum((i + 1) * (tm // 8), t // 8 - 1), cb)),
                  pl.BlockSpec((tm, LANES), lambda i: (i, U_DT // LANES)),
                  pl.BlockSpec((8, SSM_XBC), const),
                  pl.BlockSpec((1, SSM_XBC), const),
                  pl.BlockSpec((1, LANES), const)],
        out_specs=(pl.BlockSpec((tm, SSM_XBC), lambda i: (i, 0)),
                   pl.BlockSpec((tm, LANES), lambda i: (i, 0))),
        scratch_shapes=[pltpu.VMEM((tm + 16, SSM_XBC), F32)],
        compiler_params=_params("parallel"),
        name="ssm_prep",
    )(u, u, u, u, conv_w, conv_b, dt_bias)


def _ssd_kernel(xf_ref, dtf_ref, xr_ref, dtr_ref, alog_ref, of_ref, or_ref, hf_ref, hr_ref):
    @pl.when(pl.program_id(0) == 0)
    def _():
        hf_ref[...] = jnp.zeros(hf_ref.shape, F32)
        hr_ref[...] = jnp.zeros(hr_ref.shape, F32)

    lc = SSM_CHUNK
    a_row = -jnp.exp(alog_ref[...])
    n_sub = xf_ref.shape[0] // lc
    fwd, bwd = [None] * n_sub, [None] * n_sub
    for j in range(n_sub):
        rows = slice(lc * j, lc * (j + 1))
        fwd[j] = _ssd_chunk(xf_ref[rows, :], dtf_ref[rows, :], a_row, hf_ref, 0)
        jr = n_sub - 1 - j
        rows = slice(lc * jr, lc * (jr + 1))
        bwd[jr] = _ssd_chunk(xr_ref[rows, :], dtr_ref[rows, :], a_row, hr_ref, 1)
    of_ref[...] = jnp.concatenate(fwd, axis=0)
    or_ref[...] = jnp.concatenate(bwd, axis=0)


def _ssd_chunk(xbc, dt, a_row, h_ref, direction):
    lc = SSM_CHUNK
    hd, per_group = SSM_HEAD_DIM, SSM_HEADS // SSM_GROUPS
    dta = dt * a_row
    r = lax.broadcasted_iota(jnp.int32, (lc, lc), 0)
    c = lax.broadcasted_iota(jnp.int32, (lc, lc), 1)
    keep = (r >= c) if direction == 0 else (r <= c)
    tri = jnp.where(keep, 1.0, 0.0).astype(F32)
    cum = jnp.dot(tri, dta, preferred_element_type=F32, precision=lax.Precision.HIGHEST)
    cum_t = cum.T
    total = jnp.sum(dta, axis=0, keepdims=True)
    to_end = jnp.exp(total - cum)
    from_start = jnp.exp(cum)
    chunk_decay = jnp.exp(total)
    x_t = xbc[:, :SSM_WIDTH].T
    w_t = (dt * to_end).T
    outs = []
    for g in range(SSM_GROUPS):
        b_g = xbc[:, SSM_WIDTH + SSM_STATE * g:SSM_WIDTH + SSM_STATE * (g + 1)].astype(BF16)
        c_g = xbc[:, SSM_WIDTH + SSM_STATE * (SSM_GROUPS + g):
                  SSM_WIDTH + SSM_STATE * (SSM_GROUPS + g + 1)].astype(BF16)
        cb = _dot_nt(c_g, b_g)
        for h in range(per_group * g, per_group * (g + 1)):
            col = direction * SSM_HEADS + h
            seg = jnp.exp(jnp.where(keep, cum[:, col:col + 1] - cum_t[col:col + 1, :], -jnp.inf))
            xd = xbc[:, hd * h:hd * (h + 1)] * dt[:, col:col + 1]
            y = _dot((cb * seg).astype(BF16), xd.astype(BF16))
            state = h_ref[h]
            y_off = _dot_nt(c_g, state.astype(BF16))
            outs.append(y + y_off * from_start[:, col:col + 1])
            xw_t = x_t[hd * h:hd * (h + 1), :] * w_t[col:col + 1, :]
            h_ref[h] = state * chunk_decay[:, col:col + 1] + _dot(xw_t.astype(BF16), b_g)
    return jnp.concatenate(outs, axis=1)


SSD_STEP = 2 * SSM_CHUNK


def _ssd(xbc, dt, a_log, n_ctx):
    t = xbc.shape[0]
    lc = SSD_STEP
    assert n_ctx % lc == 0 and t % lc == 0
    nc = t // lc
    ncc = n_ctx // lc
    fwd = lambda s: (s, 0)
    bwd = lambda s: (jnp.where(s < ncc, ncc - 1 - s, nc - 1 - (s - ncc)), 0)
    state = pltpu.VMEM((SSM_HEADS, SSM_HEAD_DIM, SSM_STATE), F32)
    out = jax.ShapeDtypeStruct((t, SSM_WIDTH), F32)
    return pl.pallas_call(
        _ssd_kernel,
        out_shape=(out, out),
        grid=(nc,),
        in_specs=[pl.BlockSpec((lc, SSM_XBC), fwd), pl.BlockSpec((lc, LANES), fwd),
                  pl.BlockSpec((lc, SSM_XBC), bwd), pl.BlockSpec((lc, LANES), bwd),
                  pl.BlockSpec((1, LANES), lambda s: (0, 0))],
        out_specs=(pl.BlockSpec((lc, SSM_WIDTH), fwd), pl.BlockSpec((lc, SSM_WIDTH), bwd)),
        scratch_shapes=[state, state],
        compiler_params=_params("arbitrary"),
        name="ssd_scan",
    )(xbc, dt, xbc, dt, a_log)


def _merge_kernel(x_ref, gate_ref, ya_ref, yb_ref, yf_ref, yr_ref, xs_ref, z_ref,
                  bg_ref, dskip_ref, gssm_ref, wa_ref, wb_ref, wc_ref, wo_ref,
                  ml_ref, mc_ref, g2_ref, wr_ref, br_ref,
                  xo_ref, f_ref, lg_ref, *, n_ctx, tm):
    y = (yf_ref[...] + yr_ref[...] + dskip_ref[...] * xs_ref[...]) * _silu(z_ref[...])
    gw = SSM_WIDTH // SSM_GROUPS
    yc = jnp.concatenate([_rms_rows(y[:, gw * g:gw * (g + 1)], gw) for g in range(SSM_GROUPS)], axis=1)
    yc = yc * gssm_ref[...]
    gate = _sigmoid(gate_ref[...] + bg_ref[...])
    m = (gate[:, :D_MODEL] * _dot(ya_ref[...].astype(BF16), wa_ref[...])
         + gate[:, D_MODEL:2 * D_MODEL] * _dot(yb_ref[...].astype(BF16), wb_ref[...])
         + gate[:, 2 * D_MODEL:] * _dot(yc.astype(BF16), wc_ref[...]))
    out = _dot(m.astype(BF16), wo_ref[...])
    row0 = pl.program_id(0) * tm
    is_ctx = (row0 + lax.broadcasted_iota(jnp.int32, (tm, 1), 0)) < n_ctx
    gt1 = jnp.where(is_ctx, mc_ref[2:3, :], ml_ref[2:3, :])
    x_new = x_ref[...] + gt1 * out
    xo_ref[...] = x_new
    f = _modulated_norm(x_new, g2_ref[...], ml_ref[3:5, :], mc_ref[3:5, :], row0, n_ctx)
    f_ref[...] = f
    lg_ref[...] = _dot(f.astype(BF16), wr_ref[...]) + br_ref[...]


def _merge(x_all, u, ya, yb, yf, yr, xbc, b_gate, dskip, g_ssm, wa, wb, wc, wo, mod_lat, mod_ctx,
           g2, w_router, b_router, n_ctx):
    t, d = x_all.shape
    tm = ROW_TILE
    const = lambda i: (0, 0)
    rows = lambda i: (i, 0)
    full = lambda a: pl.BlockSpec(a.shape, const)
    return pl.pallas_call(
        functools.partial(_merge_kernel, n_ctx=n_ctx, tm=tm),
        out_shape=(jax.ShapeDtypeStruct((t, d), F32), jax.ShapeDtypeStruct((t, d), F32),
                   jax.ShapeDtypeStruct((t, LANES), F32)),
        grid=(t // tm,),
        in_specs=[pl.BlockSpec((tm, d), rows),
                  pl.BlockSpec((tm, GATE_COLS), lambda i: (i, U_GATE // GATE_COLS)),
                  pl.BlockSpec((tm, MLA_WIDTH), rows),
                  pl.BlockSpec((tm, DIF_WIDTH), rows),
                  pl.BlockSpec((tm, SSM_WIDTH), rows),
                  pl.BlockSpec((tm, SSM_WIDTH), rows),
                  pl.BlockSpec((tm, SSM_WIDTH), rows),
                  pl.BlockSpec((tm, SSM_WIDTH), lambda i: (i, U_Z // SSM_WIDTH)),
                  full(b_gate), full(dskip), full(g_ssm), full(wa), full(wb), full(wc), full(wo),
                  full(mod_lat), full(mod_ctx), full(g2), full(w_router), full(b_router)],
        out_specs=(pl.BlockSpec((tm, d), rows), pl.BlockSpec((tm, d), rows),
                   pl.BlockSpec((tm, LANES), rows)),
        compiler_params=_params("parallel"),
        name="merge",
    )(x_all, u, ya, yb, yf, yr, xbc, u, b_gate, dskip, g_ssm, wa, wb, wc, wo, mod_lat, mod_ctx,
      g2, w_router, b_router)


def _moe_kernel(be_ref, nb_ref, x_ref, wgu_ref, bgu_ref, wd_ref, bd_ref, o_ref, wgu_s, wd_s):
    b = pl.program_id(0)
    prev = be_ref[jnp.maximum(b - 1, 0)]
    fresh = jnp.logical_or(b == 0, be_ref[b] != prev)

    @pl.when(fresh)
    def _():
        wgu_s[...] = wgu_ref[...].astype(BF16)
        wd_s[...] = wd_ref[...].astype(BF16)

    @pl.when(b < nb_ref[0])
    def _():
        gu = _dot(x_ref[...].astype(BF16), wgu_s[...]) + bgu_ref[...]
        glu = jnp.minimum(gu[:, :D_FF], SWIGLU_LIMIT)
        lin = jnp.clip(gu[:, D_FF:], -SWIGLU_LIMIT, SWIGLU_LIMIT)
        act = glu * _sigmoid(SWIGLU_ALPHA * glu) * (lin + 1.0)
        o_ref[...] = _dot(act.astype(BF16), wd_s[...]) + bd_ref[...]

    @pl.when(b >= nb_ref[0])
    def _():
        o_ref[...] = jnp.zeros(o_ref.shape, F32)


def _moe_experts(block_e, n_used, x_sorted, w_gu, b_gu, w_down, b_down, layer):
    n_slots, d = x_sorted.shape
    n_blocks = n_slots // MOE_BLOCK
    grid_spec = pltpu.PrefetchScalarGridSpec(
        num_scalar_prefetch=2,
        grid=(n_blocks,),
        in_specs=[pl.BlockSpec((MOE_BLOCK, d), lambda b, be, nb: (b, 0)),
                  pl.BlockSpec((None, None, d, 2 * D_FF), lambda b, be, nb: (layer, be[b], 0, 0)),
                  pl.BlockSpec((None, None, 1, 2 * D_FF), lambda b, be, nb: (layer, be[b], 0, 0)),
                  pl.BlockSpec((None, None, D_FF, d), lambda b, be, nb: (layer, be[b], 0, 0)),
                  pl.BlockSpec((None, None, 1, d), lambda b, be, nb: (layer, be[b], 0, 0))],
        out_specs=pl.BlockSpec((MOE_BLOCK, d), lambda b, be, nb: (b, 0)),
        scratch_shapes=[pltpu.VMEM((d, 2 * D_FF), BF16), pltpu.VMEM((D_FF, d), BF16)],
    )
    return pl.pallas_call(
        _moe_kernel,
        out_shape=jax.ShapeDtypeStruct((n_slots, d), F32),
        grid_spec=grid_spec,
        compiler_params=_params("arbitrary"),
        name="moe_experts",
    )(block_e, n_used, x_sorted, w_gu, b_gu, w_down, b_down)


def _combine_kernel(x_ref, g_ref, ml_ref, mc_ref, *rest, n_ctx, tm):
    y_refs, o_ref = rest[:TOP_K], rest[TOP_K]
    is_ctx = (pl.program_id(0) * tm + lax.broadcasted_iota(jnp.int32, (tm, 1), 0)) < n_ctx
    gate2 = jnp.where(is_ctx, mc_ref[...], ml_ref[...])
    g = g_ref[...]
    y = g[:, 0:1] * y_refs[0][...]
    for k in range(1, TOP_K):
        y = y + g[:, k:k + 1] * y_refs[k][...]
    o_ref[...] = x_ref[...] + gate2 * y


def _combine(x_all, y_rows, gates, gate2_lat, gate2_ctx, n_ctx):
    t, d = x_all.shape
    tm = ROW_TILE
    rows = lambda i: (i, 0)
    const = lambda i: (0, 0)
    return pl.pallas_call(
        functools.partial(_combine_kernel, n_ctx=n_ctx, tm=tm),
        out_shape=jax.ShapeDtypeStruct((t, d), F32),
        grid=(t // tm,),
        in_specs=[pl.BlockSpec((tm, d), rows), pl.BlockSpec((tm, LANES), rows),
                  pl.BlockSpec((1, d), const), pl.BlockSpec((1, d), const)]
                 + [pl.BlockSpec((tm, d), rows)] * TOP_K,
        out_specs=pl.BlockSpec((tm, d), rows),
        compiler_params=_params("parallel"),
        name="moe_combine",
    )(x_all, gates, gate2_lat, gate2_ctx, *y_rows)


ROUTE_IDX, ROUTE_GATE, ROUTE_RANK = 0, TOP_K, 2 * TOP_K


def _router_kernel(lg_ref, o_ref, cnt_ref, run_ref, *, tm):
    @pl.when(pl.program_id(0) == 0)
    def _():
        run_ref[...] = jnp.zeros(run_ref.shape, F32)

    lane = lax.broadcasted_iota(jnp.int32, (tm, LANES), 1)
    lane_f = lane.astype(F32)
    lg = jnp.where(lane < N_EXPERTS, lg_ref[...], -jnp.inf)
    hots, vals = [], []
    for _ in range(TOP_K):
        mx = jnp.max(lg, axis=-1, keepdims=True)
        idx = jnp.min(jnp.where(lg == mx, lane_f, float(LANES)), axis=-1, keepdims=True)
        hot = lane_f == idx
        lg = jnp.where(hot, -jnp.inf, lg)
        hots.append((hot, idx))
        vals.append(mx)
    exps = [jnp.exp(v - vals[0]) for v in vals]
    inv = 1.0 / sum(exps[1:], exps[0])
    chosen = jnp.zeros((tm, LANES), F32)
    for hot, _ in hots:
        chosen = jnp.where(hot, 1.0, chosen)
    r = lax.broadcasted_iota(jnp.int32, (tm, tm), 0)
    c = lax.broadcasted_iota(jnp.int32, (tm, tm), 1)
    earlier = jnp.where(r > c, 1.0, 0.0).astype(BF16)
    before = _dot(earlier, chosen.astype(BF16)) + run_ref[0:1, :]
    out = jnp.zeros((tm, LANES), F32)
    for k, (hot, idx) in enumerate(hots):
        rank = jnp.sum(jnp.where(hot, before, 0.0), axis=-1, keepdims=True)
        out = jnp.where(lane == ROUTE_IDX + k, idx, out)
        out = jnp.where(lane == ROUTE_GATE + k, exps[k] * inv, out)
        out = jnp.where(lane == ROUTE_RANK + k, rank, out)
    o_ref[...] = out
    run_ref[...] = run_ref[...] + jnp.sum(chosen, axis=0, keepdims=True)
    cnt_ref[...] = run_ref[...]


def _router(logits):
    t = logits.shape[0]
    tm = ROW_TILE
    return pl.pallas_call(
        functools.partial(_router_kernel, tm=tm),
        out_shape=(jax.ShapeDtypeStruct((t, LANES), F32), jax.ShapeDtypeStruct((8, LANES), F32)),
        grid=(t // tm,),
        in_specs=[pl.BlockSpec((tm, LANES), lambda i: (i, 0))],
        out_specs=(pl.BlockSpec((tm, LANES), lambda i: (i, 0)), pl.BlockSpec((8, LANES), lambda i: (0, 0))),
        scratch_shapes=[pltpu.VMEM((8, LANES), F32)],
        compiler_params=_params("arbitrary"),
        name="moe_router",
    )(logits)


def _route(logits, n_tok):
    routed, counts = _router(logits)
    top_idx = routed[:, ROUTE_IDX:ROUTE_IDX + TOP_K].astype(jnp.int32)
    gates = routed[:, ROUTE_GATE:ROUTE_GATE + TOP_K]
    rank = routed[:, ROUTE_RANK:ROUTE_RANK + TOP_K].astype(jnp.int32)
    n_assign = n_tok * TOP_K
    counts = counts[0, :N_EXPERTS].astype(jnp.int32)
    padded = (counts + MOE_BLOCK - 1) // MOE_BLOCK * MOE_BLOCK
    pad_end = jnp.cumsum(padded)
    pad_start = pad_end - padded
    slot_of = pad_start[top_idx] + rank
    n_blocks = -(-(n_assign + N_EXPERTS * (MOE_BLOCK - 1)) // MOE_BLOCK)
    n_slots = n_blocks * MOE_BLOCK
    token = jnp.arange(n_assign, dtype=jnp.int32) // TOP_K
    slot_tok = jnp.zeros((n_slots,), jnp.int32).at[slot_of.reshape(-1)].set(token, unique_indices=True)
    block_start = jnp.arange(n_blocks, dtype=jnp.int32) * MOE_BLOCK
    block_e = jnp.minimum(jnp.searchsorted(pad_end, block_start, side='right'), N_EXPERTS - 1)
    n_used = (pad_end[-1] // MOE_BLOCK).reshape(1)
    return gates, slot_tok, slot_of, block_e.astype(jnp.int32), n_used.astype(jnp.int32)


def kernel(x, c, ctx, c_ctx, w_mod, b_mod, g_norm1, g_norm2, w_in, b_gate, mla_g_q, mla_w_uq, mla_g_kv, mla_w_ukv, mla_g_qn, mla_g_kn, dif_g_qn, dif_g_kn, dif_lambda, dif_g_sub, ssm_conv_w, ssm_conv_b, ssm_dt_bias, ssm_a_log, ssm_d, ssm_g_norm, w_up_mla, w_up_dif, w_up_ssm, w_out, moe_w_router, moe_b_router, moe_w_gu, moe_b_gu, moe_w_down, moe_b_down):
    assert x.shape[0] == 1 and ctx.shape[0] == 1
    depth = w_in.shape[0]
    seq = x.shape[1]
    n_ctx = ctx.shape[1]
    n_tok = n_ctx + seq
    d = D_MODEL
    assert n_ctx == KEY_CHUNK and n_tok % ROW_TILE == 0 and seq % GRID_W == 0

    x_all = jnp.concatenate([ctx[0], x[0]], axis=0)
    cc = jnp.zeros((8, d), F32).at[0].set(c[0]).at[1].set(c_ctx)
    mod = _mod_vectors(cc, w_mod, b_mod)
    mod = mod[:, :2].reshape(depth, 2, 6, d)

    rope_mla = _rope_tables(seq, n_ctx, MLA_ROPE, MLA_NOPE)
    rope_dif = _rope_tables(seq, n_ctx, DIF_HEAD_DIM, 0)
    w_in_all = _take_columns(w_in, _in_proj_columns()).astype(BF16)
    w_uq_all = _take_columns(mla_w_uq, _head_columns(MLA_HEADS, MLA_QK, 0, MLA_QK, LANES)).astype(BF16)
    w_uk_all = _take_columns(mla_w_ukv, _head_columns(MLA_HEADS, MLA_NOPE + MLA_V, 0, MLA_NOPE, LANES)).astype(BF16)
    w_uv_all = _take_columns(mla_w_ukv, _head_columns(MLA_HEADS, MLA_NOPE + MLA_V, MLA_NOPE, MLA_V, MLA_V)).astype(BF16)
    w_up_all = [w.astype(BF16) for w in (w_up_mla, w_up_dif, w_up_ssm, w_out)]

    for i in range(depth):
        lam_init = 0.8 - 0.6 * math.exp(-0.3 * i)
        mod_lat, mod_ctx = mod[i, 0], mod[i, 1]
        u = _in_proj(x_all, g_norm1[i][None], mod_lat[0:2], mod_ctx[0:2], w_in_all, i, n_ctx)

        qt, k, vt = _mla_prep(
            u, mla_g_q[i][None], w_uq_all[i], mla_g_kv[i][None], w_uk_all[i], w_uv_all[i],
            _pad_lanes(mla_g_qn[i][None]), _pad_lanes(mla_g_kn[i][None]), rope_mla)
        ya = _attention(qt, k, vt)

        qt, k, vt = _dif_prep(u, _pad_lanes(dif_g_qn[i][None]), _pad_lanes(dif_g_kn[i][None]), rope_dif)
        lam_rows = jnp.zeros((8, LANES), F32).at[:4, :DIF_HEAD_DIM].set(dif_lambda[i])
        yb = _attention(qt, k, vt, lam=lam_rows, gsub=dif_g_sub[i][None], lam_init=lam_init)

        conv_w = jnp.zeros((8, SSM_XBC), F32).at[:SSM_CONV].set(ssm_conv_w[i])
        xbc, dt = _ssm_prep(u, conv_w, ssm_conv_b[i][None], _pad_lanes(ssm_dt_bias[i].reshape(1, -1)), n_ctx)
        a_log = _pad_lanes(ssm_a_log[i].reshape(1, -1))
        yf, yr = _ssd(xbc, dt, a_log, n_ctx)

        dskip = jnp.repeat(ssm_d[i, 0] + ssm_d[i, 1], SSM_HEAD_DIM)[None]
        w_router = jnp.zeros((d, LANES), BF16).at[:, :N_EXPERTS].set(moe_w_router[i].astype(BF16))
        b_router = jnp.zeros((1, LANES), F32).at[0, :N_EXPERTS].set(moe_b_router[i])
        x_all, f, logits = _merge(
            x_all, u, ya, yb, yf, yr, xbc, b_gate[i][None], dskip, ssm_g_norm[i][None],
            w_up_all[0][i], w_up_all[1][i], w_up_all[2][i], w_up_all[3][i],
            mod_lat[0:5], mod_ctx[0:5], g_norm2[i][None],
            w_router, b_router, n_ctx)

        gates, slot_tok, slot_of, block_e, n_used = _route(logits, n_tok)
        y_slots = _moe_experts(block_e, n_used, f[slot_tok], moe_w_gu,
                               moe_b_gu.reshape(depth, N_EXPERTS, 1, 2 * D_FF), moe_w_down,
                               moe_b_down.reshape(depth, N_EXPERTS, 1, d), i)
        y_rows = [y_slots[slot_of[:, k]] for k in range(TOP_K)]
        x_all = _combine(x_all, y_rows, _pad_lanes(gates), mod_lat[5:6], mod_ctx[5:6], n_ctx)
    return x_all[n_ctx:][None]
```

```python
import functools
import math

import numpy as np
import jax
import jax.numpy as jnp
from jax import lax
from jax.experimental import pallas as pl
from jax.experimental.pallas import tpu as pltpu

F32 = jnp.float32
BF16 = jnp.bfloat16
LANES = 128
VMEM_LIMIT = 52 * 1024 * 1024

D_MODEL = 1024
EPS = 1e-6
ROPE_THETA = 10000.0
GRID_W = 64
N_BRANCH = 3

MLA_HEADS = 8
MLA_Q_RANK = 256
MLA_KV_RANK = 128
MLA_NOPE = 64
MLA_ROPE = 32
MLA_V = 64
MLA_QK = MLA_NOPE + MLA_ROPE
MLA_WIDTH = MLA_HEADS * MLA_V

DIF_HEADS = 4
DIF_HEAD_DIM = 64
DIF_WIDTH = DIF_HEADS * 2 * DIF_HEAD_DIM

SSM_HEADS = 8
SSM_HEAD_DIM = 64
SSM_WIDTH = SSM_HEADS * SSM_HEAD_DIM
SSM_GROUPS = 2
SSM_STATE = 128
SSM_CONV = 5
SSM_CHUNK = 128
SSM_XBC = SSM_WIDTH + 2 * SSM_GROUPS * SSM_STATE

N_EXPERTS = 32
TOP_K = 4
D_FF = 1024
SWIGLU_LIMIT = 7.0
SWIGLU_ALPHA = 1.702
MOE_BLOCK = 512

MLA_COLS = MLA_Q_RANK + MLA_KV_RANK + MLA_ROPE
DIF_COLS = 3 * DIF_WIDTH
SSM_COLS = SSM_WIDTH + SSM_XBC + 2 * SSM_HEADS
GATE_COLS = N_BRANCH * D_MODEL

ROW_TILE = 256
KEY_CHUNK = 256
LOG2E = 1.4426950408889634

U_GATE, U_DQ, U_DK, U_XBC, U_MLA, U_DV, U_Z, U_DT = 0, 3072, 4096, 5120, 6144, 6656, 7168, 7680
U_COLS = 8192
U_TILE_N = 2048


def _in_proj_columns():
    src = np.full((U_COLS,), -1, np.int64)
    dif0 = MLA_COLS
    ssm0 = MLA_COLS + DIF_COLS
    gate0 = ssm0 + SSM_COLS
    src[U_GATE:U_GATE + GATE_COLS] = gate0 + np.arange(GATE_COLS)
    for a in range(2 * DIF_HEADS):
        src[U_DQ + LANES * a:U_DQ + LANES * a + DIF_HEAD_DIM] = dif0 + DIF_HEAD_DIM * a + np.arange(DIF_HEAD_DIM)
        src[U_DK + LANES * a:U_DK + LANES * a + DIF_HEAD_DIM] = (dif0 + DIF_WIDTH + DIF_HEAD_DIM * a
                                                                  + np.arange(DIF_HEAD_DIM))
    src[U_DV:U_DV + DIF_WIDTH] = dif0 + 2 * DIF_WIDTH + np.arange(DIF_WIDTH)
    src[U_MLA:U_MLA + MLA_Q_RANK + MLA_KV_RANK] = np.arange(MLA_Q_RANK + MLA_KV_RANK)
    pe0 = U_MLA + MLA_Q_RANK + MLA_KV_RANK + MLA_NOPE
    src[pe0:pe0 + MLA_ROPE] = MLA_Q_RANK + MLA_KV_RANK + np.arange(MLA_ROPE)
    src[U_Z:U_Z + SSM_WIDTH] = ssm0 + np.arange(SSM_WIDTH)
    src[U_XBC:U_XBC + SSM_XBC] = ssm0 + SSM_WIDTH + np.arange(SSM_XBC)
    src[U_DT:U_DT + 2 * SSM_HEADS] = ssm0 + SSM_WIDTH + SSM_XBC + np.arange(2 * SSM_HEADS)
    return src


def _take_columns(w, src):
    cols = jnp.take(w, jnp.asarray(np.maximum(src, 0)), axis=-1)
    return jnp.where(jnp.asarray(src >= 0), cols, 0.0)


def _head_columns(n_heads, src_stride, src_off, width, dst_stride):
    src = np.full((n_heads * dst_stride,), -1, np.int64)
    for h in range(n_heads):
        src[h * dst_stride:h * dst_stride + width] = h * src_stride + src_off + np.arange(width)
    return src


def _pad_lanes(v, n=LANES):
    return jnp.pad(v, [(0, 0)] * (v.ndim - 1) + [(0, n - v.shape[-1])])


def _row_tile(n, cap):
    best = 8
    for t in range(8, cap + 1, 8):
        if n % t == 0:
            best = t
    return best


def _dot(a, b):
    return jnp.dot(a, b, preferred_element_type=F32)


def _dot_nt(a, b):
    return lax.dot_general(a, b, (((1,), (1,)), ((), ())), preferred_element_type=F32)


def _sigmoid(x):
    return 1.0 / (1.0 + jnp.exp(-x))


def _silu(x):
    return x * _sigmoid(x)


def _rms_rows(x, n):
    return x * lax.rsqrt(jnp.sum(x * x, axis=-1, keepdims=True) * (1.0 / n) + EPS)


def _params(*sem):
    return pltpu.CompilerParams(dimension_semantics=sem, vmem_limit_bytes=VMEM_LIMIT)


def _mod_kernel(a_ref, w_ref, b_ref, o_ref):
    a = _silu(a_ref[...]).astype(BF16)
    o_ref[0] = _dot(a, w_ref[0].astype(BF16)) + b_ref[0]


def _mod_vectors(cc, w_mod, b_mod):
    depth, d, n = w_mod.shape
    tn = 1536
    return pl.pallas_call(
        _mod_kernel,
        out_shape=jax.ShapeDtypeStruct((depth, 8, n), F32),
        grid=(depth, n // tn),
        in_specs=[pl.BlockSpec((8, d), lambda l, j: (0, 0)),
                  pl.BlockSpec((1, d, tn), lambda l, j: (l, 0, j)),
                  pl.BlockSpec((1, 1, tn), lambda l, j: (l, 0, j))],
        out_specs=pl.BlockSpec((1, 8, tn), lambda l, j: (l, 0, j)),
        compiler_params=_params("parallel", "parallel"),
        name="mod_vectors",
    )(cc, w_mod, b_mod.reshape(depth, 1, n))


def _modulated_norm(x, g, mod_lat, mod_ctx, row0, n_ctx):
    rows = x.shape[0]
    is_ctx = (row0 + lax.broadcasted_iota(jnp.int32, (rows, 1), 0)) < n_ctx
    shift = jnp.where(is_ctx, mod_ctx[0:1, :], mod_lat[0:1, :])
    scale = jnp.where(is_ctx, mod_ctx[1:2, :], mod_lat[1:2, :])
    return _rms_rows(x, x.shape[1]) * g * (1.0 + scale) + shift


def _in_proj_kernel(x_ref, g_ref, ml_ref, mc_ref, w_ref, o_ref, h_ref, *, n_ctx, tm):
    @pl.when(pl.program_id(1) == 0)
    def _():
        h = _modulated_norm(x_ref[...], g_ref[...], ml_ref[...], mc_ref[...], pl.program_id(0) * tm, n_ctx)
        h_ref[...] = h.astype(BF16)

    o_ref[...] = _dot(h_ref[...], w_ref[...])


def _in_proj(x_all, g, mod_lat, mod_ctx, w, layer, n_ctx):
    t, d = x_all.shape
    n = w.shape[2]
    tm = _row_tile(t, 1280)
    return pl.pallas_call(
        functools.partial(_in_proj_kernel, n_ctx=n_ctx, tm=tm),
        out_shape=jax.ShapeDtypeStruct((t, n), F32),
        grid=(t // tm, n // U_TILE_N),
        in_specs=[pl.BlockSpec((tm, d), lambda i, j: (i, 0)),
                  pl.BlockSpec((1, d), lambda i, j: (0, 0)),
                  pl.BlockSpec((2, d), lambda i, j: (0, 0)),
                  pl.BlockSpec((2, d), lambda i, j: (0, 0)),
                  pl.BlockSpec((None, d, U_TILE_N), lambda i, j: (layer, 0, j))],
        out_specs=pl.BlockSpec((tm, U_TILE_N), lambda i, j: (i, j)),
        scratch_shapes=[pltpu.VMEM((tm, d), BF16)],
        compiler_params=_params("parallel", "arbitrary"),
        name="in_proj",
    )(x_all, g, mod_lat, mod_ctx, w)


def _rope_tables(seq_len, n_ctx, rot_dim, lane0):
    n_rows = seq_len // GRID_W
    row = jnp.repeat(jnp.arange(n_rows), GRID_W).astype(F32)
    col = jnp.tile(jnp.arange(GRID_W), n_rows).astype(F32)
    axis_dim = rot_dim // 2
    half = axis_dim // 2
    inv = ROPE_THETA ** (-jnp.arange(0, axis_dim, 2, dtype=F32) / axis_dim)
    ang_r = row[:, None] * inv
    ang_c = col[:, None] * inv
    zeros = jnp.zeros((seq_len, half), F32)
    cos = jnp.concatenate([jnp.cos(ang_r), jnp.cos(ang_r), jnp.cos(ang_c), jnp.cos(ang_c)], axis=1)
    s1 = jnp.concatenate([zeros, jnp.sin(ang_r), zeros, jnp.sin(ang_c)], axis=1)
    s2 = jnp.concatenate([-jnp.sin(ang_r), zeros, -jnp.sin(ang_c), zeros], axis=1)

    def place(tab, fill):
        full = jnp.full((seq_len, LANES), fill, F32).at[:, lane0:lane0 + rot_dim].set(tab)
        ctx = jnp.full((n_ctx, LANES), fill, F32)
        return jnp.concatenate([ctx, full], axis=0)

    return place(cos, 1.0), place(s1, 0.0), place(s2, 0.0)


def _norm_rope_blocks(blocks, gains, n, cos, s1, s2, half):
    sums = [jnp.sum(x * x, axis=-1, keepdims=True) for x in blocks]
    inv = [lax.rsqrt(s * (1.0 / n) + EPS) for s in sums]
    normed = [x * r * g for x, r, g in zip(blocks, inv, gains)]
    fwd = [pltpu.roll(y, half, 1) for y in normed]
    bwd = [pltpu.roll(y, LANES - half, 1) for y in normed]
    return [y * cos + a * s1 + b * s2 for y, a, b in zip(normed, fwd, bwd)]


def _mla_prep_kernel(u_ref, gq_ref, wuq_ref, gkv_ref, wk_ref, wv_ref, gqn_ref, gkn_ref,
                     cos_ref, s1_ref, s2_ref, qt_ref, k_ref, vt_ref):
    u = u_ref[...]
    cq = u[:, :MLA_Q_RANK]
    ckv = u[:, MLA_Q_RANK:MLA_Q_RANK + MLA_KV_RANK]
    pe = u[:, MLA_Q_RANK + MLA_KV_RANK:]
    q = _dot((_rms_rows(cq, MLA_Q_RANK) * gq_ref[...]).astype(BF16), wuq_ref[...])
    kv_in = (_rms_rows(ckv, MLA_KV_RANK) * gkv_ref[...]).astype(BF16)
    kn = _dot(kv_in, wk_ref[...])
    v = _dot(kv_in, wv_ref[...])
    cos, s1, s2 = cos_ref[...], s1_ref[...], s2_ref[...]
    half = MLA_ROPE // 4
    q_scale = MLA_QK ** -0.5 * LOG2E
    n = MLA_HEADS
    blocks = ([q[:, LANES * h:LANES * (h + 1)] for h in range(n)]
              + [kn[:, LANES * h:LANES * (h + 1)] + pe for h in range(n)])
    roped = _norm_rope_blocks(blocks, [gqn_ref[...]] * n + [gkn_ref[...]] * n, MLA_QK, cos, s1, s2, half)
    for h in range(n):
        qt_ref[h] = (roped[h] * q_scale).T.astype(BF16)
        k_ref[h, 0] = roped[n + h].astype(BF16)
    tm = v.shape[0]
    vt_ref[:, 0] = _with_sum_rows(v.T.reshape(MLA_HEADS, MLA_V, tm)).astype(BF16)


def _mla_prep(u, gq, wuq, gkv, wk, wv, gqn, gkn, tabs):
    t = u.shape[0]
    tm = ROW_TILE
    nt = t // tm
    const = lambda i: (0, 0)
    rows = lambda i: (i, 0)
    return pl.pallas_call(
        _mla_prep_kernel,
        out_shape=(jax.ShapeDtypeStruct((MLA_HEADS, LANES, t), BF16),
                   jax.ShapeDtypeStruct((MLA_HEADS, nt, tm, LANES), BF16),
                   jax.ShapeDtypeStruct((MLA_HEADS, nt, MLA_V + SUM_ROWS, tm), BF16)),
        grid=(nt,),
        in_specs=[pl.BlockSpec((tm, 512), lambda i: (i, U_MLA // 512)),
                  pl.BlockSpec((1, MLA_Q_RANK), const),
                  pl.BlockSpec(wuq.shape, const),
                  pl.BlockSpec((1, MLA_KV_RANK), const),
                  pl.BlockSpec(wk.shape, const),
                  pl.BlockSpec(wv.shape, const),
                  pl.BlockSpec((1, LANES), const),
                  pl.BlockSpec((1, LANES), const),
                  pl.BlockSpec((tm, LANES), rows),
                  pl.BlockSpec((tm, LANES), rows),
                  pl.BlockSpec((tm, LANES), rows)],
        out_specs=(pl.BlockSpec((MLA_HEADS, LANES, tm), lambda i: (0, 0, i)),
                   pl.BlockSpec((MLA_HEADS, 1, tm, LANES), lambda i: (0, i, 0, 0)),
                   pl.BlockSpec((MLA_HEADS, 1, MLA_V + SUM_ROWS, tm), lambda i: (0, i, 0, 0))),
        compiler_params=_params("parallel"),
        name="mla_prep",
    )(u, gq, wuq, gkv, wk, wv, gqn, gkn, *tabs)


def _dif_prep_kernel(q_ref, k_ref, v_ref, gq_ref, gk_ref, cos_ref, s1_ref, s2_ref,
                     qt_out, k_out, vt_out):
    cos, s1, s2 = cos_ref[...], s1_ref[...], s2_ref[...]
    half = DIF_HEAD_DIM // 4
    q_scale = DIF_HEAD_DIM ** -0.5 * LOG2E
    n = 2 * DIF_HEADS
    blocks = ([q_ref[:, LANES * a:LANES * (a + 1)] for a in range(n)]
              + [k_ref[:, LANES * a:LANES * (a + 1)] for a in range(n)])
    roped = _norm_rope_blocks(blocks, [gq_ref[...]] * n + [gk_ref[...]] * n, DIF_HEAD_DIM, cos, s1, s2, half)
    for a in range(n):
        qt_out[a] = (roped[a] * q_scale).T.astype(BF16)
        k_out[a, 0] = roped[n + a].astype(BF16)
    v = v_ref[...]
    vt_out[:, 0] = _with_sum_rows(v.T.reshape(DIF_HEADS, 2 * DIF_HEAD_DIM, v.shape[0])).astype(BF16)


def _dif_prep(u, gq, gk, tabs):
    t = u.shape[0]
    tm = ROW_TILE
    nt = t // tm
    nsub = 2 * DIF_HEADS
    const = lambda i: (0, 0)
    rows = lambda i: (i, 0)
    return pl.pallas_call(
        _dif_prep_kernel,
        out_shape=(jax.ShapeDtypeStruct((nsub, LANES, t), BF16),
                   jax.ShapeDtypeStruct((nsub, nt, tm, LANES), BF16),
                   jax.ShapeDtypeStruct((DIF_HEADS, nt, 2 * DIF_HEAD_DIM + SUM_ROWS, tm), BF16)),
        grid=(nt,),
        in_specs=[pl.BlockSpec((tm, 1024), lambda i: (i, U_DQ // 1024)),
                  pl.BlockSpec((tm, 1024), lambda i: (i, U_DK // 1024)),
                  pl.BlockSpec((tm, 512), lambda i: (i, U_DV // 512)),
                  pl.BlockSpec((1, LANES), const),
                  pl.BlockSpec((1, LANES), const),
                  pl.BlockSpec((tm, LANES), rows),
                  pl.BlockSpec((tm, LANES), rows),
                  pl.BlockSpec((tm, LANES), rows)],
        out_specs=(pl.BlockSpec((nsub, LANES, tm), lambda i: (0, 0, i)),
                   pl.BlockSpec((nsub, 1, tm, LANES), lambda i: (0, i, 0, 0)),
                   pl.BlockSpec((DIF_HEADS, 1, 2 * DIF_HEAD_DIM + SUM_ROWS, tm), lambda i: (0, i, 0, 0))),
        compiler_params=_params("parallel"),
        name="dif_prep",
    )(u, u, u, gq, gk, *tabs)


ATTN_GROUP = 2
ATTN_TRIP = 64
SUM_ROWS = 16


def _with_sum_rows(vt):
    heads, _, keys = vt.shape
    row = lax.broadcasted_iota(jnp.int32, (heads, SUM_ROWS, keys), 1)
    return jnp.concatenate([vt, jnp.where(row == 0, 1.0, 0.0).astype(vt.dtype)], axis=1)


def _attn_group(qt_ref, k_ref, vt_ref, v_of_sub, s_buf, p_buf, acc_ref, n_chunks):
    tq = qt_ref.shape[2]
    last = n_chunks - 1

    def scores(c):
        out = []
        for a in range(ATTN_GROUP):
            s = _dot(k_ref[a, c], qt_ref[a])
            out.append((s, jnp.max(s, axis=0, keepdims=True)))
        return out

    def stash(sc, slot):
        for a in range(ATTN_GROUP):
            s_buf[slot, a] = sc[a][0]
        return tuple(mx for (_, mx) in sc)

    def softmax(s_of, mx, slot, ms):
        new_ms, alphas = [], []
        for a in range(ATTN_GROUP):
            m_new = jnp.maximum(ms[a], mx[a])
            alphas.append(jnp.exp2(ms[a] - m_new))
            p_buf[slot, a] = jnp.exp2(s_of(a) - m_new).astype(BF16)
            new_ms.append(m_new)
        return tuple(new_ms), tuple(alphas)

    def values(c, slot, alphas):
        for a in range(ATTN_GROUP):
            acc_ref[a] = alphas[a] * acc_ref[a] + _dot(vt_ref[v_of_sub[a], c], p_buf[slot, a])

    per_trip = math.gcd(last, ATTN_TRIP)

    def trip(t, state):
        ms, alphas, mx0 = state
        c = per_trip * t + 1
        for pair in range(per_trip // 2):
            cur, nxt = pair % 2, 1 - pair % 2
            even = scores(c + 1)
            values(c - 1, 0, alphas)
            ms, alphas = softmax(lambda a: s_buf[cur, a], mx0, 1, ms)
            mx0 = stash(scores(jnp.minimum(c + 2, last)), nxt)
            values(c, 1, alphas)
            ms, alphas = softmax(lambda a: even[a][0], [mx for (_, mx) in even], 0, ms)
            c = c + 2
        return ms, alphas, mx0

    acc_ref[...] = jnp.zeros(acc_ref.shape, F32)
    ms = tuple(jnp.full((1, tq), -jnp.inf, F32) for _ in range(ATTN_GROUP))
    first = scores(0)
    ms, alphas = softmax(lambda a: first[a][0], [mx for (_, mx) in first], 0, ms)
    mx0 = stash(scores(1), 0)
    n_trips = jnp.where(pl.program_id(1) > 0, last // per_trip, 0)
    ms, alphas, _ = lax.fori_loop(0, n_trips, trip, (ms, alphas, mx0))
    values(per_trip * n_trips, 0, alphas)


def _normalised(acc_ref, a):
    dv = acc_ref.shape[1] - SUM_ROWS
    return acc_ref[a, :dv] * (1.0 / acc_ref[a, dv:dv + 1])


def _mla_attn_kernel(qt_ref, k_ref, vt_ref, o_ref, s_buf, p_buf, acc_ref, *, n_chunks):
    _attn_group(qt_ref, k_ref, vt_ref, tuple(range(ATTN_GROUP)), s_buf, p_buf, acc_ref, n_chunks)
    o = jnp.concatenate([_normalised(acc_ref, a) for a in range(ATTN_GROUP)], axis=0)
    o_ref[...] = o.T


def _dif_attn_kernel(lam_ref, gsub_ref, qt_ref, k_ref, vt_ref, o_ref, s_buf, p_buf, acc_ref,
                     *, n_chunks, lam_init):
    _attn_group(qt_ref, k_ref, vt_ref, tuple(a // 2 for a in range(ATTN_GROUP)), s_buf, p_buf,
                acc_ref, n_chunks)
    lp = lam_ref[...]
    lam = (jnp.exp(jnp.sum(lp[0:1] * lp[1:2], axis=-1, keepdims=True))
           - jnp.exp(jnp.sum(lp[2:3] * lp[3:4], axis=-1, keepdims=True)) + lam_init)
    outs = []
    for h in range(ATTN_GROUP // 2):
        o = _normalised(acc_ref, 2 * h) - lam * _normalised(acc_ref, 2 * h + 1)
        outs.append(o * lax.rsqrt(jnp.mean(o * o, axis=0, keepdims=True) + EPS))
    gsub = gsub_ref[...] * (1.0 - lam_init)
    o_ref[...] = jnp.concatenate(outs, axis=0).T * jnp.concatenate([gsub] * len(outs), axis=1)


def _attention(qt, k, vt, *, lam=None, gsub=None, lam_init=None):
    nsub, _, t = qt.shape
    n_chunks = k.shape[1]
    assert (n_chunks - 1) % 4 == 0 and nsub % ATTN_GROUP == 0
    tq = KEY_CHUNK
    dv = vt.shape[2] - SUM_ROWS
    groups = nsub // ATTN_GROUP
    n_v = vt.shape[0] // groups
    out_w = n_v * dv
    resident = dict(pipeline_mode=pl.Buffered(1))
    specs = [pl.BlockSpec((ATTN_GROUP, LANES, tq), lambda g, i: (g, 0, i)),
             pl.BlockSpec((ATTN_GROUP, n_chunks, KEY_CHUNK, LANES), lambda g, i: (g, 0, 0, 0), **resident),
             pl.BlockSpec((n_v, n_chunks, dv + SUM_ROWS, KEY_CHUNK), lambda g, i: (g, 0, 0, 0), **resident)]
    if lam is None:
        body = functools.partial(_mla_attn_kernel, n_chunks=n_chunks)
        args = (qt, k, vt)
    else:
        body = functools.partial(_dif_attn_kernel, n_chunks=n_chunks, lam_init=lam_init)
        specs = [pl.BlockSpec((8, LANES), lambda g, i: (0, 0)),
                 pl.BlockSpec((1, LANES), lambda g, i: (0, 0))] + specs
        args = (lam, gsub, qt, k, vt)
    return pl.pallas_call(
        body,
        out_shape=jax.ShapeDtypeStruct((t, groups * out_w), F32),
        grid=(groups, t // tq),
        in_specs=specs,
        out_specs=pl.BlockSpec((tq, out_w), lambda g, i: (i, g)),
        scratch_shapes=[pltpu.VMEM((2, ATTN_GROUP, KEY_CHUNK, tq), F32),
                        pltpu.VMEM((2, ATTN_GROUP, KEY_CHUNK, tq), BF16),
                        pltpu.VMEM((ATTN_GROUP, dv + SUM_ROWS, tq), F32)],
        compiler_params=_params("parallel", "arbitrary"),
        name="mla_attention" if lam is None else "dif_attention",
    )(*args)


def _ssm_prep_kernel(x_ref, prev_ref, next_ref, dt_ref, w_ref, b_ref, dtb_ref, xo_ref, dto_ref, e_ref,
                     *, n_ctx, n_tok, tm):
    row0 = pl.program_id(0) * tm
    pad = SSM_CONV // 2
    has_prev = jnp.logical_and(row0 != 0, row0 != n_ctx)
    has_next = jnp.logical_and(row0 + tm != n_ctx, row0 + tm != n_tok)
    e_ref[0:8] = jnp.where(has_prev, prev_ref[...], 0.0)
    e_ref[8:8 + tm] = x_ref[...]
    e_ref[8 + tm:16 + tm] = jnp.where(has_next, next_ref[...], 0.0)
    acc = jnp.zeros(x_ref.shape, F32) + b_ref[...]
    for k in range(SSM_CONV):
        acc = acc + w_ref[k:k + 1, :] * e_ref[pl.ds(8 - pad + k, tm), :]
    xo_ref[...] = _silu(acc)
    d = dt_ref[...] + dtb_ref[...]
    dto_ref[...] = jnp.maximum(d, 0.0) + jnp.log1p(jnp.exp(-jnp.abs(d)))


def _ssm_prep(u, conv_w, conv_b, dt_bias, n_ctx):
    t = u.shape[0]
    tm = ROW_TILE
    nt = t // tm
    cb = U_XBC // SSM_XBC
    const = lambda i: (0, 0)
    return pl.pallas_call(
        functools.partial(_ssm_prep_kernel, n_ctx=n_ctx, n_tok=t, tm=tm),
        out_shape=(jax.ShapeDtypeStruct((t, SSM_XBC), F32), jax.ShapeDtypeStruct((t, LANES), F32)),
        grid=(nt,),
        in_specs=[pl.BlockSpec((tm, SSM_XBC), lambda i: (i, cb)),
                  pl.BlockSpec((8, SSM_XBC), lambda i: (jnp.maximum(i * (tm // 8) - 1, 0), cb)),
                  pl.BlockSpec((8, SSM_XBC), lambda i: (jnp.minimum((i + 1) * (tm // 8), t // 8 - 1), cb)),
                  pl.BlockSpec((tm, LANES), lambda i: (i, U_DT // LANES)),
                  pl.BlockSpec((8, SSM_XBC), const),
                  pl.BlockSpec((1, SSM_XBC), const),
                  pl.BlockSpec((1, LANES), const)],
        out_specs=(pl.BlockSpec((tm, SSM_XBC), lambda i: (i, 0)),
                   pl.BlockSpec((tm, LANES), lambda i: (i, 0))),
        scratch_shapes=[pltpu.VMEM((tm + 16, SSM_XBC), F32)],
        compiler_params=_params("parallel"),
        name="ssm_prep",
    )(u, u, u, u, conv_w, conv_b, dt_bias)


def _ssd_kernel(xf_ref, dtf_ref, xr_ref, dtr_ref, alog_ref, of_ref, or_ref, hf_ref, hr_ref):
    @pl.when(pl.program_id(0) == 0)
    def _():
        hf_ref[...] = jnp.zeros(hf_ref.shape, F32)
        hr_ref[...] = jnp.zeros(hr_ref.shape, F32)

    lc = SSM_CHUNK
    a_row = -jnp.exp(alog_ref[...])
    n_sub = xf_ref.shape[0] // lc
    fwd, bwd = [None] * n_sub, [None] * n_sub
    for j in range(n_sub):
        rows = slice(lc * j, lc * (j + 1))
        fwd[j] = _ssd_chunk(xf_ref[rows, :], dtf_ref[rows, :], a_row, hf_ref, 0)
        jr = n_sub - 1 - j
        rows = slice(lc * jr, lc * (jr + 1))
        bwd[jr] = _ssd_chunk(xr_ref[rows, :], dtr_ref[rows, :], a_row, hr_ref, 1)
    of_ref[...] = jnp.concatenate(fwd, axis=0)
    or_ref[...] = jnp.concatenate(bwd, axis=0)


def _ssd_chunk(xbc, dt, a_row, h_ref, direction):
    lc = SSM_CHUNK
    hd, per_group = SSM_HEAD_DIM, SSM_HEADS // SSM_GROUPS
    dta = dt * a_row
    r = lax.broadcasted_iota(jnp.int32, (lc, lc), 0)
    c = lax.broadcasted_iota(jnp.int32, (lc, lc), 1)
    keep = (r >= c) if direction == 0 else (r <= c)
    tri = jnp.where(keep, 1.0, 0.0).astype(F32)
    cum = jnp.dot(tri, dta, preferred_element_type=F32, precision=lax.Precision.HIGHEST)
    cum_t = cum.T
    dt_t = dt.T
    total = jnp.sum(dta, axis=0, keepdims=True)
    chunk_decay = jnp.exp(total)
    w_t = (dt * jnp.exp(total - cum)).T
    outs = []
    for g in range(SSM_GROUPS):
        b_f32 = xbc[:, SSM_WIDTH + SSM_STATE * g:SSM_WIDTH + SSM_STATE * (g + 1)]
        c_g = xbc[:, SSM_WIDTH + SSM_STATE * (SSM_GROUPS + g):
                  SSM_WIDTH + SSM_STATE * (SSM_GROUPS + g + 1)].astype(BF16)
        cb = _dot_nt(c_g, b_f32.astype(BF16))
        b_t = b_f32.T
        for h in range(per_group * g, per_group * (g + 1)):
            col = direction * SSM_HEADS + h
            cum_b = jnp.broadcast_to(cum[:, col:col + 1], (lc, lc))
            seg = jnp.exp(jnp.where(keep, cum_b - cum_t[col:col + 1, :], -jnp.inf))
            x_h = xbc[:, hd * h:hd * (h + 1)].astype(BF16)
            y = _dot((cb * seg * dt_t[col:col + 1, :]).astype(BF16), x_h)
            state = h_ref[h]
            y_off = _dot(c_g, state.astype(BF16))
            outs.append(y + y_off * jnp.exp(cum_b[:, :hd]))
            upd = _dot((b_t * w_t[col:col + 1, :]).astype(BF16), x_h)
            h_ref[h] = state * chunk_decay[:, col:col + 1] + upd
    return jnp.concatenate(outs, axis=1)


SSD_STEP = 2 * SSM_CHUNK


def _ssd(xbc, dt, a_log, n_ctx):
    t = xbc.shape[0]
    lc = SSD_STEP
    assert n_ctx % lc == 0 and t % lc == 0
    nc = t // lc
    ncc = n_ctx // lc
    fwd = lambda s: (s, 0)
    bwd = lambda s: (jnp.where(s < ncc, ncc - 1 - s, nc - 1 - (s - ncc)), 0)
    state = pltpu.VMEM((SSM_HEADS, SSM_STATE, SSM_HEAD_DIM), F32)
    out = jax.ShapeDtypeStruct((t, SSM_WIDTH), F32)
    return pl.pallas_call(
        _ssd_kernel,
        out_shape=(out, out),
        grid=(nc,),
        in_specs=[pl.BlockSpec((lc, SSM_XBC), fwd), pl.BlockSpec((lc, LANES), fwd),
                  pl.BlockSpec((lc, SSM_XBC), bwd), pl.BlockSpec((lc, LANES), bwd),
                  pl.BlockSpec((1, LANES), lambda s: (0, 0))],
        out_specs=(pl.BlockSpec((lc, SSM_WIDTH), fwd), pl.BlockSpec((lc, SSM_WIDTH), bwd)),
        scratch_shapes=[state, state],
        compiler_params=_params("arbitrary"),
        name="ssd_scan",
    )(xbc, dt, xbc, dt, a_log)


def _merge_kernel(x_ref, gate_ref, ya_ref, yb_ref, yf_ref, yr_ref, xs_ref, z_ref,
                  bg_ref, dskip_ref, gssm_ref, wa_ref, wb_ref, wc_ref, wo_ref,
                  ml_ref, mc_ref, g2_ref, wr_ref, br_ref,
                  xo_ref, f_ref, lg_ref, *, n_ctx, tm):
    y = (yf_ref[...] + yr_ref[...] + dskip_ref[...] * xs_ref[...]) * _silu(z_ref[...])
    gw = SSM_WIDTH // SSM_GROUPS
    yc = jnp.concatenate([_rms_rows(y[:, gw * g:gw * (g + 1)], gw) for g in range(SSM_GROUPS)], axis=1)
    yc = yc * gssm_ref[...]
    gate = _sigmoid(gate_ref[...] + bg_ref[...])
    m = (gate[:, :D_MODEL] * _dot(ya_ref[...].astype(BF16), wa_ref[...])
         + gate[:, D_MODEL:2 * D_MODEL] * _dot(yb_ref[...].astype(BF16), wb_ref[...])
         + gate[:, 2 * D_MODEL:] * _dot(yc.astype(BF16), wc_ref[...]))
    out = _dot(m.astype(BF16), wo_ref[...])
    row0 = pl.program_id(0) * tm
    is_ctx = (row0 + lax.broadcasted_iota(jnp.int32, (tm, 1), 0)) < n_ctx
    gt1 = jnp.where(is_ctx, mc_ref[2:3, :], ml_ref[2:3, :])
    x_new = x_ref[...] + gt1 * out
    xo_ref[...] = x_new
    f = _modulated_norm(x_new, g2_ref[...], ml_ref[3:5, :], mc_ref[3:5, :], row0, n_ctx)
    f_ref[...] = f
    lg_ref[...] = _dot(f.astype(BF16), wr_ref[...]) + br_ref[...]


def _merge(x_all, u, ya, yb, yf, yr, xbc, b_gate, dskip, g_ssm, wa, wb, wc, wo, mod_lat, mod_ctx,
           g2, w_router, b_router, n_ctx):
    t, d = x_all.shape
    tm = ROW_TILE
    const = lambda i: (0, 0)
    rows = lambda i: (i, 0)
    full = lambda a: pl.BlockSpec(a.shape, const)
    return pl.pallas_call(
        functools.partial(_merge_kernel, n_ctx=n_ctx, tm=tm),
        out_shape=(jax.ShapeDtypeStruct((t, d), F32), jax.ShapeDtypeStruct((t, d), F32),
                   jax.ShapeDtypeStruct((t, LANES), F32)),
        grid=(t // tm,),
        in_specs=[pl.BlockSpec((tm, d), rows),
                  pl.BlockSpec((tm, GATE_COLS), lambda i: (i, U_GATE // GATE_COLS)),
                  pl.BlockSpec((tm, MLA_WIDTH), rows),
                  pl.BlockSpec((tm, DIF_WIDTH), rows),
                  pl.BlockSpec((tm, SSM_WIDTH), rows),
                  pl.BlockSpec((tm, SSM_WIDTH), rows),
                  pl.BlockSpec((tm, SSM_WIDTH), rows),
                  pl.BlockSpec((tm, SSM_WIDTH), lambda i: (i, U_Z // SSM_WIDTH)),
                  full(b_gate), full(dskip), full(g_ssm), full(wa), full(wb), full(wc), full(wo),
                  full(mod_lat), full(mod_ctx), full(g2), full(w_router), full(b_router)],
        out_specs=(pl.BlockSpec((tm, d), rows), pl.BlockSpec((tm, d), rows),
                   pl.BlockSpec((tm, LANES), rows)),
        compiler_params=_params("parallel"),
        name="merge",
    )(x_all, u, ya, yb, yf, yr, xbc, u, b_gate, dskip, g_ssm, wa, wb, wc, wo, mod_lat, mod_ctx,
      g2, w_router, b_router)


def _moe_kernel(be_ref, nb_ref, x_ref, wgu_ref, bgu_ref, wd_ref, bd_ref, o_ref, wgu_s, wd_s):
    b = pl.program_id(0)
    prev = be_ref[jnp.maximum(b - 1, 0)]
    fresh = jnp.logical_or(b == 0, be_ref[b] != prev)

    @pl.when(fresh)
    def _():
        wgu_s[...] = wgu_ref[...].astype(BF16)
        wd_s[...] = wd_ref[...].astype(BF16)

    @pl.when(b < nb_ref[0])
    def _():
        gu = _dot(x_ref[...].astype(BF16), wgu_s[...]) + bgu_ref[...]
        glu = jnp.minimum(gu[:, :D_FF], SWIGLU_LIMIT)
        lin = jnp.clip(gu[:, D_FF:], -SWIGLU_LIMIT, SWIGLU_LIMIT)
        act = glu * _sigmoid(SWIGLU_ALPHA * glu) * (lin + 1.0)
        o_ref[...] = _dot(act.astype(BF16), wd_s[...]) + bd_ref[...]

    @pl.when(b >= nb_ref[0])
    def _():
        o_ref[...] = jnp.zeros(o_ref.shape, F32)


def _moe_experts(block_e, n_used, x_sorted, w_gu, b_gu, w_down, b_down, layer):
    n_slots, d = x_sorted.shape
    n_blocks = n_slots // MOE_BLOCK
    grid_spec = pltpu.PrefetchScalarGridSpec(
        num_scalar_prefetch=2,
        grid=(n_blocks,),
        in_specs=[pl.BlockSpec((MOE_BLOCK, d), lambda b, be, nb: (b, 0)),
                  pl.BlockSpec((None, None, d, 2 * D_FF), lambda b, be, nb: (layer, be[b], 0, 0)),
                  pl.BlockSpec((None, None, 1, 2 * D_FF), lambda b, be, nb: (layer, be[b], 0, 0)),
                  pl.BlockSpec((None, None, D_FF, d), lambda b, be, nb: (layer, be[b], 0, 0)),
                  pl.BlockSpec((None, None, 1, d), lambda b, be, nb: (layer, be[b], 0, 0))],
        out_specs=pl.BlockSpec((MOE_BLOCK, d), lambda b, be, nb: (b, 0)),
        scratch_shapes=[pltpu.VMEM((d, 2 * D_FF), BF16), pltpu.VMEM((D_FF, d), BF16)],
    )
    return pl.pallas_call(
        _moe_kernel,
        out_shape=jax.ShapeDtypeStruct((n_slots, d), F32),
        grid_spec=grid_spec,
        compiler_params=_params("arbitrary"),
        name="moe_experts",
    )(block_e, n_used, x_sorted, w_gu, b_gu, w_down, b_down)


def _combine_kernel(x_ref, g_ref, ml_ref, mc_ref, *rest, n_ctx, tm):
    y_refs, o_ref = rest[:TOP_K], rest[TOP_K]
    is_ctx = (pl.program_id(0) * tm + lax.broadcasted_iota(jnp.int32, (tm, 1), 0)) < n_ctx
    gate2 = jnp.where(is_ctx, mc_ref[...], ml_ref[...])
    g = g_ref[...]
    y = g[:, 0:1] * y_refs[0][...]
    for k in range(1, TOP_K):
        y = y + g[:, k:k + 1] * y_refs[k][...]
    o_ref[...] = x_ref[...] + gate2 * y


def _combine(x_all, y_rows, gates, gate2_lat, gate2_ctx, n_ctx):
    t, d = x_all.shape
    tm = ROW_TILE
    rows = lambda i: (i, 0)
    const = lambda i: (0, 0)
    return pl.pallas_call(
        functools.partial(_combine_kernel, n_ctx=n_ctx, tm=tm),
        out_shape=jax.ShapeDtypeStruct((t, d), F32),
        grid=(t // tm,),
        in_specs=[pl.BlockSpec((tm, d), rows), pl.BlockSpec((tm, LANES), rows),
                  pl.BlockSpec((1, d), const), pl.BlockSpec((1, d), const)]
                 + [pl.BlockSpec((tm, d), rows)] * TOP_K,
        out_specs=pl.BlockSpec((tm, d), rows),
        compiler_params=_params("parallel"),
        name="moe_combine",
    )(x_all, gates, gate2_lat, gate2_ctx, *y_rows)


ROUTE_IDX, ROUTE_GATE, ROUTE_RANK = 0, TOP_K, 2 * TOP_K


def _router_kernel(lg_ref, o_ref, cnt_ref, run_ref, *, tm):
    @pl.when(pl.program_id(0) == 0)
    def _():
        run_ref[...] = jnp.zeros(run_ref.shape, F32)

    lane = lax.broadcasted_iota(jnp.int32, (tm, LANES), 1)
    lane_f = lane.astype(F32)
    lg = jnp.where(lane < N_EXPERTS, lg_ref[...], -jnp.inf)
    hots, vals = [], []
    for _ in range(TOP_K):
        mx = jnp.max(lg, axis=-1, keepdims=True)
        idx = jnp.min(jnp.where(lg == mx, lane_f, float(LANES)), axis=-1, keepdims=True)
        hot = lane_f == idx
        lg = jnp.where(hot, -jnp.inf, lg)
        hots.append((hot, idx))
        vals.append(mx)
    exps = [jnp.exp(v - vals[0]) for v in vals]
    inv = 1.0 / sum(exps[1:], exps[0])
    chosen = jnp.zeros((tm, LANES), F32)
    for hot, _ in hots:
        chosen = jnp.where(hot, 1.0, chosen)
    r = lax.broadcasted_iota(jnp.int32, (tm, tm), 0)
    c = lax.broadcasted_iota(jnp.int32, (tm, tm), 1)
    earlier = jnp.where(r > c, 1.0, 0.0).astype(BF16)
    before = _dot(earlier, chosen.astype(BF16)) + run_ref[0:1, :]
    out = jnp.zeros((tm, LANES), F32)
    for k, (hot, idx) in enumerate(hots):
        rank = jnp.sum(jnp.where(hot, before, 0.0), axis=-1, keepdims=True)
        out = jnp.where(lane == ROUTE_IDX + k, idx, out)
        out = jnp.where(lane == ROUTE_GATE + k, exps[k] * inv, out)
        out = jnp.where(lane == ROUTE_RANK + k, rank, out)
    o_ref[...] = out
    run_ref[...] = run_ref[...] + jnp.sum(chosen, axis=0, keepdims=True)
    cnt_ref[...] = run_ref[...]


def _router(logits):
    t = logits.shape[0]
    tm = ROW_TILE
    return pl.pallas_call(
        functools.partial(_router_kernel, tm=tm),
        out_shape=(jax.ShapeDtypeStruct((t, LANES), F32), jax.ShapeDtypeStruct((8, LANES), F32)),
        grid=(t // tm,),
        in_specs=[pl.BlockSpec((tm, LANES), lambda i: (i, 0))],
        out_specs=(pl.BlockSpec((tm, LANES), lambda i: (i, 0)), pl.BlockSpec((8, LANES), lambda i: (0, 0))),
        scratch_shapes=[pltpu.VMEM((8, LANES), F32)],
        compiler_params=_params("arbitrary"),
        name="moe_router",
    )(logits)


def _route(logits, n_tok):
    routed, counts = _router(logits)
    top_idx = routed[:, ROUTE_IDX:ROUTE_IDX + TOP_K].astype(jnp.int32)
    gates = routed[:, ROUTE_GATE:ROUTE_GATE + TOP_K]
    rank = routed[:, ROUTE_RANK:ROUTE_RANK + TOP_K].astype(jnp.int32)
    n_assign = n_tok * TOP_K
    counts = counts[0, :N_EXPERTS].astype(jnp.int32)
    padded = (counts + MOE_BLOCK - 1) // MOE_BLOCK * MOE_BLOCK
    pad_end = jnp.cumsum(padded)
    pad_start = pad_end - padded
    slot_of = pad_start[top_idx] + rank
    n_blocks = -(-(n_assign + N_EXPERTS * (MOE_BLOCK - 1)) // MOE_BLOCK)
    n_slots = n_blocks * MOE_BLOCK
    token = jnp.arange(n_assign, dtype=jnp.int32) // TOP_K
    slot_tok = jnp.zeros((n_slots,), jnp.int32).at[slot_of.reshape(-1)].set(token, unique_indices=True)
    block_start = jnp.arange(n_blocks, dtype=jnp.int32) * MOE_BLOCK
    block_e = jnp.minimum(jnp.searchsorted(pad_end, block_start, side='right'), N_EXPERTS - 1)
    n_used = (pad_end[-1] // MOE_BLOCK).reshape(1)
    return gates, slot_tok, slot_of, block_e.astype(jnp.int32), n_used.astype(jnp.int32)


def kernel(x, c, ctx, c_ctx, w_mod, b_mod, g_norm1, g_norm2, w_in, b_gate, mla_g_q, mla_w_uq, mla_g_kv, mla_w_ukv, mla_g_qn, mla_g_kn, dif_g_qn, dif_g_kn, dif_lambda, dif_g_sub, ssm_conv_w, ssm_conv_b, ssm_dt_bias, ssm_a_log, ssm_d, ssm_g_norm, w_up_mla, w_up_dif, w_up_ssm, w_out, moe_w_router, moe_b_router, moe_w_gu, moe_b_gu, moe_w_down, moe_b_down):
    assert x.shape[0] == 1 and ctx.shape[0] == 1
    depth = w_in.shape[0]
    seq = x.shape[1]
    n_ctx = ctx.shape[1]
    n_tok = n_ctx + seq
    d = D_MODEL
    assert n_ctx == KEY_CHUNK and n_tok % ROW_TILE == 0 and seq % GRID_W == 0

    x_all = jnp.concatenate([ctx[0], x[0]], axis=0)
    cc = jnp.zeros((8, d), F32).at[0].set(c[0]).at[1].set(c_ctx)
    mod = _mod_vectors(cc, w_mod, b_mod)
    mod = mod[:, :2].reshape(depth, 2, 6, d)

    rope_mla = _rope_tables(seq, n_ctx, MLA_ROPE, MLA_NOPE)
    rope_dif = _rope_tables(seq, n_ctx, DIF_HEAD_DIM, 0)
    w_in_all = _take_columns(w_in, _in_proj_columns()).astype(BF16)
    w_uq_all = _take_columns(mla_w_uq, _head_columns(MLA_HEADS, MLA_QK, 0, MLA_QK, LANES)).astype(BF16)
    w_uk_all = _take_columns(mla_w_ukv, _head_columns(MLA_HEADS, MLA_NOPE + MLA_V, 0, MLA_NOPE, LANES)).astype(BF16)
    w_uv_all = _take_columns(mla_w_ukv, _head_columns(MLA_HEADS, MLA_NOPE + MLA_V, MLA_NOPE, MLA_V, MLA_V)).astype(BF16)
    w_up_all = [w.astype(BF16) for w in (w_up_mla, w_up_dif, w_up_ssm, w_out)]

    for i in range(depth):
        lam_init = 0.8 - 0.6 * math.exp(-0.3 * i)
        mod_lat, mod_ctx = mod[i, 0], mod[i, 1]
        u = _in_proj(x_all, g_norm1[i][None], mod_lat[0:2], mod_ctx[0:2], w_in_all, i, n_ctx)

        qt, k, vt = _mla_prep(
            u, mla_g_q[i][None], w_uq_all[i], mla_g_kv[i][None], w_uk_all[i], w_uv_all[i],
            _pad_lanes(mla_g_qn[i][None]), _pad_lanes(mla_g_kn[i][None]), rope_mla)
        ya = _attention(qt, k, vt)

        qt, k, vt = _dif_prep(u, _pad_lanes(dif_g_qn[i][None]), _pad_lanes(dif_g_kn[i][None]), rope_dif)
        lam_rows = jnp.zeros((8, LANES), F32).at[:4, :DIF_HEAD_DIM].set(dif_lambda[i])
        yb = _attention(qt, k, vt, lam=lam_rows, gsub=dif_g_sub[i][None], lam_init=lam_init)

        conv_w = jnp.zeros((8, SSM_XBC), F32).at[:SSM_CONV].set(ssm_conv_w[i])
        xbc, dt = _ssm_prep(u, conv_w, ssm_conv_b[i][None], _pad_lanes(ssm_dt_bias[i].reshape(1, -1)), n_ctx)
        a_log = _pad_lanes(ssm_a_log[i].reshape(1, -1))
        yf, yr = _ssd(xbc, dt, a_log, n_ctx)

        dskip = jnp.repeat(ssm_d[i, 0] + ssm_d[i, 1], SSM_HEAD_DIM)[None]
        w_router = jnp.zeros((d, LANES), BF16).at[:, :N_EXPERTS].set(moe_w_router[i].astype(BF16))
        b_router = jnp.zeros((1, LANES), F32).at[0, :N_EXPERTS].set(moe_b_router[i])
        x_all, f, logits = _merge(
            x_all, u, ya, yb, yf, yr, xbc, b_gate[i][None], dskip, ssm_g_norm[i][None],
            w_up_all[0][i], w_up_all[1][i], w_up_all[2][i], w_up_all[3][i],
            mod_lat[0:5], mod_ctx[0:5], g_norm2[i][None],
            w_router, b_router, n_ctx)

        gates, slot_tok, slot_of, block_e, n_used = _route(logits, n_tok)
        y_slots = _moe_experts(block_e, n_used, f[slot_tok], moe_w_gu,
                               moe_b_gu.reshape(depth, N_EXPERTS, 1, 2 * D_FF), moe_w_down,
                               moe_b_down.reshape(depth, N_EXPERTS, 1, d), i)
        y_rows = [y_slots[slot_of[:, k]] for k in range(TOP_K)]
        x_all = _combine(x_all, y_rows, _pad_lanes(gates), mod_lat[5:6], mod_ctx[5:6], n_ctx)
    return x_all[n_ctx:][None]
```

```python
import functools
import math

import numpy as np
import jax
import jax.numpy as jnp
from jax import lax
from jax.experimental import pallas as pl
from jax.experimental.pallas import tpu as pltpu

F32 = jnp.float32
BF16 = jnp.bfloat16
LANES = 128
VMEM_LIMIT = 52 * 1024 * 1024

D_MODEL = 1024
EPS = 1e-6
ROPE_THETA = 10000.0
GRID_W = 64
N_BRANCH = 3

MLA_HEADS = 8
MLA_Q_RANK = 256
MLA_KV_RANK = 128
MLA_NOPE = 64
MLA_ROPE = 32
MLA_V = 64
MLA_QK = MLA_NOPE + MLA_ROPE
MLA_WIDTH = MLA_HEADS * MLA_V

DIF_HEADS = 4
DIF_HEAD_DIM = 64
DIF_WIDTH = DIF_HEADS * 2 * DIF_HEAD_DIM

SSM_HEADS = 8
SSM_HEAD_DIM = 64
SSM_WIDTH = SSM_HEADS * SSM_HEAD_DIM
SSM_GROUPS = 2
SSM_STATE = 128
SSM_CONV = 5
SSM_CHUNK = 128
SSM_XBC = SSM_WIDTH + 2 * SSM_GROUPS * SSM_STATE

N_EXPERTS = 32
TOP_K = 4
D_FF = 1024
SWIGLU_LIMIT = 7.0
SWIGLU_ALPHA = 1.702
MOE_BLOCK = 512

MLA_COLS = MLA_Q_RANK + MLA_KV_RANK + MLA_ROPE
DIF_COLS = 3 * DIF_WIDTH
SSM_COLS = SSM_WIDTH + SSM_XBC + 2 * SSM_HEADS
GATE_COLS = N_BRANCH * D_MODEL

ROW_TILE = 256
KEY_CHUNK = 256
LOG2E = 1.4426950408889634

U_GATE, U_DQ, U_DK, U_XBC, U_MLA, U_DV, U_Z, U_DT = 0, 3072, 4096, 5120, 6144, 6656, 7168, 7680
U_COLS = 8192
U_TILE_N = 2048


def _in_proj_columns():
    src = np.full((U_COLS,), -1, np.int64)
    dif0 = MLA_COLS
    ssm0 = MLA_COLS + DIF_COLS
    gate0 = ssm0 + SSM_COLS
    src[U_GATE:U_GATE + GATE_COLS] = gate0 + np.arange(GATE_COLS)
    for a in range(2 * DIF_HEADS):
        src[U_DQ + LANES * a:U_DQ + LANES * a + DIF_HEAD_DIM] = dif0 + DIF_HEAD_DIM * a + np.arange(DIF_HEAD_DIM)
        src[U_DK + LANES * a:U_DK + LANES * a + DIF_HEAD_DIM] = (dif0 + DIF_WIDTH + DIF_HEAD_DIM * a
                                                                  + np.arange(DIF_HEAD_DIM))
    src[U_DV:U_DV + DIF_WIDTH] = dif0 + 2 * DIF_WIDTH + np.arange(DIF_WIDTH)
    src[U_MLA:U_MLA + MLA_Q_RANK + MLA_KV_RANK] = np.arange(MLA_Q_RANK + MLA_KV_RANK)
    pe0 = U_MLA + MLA_Q_RANK + MLA_KV_RANK + MLA_NOPE
    src[pe0:pe0 + MLA_ROPE] = MLA_Q_RANK + MLA_KV_RANK + np.arange(MLA_ROPE)
    src[U_Z:U_Z + SSM_WIDTH] = ssm0 + np.arange(SSM_WIDTH)
    src[U_XBC:U_XBC + SSM_XBC] = ssm0 + SSM_WIDTH + np.arange(SSM_XBC)
    src[U_DT:U_DT + 2 * SSM_HEADS] = ssm0 + SSM_WIDTH + SSM_XBC + np.arange(2 * SSM_HEADS)
    return src


def _take_columns(w, src):
    cols = jnp.take(w, jnp.asarray(np.maximum(src, 0)), axis=-1)
    return jnp.where(jnp.asarray(src >= 0), cols, 0.0)


def _head_columns(n_heads, src_stride, src_off, width, dst_stride):
    src = np.full((n_heads * dst_stride,), -1, np.int64)
    for h in range(n_heads):
        src[h * dst_stride:h * dst_stride + width] = h * src_stride + src_off + np.arange(width)
    return src


def _pad_lanes(v, n=LANES):
    return jnp.pad(v, [(0, 0)] * (v.ndim - 1) + [(0, n - v.shape[-1])])


def _row_tile(n, cap):
    best = 8
    for t in range(8, cap + 1, 8):
        if n % t == 0:
            best = t
    return best


def _dot(a, b):
    return jnp.dot(a, b, preferred_element_type=F32)


def _dot_nt(a, b):
    return lax.dot_general(a, b, (((1,), (1,)), ((), ())), preferred_element_type=F32)


def _sigmoid(x):
    return 1.0 / (1.0 + jnp.exp(-x))


def _silu(x):
    return x * _sigmoid(x)


def _rms_rows(x, n):
    return x * lax.rsqrt(jnp.sum(x * x, axis=-1, keepdims=True) * (1.0 / n) + EPS)


def _params(*sem):
    return pltpu.CompilerParams(dimension_semantics=sem, vmem_limit_bytes=VMEM_LIMIT)


def _mod_kernel(a_ref, w_ref, b_ref, o_ref):
    a = _silu(a_ref[...]).astype(BF16)
    o_ref[0] = _dot(a, w_ref[0].astype(BF16)) + b_ref[0]


def _mod_vectors(cc, w_mod, b_mod):
    depth, d, n = w_mod.shape
    tn = 1536
    return pl.pallas_call(
        _mod_kernel,
        out_shape=jax.ShapeDtypeStruct((depth, 8, n), F32),
        grid=(depth, n // tn),
        in_specs=[pl.BlockSpec((8, d), lambda l, j: (0, 0)),
                  pl.BlockSpec((1, d, tn), lambda l, j: (l, 0, j)),
                  pl.BlockSpec((1, 1, tn), lambda l, j: (l, 0, j))],
        out_specs=pl.BlockSpec((1, 8, tn), lambda l, j: (l, 0, j)),
        compiler_params=_params("parallel", "parallel"),
        name="mod_vectors",
    )(cc, w_mod, b_mod.reshape(depth, 1, n))


def _modulated_norm(x, g, mod_lat, mod_ctx, row0, n_ctx):
    rows = x.shape[0]
    is_ctx = (row0 + lax.broadcasted_iota(jnp.int32, (rows, 1), 0)) < n_ctx
    shift = jnp.where(is_ctx, mod_ctx[0:1, :], mod_lat[0:1, :])
    scale = jnp.where(is_ctx, mod_ctx[1:2, :], mod_lat[1:2, :])
    return _rms_rows(x, x.shape[1]) * g * (1.0 + scale) + shift


def _in_proj_kernel(x_ref, g_ref, ml_ref, mc_ref, w_ref, o_ref, h_ref, *, n_ctx, tm):
    @pl.when(pl.program_id(1) == 0)
    def _():
        h = _modulated_norm(x_ref[...], g_ref[...], ml_ref[...], mc_ref[...], pl.program_id(0) * tm, n_ctx)
        h_ref[...] = h.astype(BF16)

    o_ref[...] = _dot(h_ref[...], w_ref[...])


def _in_proj(x_all, g, mod_lat, mod_ctx, w, layer, n_ctx):
    t, d = x_all.shape
    n = w.shape[2]
    tm = _row_tile(t, 1280)
    return pl.pallas_call(
        functools.partial(_in_proj_kernel, n_ctx=n_ctx, tm=tm),
        out_shape=jax.ShapeDtypeStruct((t, n), F32),
        grid=(t // tm, n // U_TILE_N),
        in_specs=[pl.BlockSpec((tm, d), lambda i, j: (i, 0)),
                  pl.BlockSpec((1, d), lambda i, j: (0, 0)),
                  pl.BlockSpec((2, d), lambda i, j: (0, 0)),
                  pl.BlockSpec((2, d), lambda i, j: (0, 0)),
                  pl.BlockSpec((None, d, U_TILE_N), lambda i, j: (layer, 0, j))],
        out_specs=pl.BlockSpec((tm, U_TILE_N), lambda i, j: (i, j)),
        scratch_shapes=[pltpu.VMEM((tm, d), BF16)],
        compiler_params=_params("parallel", "arbitrary"),
        name="in_proj",
    )(x_all, g, mod_lat, mod_ctx, w)


def _rope_tables(seq_len, n_ctx, rot_dim, lane0):
    n_rows = seq_len // GRID_W
    row = jnp.repeat(jnp.arange(n_rows), GRID_W).astype(F32)
    col = jnp.tile(jnp.arange(GRID_W), n_rows).astype(F32)
    axis_dim = rot_dim // 2
    half = axis_dim // 2
    inv = ROPE_THETA ** (-jnp.arange(0, axis_dim, 2, dtype=F32) / axis_dim)
    ang_r = row[:, None] * inv
    ang_c = col[:, None] * inv
    zeros = jnp.zeros((seq_len, half), F32)
    cos = jnp.concatenate([jnp.cos(ang_r), jnp.cos(ang_r), jnp.cos(ang_c), jnp.cos(ang_c)], axis=1)
    s1 = jnp.concatenate([zeros, jnp.sin(ang_r), zeros, jnp.sin(ang_c)], axis=1)
    s2 = jnp.concatenate([-jnp.sin(ang_r), zeros, -jnp.sin(ang_c), zeros], axis=1)

    def place(tab, fill):
        full = jnp.full((seq_len, LANES), fill, F32).at[:, lane0:lane0 + rot_dim].set(tab)
        ctx = jnp.full((n_ctx, LANES), fill, F32)
        return jnp.concatenate([ctx, full], axis=0)

    return place(cos, 1.0), place(s1, 0.0), place(s2, 0.0)


def _norm_rope_blocks(blocks, gains, n, cos, s1, s2, half):
    sums = [jnp.sum(x * x, axis=-1, keepdims=True) for x in blocks]
    inv = [lax.rsqrt(s * (1.0 / n) + EPS) for s in sums]
    normed = [x * r * g for x, r, g in zip(blocks, inv, gains)]
    fwd = [pltpu.roll(y, half, 1) for y in normed]
    bwd = [pltpu.roll(y, LANES - half, 1) for y in normed]
    return [y * cos + a * s1 + b * s2 for y, a, b in zip(normed, fwd, bwd)]


def _mla_prep_kernel(u_ref, gq_ref, wuq_ref, gkv_ref, wk_ref, wv_ref, gqn_ref, gkn_ref,
                     cos_ref, s1_ref, s2_ref, qt_ref, k_ref, vt_ref):
    u = u_ref[...]
    cq = u[:, :MLA_Q_RANK]
    ckv = u[:, MLA_Q_RANK:MLA_Q_RANK + MLA_KV_RANK]
    pe = u[:, MLA_Q_RANK + MLA_KV_RANK:]
    q = _dot((_rms_rows(cq, MLA_Q_RANK) * gq_ref[...]).astype(BF16), wuq_ref[...])
    kv_in = (_rms_rows(ckv, MLA_KV_RANK) * gkv_ref[...]).astype(BF16)
    kn = _dot(kv_in, wk_ref[...])
    v = _dot(kv_in, wv_ref[...])
    cos, s1, s2 = cos_ref[...], s1_ref[...], s2_ref[...]
    half = MLA_ROPE // 4
    q_scale = MLA_QK ** -0.5 * LOG2E
    n = MLA_HEADS
    blocks = ([q[:, LANES * h:LANES * (h + 1)] for h in range(n)]
              + [kn[:, LANES * h:LANES * (h + 1)] + pe for h in range(n)])
    roped = _norm_rope_blocks(blocks, [gqn_ref[...]] * n + [gkn_ref[...]] * n, MLA_QK, cos, s1, s2, half)
    for h in range(n):
        qt_ref[h] = (roped[h] * q_scale).T.astype(BF16)
        k_ref[h, 0] = roped[n + h].astype(BF16)
    tm = v.shape[0]
    vt_ref[:, 0] = _with_sum_rows(v.T.reshape(MLA_HEADS, MLA_V, tm)).astype(BF16)


def _mla_prep(u, gq, wuq, gkv, wk, wv, gqn, gkn, tabs):
    t = u.shape[0]
    tm = ROW_TILE
    nt = t // tm
    const = lambda i: (0, 0)
    rows = lambda i: (i, 0)
    return pl.pallas_call(
        _mla_prep_kernel,
        out_shape=(jax.ShapeDtypeStruct((MLA_HEADS, LANES, t), BF16),
                   jax.ShapeDtypeStruct((MLA_HEADS, nt, tm, LANES), BF16),
                   jax.ShapeDtypeStruct((MLA_HEADS, nt, MLA_V + SUM_ROWS, tm), BF16)),
        grid=(nt,),
        in_specs=[pl.BlockSpec((tm, 512), lambda i: (i, U_MLA // 512)),
                  pl.BlockSpec((1, MLA_Q_RANK), const),
                  pl.BlockSpec(wuq.shape, const),
                  pl.BlockSpec((1, MLA_KV_RANK), const),
                  pl.BlockSpec(wk.shape, const),
                  pl.BlockSpec(wv.shape, const),
                  pl.BlockSpec((1, LANES), const),
                  pl.BlockSpec((1, LANES), const),
                  pl.BlockSpec((tm, LANES), rows),
                  pl.BlockSpec((tm, LANES), rows),
                  pl.BlockSpec((tm, LANES), rows)],
        out_specs=(pl.BlockSpec((MLA_HEADS, LANES, tm), lambda i: (0, 0, i)),
                   pl.BlockSpec((MLA_HEADS, 1, tm, LANES), lambda i: (0, i, 0, 0)),
                   pl.BlockSpec((MLA_HEADS, 1, MLA_V + SUM_ROWS, tm), lambda i: (0, i, 0, 0))),
        compiler_params=_params("parallel"),
        name="mla_prep",
    )(u, gq, wuq, gkv, wk, wv, gqn, gkn, *tabs)


def _dif_prep_kernel(q_ref, k_ref, v_ref, gq_ref, gk_ref, shift_ref, cos_ref, s1_ref, s2_ref,
                     qt_out, k_out, vt_out):
    cos, s1, s2 = cos_ref[...], s1_ref[...], s2_ref[...]
    half = DIF_HEAD_DIM // 4
    q_scale = DIF_HEAD_DIM ** -0.5 * LOG2E
    n = 2 * DIF_HEADS
    blocks = ([q_ref[:, LANES * a:LANES * (a + 1)] for a in range(n)]
              + [k_ref[:, LANES * a:LANES * (a + 1)] for a in range(n)])
    roped = _norm_rope_blocks(blocks, [gq_ref[...]] * n + [gk_ref[...]] * n, DIF_HEAD_DIM, cos, s1, s2, half)
    spare = lax.broadcasted_iota(jnp.int32, roped[0].shape, 1) == DIF_HEAD_DIM
    for a in range(n):
        qt_out[a] = jnp.where(spare, -shift_ref[...], roped[a] * q_scale).T.astype(BF16)
        k_out[a, 0] = jnp.where(spare, 1.0, roped[n + a]).astype(BF16)
    v = v_ref[...]
    vt_out[:, 0] = _with_sum_rows(v.T.reshape(DIF_HEADS, 2 * DIF_HEAD_DIM, v.shape[0])).astype(BF16)


def _dif_prep(u, gq, gk, shift, tabs):
    t = u.shape[0]
    tm = ROW_TILE
    nt = t // tm
    nsub = 2 * DIF_HEADS
    const = lambda i: (0, 0)
    rows = lambda i: (i, 0)
    return pl.pallas_call(
        _dif_prep_kernel,
        out_shape=(jax.ShapeDtypeStruct((nsub, LANES, t), BF16),
                   jax.ShapeDtypeStruct((nsub, nt, tm, LANES), BF16),
                   jax.ShapeDtypeStruct((DIF_HEADS, nt, 2 * DIF_HEAD_DIM + SUM_ROWS, tm), BF16)),
        grid=(nt,),
        in_specs=[pl.BlockSpec((tm, 1024), lambda i: (i, U_DQ // 1024)),
                  pl.BlockSpec((tm, 1024), lambda i: (i, U_DK // 1024)),
                  pl.BlockSpec((tm, 512), lambda i: (i, U_DV // 512)),
                  pl.BlockSpec((1, LANES), const),
                  pl.BlockSpec((1, LANES), const),
                  pl.BlockSpec((1, LANES), const),
                  pl.BlockSpec((tm, LANES), rows),
                  pl.BlockSpec((tm, LANES), rows),
                  pl.BlockSpec((tm, LANES), rows)],
        out_specs=(pl.BlockSpec((nsub, LANES, tm), lambda i: (0, 0, i)),
                   pl.BlockSpec((nsub, 1, tm, LANES), lambda i: (0, i, 0, 0)),
                   pl.BlockSpec((DIF_HEADS, 1, 2 * DIF_HEAD_DIM + SUM_ROWS, tm), lambda i: (0, i, 0, 0))),
        compiler_params=_params("parallel"),
        name="dif_prep",
    )(u, u, u, gq, gk, shift, *tabs)


ATTN_GROUP = 2
ATTN_TRIP = 64
SUM_ROWS = 16


def _with_sum_rows(vt):
    heads, _, keys = vt.shape
    row = lax.broadcasted_iota(jnp.int32, (heads, SUM_ROWS, keys), 1)
    return jnp.concatenate([vt, jnp.where(row == 0, 1.0, 0.0).astype(vt.dtype)], axis=1)


def _attn_group(qt_ref, k_ref, vt_ref, v_of_sub, s_buf, p_buf, acc_ref, n_chunks):
    tq = qt_ref.shape[2]
    last = n_chunks - 1

    def scores(c):
        out = []
        for a in range(ATTN_GROUP):
            s = _dot(k_ref[a, c], qt_ref[a])
            out.append((s, jnp.max(s, axis=0, keepdims=True)))
        return out

    def stash(sc, slot):
        for a in range(ATTN_GROUP):
            s_buf[slot, a] = sc[a][0]
        return tuple(mx for (_, mx) in sc)

    def softmax(s_of, mx, slot, ms):
        new_ms, alphas = [], []
        for a in range(ATTN_GROUP):
            m_new = jnp.maximum(ms[a], mx[a])
            alphas.append(jnp.exp2(ms[a] - m_new))
            p_buf[slot, a] = jnp.exp2(s_of(a) - m_new).astype(BF16)
            new_ms.append(m_new)
        return tuple(new_ms), tuple(alphas)

    def values(c, slot, alphas):
        for a in range(ATTN_GROUP):
            acc_ref[a] = alphas[a] * acc_ref[a] + _dot(vt_ref[v_of_sub[a], c], p_buf[slot, a])

    per_trip = math.gcd(last, ATTN_TRIP)

    def trip(t, state):
        ms, alphas, mx0 = state
        c = per_trip * t + 1
        for pair in range(per_trip // 2):
            cur, nxt = pair % 2, 1 - pair % 2
            even = scores(c + 1)
            values(c - 1, 0, alphas)
            ms, alphas = softmax(lambda a: s_buf[cur, a], mx0, 1, ms)
            mx0 = stash(scores(jnp.minimum(c + 2, last)), nxt)
            values(c, 1, alphas)
            ms, alphas = softmax(lambda a: even[a][0], [mx for (_, mx) in even], 0, ms)
            c = c + 2
        return ms, alphas, mx0

    acc_ref[...] = jnp.zeros(acc_ref.shape, F32)
    ms = tuple(jnp.full((1, tq), -jnp.inf, F32) for _ in range(ATTN_GROUP))
    first = scores(0)
    ms, alphas = softmax(lambda a: first[a][0], [mx for (_, mx) in first], 0, ms)
    mx0 = stash(scores(1), 0)
    n_trips = jnp.where(pl.program_id(1) > 0, last // per_trip, 0)
    ms, alphas, _ = lax.fori_loop(0, n_trips, trip, (ms, alphas, mx0))
    values(per_trip * n_trips, 0, alphas)


def _normalised(acc_ref, a):
    dv = acc_ref.shape[1] - SUM_ROWS
    return acc_ref[a, :dv] * (1.0 / acc_ref[a, dv:dv + 1])


PV_SPAN = 32
MAX_SHIFT = 60.0


def _attn_group_shifted(qt_ref, k_ref, vt_ref, v_of_sub, p_buf, acc_ref, n_chunks):
    span = math.gcd(n_chunks - 1, PV_SPAN)

    def accumulate(c0, n, slot):
        for a in range(ATTN_GROUP):
            for i in range(n):
                s = _dot(k_ref[a, c0 + i], qt_ref[a])
                p_buf[slot, a, KEY_CHUNK * i:KEY_CHUNK * (i + 1)] = jnp.exp2(s).astype(BF16)
            vt = jnp.concatenate([vt_ref[v_of_sub[a], c0 + i] for i in range(n)], axis=1)
            acc_ref[a] = acc_ref[a] + _dot(vt, p_buf[slot, a, :KEY_CHUNK * n])

    acc_ref[...] = jnp.zeros(acc_ref.shape, F32)
    accumulate(0, 1, 0)

    @pl.when(pl.program_id(1) > 0)
    def _():
        for j in range((n_chunks - 1) // span):
            accumulate(1 + span * j, span, j % 2)


def _attend(shifted, qt_ref, k_ref, vt_ref, v_of_sub, scratch, n_chunks):
    if shifted:
        p_buf, acc_ref = scratch
        _attn_group_shifted(qt_ref, k_ref, vt_ref, v_of_sub, p_buf, acc_ref, n_chunks)
    else:
        s_buf, p_buf, acc_ref = scratch
        _attn_group(qt_ref, k_ref, vt_ref, v_of_sub, s_buf, p_buf, acc_ref, n_chunks)
    return acc_ref


def _mla_attn_kernel(qt_ref, k_ref, vt_ref, o_ref, *scratch, n_chunks, shifted):
    acc_ref = _attend(shifted, qt_ref, k_ref, vt_ref, tuple(range(ATTN_GROUP)), scratch, n_chunks)
    o = jnp.concatenate([_normalised(acc_ref, a) for a in range(ATTN_GROUP)], axis=0)
    o_ref[...] = o.T


def _dif_attn_kernel(lam_ref, gsub_ref, qt_ref, k_ref, vt_ref, o_ref, *scratch, n_chunks, shifted, lam_init):
    acc_ref = _attend(shifted, qt_ref, k_ref, vt_ref, tuple(a // 2 for a in range(ATTN_GROUP)), scratch,
                      n_chunks)
    lp = lam_ref[...]
    lam = (jnp.exp(jnp.sum(lp[0:1] * lp[1:2], axis=-1, keepdims=True))
           - jnp.exp(jnp.sum(lp[2:3] * lp[3:4], axis=-1, keepdims=True)) + lam_init)
    outs = []
    for h in range(ATTN_GROUP // 2):
        o = _normalised(acc_ref, 2 * h) - lam * _normalised(acc_ref, 2 * h + 1)
        outs.append(o * lax.rsqrt(jnp.mean(o * o, axis=0, keepdims=True) + EPS))
    gsub = gsub_ref[...] * (1.0 - lam_init)
    o_ref[...] = jnp.concatenate(outs, axis=0).T * jnp.concatenate([gsub] * len(outs), axis=1)


def _attention(qt, k, vt, *, shifted=False, lam=None, gsub=None, lam_init=None):
    nsub, _, t = qt.shape
    n_chunks = k.shape[1]
    assert (n_chunks - 1) % 4 == 0 and nsub % ATTN_GROUP == 0
    tq = KEY_CHUNK
    dv = vt.shape[2] - SUM_ROWS
    groups = nsub // ATTN_GROUP
    n_v = vt.shape[0] // groups
    out_w = n_v * dv
    resident = dict(pipeline_mode=pl.Buffered(1))
    specs = [pl.BlockSpec((ATTN_GROUP, LANES, tq), lambda g, i: (g, 0, i)),
             pl.BlockSpec((ATTN_GROUP, n_chunks, KEY_CHUNK, LANES), lambda g, i: (g, 0, 0, 0), **resident),
             pl.BlockSpec((n_v, n_chunks, dv + SUM_ROWS, KEY_CHUNK), lambda g, i: (g, 0, 0, 0), **resident)]
    if lam is None:
        body = functools.partial(_mla_attn_kernel, n_chunks=n_chunks, shifted=shifted)
        args = (qt, k, vt)
    else:
        body = functools.partial(_dif_attn_kernel, n_chunks=n_chunks, shifted=shifted, lam_init=lam_init)
        specs = [pl.BlockSpec((8, LANES), lambda g, i: (0, 0)),
                 pl.BlockSpec((1, LANES), lambda g, i: (0, 0))] + specs
        args = (lam, gsub, qt, k, vt)
    acc = pltpu.VMEM((ATTN_GROUP, dv + SUM_ROWS, tq), F32)
    if shifted:
        span = math.gcd(n_chunks - 1, PV_SPAN)
        scratch = [pltpu.VMEM((2, ATTN_GROUP, span * KEY_CHUNK, tq), BF16), acc]
    else:
        scratch = [pltpu.VMEM((2, ATTN_GROUP, KEY_CHUNK, tq), F32),
                   pltpu.VMEM((2, ATTN_GROUP, KEY_CHUNK, tq), BF16), acc]
    return pl.pallas_call(
        body,
        out_shape=jax.ShapeDtypeStruct((t, groups * out_w), F32),
        grid=(groups, t // tq),
        in_specs=specs,
        out_specs=pl.BlockSpec((tq, out_w), lambda g, i: (i, g)),
        scratch_shapes=scratch,
        compiler_params=_params("parallel", "arbitrary"),
        name=("mla_attention" if lam is None else "dif_attention") + ("_shifted" if shifted else ""),
    )(*args)


def _ssm_prep_kernel(x_ref, prev_ref, next_ref, dt_ref, w_ref, b_ref, dtb_ref, xo_ref, dto_ref, e_ref,
                     *, n_ctx, n_tok, tm):
    row0 = pl.program_id(0) * tm
    pad = SSM_CONV // 2
    has_prev = jnp.logical_and(row0 != 0, row0 != n_ctx)
    has_next = jnp.logical_and(row0 + tm != n_ctx, row0 + tm != n_tok)
    e_ref[0:8] = jnp.where(has_prev, prev_ref[...], 0.0)
    e_ref[8:8 + tm] = x_ref[...]
    e_ref[8 + tm:16 + tm] = jnp.where(has_next, next_ref[...], 0.0)
    acc = jnp.zeros(x_ref.shape, F32) + b_ref[...]
    for k in range(SSM_CONV):
        acc = acc + w_ref[k:k + 1, :] * e_ref[pl.ds(8 - pad + k, tm), :]
    xo_ref[...] = _silu(acc)
    d = dt_ref[...] + dtb_ref[...]
    dto_ref[...] = jnp.maximum(d, 0.0) + jnp.log1p(jnp.exp(-jnp.abs(d)))


def _ssm_prep(u, conv_w, conv_b, dt_bias, n_ctx):
    t = u.shape[0]
    tm = ROW_TILE
    nt = t // tm
    cb = U_XBC // SSM_XBC
    const = lambda i: (0, 0)
    return pl.pallas_call(
        functools.partial(_ssm_prep_kernel, n_ctx=n_ctx, n_tok=t, tm=tm),
        out_shape=(jax.ShapeDtypeStruct((t, SSM_XBC), F32), jax.ShapeDtypeStruct((t, LANES), F32)),
        grid=(nt,),
        in_specs=[pl.BlockSpec((tm, SSM_XBC), lambda i: (i, cb)),
                  pl.BlockSpec((8, SSM_XBC), lambda i: (jnp.maximum(i * (tm // 8) - 1, 0), cb)),
                  pl.BlockSpec((8, SSM_XBC), lambda i: (jnp.minimum((i + 1) * (tm // 8), t // 8 - 1), cb)),
                  pl.BlockSpec((tm, LANES), lambda i: (i, U_DT // LANES)),
                  pl.BlockSpec((8, SSM_XBC), const),
                  pl.BlockSpec((1, SSM_XBC), const),
                  pl.BlockSpec((1, LANES), const)],
        out_specs=(pl.BlockSpec((tm, SSM_XBC), lambda i: (i, 0)),
                   pl.BlockSpec((tm, LANES), lambda i: (i, 0))),
        scratch_shapes=[pltpu.VMEM((tm + 16, SSM_XBC), F32)],
        compiler_params=_params("parallel"),
        name="ssm_prep",
    )(u, u, u, u, conv_w, conv_b, dt_bias)


def _ssd_kernel(xf_ref, dtf_ref, xr_ref, dtr_ref, alog_ref, of_ref, or_ref, hf_ref, hr_ref):
    @pl.when(pl.program_id(0) == 0)
    def _():
        hf_ref[...] = jnp.zeros(hf_ref.shape, F32)
        hr_ref[...] = jnp.zeros(hr_ref.shape, F32)

    lc = SSM_CHUNK
    a_row = -jnp.exp(alog_ref[...])
    n_sub = xf_ref.shape[0] // lc
    fwd, bwd = [None] * n_sub, [None] * n_sub
    for j in range(n_sub):
        rows = slice(lc * j, lc * (j + 1))
        fwd[j] = _ssd_chunk(xf_ref[rows, :], dtf_ref[rows, :], a_row, hf_ref, 0)
        jr = n_sub - 1 - j
        rows = slice(lc * jr, lc * (jr + 1))
        bwd[jr] = _ssd_chunk(xr_ref[rows, :], dtr_ref[rows, :], a_row, hr_ref, 1)
    of_ref[...] = jnp.concatenate(fwd, axis=0)
    or_ref[...] = jnp.concatenate(bwd, axis=0)


def _ssd_chunk(xbc, dt, a_row, h_ref, direction):
    lc = SSM_CHUNK
    hd, per_group = SSM_HEAD_DIM, SSM_HEADS // SSM_GROUPS
    dta = dt * a_row
    r = lax.broadcasted_iota(jnp.int32, (lc, lc), 0)
    c = lax.broadcasted_iota(jnp.int32, (lc, lc), 1)
    keep = (r >= c) if direction == 0 else (r <= c)
    tri = jnp.where(keep, 1.0, 0.0).astype(F32)
    cum = jnp.dot(tri, dta, preferred_element_type=F32, precision=lax.Precision.HIGHEST)
    cum_t = cum.T
    dt_t = dt.T
    total = jnp.sum(dta, axis=0, keepdims=True)
    chunk_decay = jnp.exp(total)
    w_t = (dt * jnp.exp(total - cum)).T
    outs = []
    for g in range(SSM_GROUPS):
        b_f32 = xbc[:, SSM_WIDTH + SSM_STATE * g:SSM_WIDTH + SSM_STATE * (g + 1)]
        c_g = xbc[:, SSM_WIDTH + SSM_STATE * (SSM_GROUPS + g):
                  SSM_WIDTH + SSM_STATE * (SSM_GROUPS + g + 1)].astype(BF16)
        cb = _dot_nt(c_g, b_f32.astype(BF16))
        b_t = b_f32.T
        for h in range(per_group * g, per_group * (g + 1)):
            col = direction * SSM_HEADS + h
            cum_b = jnp.broadcast_to(cum[:, col:col + 1], (lc, lc))
            seg = jnp.exp(jnp.where(keep, cum_b - cum_t[col:col + 1, :], -jnp.inf))
            x_h = xbc[:, hd * h:hd * (h + 1)].astype(BF16)
            y = _dot((cb * seg * dt_t[col:col + 1, :]).astype(BF16), x_h)
            state = h_ref[h]
            y_off = _dot(c_g, state.astype(BF16))
            outs.append(y + y_off * jnp.exp(cum_b[:, :hd]))
            upd = _dot((b_t * w_t[col:col + 1, :]).astype(BF16), x_h)
            h_ref[h] = state * chunk_decay[:, col:col + 1] + upd
    return jnp.concatenate(outs, axis=1)


SSD_STEP = 2 * SSM_CHUNK


def _ssd(xbc, dt, a_log, n_ctx):
    t = xbc.shape[0]
    lc = SSD_STEP
    assert n_ctx % lc == 0 and t % lc == 0
    nc = t // lc
    ncc = n_ctx // lc
    fwd = lambda s: (s, 0)
    bwd = lambda s: (jnp.where(s < ncc, ncc - 1 - s, nc - 1 - (s - ncc)), 0)
    state = pltpu.VMEM((SSM_HEADS, SSM_STATE, SSM_HEAD_DIM), F32)
    out = jax.ShapeDtypeStruct((t, SSM_WIDTH), F32)
    return pl.pallas_call(
        _ssd_kernel,
        out_shape=(out, out),
        grid=(nc,),
        in_specs=[pl.BlockSpec((lc, SSM_XBC), fwd), pl.BlockSpec((lc, LANES), fwd),
                  pl.BlockSpec((lc, SSM_XBC), bwd), pl.BlockSpec((lc, LANES), bwd),
                  pl.BlockSpec((1, LANES), lambda s: (0, 0))],
        out_specs=(pl.BlockSpec((lc, SSM_WIDTH), fwd), pl.BlockSpec((lc, SSM_WIDTH), bwd)),
        scratch_shapes=[state, state],
        compiler_params=_params("arbitrary"),
        name="ssd_scan",
    )(xbc, dt, xbc, dt, a_log)


def _merge_kernel(x_ref, gate_ref, ya_ref, yb_ref, yf_ref, yr_ref, xs_ref, z_ref,
                  bg_ref, dskip_ref, gssm_ref, wa_ref, wb_ref, wc_ref, wo_ref,
                  ml_ref, mc_ref, g2_ref, wr_ref, br_ref,
                  xo_ref, f_ref, lg_ref, *, n_ctx, tm):
    y = (yf_ref[...] + yr_ref[...] + dskip_ref[...] * xs_ref[...]) * _silu(z_ref[...])
    gw = SSM_WIDTH // SSM_GROUPS
    yc = jnp.concatenate([_rms_rows(y[:, gw * g:gw * (g + 1)], gw) for g in range(SSM_GROUPS)], axis=1)
    yc = yc * gssm_ref[...]
    gate = _sigmoid(gate_ref[...] + bg_ref[...])
    m = (gate[:, :D_MODEL] * _dot(ya_ref[...].astype(BF16), wa_ref[...])
         + gate[:, D_MODEL:2 * D_MODEL] * _dot(yb_ref[...].astype(BF16), wb_ref[...])
         + gate[:, 2 * D_MODEL:] * _dot(yc.astype(BF16), wc_ref[...]))
    out = _dot(m.astype(BF16), wo_ref[...])
    row0 = pl.program_id(0) * tm
    is_ctx = (row0 + lax.broadcasted_iota(jnp.int32, (tm, 1), 0)) < n_ctx
    gt1 = jnp.where(is_ctx, mc_ref[2:3, :], ml_ref[2:3, :])
    x_new = x_ref[...] + gt1 * out
    xo_ref[...] = x_new
    f = _modulated_norm(x_new, g2_ref[...], ml_ref[3:5, :], mc_ref[3:5, :], row0, n_ctx)
    f_ref[...] = f
    lg_ref[...] = _dot(f.astype(BF16), wr_ref[...]) + br_ref[...]


def _merge(x_all, u, ya, yb, yf, yr, xbc, b_gate, dskip, g_ssm, wa, wb, wc, wo, mod_lat, mod_ctx,
           g2, w_router, b_router, n_ctx):
    t, d = x_all.shape
    tm = ROW_TILE
    const = lambda i: (0, 0)
    rows = lambda i: (i, 0)
    full = lambda a: pl.BlockSpec(a.shape, const)
    return pl.pallas_call(
        functools.partial(_merge_kernel, n_ctx=n_ctx, tm=tm),
        out_shape=(jax.ShapeDtypeStruct((t, d), F32), jax.ShapeDtypeStruct((t, d), F32),
                   jax.ShapeDtypeStruct((t, LANES), F32)),
        grid=(t // tm,),
        in_specs=[pl.BlockSpec((tm, d), rows),
                  pl.BlockSpec((tm, GATE_COLS), lambda i: (i, U_GATE // GATE_COLS)),
                  pl.BlockSpec((tm, MLA_WIDTH), rows),
                  pl.BlockSpec((tm, DIF_WIDTH), rows),
                  pl.BlockSpec((tm, SSM_WIDTH), rows),
                  pl.BlockSpec((tm, SSM_WIDTH), rows),
                  pl.BlockSpec((tm, SSM_WIDTH), rows),
                  pl.BlockSpec((tm, SSM_WIDTH), lambda i: (i, U_Z // SSM_WIDTH)),
                  full(b_gate), full(dskip), full(g_ssm), full(wa), full(wb), full(wc), full(wo),
                  full(mod_lat), full(mod_ctx), full(g2), full(w_router), full(b_router)],
        out_specs=(pl.BlockSpec((tm, d), rows), pl.BlockSpec((tm, d), rows),
                   pl.BlockSpec((tm, LANES), rows)),
        compiler_params=_params("parallel"),
        name="merge",
    )(x_all, u, ya, yb, yf, yr, xbc, u, b_gate, dskip, g_ssm, wa, wb, wc, wo, mod_lat, mod_ctx,
      g2, w_router, b_router)


def _moe_kernel(be_ref, nb_ref, x_ref, wgu_ref, bgu_ref, wd_ref, bd_ref, o_ref, wgu_s, wd_s):
    b = pl.program_id(0)
    prev = be_ref[jnp.maximum(b - 1, 0)]
    fresh = jnp.logical_or(b == 0, be_ref[b] != prev)

    @pl.when(fresh)
    def _():
        wgu_s[...] = wgu_ref[...].astype(BF16)
        wd_s[...] = wd_ref[...].astype(BF16)

    @pl.when(b < nb_ref[0])
    def _():
        gu = _dot(x_ref[...].astype(BF16), wgu_s[...]) + bgu_ref[...]
        glu = jnp.minimum(gu[:, :D_FF], SWIGLU_LIMIT)
        lin = jnp.clip(gu[:, D_FF:], -SWIGLU_LIMIT, SWIGLU_LIMIT)
        act = glu * _sigmoid(SWIGLU_ALPHA * glu) * (lin + 1.0)
        o_ref[...] = _dot(act.astype(BF16), wd_s[...]) + bd_ref[...]

    @pl.when(b >= nb_ref[0])
    def _():
        o_ref[...] = jnp.zeros(o_ref.shape, F32)


def _moe_experts(block_e, n_used, x_sorted, w_gu, b_gu, w_down, b_down, layer):
    n_slots, d = x_sorted.shape
    n_blocks = n_slots // MOE_BLOCK
    grid_spec = pltpu.PrefetchScalarGridSpec(
        num_scalar_prefetch=2,
        grid=(n_blocks,),
        in_specs=[pl.BlockSpec((MOE_BLOCK, d), lambda b, be, nb: (b, 0)),
                  pl.BlockSpec((None, None, d, 2 * D_FF), lambda b, be, nb: (layer, be[b], 0, 0)),
                  pl.BlockSpec((None, None, 1, 2 * D_FF), lambda b, be, nb: (layer, be[b], 0, 0)),
                  pl.BlockSpec((None, None, D_FF, d), lambda b, be, nb: (layer, be[b], 0, 0)),
                  pl.BlockSpec((None, None, 1, d), lambda b, be, nb: (layer, be[b], 0, 0))],
        out_specs=pl.BlockSpec((MOE_BLOCK, d), lambda b, be, nb: (b, 0)),
        scratch_shapes=[pltpu.VMEM((d, 2 * D_FF), BF16), pltpu.VMEM((D_FF, d), BF16)],
    )
    return pl.pallas_call(
        _moe_kernel,
        out_shape=jax.ShapeDtypeStruct((n_slots, d), F32),
        grid_spec=grid_spec,
        compiler_params=_params("arbitrary"),
        name="moe_experts",
    )(block_e, n_used, x_sorted, w_gu, b_gu, w_down, b_down)


def _combine_kernel(x_ref, g_ref, ml_ref, mc_ref, *rest, n_ctx, tm):
    y_refs, o_ref = rest[:TOP_K], rest[TOP_K]
    is_ctx = (pl.program_id(0) * tm + lax.broadcasted_iota(jnp.int32, (tm, 1), 0)) < n_ctx
    gate2 = jnp.where(is_ctx, mc_ref[...], ml_ref[...])
    g = g_ref[...]
    y = g[:, 0:1] * y_refs[0][...]
    for k in range(1, TOP_K):
        y = y + g[:, k:k + 1] * y_refs[k][...]
    o_ref[...] = x_ref[...] + gate2 * y


def _combine(x_all, y_rows, gates, gate2_lat, gate2_ctx, n_ctx):
    t, d = x_all.shape
    tm = ROW_TILE
    rows = lambda i: (i, 0)
    const = lambda i: (0, 0)
    return pl.pallas_call(
        functools.partial(_combine_kernel, n_ctx=n_ctx, tm=tm),
        out_shape=jax.ShapeDtypeStruct((t, d), F32),
        grid=(t // tm,),
        in_specs=[pl.BlockSpec((tm, d), rows), pl.BlockSpec((tm, LANES), rows),
                  pl.BlockSpec((1, d), const), pl.BlockSpec((1, d), const)]
                 + [pl.BlockSpec((tm, d), rows)] * TOP_K,
        out_specs=pl.BlockSpec((tm, d), rows),
        compiler_params=_params("parallel"),
        name="moe_combine",
    )(x_all, gates, gate2_lat, gate2_ctx, *y_rows)


ROUTE_IDX, ROUTE_GATE, ROUTE_RANK = 0, TOP_K, 2 * TOP_K


def _router_kernel(lg_ref, o_ref, cnt_ref, run_ref, *, tm):
    @pl.when(pl.program_id(0) == 0)
    def _():
        run_ref[...] = jnp.zeros(run_ref.shape, F32)

    lane = lax.broadcasted_iota(jnp.int32, (tm, LANES), 1)
    lane_f = lane.astype(F32)
    lg = jnp.where(lane < N_EXPERTS, lg_ref[...], -jnp.inf)
    hots, vals = [], []
    for _ in range(TOP_K):
        mx = jnp.max(lg, axis=-1, keepdims=True)
        idx = jnp.min(jnp.where(lg == mx, lane_f, float(LANES)), axis=-1, keepdims=True)
        hot = lane_f == idx
        lg = jnp.where(hot, -jnp.inf, lg)
        hots.append((hot, idx))
        vals.append(mx)
    exps = [jnp.exp(v - vals[0]) for v in vals]
    inv = 1.0 / sum(exps[1:], exps[0])
    chosen = jnp.zeros((tm, LANES), F32)
    for hot, _ in hots:
        chosen = jnp.where(hot, 1.0, chosen)
    r = lax.broadcasted_iota(jnp.int32, (tm, tm), 0)
    c = lax.broadcasted_iota(jnp.int32, (tm, tm), 1)
    earlier = jnp.where(r > c, 1.0, 0.0).astype(BF16)
    before = _dot(earlier, chosen.astype(BF16)) + run_ref[0:1, :]
    out = jnp.zeros((tm, LANES), F32)
    for k, (hot, idx) in enumerate(hots):
        rank = jnp.sum(jnp.where(hot, before, 0.0), axis=-1, keepdims=True)
        out = jnp.where(lane == ROUTE_IDX + k, idx, out)
        out = jnp.where(lane == ROUTE_GATE + k, exps[k] * inv, out)
        out = jnp.where(lane == ROUTE_RANK + k, rank, out)
    o_ref[...] = out
    run_ref[...] = run_ref[...] + jnp.sum(chosen, axis=0, keepdims=True)
    cnt_ref[...] = run_ref[...]


def _router(logits):
    t = logits.shape[0]
    tm = ROW_TILE
    return pl.pallas_call(
        functools.partial(_router_kernel, tm=tm),
        out_shape=(jax.ShapeDtypeStruct((t, LANES), F32), jax.ShapeDtypeStruct((8, LANES), F32)),
        grid=(t // tm,),
        in_specs=[pl.BlockSpec((tm, LANES), lambda i: (i, 0))],
        out_specs=(pl.BlockSpec((tm, LANES), lambda i: (i, 0)), pl.BlockSpec((8, LANES), lambda i: (0, 0))),
        scratch_shapes=[pltpu.VMEM((8, LANES), F32)],
        compiler_params=_params("arbitrary"),
        name="moe_router",
    )(logits)


def _route(logits, n_tok):
    routed, counts = _router(logits)
    top_idx = routed[:, ROUTE_IDX:ROUTE_IDX + TOP_K].astype(jnp.int32)
    gates = routed[:, ROUTE_GATE:ROUTE_GATE + TOP_K]
    rank = routed[:, ROUTE_RANK:ROUTE_RANK + TOP_K].astype(jnp.int32)
    n_assign = n_tok * TOP_K
    counts = counts[0, :N_EXPERTS].astype(jnp.int32)
    padded = (counts + MOE_BLOCK - 1) // MOE_BLOCK * MOE_BLOCK
    pad_end = jnp.cumsum(padded)
    pad_start = pad_end - padded
    slot_of = pad_start[top_idx] + rank
    n_blocks = -(-(n_assign + N_EXPERTS * (MOE_BLOCK - 1)) // MOE_BLOCK)
    n_slots = n_blocks * MOE_BLOCK
    token = jnp.arange(n_assign, dtype=jnp.int32) // TOP_K
    slot_tok = jnp.zeros((n_slots,), jnp.int32).at[slot_of.reshape(-1)].set(token, unique_indices=True)
    block_start = jnp.arange(n_blocks, dtype=jnp.int32) * MOE_BLOCK
    block_e = jnp.minimum(jnp.searchsorted(pad_end, block_start, side='right'), N_EXPERTS - 1)
    n_used = (pad_end[-1] // MOE_BLOCK).reshape(1)
    return gates, slot_tok, slot_of, block_e.astype(jnp.int32), n_used.astype(jnp.int32)


def kernel(x, c, ctx, c_ctx, w_mod, b_mod, g_norm1, g_norm2, w_in, b_gate, mla_g_q, mla_w_uq, mla_g_kv, mla_w_ukv, mla_g_qn, mla_g_kn, dif_g_qn, dif_g_kn, dif_lambda, dif_g_sub, ssm_conv_w, ssm_conv_b, ssm_dt_bias, ssm_a_log, ssm_d, ssm_g_norm, w_up_mla, w_up_dif, w_up_ssm, w_out, moe_w_router, moe_b_router, moe_w_gu, moe_b_gu, moe_w_down, moe_b_down):
    assert x.shape[0] == 1 and ctx.shape[0] == 1
    depth = w_in.shape[0]
    seq = x.shape[1]
    n_ctx = ctx.shape[1]
    n_tok = n_ctx + seq
    d = D_MODEL
    assert n_ctx == KEY_CHUNK and n_tok % ROW_TILE == 0 and seq % GRID_W == 0

    x_all = jnp.concatenate([ctx[0], x[0]], axis=0)
    cc = jnp.zeros((8, d), F32).at[0].set(c[0]).at[1].set(c_ctx)
    mod = _mod_vectors(cc, w_mod, b_mod)
    mod = mod[:, :2].reshape(depth, 2, 6, d)

    rope_mla = _rope_tables(seq, n_ctx, MLA_ROPE, MLA_NOPE)
    rope_dif = _rope_tables(seq, n_ctx, DIF_HEAD_DIM, 0)
    w_in_all = _take_columns(w_in, _in_proj_columns()).astype(BF16)
    w_uq_all = _take_columns(mla_w_uq, _head_columns(MLA_HEADS, MLA_QK, 0, MLA_QK, LANES)).astype(BF16)
    w_uk_all = _take_columns(mla_w_ukv, _head_columns(MLA_HEADS, MLA_NOPE + MLA_V, 0, MLA_NOPE, LANES)).astype(BF16)
    w_uv_all = _take_columns(mla_w_ukv, _head_columns(MLA_HEADS, MLA_NOPE + MLA_V, MLA_NOPE, MLA_V, MLA_V)).astype(BF16)
    w_up_all = [w.astype(BF16) for w in (w_up_mla, w_up_dif, w_up_ssm, w_out)]

    for i in range(depth):
        lam_init = 0.8 - 0.6 * math.exp(-0.3 * i)
        mod_lat, mod_ctx = mod[i, 0], mod[i, 1]
        u = _in_proj(x_all, g_norm1[i][None], mod_lat[0:2], mod_ctx[0:2], w_in_all, i, n_ctx)

        qt, k, vt = _mla_prep(
            u, mla_g_q[i][None], w_uq_all[i], mla_g_kv[i][None], w_uk_all[i], w_uv_all[i],
            _pad_lanes(mla_g_qn[i][None]), _pad_lanes(mla_g_kn[i][None]), rope_mla)
        ya = _attention(qt, k, vt)

        bound = (DIF_HEAD_DIM ** 0.5 * LOG2E) * jnp.max(jnp.abs(dif_g_qn[i])) * jnp.max(jnp.abs(dif_g_kn[i]))
        qt, k, vt = _dif_prep(u, _pad_lanes(dif_g_qn[i][None]), _pad_lanes(dif_g_kn[i][None]),
                              jnp.full((1, LANES), bound, F32), rope_dif)
        lam_rows = jnp.zeros((8, LANES), F32).at[:4, :DIF_HEAD_DIM].set(dif_lambda[i])
        dif_attention = functools.partial(_attention, lam=lam_rows, gsub=dif_g_sub[i][None], lam_init=lam_init)
        yb = lax.cond(bound <= MAX_SHIFT, functools.partial(dif_attention, shifted=True),
                      functools.partial(dif_attention, shifted=False), qt, k, vt)

        conv_w = jnp.zeros((8, SSM_XBC), F32).at[:SSM_CONV].set(ssm_conv_w[i])
        xbc, dt = _ssm_prep(u, conv_w, ssm_conv_b[i][None], _pad_lanes(ssm_dt_bias[i].reshape(1, -1)), n_ctx)
        a_log = _pad_lanes(ssm_a_log[i].reshape(1, -1))
        yf, yr = _ssd(xbc, dt, a_log, n_ctx)

        dskip = jnp.repeat(ssm_d[i, 0] + ssm_d[i, 1], SSM_HEAD_DIM)[None]
        w_router = jnp.zeros((d, LANES), BF16).at[:, :N_EXPERTS].set(moe_w_router[i].astype(BF16))
        b_router = jnp.zeros((1, LANES), F32).at[0, :N_EXPERTS].set(moe_b_router[i])
        x_all, f, logits = _merge(
            x_all, u, ya, yb, yf, yr, xbc, b_gate[i][None], dskip, ssm_g_norm[i][None],
            w_up_all[0][i], w_up_all[1][i], w_up_all[2][i], w_up_all[3][i],
            mod_lat[0:5], mod_ctx[0:5], g_norm2[i][None],
            w_router, b_router, n_ctx)

        gates, slot_tok, slot_of, block_e, n_used = _route(logits, n_tok)
        y_slots = _moe_experts(block_e, n_used, f[slot_tok], moe_w_gu,
                               moe_b_gu.reshape(depth, N_EXPERTS, 1, 2 * D_FF), moe_w_down,
                               moe_b_down.reshape(depth, N_EXPERTS, 1, d), i)
        y_rows = [y_slots[slot_of[:, k]] for k in range(TOP_K)]
        x_all = _combine(x_all, y_rows, _pad_lanes(gates), mod_lat[5:6], mod_ctx[5:6], n_ctx)
    return x_all[n_ctx:][None]
```

```python
import functools
import math

import numpy as np
import jax
import jax.numpy as jnp
from jax import lax
from jax.experimental import pallas as pl
from jax.experimental.pallas import tpu as pltpu

F32 = jnp.float32
BF16 = jnp.bfloat16
LANES = 128
VMEM_REQUEST_CAP = 56 * 1024 * 1024
VMEM_TEMPORARIES = 16 * 1024 * 1024

D_MODEL = 1024
EPS = 1e-6
ROPE_THETA = 10000.0
GRID_W = 64
N_BRANCH = 3

MLA_HEADS = 8
MLA_Q_RANK = 256
MLA_KV_RANK = 128
MLA_NOPE = 64
MLA_ROPE = 32
MLA_V = 64
MLA_QK = MLA_NOPE + MLA_ROPE
MLA_WIDTH = MLA_HEADS * MLA_V

DIF_HEADS = 4
DIF_HEAD_DIM = 64
DIF_WIDTH = DIF_HEADS * 2 * DIF_HEAD_DIM

SSM_HEADS = 8
SSM_HEAD_DIM = 64
SSM_WIDTH = SSM_HEADS * SSM_HEAD_DIM
SSM_GROUPS = 2
SSM_STATE = 128
SSM_CONV = 5
SSM_CHUNK = 128
SSM_XBC = SSM_WIDTH + 2 * SSM_GROUPS * SSM_STATE

N_EXPERTS = 32
TOP_K = 4
D_FF = 1024
SWIGLU_LIMIT = 7.0
SWIGLU_ALPHA = 1.702
MOE_BLOCK = 512

MLA_COLS = MLA_Q_RANK + MLA_KV_RANK + MLA_ROPE
DIF_COLS = 3 * DIF_WIDTH
SSM_COLS = SSM_WIDTH + SSM_XBC + 2 * SSM_HEADS
GATE_COLS = N_BRANCH * D_MODEL

ROW_TILE = 256
KEY_CHUNK = 256
LOG2E = 1.4426950408889634

U_GATE, U_DQ, U_DK, U_XBC, U_MLA, U_DV, U_Z, U_DT = 0, 3072, 4096, 5120, 6144, 6656, 7168, 7680
U_COLS = 8192
U_TILE_N = 2048


def _in_proj_columns():
    src = np.full((U_COLS,), -1, np.int64)
    dif0 = MLA_COLS
    ssm0 = MLA_COLS + DIF_COLS
    gate0 = ssm0 + SSM_COLS
    src[U_GATE:U_GATE + GATE_COLS] = gate0 + np.arange(GATE_COLS)
    for a in range(2 * DIF_HEADS):
        src[U_DQ + LANES * a:U_DQ + LANES * a + DIF_HEAD_DIM] = dif0 + DIF_HEAD_DIM * a + np.arange(DIF_HEAD_DIM)
        src[U_DK + LANES * a:U_DK + LANES * a + DIF_HEAD_DIM] = (dif0 + DIF_WIDTH + DIF_HEAD_DIM * a
                                                                  + np.arange(DIF_HEAD_DIM))
    src[U_DV:U_DV + DIF_WIDTH] = dif0 + 2 * DIF_WIDTH + np.arange(DIF_WIDTH)
    src[U_MLA:U_MLA + MLA_Q_RANK + MLA_KV_RANK] = np.arange(MLA_Q_RANK + MLA_KV_RANK)
    pe0 = U_MLA + MLA_Q_RANK + MLA_KV_RANK + MLA_NOPE
    src[pe0:pe0 + MLA_ROPE] = MLA_Q_RANK + MLA_KV_RANK + np.arange(MLA_ROPE)
    src[U_Z:U_Z + SSM_WIDTH] = ssm0 + np.arange(SSM_WIDTH)
    src[U_XBC:U_XBC + SSM_XBC] = ssm0 + SSM_WIDTH + np.arange(SSM_XBC)
    src[U_DT:U_DT + 2 * SSM_HEADS] = ssm0 + SSM_WIDTH + SSM_XBC + np.arange(2 * SSM_HEADS)
    return src


def _take_columns(w, src):
    cols = jnp.take(w, jnp.asarray(np.maximum(src, 0)), axis=-1)
    return jnp.where(jnp.asarray(src >= 0), cols, 0.0)


def _head_columns(n_heads, src_stride, src_off, width, dst_stride):
    src = np.full((n_heads * dst_stride,), -1, np.int64)
    for h in range(n_heads):
        src[h * dst_stride:h * dst_stride + width] = h * src_stride + src_off + np.arange(width)
    return src


def _pad_lanes(v, n=LANES):
    return jnp.pad(v, [(0, 0)] * (v.ndim - 1) + [(0, n - v.shape[-1])])


def _row_tile(n, cap):
    best = 8
    for t in range(8, cap + 1, 8):
        if n % t == 0:
            best = t
    return best


def _dot(a, b):
    return jnp.dot(a, b, preferred_element_type=F32)


def _dot_nt(a, b):
    return lax.dot_general(a, b, (((1,), (1,)), ((), ())), preferred_element_type=F32)


def _sigmoid(x):
    return 1.0 / (1.0 + jnp.exp(-x))


def _silu(x):
    return x * _sigmoid(x)


def _rms_rows(x, n):
    return x * lax.rsqrt(jnp.sum(x * x, axis=-1, keepdims=True) * (1.0 / n) + EPS)


def _as_tuple(x):
    return tuple(x) if isinstance(x, (tuple, list)) else (x,)


def _window_bytes(spec, dtype):
    if spec.block_shape is None:
        return 0
    dims = [1 if n is None else n for n in spec.block_shape]
    copies = 2 if spec.pipeline_mode is None else spec.pipeline_mode.buffer_count
    return math.prod(dims) * jnp.dtype(dtype).itemsize * copies


def _pallas_call(kernel, *, name, out_shape, grid, in_specs, out_specs, semantics, scratch_shapes=(),
                 num_scalar_prefetch=0):
    def call(*args):
        need = VMEM_TEMPORARIES
        need += sum(_window_bytes(s, a.dtype) for s, a in zip(in_specs, args[num_scalar_prefetch:]))
        need += sum(_window_bytes(s, o.dtype) for s, o in zip(_as_tuple(out_specs), _as_tuple(out_shape)))
        need += sum(math.prod(b.shape) * jnp.dtype(b.dtype).itemsize for b in scratch_shapes)
        params = pltpu.CompilerParams(dimension_semantics=semantics,
                                      vmem_limit_bytes=min(need, VMEM_REQUEST_CAP))
        grid_spec = pltpu.PrefetchScalarGridSpec(
            num_scalar_prefetch=num_scalar_prefetch, grid=grid, in_specs=list(in_specs), out_specs=out_specs,
            scratch_shapes=list(scratch_shapes))
        launch = pl.pallas_call(kernel, out_shape=out_shape, grid_spec=grid_spec, compiler_params=params, name=name)
        return launch(*args)
    return call


def _mod_kernel(a_ref, w_ref, b_ref, o_ref):
    a = _silu(a_ref[...]).astype(BF16)
    o_ref[0] = _dot(a, w_ref[0].astype(BF16)) + b_ref[0]


def _mod_vectors(cc, w_mod, b_mod):
    depth, d, n = w_mod.shape
    tn = 1536
    return _pallas_call(
        _mod_kernel,
        out_shape=jax.ShapeDtypeStruct((depth, 8, n), F32),
        grid=(depth, n // tn),
        in_specs=[pl.BlockSpec((8, d), lambda l, j: (0, 0)),
                  pl.BlockSpec((1, d, tn), lambda l, j: (l, 0, j)),
                  pl.BlockSpec((1, 1, tn), lambda l, j: (l, 0, j))],
        out_specs=pl.BlockSpec((1, 8, tn), lambda l, j: (l, 0, j)),
        semantics=("parallel", "parallel"),
        name="mod_vectors",
    )(cc, w_mod, b_mod.reshape(depth, 1, n))


def _modulated_norm(x, g, mod_lat, mod_ctx, row0, n_ctx):
    rows = x.shape[0]
    is_ctx = (row0 + lax.broadcasted_iota(jnp.int32, (rows, 1), 0)) < n_ctx
    shift = jnp.where(is_ctx, mod_ctx[0:1, :], mod_lat[0:1, :])
    scale = jnp.where(is_ctx, mod_ctx[1:2, :], mod_lat[1:2, :])
    return _rms_rows(x, x.shape[1]) * g * (1.0 + scale) + shift


def _in_proj_kernel(x_ref, g_ref, ml_ref, mc_ref, w_ref, o_ref, h_ref, *, n_ctx, tm):
    @pl.when(pl.program_id(1) == 0)
    def _():
        h = _modulated_norm(x_ref[...], g_ref[...], ml_ref[...], mc_ref[...], pl.program_id(0) * tm, n_ctx)
        h_ref[...] = h.astype(BF16)

    o_ref[...] = _dot(h_ref[...], w_ref[...])


def _in_proj(x_all, g, mod_lat, mod_ctx, w, layer, n_ctx):
    t, d = x_all.shape
    n = w.shape[2]
    tm = _row_tile(t, 1280)
    return _pallas_call(
        functools.partial(_in_proj_kernel, n_ctx=n_ctx, tm=tm),
        out_shape=jax.ShapeDtypeStruct((t, n), F32),
        grid=(t // tm, n // U_TILE_N),
        in_specs=[pl.BlockSpec((tm, d), lambda i, j: (i, 0)),
                  pl.BlockSpec((1, d), lambda i, j: (0, 0)),
                  pl.BlockSpec((2, d), lambda i, j: (0, 0)),
                  pl.BlockSpec((2, d), lambda i, j: (0, 0)),
                  pl.BlockSpec((None, d, U_TILE_N), lambda i, j: (layer, 0, j))],
        out_specs=pl.BlockSpec((tm, U_TILE_N), lambda i, j: (i, j)),
        scratch_shapes=[pltpu.VMEM((tm, d), BF16)],
        semantics=("parallel", "arbitrary"),
        name="in_proj",
    )(x_all, g, mod_lat, mod_ctx, w)


def _rope_tables(seq_len, n_ctx, rot_dim, lane0):
    n_rows = seq_len // GRID_W
    row = jnp.repeat(jnp.arange(n_rows), GRID_W).astype(F32)
    col = jnp.tile(jnp.arange(GRID_W), n_rows).astype(F32)
    axis_dim = rot_dim // 2
    half = axis_dim // 2
    inv = ROPE_THETA ** (-jnp.arange(0, axis_dim, 2, dtype=F32) / axis_dim)
    ang_r = row[:, None] * inv
    ang_c = col[:, None] * inv
    zeros = jnp.zeros((seq_len, half), F32)
    cos = jnp.concatenate([jnp.cos(ang_r), jnp.cos(ang_r), jnp.cos(ang_c), jnp.cos(ang_c)], axis=1)
    s1 = jnp.concatenate([zeros, jnp.sin(ang_r), zeros, jnp.sin(ang_c)], axis=1)
    s2 = jnp.concatenate([-jnp.sin(ang_r), zeros, -jnp.sin(ang_c), zeros], axis=1)

    def place(tab, fill):
        full = jnp.full((seq_len, LANES), fill, F32).at[:, lane0:lane0 + rot_dim].set(tab)
        ctx = jnp.full((n_ctx, LANES), fill, F32)
        return jnp.concatenate([ctx, full], axis=0)

    return place(cos, 1.0), place(s1, 0.0), place(s2, 0.0)


def _norm_rope_blocks(blocks, gains, n, cos, s1, s2, half):
    sums = [jnp.sum(x * x, axis=-1, keepdims=True) for x in blocks]
    inv = [lax.rsqrt(s * (1.0 / n) + EPS) for s in sums]
    normed = [x * r * g for x, r, g in zip(blocks, inv, gains)]
    fwd = [pltpu.roll(y, half, 1) for y in normed]
    bwd = [pltpu.roll(y, LANES - half, 1) for y in normed]
    return [y * cos + a * s1 + b * s2 for y, a, b in zip(normed, fwd, bwd)]


def _mla_prep_kernel(u_ref, gq_ref, wuq_ref, gkv_ref, wk_ref, wv_ref, gqn_ref, gkn_ref,
                     cos_ref, s1_ref, s2_ref, qt_ref, k_ref, vt_ref):
    u = u_ref[...]
    cq = u[:, :MLA_Q_RANK]
    ckv = u[:, MLA_Q_RANK:MLA_Q_RANK + MLA_KV_RANK]
    pe = u[:, MLA_Q_RANK + MLA_KV_RANK:]
    q = _dot((_rms_rows(cq, MLA_Q_RANK) * gq_ref[...]).astype(BF16), wuq_ref[...])
    kv_in = (_rms_rows(ckv, MLA_KV_RANK) * gkv_ref[...]).astype(BF16)
    kn = _dot(kv_in, wk_ref[...])
    v = _dot(kv_in, wv_ref[...])
    cos, s1, s2 = cos_ref[...], s1_ref[...], s2_ref[...]
    half = MLA_ROPE // 4
    q_scale = MLA_QK ** -0.5 * LOG2E
    n = MLA_HEADS
    blocks = ([q[:, LANES * h:LANES * (h + 1)] for h in range(n)]
              + [kn[:, LANES * h:LANES * (h + 1)] + pe for h in range(n)])
    roped = _norm_rope_blocks(blocks, [gqn_ref[...]] * n + [gkn_ref[...]] * n, MLA_QK, cos, s1, s2, half)
    for h in range(n):
        qt_ref[h] = (roped[h] * q_scale).T.astype(BF16)
        k_ref[h, 0] = roped[n + h].astype(BF16)
    tm = v.shape[0]
    vt_ref[:, 0] = _with_sum_rows(v.T.reshape(MLA_HEADS, MLA_V, tm)).astype(BF16)


def _mla_prep(u, gq, wuq, gkv, wk, wv, gqn, gkn, tabs):
    t = u.shape[0]
    tm = ROW_TILE
    nt = t // tm
    const = lambda i: (0, 0)
    rows = lambda i: (i, 0)
    return _pallas_call(
        _mla_prep_kernel,
        out_shape=(jax.ShapeDtypeStruct((MLA_HEADS, LANES, t), BF16),
                   jax.ShapeDtypeStruct((MLA_HEADS, nt, tm, LANES), BF16),
                   jax.ShapeDtypeStruct((MLA_HEADS, nt, MLA_V + SUM_ROWS, tm), BF16)),
        grid=(nt,),
        in_specs=[pl.BlockSpec((tm, 512), lambda i: (i, U_MLA // 512)),
                  pl.BlockSpec((1, MLA_Q_RANK), const),
                  pl.BlockSpec(wuq.shape, const),
                  pl.BlockSpec((1, MLA_KV_RANK), const),
                  pl.BlockSpec(wk.shape, const),
                  pl.BlockSpec(wv.shape, const),
                  pl.BlockSpec((1, LANES), const),
                  pl.BlockSpec((1, LANES), const),
                  pl.BlockSpec((tm, LANES), rows),
                  pl.BlockSpec((tm, LANES), rows),
                  pl.BlockSpec((tm, LANES), rows)],
        out_specs=(pl.BlockSpec((MLA_HEADS, LANES, tm), lambda i: (0, 0, i)),
                   pl.BlockSpec((MLA_HEADS, 1, tm, LANES), lambda i: (0, i, 0, 0)),
                   pl.BlockSpec((MLA_HEADS, 1, MLA_V + SUM_ROWS, tm), lambda i: (0, i, 0, 0))),
        semantics=("parallel",),
        name="mla_prep",
    )(u, gq, wuq, gkv, wk, wv, gqn, gkn, *tabs)


def _dif_prep_kernel(q_ref, k_ref, v_ref, gq_ref, gk_ref, shift_ref, cos_ref, s1_ref, s2_ref,
                     qt_out, k_out, vt_out):
    cos, s1, s2 = cos_ref[...], s1_ref[...], s2_ref[...]
    half = DIF_HEAD_DIM // 4
    q_scale = DIF_HEAD_DIM ** -0.5 * LOG2E
    n = 2 * DIF_HEADS
    blocks = ([q_ref[:, LANES * a:LANES * (a + 1)] for a in range(n)]
              + [k_ref[:, LANES * a:LANES * (a + 1)] for a in range(n)])
    roped = _norm_rope_blocks(blocks, [gq_ref[...]] * n + [gk_ref[...]] * n, DIF_HEAD_DIM, cos, s1, s2, half)
    spare = lax.broadcasted_iota(jnp.int32, roped[0].shape, 1) == DIF_HEAD_DIM
    for a in range(n):
        qt_out[a] = jnp.where(spare, -shift_ref[...], roped[a] * q_scale).T.astype(BF16)
        k_out[a, 0] = jnp.where(spare, 1.0, roped[n + a]).astype(BF16)
    v = v_ref[...]
    vt_out[:, 0] = _with_sum_rows(v.T.reshape(DIF_HEADS, 2 * DIF_HEAD_DIM, v.shape[0])).astype(BF16)


def _dif_prep(u, gq, gk, shift, tabs):
    t = u.shape[0]
    tm = ROW_TILE
    nt = t // tm
    nsub = 2 * DIF_HEADS
    const = lambda i: (0, 0)
    rows = lambda i: (i, 0)
    return _pallas_call(
        _dif_prep_kernel,
        out_shape=(jax.ShapeDtypeStruct((nsub, LANES, t), BF16),
                   jax.ShapeDtypeStruct((nsub, nt, tm, LANES), BF16),
                   jax.ShapeDtypeStruct((DIF_HEADS, nt, 2 * DIF_HEAD_DIM + SUM_ROWS, tm), BF16)),
        grid=(nt,),
        in_specs=[pl.BlockSpec((tm, 1024), lambda i: (i, U_DQ // 1024)),
                  pl.BlockSpec((tm, 1024), lambda i: (i, U_DK // 1024)),
                  pl.BlockSpec((tm, 512), lambda i: (i, U_DV // 512)),
                  pl.BlockSpec((1, LANES), const),
                  pl.BlockSpec((1, LANES), const),
                  pl.BlockSpec((1, LANES), const),
                  pl.BlockSpec((tm, LANES), rows),
                  pl.BlockSpec((tm, LANES), rows),
                  pl.BlockSpec((tm, LANES), rows)],
        out_specs=(pl.BlockSpec((nsub, LANES, tm), lambda i: (0, 0, i)),
                   pl.BlockSpec((nsub, 1, tm, LANES), lambda i: (0, i, 0, 0)),
                   pl.BlockSpec((DIF_HEADS, 1, 2 * DIF_HEAD_DIM + SUM_ROWS, tm), lambda i: (0, i, 0, 0))),
        semantics=("parallel",),
        name="dif_prep",
    )(u, u, u, gq, gk, shift, *tabs)


ATTN_GROUP = 2
ATTN_TRIP = 64
SUM_ROWS = 16


def _with_sum_rows(vt):
    heads, _, keys = vt.shape
    row = lax.broadcasted_iota(jnp.int32, (heads, SUM_ROWS, keys), 1)
    return jnp.concatenate([vt, jnp.where(row == 0, 1.0, 0.0).astype(vt.dtype)], axis=1)


def _attn_group(qt_ref, k_ref, vt_ref, v_of_sub, s_buf, p_buf, acc_ref, n_chunks):
    tq = qt_ref.shape[2]
    last = n_chunks - 1

    def scores(c):
        out = []
        for a in range(ATTN_GROUP):
            s = _dot(k_ref[a, c], qt_ref[a])
            out.append((s, jnp.max(s, axis=0, keepdims=True)))
        return out

    def stash(sc, slot):
        for a in range(ATTN_GROUP):
            s_buf[slot, a] = sc[a][0]
        return tuple(mx for (_, mx) in sc)

    def softmax(s_of, mx, slot, ms):
        new_ms, alphas = [], []
        for a in range(ATTN_GROUP):
            m_new = jnp.maximum(ms[a], mx[a])
            alphas.append(jnp.exp2(ms[a] - m_new))
            p_buf[slot, a] = jnp.exp2(s_of(a) - m_new).astype(BF16)
            new_ms.append(m_new)
        return tuple(new_ms), tuple(alphas)

    def values(c, slot, alphas):
        for a in range(ATTN_GROUP):
            acc_ref[a] = alphas[a] * acc_ref[a] + _dot(vt_ref[v_of_sub[a], c], p_buf[slot, a])

    per_trip = math.gcd(last, ATTN_TRIP)

    def trip(t, state):
        ms, alphas, mx0 = state
        c = per_trip * t + 1
        for pair in range(per_trip // 2):
            cur, nxt = pair % 2, 1 - pair % 2
            even = scores(c + 1)
            values(c - 1, 0, alphas)
            ms, alphas = softmax(lambda a: s_buf[cur, a], mx0, 1, ms)
            mx0 = stash(scores(jnp.minimum(c + 2, last)), nxt)
            values(c, 1, alphas)
            ms, alphas = softmax(lambda a: even[a][0], [mx for (_, mx) in even], 0, ms)
            c = c + 2
        return ms, alphas, mx0

    acc_ref[...] = jnp.zeros(acc_ref.shape, F32)
    ms = tuple(jnp.full((1, tq), -jnp.inf, F32) for _ in range(ATTN_GROUP))
    first = scores(0)
    ms, alphas = softmax(lambda a: first[a][0], [mx for (_, mx) in first], 0, ms)
    mx0 = stash(scores(1), 0)
    n_trips = jnp.where(pl.program_id(1) > 0, last // per_trip, 0)
    ms, alphas, _ = lax.fori_loop(0, n_trips, trip, (ms, alphas, mx0))
    values(per_trip * n_trips, 0, alphas)


def _normalised(acc_ref, a):
    dv = acc_ref.shape[1] - SUM_ROWS
    return acc_ref[a, :dv] * (1.0 / acc_ref[a, dv:dv + 1])


PV_SPAN = 32
MAX_SHIFT = 60.0


def _attn_group_shifted(qt_ref, k_ref, vt_ref, v_of_sub, p_buf, acc_ref, n_chunks):
    span = math.gcd(n_chunks - 1, PV_SPAN)

    def accumulate(c0, n, slot):
        for a in range(ATTN_GROUP):
            for i in range(n):
                s = _dot(k_ref[a, c0 + i], qt_ref[a])
                p_buf[slot, a, KEY_CHUNK * i:KEY_CHUNK * (i + 1)] = jnp.exp2(s).astype(BF16)
            vt = jnp.concatenate([vt_ref[v_of_sub[a], c0 + i] for i in range(n)], axis=1)
            acc_ref[a] = acc_ref[a] + _dot(vt, p_buf[slot, a, :KEY_CHUNK * n])

    acc_ref[...] = jnp.zeros(acc_ref.shape, F32)
    accumulate(0, 1, 0)

    @pl.when(pl.program_id(1) > 0)
    def _():
        for j in range((n_chunks - 1) // span):
            accumulate(1 + span * j, span, j % 2)


def _attend(shifted, qt_ref, k_ref, vt_ref, v_of_sub, scratch, n_chunks):
    if shifted:
        p_buf, acc_ref = scratch
        _attn_group_shifted(qt_ref, k_ref, vt_ref, v_of_sub, p_buf, acc_ref, n_chunks)
    else:
        s_buf, p_buf, acc_ref = scratch
        _attn_group(qt_ref, k_ref, vt_ref, v_of_sub, s_buf, p_buf, acc_ref, n_chunks)
    return acc_ref


def _mla_attn_kernel(qt_ref, k_ref, vt_ref, o_ref, *scratch, n_chunks, shifted):
    acc_ref = _attend(shifted, qt_ref, k_ref, vt_ref, tuple(range(ATTN_GROUP)), scratch, n_chunks)
    o = jnp.concatenate([_normalised(acc_ref, a) for a in range(ATTN_GROUP)], axis=0)
    o_ref[...] = o.T


def _dif_attn_kernel(lam_ref, gsub_ref, qt_ref, k_ref, vt_ref, o_ref, *scratch, n_chunks, shifted, lam_init):
    acc_ref = _attend(shifted, qt_ref, k_ref, vt_ref, tuple(a // 2 for a in range(ATTN_GROUP)), scratch,
                      n_chunks)
    lp = lam_ref[...]
    lam = (jnp.exp(jnp.sum(lp[0:1] * lp[1:2], axis=-1, keepdims=True))
           - jnp.exp(jnp.sum(lp[2:3] * lp[3:4], axis=-1, keepdims=True)) + lam_init)
    outs = []
    for h in range(ATTN_GROUP // 2):
        o = _normalised(acc_ref, 2 * h) - lam * _normalised(acc_ref, 2 * h + 1)
        outs.append(o * lax.rsqrt(jnp.mean(o * o, axis=0, keepdims=True) + EPS))
    gsub = gsub_ref[...] * (1.0 - lam_init)
    o_ref[...] = jnp.concatenate(outs, axis=0).T * jnp.concatenate([gsub] * len(outs), axis=1)


def _attention(qt, k, vt, *, shifted=False, lam=None, gsub=None, lam_init=None):
    nsub, _, t = qt.shape
    n_chunks = k.shape[1]
    assert (n_chunks - 1) % 4 == 0 and nsub % ATTN_GROUP == 0
    tq = KEY_CHUNK
    dv = vt.shape[2] - SUM_ROWS
    groups = nsub // ATTN_GROUP
    n_v = vt.shape[0] // groups
    out_w = n_v * dv
    resident = dict(pipeline_mode=pl.Buffered(1))
    specs = [pl.BlockSpec((ATTN_GROUP, LANES, tq), lambda g, i: (g, 0, i)),
             pl.BlockSpec((ATTN_GROUP, n_chunks, KEY_CHUNK, LANES), lambda g, i: (g, 0, 0, 0), **resident),
             pl.BlockSpec((n_v, n_chunks, dv + SUM_ROWS, KEY_CHUNK), lambda g, i: (g, 0, 0, 0), **resident)]
    if lam is None:
        body = functools.partial(_mla_attn_kernel, n_chunks=n_chunks, shifted=shifted)
        args = (qt, k, vt)
    else:
        body = functools.partial(_dif_attn_kernel, n_chunks=n_chunks, shifted=shifted, lam_init=lam_init)
        specs = [pl.BlockSpec((8, LANES), lambda g, i: (0, 0)),
                 pl.BlockSpec((1, LANES), lambda g, i: (0, 0))] + specs
        args = (lam, gsub, qt, k, vt)
    acc = pltpu.VMEM((ATTN_GROUP, dv + SUM_ROWS, tq), F32)
    if shifted:
        span = math.gcd(n_chunks - 1, PV_SPAN)
        scratch = [pltpu.VMEM((2, ATTN_GROUP, span * KEY_CHUNK, tq), BF16), acc]
    else:
        scratch = [pltpu.VMEM((2, ATTN_GROUP, KEY_CHUNK, tq), F32),
                   pltpu.VMEM((2, ATTN_GROUP, KEY_CHUNK, tq), BF16), acc]
    return _pallas_call(
        body,
        out_shape=jax.ShapeDtypeStruct((t, groups * out_w), F32),
        grid=(groups, t // tq),
        in_specs=specs,
        out_specs=pl.BlockSpec((tq, out_w), lambda g, i: (i, g)),
        scratch_shapes=scratch,
        semantics=("parallel", "arbitrary"),
        name=("mla_attention" if lam is None else "dif_attention") + ("_shifted" if shifted else ""),
    )(*args)


def _ssm_prep_kernel(x_ref, prev_ref, next_ref, dt_ref, w_ref, b_ref, dtb_ref, xo_ref, dto_ref, e_ref,
                     *, n_ctx, n_tok, tm):
    row0 = pl.program_id(0) * tm
    pad = SSM_CONV // 2
    has_prev = jnp.logical_and(row0 != 0, row0 != n_ctx)
    has_next = jnp.logical_and(row0 + tm != n_ctx, row0 + tm != n_tok)
    e_ref[0:8] = jnp.where(has_prev, prev_ref[...], 0.0)
    e_ref[8:8 + tm] = x_ref[...]
    e_ref[8 + tm:16 + tm] = jnp.where(has_next, next_ref[...], 0.0)
    acc = jnp.zeros(x_ref.shape, F32) + b_ref[...]
    for k in range(SSM_CONV):
        acc = acc + w_ref[k:k + 1, :] * e_ref[pl.ds(8 - pad + k, tm), :]
    xo_ref[...] = _silu(acc)
    d = dt_ref[...] + dtb_ref[...]
    dto_ref[...] = jnp.maximum(d, 0.0) + jnp.log1p(jnp.exp(-jnp.abs(d)))


def _ssm_prep(u, conv_w, conv_b, dt_bias, n_ctx):
    t = u.shape[0]
    tm = ROW_TILE
    nt = t // tm
    cb = U_XBC // SSM_XBC
    const = lambda i: (0, 0)
    return _pallas_call(
        functools.partial(_ssm_prep_kernel, n_ctx=n_ctx, n_tok=t, tm=tm),
        out_shape=(jax.ShapeDtypeStruct((t, SSM_XBC), F32), jax.ShapeDtypeStruct((t, LANES), F32)),
        grid=(nt,),
        in_specs=[pl.BlockSpec((tm, SSM_XBC), lambda i: (i, cb)),
                  pl.BlockSpec((8, SSM_XBC), lambda i: (jnp.maximum(i * (tm // 8) - 1, 0), cb)),
                  pl.BlockSpec((8, SSM_XBC), lambda i: (jnp.minimum((i + 1) * (tm // 8), t // 8 - 1), cb)),
                  pl.BlockSpec((tm, LANES), lambda i: (i, U_DT // LANES)),
                  pl.BlockSpec((8, SSM_XBC), const),
                  pl.BlockSpec((1, SSM_XBC), const),
                  pl.BlockSpec((1, LANES), const)],
        out_specs=(pl.BlockSpec((tm, SSM_XBC), lambda i: (i, 0)),
                   pl.BlockSpec((tm, LANES), lambda i: (i, 0))),
        scratch_shapes=[pltpu.VMEM((tm + 16, SSM_XBC), F32)],
        semantics=("parallel",),
        name="ssm_prep",
    )(u, u, u, u, conv_w, conv_b, dt_bias)


def _ssd_kernel(xf_ref, dtf_ref, xr_ref, dtr_ref, alog_ref, of_ref, or_ref, hf_ref, hr_ref):
    @pl.when(pl.program_id(0) == 0)
    def _():
        hf_ref[...] = jnp.zeros(hf_ref.shape, F32)
        hr_ref[...] = jnp.zeros(hr_ref.shape, F32)

    lc = SSM_CHUNK
    a_row = -jnp.exp(alog_ref[...])
    n_sub = xf_ref.shape[0] // lc
    fwd, bwd = [None] * n_sub, [None] * n_sub
    for j in range(n_sub):
        rows = slice(lc * j, lc * (j + 1))
        fwd[j] = _ssd_chunk(xf_ref[rows, :], dtf_ref[rows, :], a_row, hf_ref, 0)
        jr = n_sub - 1 - j
        rows = slice(lc * jr, lc * (jr + 1))
        bwd[jr] = _ssd_chunk(xr_ref[rows, :], dtr_ref[rows, :], a_row, hr_ref, 1)
    of_ref[...] = jnp.concatenate(fwd, axis=0)
    or_ref[...] = jnp.concatenate(bwd, axis=0)


def _ssd_chunk(xbc, dt, a_row, h_ref, direction):
    lc = SSM_CHUNK
    hd, per_group = SSM_HEAD_DIM, SSM_HEADS // SSM_GROUPS
    dta = dt * a_row
    r = lax.broadcasted_iota(jnp.int32, (lc, lc), 0)
    c = lax.broadcasted_iota(jnp.int32, (lc, lc), 1)
    keep = (r >= c) if direction == 0 else (r <= c)
    tri = jnp.where(keep, 1.0, 0.0).astype(F32)
    cum = jnp.dot(tri, dta, preferred_element_type=F32, precision=lax.Precision.HIGHEST)
    cum_t = cum.T
    dt_t = dt.T
    total = jnp.sum(dta, axis=0, keepdims=True)
    chunk_decay = jnp.exp(total)
    w_t = (dt * jnp.exp(total - cum)).T
    outs = []
    for g in range(SSM_GROUPS):
        b_f32 = xbc[:, SSM_WIDTH + SSM_STATE * g:SSM_WIDTH + SSM_STATE * (g + 1)]
        c_g = xbc[:, SSM_WIDTH + SSM_STATE * (SSM_GROUPS + g):
                  SSM_WIDTH + SSM_STATE * (SSM_GROUPS + g + 1)].astype(BF16)
        cb = _dot_nt(c_g, b_f32.astype(BF16))
        b_t = b_f32.T
        for h in range(per_group * g, per_group * (g + 1)):
            col = direction * SSM_HEADS + h
            cum_b = jnp.broadcast_to(cum[:, col:col + 1], (lc, lc))
            seg = jnp.exp(jnp.where(keep, cum_b - cum_t[col:col + 1, :], -jnp.inf))
            x_h = xbc[:, hd * h:hd * (h + 1)].astype(BF16)
            y = _dot((cb * seg * dt_t[col:col + 1, :]).astype(BF16), x_h)
            state = h_ref[h]
            y_off = _dot(c_g, state.astype(BF16))
            outs.append(y + y_off * jnp.exp(cum_b[:, :hd]))
            upd = _dot((b_t * w_t[col:col + 1, :]).astype(BF16), x_h)
            h_ref[h] = state * chunk_decay[:, col:col + 1] + upd
    return jnp.concatenate(outs, axis=1)


SSD_STEP = 2 * SSM_CHUNK


def _ssd(xbc, dt, a_log, n_ctx):
    t = xbc.shape[0]
    lc = SSD_STEP
    assert n_ctx % lc == 0 and t % lc == 0
    nc = t // lc
    ncc = n_ctx // lc
    fwd = lambda s: (s, 0)
    bwd = lambda s: (jnp.where(s < ncc, ncc - 1 - s, nc - 1 - (s - ncc)), 0)
    state = pltpu.VMEM((SSM_HEADS, SSM_STATE, SSM_HEAD_DIM), F32)
    out = jax.ShapeDtypeStruct((t, SSM_WIDTH), F32)
    return _pallas_call(
        _ssd_kernel,
        out_shape=(out, out),
        grid=(nc,),
        in_specs=[pl.BlockSpec((lc, SSM_XBC), fwd), pl.BlockSpec((lc, LANES), fwd),
                  pl.BlockSpec((lc, SSM_XBC), bwd), pl.BlockSpec((lc, LANES), bwd),
                  pl.BlockSpec((1, LANES), lambda s: (0, 0))],
        out_specs=(pl.BlockSpec((lc, SSM_WIDTH), fwd), pl.BlockSpec((lc, SSM_WIDTH), bwd)),
        scratch_shapes=[state, state],
        semantics=("arbitrary",),
        name="ssd_scan",
    )(xbc, dt, xbc, dt, a_log)


def _merge_kernel(x_ref, gate_ref, ya_ref, yb_ref, yf_ref, yr_ref, xs_ref, z_ref,
                  bg_ref, dskip_ref, gssm_ref, wa_ref, wb_ref, wc_ref, wo_ref,
                  ml_ref, mc_ref, g2_ref, wr_ref, br_ref,
                  xo_ref, f_ref, lg_ref, *, n_ctx, tm):
    y = (yf_ref[...] + yr_ref[...] + dskip_ref[...] * xs_ref[...]) * _silu(z_ref[...])
    gw = SSM_WIDTH // SSM_GROUPS
    yc = jnp.concatenate([_rms_rows(y[:, gw * g:gw * (g + 1)], gw) for g in range(SSM_GROUPS)], axis=1)
    yc = yc * gssm_ref[...]
    gate = _sigmoid(gate_ref[...] + bg_ref[...])
    m = (gate[:, :D_MODEL] * _dot(ya_ref[...].astype(BF16), wa_ref[...])
         + gate[:, D_MODEL:2 * D_MODEL] * _dot(yb_ref[...].astype(BF16), wb_ref[...])
         + gate[:, 2 * D_MODEL:] * _dot(yc.astype(BF16), wc_ref[...]))
    out = _dot(m.astype(BF16), wo_ref[...])
    row0 = pl.program_id(0) * tm
    is_ctx = (row0 + lax.broadcasted_iota(jnp.int32, (tm, 1), 0)) < n_ctx
    gt1 = jnp.where(is_ctx, mc_ref[2:3, :], ml_ref[2:3, :])
    x_new = x_ref[...] + gt1 * out
    xo_ref[...] = x_new
    f = _modulated_norm(x_new, g2_ref[...], ml_ref[3:5, :], mc_ref[3:5, :], row0, n_ctx)
    f_ref[...] = f
    lg_ref[...] = _dot(f.astype(BF16), wr_ref[...]) + br_ref[...]


def _merge(x_all, u, ya, yb, yf, yr, xbc, b_gate, dskip, g_ssm, wa, wb, wc, wo, mod_lat, mod_ctx,
           g2, w_router, b_router, n_ctx):
    t, d = x_all.shape
    tm = ROW_TILE
    const = lambda i: (0, 0)
    rows = lambda i: (i, 0)
    full = lambda a: pl.BlockSpec(a.shape, const)
    return _pallas_call(
        functools.partial(_merge_kernel, n_ctx=n_ctx, tm=tm),
        out_shape=(jax.ShapeDtypeStruct((t, d), F32), jax.ShapeDtypeStruct((t, d), F32),
                   jax.ShapeDtypeStruct((t, LANES), F32)),
        grid=(t // tm,),
        in_specs=[pl.BlockSpec((tm, d), rows),
                  pl.BlockSpec((tm, GATE_COLS), lambda i: (i, U_GATE // GATE_COLS)),
                  pl.BlockSpec((tm, MLA_WIDTH), rows),
                  pl.BlockSpec((tm, DIF_WIDTH), rows),
                  pl.BlockSpec((tm, SSM_WIDTH), rows),
                  pl.BlockSpec((tm, SSM_WIDTH), rows),
                  pl.BlockSpec((tm, SSM_WIDTH), rows),
                  pl.BlockSpec((tm, SSM_WIDTH), lambda i: (i, U_Z // SSM_WIDTH)),
                  full(b_gate), full(dskip), full(g_ssm), full(wa), full(wb), full(wc), full(wo),
                  full(mod_lat), full(mod_ctx), full(g2), full(w_router), full(b_router)],
        out_specs=(pl.BlockSpec((tm, d), rows), pl.BlockSpec((tm, d), rows),
                   pl.BlockSpec((tm, LANES), rows)),
        semantics=("parallel",),
        name="merge",
    )(x_all, u, ya, yb, yf, yr, xbc, u, b_gate, dskip, g_ssm, wa, wb, wc, wo, mod_lat, mod_ctx,
      g2, w_router, b_router)


def _moe_kernel(be_ref, nb_ref, x_ref, wgu_ref, bgu_ref, wd_ref, bd_ref, o_ref, wgu_s, wd_s):
    b = pl.program_id(0)
    prev = be_ref[jnp.maximum(b - 1, 0)]
    fresh = jnp.logical_or(b == 0, be_ref[b] != prev)

    @pl.when(fresh)
    def _():
        wgu_s[...] = wgu_ref[...].astype(BF16)
        wd_s[...] = wd_ref[...].astype(BF16)

    @pl.when(b < nb_ref[0])
    def _():
        gu = _dot(x_ref[...].astype(BF16), wgu_s[...]) + bgu_ref[...]
        glu = jnp.minimum(gu[:, :D_FF], SWIGLU_LIMIT)
        lin = jnp.clip(gu[:, D_FF:], -SWIGLU_LIMIT, SWIGLU_LIMIT)
        act = glu * _sigmoid(SWIGLU_ALPHA * glu) * (lin + 1.0)
        o_ref[...] = _dot(act.astype(BF16), wd_s[...]) + bd_ref[...]

    @pl.when(b >= nb_ref[0])
    def _():
        o_ref[...] = jnp.zeros(o_ref.shape, F32)


def _moe_experts(block_e, n_used, x_sorted, w_gu, b_gu, w_down, b_down, layer):
    n_slots, d = x_sorted.shape
    n_blocks = n_slots // MOE_BLOCK
    by_expert = lambda b, be, nb: (layer, be[b], 0, 0)
    return _pallas_call(
        _moe_kernel,
        out_shape=jax.ShapeDtypeStruct((n_slots, d), F32),
        num_scalar_prefetch=2,
        grid=(n_blocks,),
        in_specs=[pl.BlockSpec((MOE_BLOCK, d), lambda b, be, nb: (b, 0)),
                  pl.BlockSpec((None, None, d, 2 * D_FF), by_expert),
                  pl.BlockSpec((None, None, 1, 2 * D_FF), by_expert),
                  pl.BlockSpec((None, None, D_FF, d), by_expert),
                  pl.BlockSpec((None, None, 1, d), by_expert)],
        out_specs=pl.BlockSpec((MOE_BLOCK, d), lambda b, be, nb: (b, 0)),
        scratch_shapes=[pltpu.VMEM((d, 2 * D_FF), BF16), pltpu.VMEM((D_FF, d), BF16)],
        semantics=("arbitrary",),
        name="moe_experts",
    )(block_e, n_used, x_sorted, w_gu, b_gu, w_down, b_down)


def _combine_kernel(x_ref, g_ref, ml_ref, mc_ref, *rest, n_ctx, tm):
    y_refs, o_ref = rest[:TOP_K], rest[TOP_K]
    is_ctx = (pl.program_id(0) * tm + lax.broadcasted_iota(jnp.int32, (tm, 1), 0)) < n_ctx
    gate2 = jnp.where(is_ctx, mc_ref[...], ml_ref[...])
    g = g_ref[...]
    y = g[:, 0:1] * y_refs[0][...]
    for k in range(1, TOP_K):
        y = y + g[:, k:k + 1] * y_refs[k][...]
    o_ref[...] = x_ref[...] + gate2 * y


def _combine(x_all, y_rows, gates, gate2_lat, gate2_ctx, n_ctx):
    t, d = x_all.shape
    tm = ROW_TILE
    rows = lambda i: (i, 0)
    const = lambda i: (0, 0)
    return _pallas_call(
        functools.partial(_combine_kernel, n_ctx=n_ctx, tm=tm),
        out_shape=jax.ShapeDtypeStruct((t, d), F32),
        grid=(t // tm,),
        in_specs=[pl.BlockSpec((tm, d), rows), pl.BlockSpec((tm, LANES), rows),
                  pl.BlockSpec((1, d), const), pl.BlockSpec((1, d), const)]
                 + [pl.BlockSpec((tm, d), rows)] * TOP_K,
        out_specs=pl.BlockSpec((tm, d), rows),
        semantics=("parallel",),
        name="moe_combine",
    )(x_all, gates, gate2_lat, gate2_ctx, *y_rows)


ROUTE_IDX, ROUTE_GATE, ROUTE_RANK = 0, TOP_K, 2 * TOP_K


def _router_kernel(lg_ref, o_ref, cnt_ref, run_ref, *, tm):
    @pl.when(pl.program_id(0) == 0)
    def _():
        run_ref[...] = jnp.zeros(run_ref.shape, F32)

    lane = lax.broadcasted_iota(jnp.int32, (tm, LANES), 1)
    lane_f = lane.astype(F32)
    lg = jnp.where(lane < N_EXPERTS, lg_ref[...], -jnp.inf)
    hots, vals = [], []
    for _ in range(TOP_K):
        mx = jnp.max(lg, axis=-1, keepdims=True)
        idx = jnp.min(jnp.where(lg == mx, lane_f, float(LANES)), axis=-1, keepdims=True)
        hot = lane_f == idx
        lg = jnp.where(hot, -jnp.inf, lg)
        hots.append((hot, idx))
        vals.append(mx)
    exps = [jnp.exp(v - vals[0]) for v in vals]
    inv = 1.0 / sum(exps[1:], exps[0])
    chosen = jnp.zeros((tm, LANES), F32)
    for hot, _ in hots:
        chosen = jnp.where(hot, 1.0, chosen)
    r = lax.broadcasted_iota(jnp.int32, (tm, tm), 0)
    c = lax.broadcasted_iota(jnp.int32, (tm, tm), 1)
    earlier = jnp.where(r > c, 1.0, 0.0).astype(BF16)
    before = _dot(earlier, chosen.astype(BF16)) + run_ref[0:1, :]
    out = jnp.zeros((tm, LANES), F32)
    for k, (hot, idx) in enumerate(hots):
        rank = jnp.sum(jnp.where(hot, before, 0.0), axis=-1, keepdims=True)
        out = jnp.where(lane == ROUTE_IDX + k, idx, out)
        out = jnp.where(lane == ROUTE_GATE + k, exps[k] * inv, out)
        out = jnp.where(lane == ROUTE_RANK + k, rank, out)
    o_ref[...] = out
    run_ref[...] = run_ref[...] + jnp.sum(chosen, axis=0, keepdims=True)
    cnt_ref[...] = run_ref[...]


def _router(logits):
    t = logits.shape[0]
    tm = ROW_TILE
    return _pallas_call(
        functools.partial(_router_kernel, tm=tm),
        out_shape=(jax.ShapeDtypeStruct((t, LANES), F32), jax.ShapeDtypeStruct((8, LANES), F32)),
        grid=(t // tm,),
        in_specs=[pl.BlockSpec((tm, LANES), lambda i: (i, 0))],
        out_specs=(pl.BlockSpec((tm, LANES), lambda i: (i, 0)), pl.BlockSpec((8, LANES), lambda i: (0, 0))),
        scratch_shapes=[pltpu.VMEM((8, LANES), F32)],
        semantics=("arbitrary",),
        name="moe_router",
    )(logits)


def _route(logits, n_tok):
    routed, counts = _router(logits)
    top_idx = routed[:, ROUTE_IDX:ROUTE_IDX + TOP_K].astype(jnp.int32)
    gates = routed[:, ROUTE_GATE:ROUTE_GATE + TOP_K]
    rank = routed[:, ROUTE_RANK:ROUTE_RANK + TOP_K].astype(jnp.int32)
    n_assign = n_tok * TOP_K
    counts = counts[0, :N_EXPERTS].astype(jnp.int32)
    padded = (counts + MOE_BLOCK - 1) // MOE_BLOCK * MOE_BLOCK
    pad_end = jnp.cumsum(padded)
    pad_start = pad_end - padded
    slot_of = pad_start[top_idx] + rank
    n_blocks = -(-(n_assign + N_EXPERTS * (MOE_BLOCK - 1)) // MOE_BLOCK)
    n_slots = n_blocks * MOE_BLOCK
    token = jnp.arange(n_assign, dtype=jnp.int32) // TOP_K
    slot_tok = jnp.zeros((n_slots,), jnp.int32).at[slot_of.reshape(-1)].set(token, unique_indices=True)
    block_start = jnp.arange(n_blocks, dtype=jnp.int32) * MOE_BLOCK
    block_e = jnp.minimum(jnp.searchsorted(pad_end, block_start, side='right'), N_EXPERTS - 1)
    n_used = (pad_end[-1] // MOE_BLOCK).reshape(1)
    return gates, slot_tok, slot_of, block_e.astype(jnp.int32), n_used.astype(jnp.int32)


def kernel(x, c, ctx, c_ctx, w_mod, b_mod, g_norm1, g_norm2, w_in, b_gate, mla_g_q, mla_w_uq, mla_g_kv, mla_w_ukv, mla_g_qn, mla_g_kn, dif_g_qn, dif_g_kn, dif_lambda, dif_g_sub, ssm_conv_w, ssm_conv_b, ssm_dt_bias, ssm_a_log, ssm_d, ssm_g_norm, w_up_mla, w_up_dif, w_up_ssm, w_out, moe_w_router, moe_b_router, moe_w_gu, moe_b_gu, moe_w_down, moe_b_down):
    assert x.shape[0] == 1 and ctx.shape[0] == 1
    depth = w_in.shape[0]
    seq = x.shape[1]
    n_ctx = ctx.shape[1]
    n_tok = n_ctx + seq
    d = D_MODEL
    assert n_ctx == KEY_CHUNK and n_tok % ROW_TILE == 0 and seq % GRID_W == 0

    x_all = jnp.concatenate([ctx[0], x[0]], axis=0)
    cc = jnp.zeros((8, d), F32).at[0].set(c[0]).at[1].set(c_ctx)
    mod = _mod_vectors(cc, w_mod, b_mod)
    mod = mod[:, :2].reshape(depth, 2, 6, d)

    rope_mla = _rope_tables(seq, n_ctx, MLA_ROPE, MLA_NOPE)
    rope_dif = _rope_tables(seq, n_ctx, DIF_HEAD_DIM, 0)
    w_in_all = _take_columns(w_in, _in_proj_columns()).astype(BF16)
    w_uq_all = _take_columns(mla_w_uq, _head_columns(MLA_HEADS, MLA_QK, 0, MLA_QK, LANES)).astype(BF16)
    w_uk_all = _take_columns(mla_w_ukv, _head_columns(MLA_HEADS, MLA_NOPE + MLA_V, 0, MLA_NOPE, LANES)).astype(BF16)
    w_uv_all = _take_columns(mla_w_ukv, _head_columns(MLA_HEADS, MLA_NOPE + MLA_V, MLA_NOPE, MLA_V, MLA_V)).astype(BF16)
    w_up_all = [w.astype(BF16) for w in (w_up_mla, w_up_dif, w_up_ssm, w_out)]

    for i in range(depth):
        lam_init = 0.8 - 0.6 * math.exp(-0.3 * i)
        mod_lat, mod_ctx = mod[i, 0], mod[i, 1]
        u = _in_proj(x_all, g_norm1[i][None], mod_lat[0:2], mod_ctx[0:2], w_in_all, i, n_ctx)

        qt, k, vt = _mla_prep(
            u, mla_g_q[i][None], w_uq_all[i], mla_g_kv[i][None], w_uk_all[i], w_uv_all[i],
            _pad_lanes(mla_g_qn[i][None]), _pad_lanes(mla_g_kn[i][None]), rope_mla)
        ya = _attention(qt, k, vt)

        bound = (DIF_HEAD_DIM ** 0.5 * LOG2E) * jnp.max(jnp.abs(dif_g_qn[i])) * jnp.max(jnp.abs(dif_g_kn[i]))
        qt, k, vt = _dif_prep(u, _pad_lanes(dif_g_qn[i][None]), _pad_lanes(dif_g_kn[i][None]),
                              jnp.full((1, LANES), bound, F32), rope_dif)
        lam_rows = jnp.zeros((8, LANES), F32).at[:4, :DIF_HEAD_DIM].set(dif_lambda[i])
        dif_attention = functools.partial(_attention, lam=lam_rows, gsub=dif_g_sub[i][None], lam_init=lam_init)
        yb = lax.cond(bound <= MAX_SHIFT, functools.partial(dif_attention, shifted=True),
                      functools.partial(dif_attention, shifted=False), qt, k, vt)

        conv_w = jnp.zeros((8, SSM_XBC), F32).at[:SSM_CONV].set(ssm_conv_w[i])
        xbc, dt = _ssm_prep(u, conv_w, ssm_conv_b[i][None], _pad_lanes(ssm_dt_bias[i].reshape(1, -1)), n_ctx)
        a_log = _pad_lanes(ssm_a_log[i].reshape(1, -1))
        yf, yr = _ssd(xbc, dt, a_log, n_ctx)

        dskip = jnp.repeat(ssm_d[i, 0] + ssm_d[i, 1], SSM_HEAD_DIM)[None]
        w_router = jnp.zeros((d, LANES), BF16).at[:, :N_EXPERTS].set(moe_w_router[i].astype(BF16))
        b_router = jnp.zeros((1, LANES), F32).at[0, :N_EXPERTS].set(moe_b_router[i])
        x_all, f, logits = _merge(
            x_all, u, ya, yb, yf, yr, xbc, b_gate[i][None], dskip, ssm_g_norm[i][None],
            w_up_all[0][i], w_up_all[1][i], w_up_all[2][i], w_up_all[3][i],
            mod_lat[0:5], mod_ctx[0:5], g_norm2[i][None],
            w_router, b_router, n_ctx)

        gates, slot_tok, slot_of, block_e, n_used = _route(logits, n_tok)
        y_slots = _moe_experts(block_e, n_used, f[slot_tok], moe_w_gu,
                               moe_b_gu.reshape(depth, N_EXPERTS, 1, 2 * D_FF), moe_w_down,
                               moe_b_down.reshape(depth, N_EXPERTS, 1, d), i)
        y_rows = [y_slots[slot_of[:, k]] for k in range(TOP_K)]
        x_all = _combine(x_all, y_rows, _pad_lanes(gates), mod_lat[5:6], mod_ctx[5:6], n_ctx)
    return x_all[n_ctx:][None]
```

```python
import collections
import functools
import math

import numpy as np
import jax
import jax.numpy as jnp
from jax import lax
from jax.experimental import pallas as pl
from jax.experimental.pallas import tpu as pltpu

F32 = jnp.float32
BF16 = jnp.bfloat16
LANES = 128
VMEM_REQUEST_CAP = 56 * 1024 * 1024
VMEM_TEMPORARIES = 16 * 1024 * 1024

D_MODEL = 1024
EPS = 1e-6
ROPE_THETA = 10000.0
GRID_W = 64
N_BRANCH = 3

MLA_HEADS = 8
MLA_Q_RANK = 256
MLA_KV_RANK = 128
MLA_NOPE = 64
MLA_ROPE = 32
MLA_V = 64
MLA_QK = MLA_NOPE + MLA_ROPE
MLA_WIDTH = MLA_HEADS * MLA_V

DIF_HEADS = 4
DIF_HEAD_DIM = 64
DIF_WIDTH = DIF_HEADS * 2 * DIF_HEAD_DIM

SSM_HEADS = 8
SSM_HEAD_DIM = 64
SSM_WIDTH = SSM_HEADS * SSM_HEAD_DIM
SSM_GROUPS = 2
SSM_STATE = 128
SSM_CONV = 5
SSM_CHUNK = 128
SSM_XBC = SSM_WIDTH + 2 * SSM_GROUPS * SSM_STATE

N_EXPERTS = 32
TOP_K = 4
D_FF = 1024
SWIGLU_LIMIT = 7.0
SWIGLU_ALPHA = 1.702
MOE_BLOCK = 512

MLA_COLS = MLA_Q_RANK + MLA_KV_RANK + MLA_ROPE
DIF_COLS = 3 * DIF_WIDTH
SSM_COLS = SSM_WIDTH + SSM_XBC + 2 * SSM_HEADS
GATE_COLS = N_BRANCH * D_MODEL

ROW_TILE = 256
KEY_CHUNK = 256
LOG2E = 1.4426950408889634

U_GATE, U_DQ, U_DK, U_XBC, U_MLA, U_DV, U_Z, U_DT = 0, 3072, 4096, 5120, 6144, 6656, 7168, 7680
U_COLS = 8192
U_TILE_N = 2048


def _in_proj_columns():
    src = np.full((U_COLS,), -1, np.int64)
    dif0 = MLA_COLS
    ssm0 = MLA_COLS + DIF_COLS
    gate0 = ssm0 + SSM_COLS
    src[U_GATE:U_GATE + GATE_COLS] = gate0 + np.arange(GATE_COLS)
    for a in range(2 * DIF_HEADS):
        src[U_DQ + LANES * a:U_DQ + LANES * a + DIF_HEAD_DIM] = dif0 + DIF_HEAD_DIM * a + np.arange(DIF_HEAD_DIM)
        src[U_DK + LANES * a:U_DK + LANES * a + DIF_HEAD_DIM] = (dif0 + DIF_WIDTH + DIF_HEAD_DIM * a
                                                                  + np.arange(DIF_HEAD_DIM))
    src[U_DV:U_DV + DIF_WIDTH] = dif0 + 2 * DIF_WIDTH + np.arange(DIF_WIDTH)
    src[U_MLA:U_MLA + MLA_Q_RANK + MLA_KV_RANK] = np.arange(MLA_Q_RANK + MLA_KV_RANK)
    pe0 = U_MLA + MLA_Q_RANK + MLA_KV_RANK + MLA_NOPE
    src[pe0:pe0 + MLA_ROPE] = MLA_Q_RANK + MLA_KV_RANK + np.arange(MLA_ROPE)
    src[U_Z:U_Z + SSM_WIDTH] = ssm0 + np.arange(SSM_WIDTH)
    src[U_XBC:U_XBC + SSM_XBC] = ssm0 + SSM_WIDTH + np.arange(SSM_XBC)
    src[U_DT:U_DT + 2 * SSM_HEADS] = ssm0 + SSM_WIDTH + SSM_XBC + np.arange(2 * SSM_HEADS)
    return src


def _take_columns(w, src):
    pieces, start = [], 0
    for stop in range(1, len(src) + 1):
        same_run = stop < len(src) and (src[stop] == src[stop - 1] + 1 if src[stop - 1] >= 0 else src[stop] < 0)
        if not same_run:
            if src[start] < 0:
                pieces.append(jnp.zeros(w.shape[:-1] + (stop - start,), w.dtype))
            else:
                pieces.append(w[..., src[start]:src[start] + stop - start])
            start = stop
    return jnp.concatenate(pieces, axis=-1)


def _head_columns(n_heads, src_stride, src_off, width, dst_stride):
    src = np.full((n_heads * dst_stride,), -1, np.int64)
    for h in range(n_heads):
        src[h * dst_stride:h * dst_stride + width] = h * src_stride + src_off + np.arange(width)
    return src


def _pad_lanes(v, n=LANES):
    return jnp.pad(v, [(0, 0)] * (v.ndim - 1) + [(0, n - v.shape[-1])])


def _row_tile(n, cap):
    best = 8
    for t in range(8, cap + 1, 8):
        if n % t == 0:
            best = t
    return best


def _dot(a, b):
    return jnp.dot(a, b, preferred_element_type=F32)


def _dot_nt(a, b):
    return lax.dot_general(a, b, (((1,), (1,)), ((), ())), preferred_element_type=F32)


def _sigmoid(x):
    return 1.0 / (1.0 + jnp.exp(-x))


def _silu(x):
    return x * _sigmoid(x)


def _rms_rows(x, n):
    return x * lax.rsqrt(jnp.sum(x * x, axis=-1, keepdims=True) * (1.0 / n) + EPS)


def _as_tuple(x):
    return tuple(x) if isinstance(x, (tuple, list)) else (x,)


def _window_bytes(spec, dtype):
    if spec.block_shape is None:
        return 0
    dims = [1 if n is None else n for n in spec.block_shape]
    copies = 2 if spec.pipeline_mode is None else spec.pipeline_mode.buffer_count
    return math.prod(dims) * jnp.dtype(dtype).itemsize * copies


def _pallas_call(kernel, *, name, out_shape, grid, in_specs, out_specs, semantics, scratch_shapes=(),
                 num_scalar_prefetch=0):
    def call(*args):
        need = VMEM_TEMPORARIES
        need += sum(_window_bytes(s, a.dtype) for s, a in zip(in_specs, args[num_scalar_prefetch:]))
        need += sum(_window_bytes(s, o.dtype) for s, o in zip(_as_tuple(out_specs), _as_tuple(out_shape)))
        need += sum(math.prod(b.shape) * jnp.dtype(b.dtype).itemsize for b in scratch_shapes)
        params = pltpu.CompilerParams(dimension_semantics=semantics,
                                      vmem_limit_bytes=min(need, VMEM_REQUEST_CAP))
        grid_spec = pltpu.PrefetchScalarGridSpec(
            num_scalar_prefetch=num_scalar_prefetch, grid=grid, in_specs=list(in_specs), out_specs=out_specs,
            scratch_shapes=list(scratch_shapes))
        launch = pl.pallas_call(kernel, out_shape=out_shape, grid_spec=grid_spec, compiler_params=params, name=name)
        return launch(*args)
    return call


_Stage = collections.namedtuple("_Stage", "kernel out_shape in_specs out_specs scratch_shapes args")


def _run_stages(name, grid, stages):
    n_in = [len(s.in_specs) for s in stages]
    n_out = [len(s.out_shape) for s in stages]
    n_scr = [len(s.scratch_shapes) for s in stages]

    def body(*refs):
        ins, outs, scr = refs[:sum(n_in)], refs[sum(n_in):sum(n_in) + sum(n_out)], refs[sum(n_in) + sum(n_out):]
        for j, stage in enumerate(stages):
            stage.kernel(*ins[sum(n_in[:j]):sum(n_in[:j + 1])], *outs[sum(n_out[:j]):sum(n_out[:j + 1])],
                         *scr[sum(n_scr[:j]):sum(n_scr[:j + 1])])

    flat = lambda field: [x for s in stages for x in getattr(s, field)]
    outs = _pallas_call(body, name=name, grid=grid, out_shape=tuple(flat("out_shape")),
                        in_specs=flat("in_specs"), out_specs=tuple(flat("out_specs")),
                        scratch_shapes=flat("scratch_shapes"), semantics=("parallel",))(*flat("args"))
    return [outs[sum(n_out[:j]):sum(n_out[:j + 1])] for j in range(len(stages))]


def _mod_kernel(a_ref, w_ref, b_ref, o_ref):
    a = _silu(a_ref[...]).astype(BF16)
    o_ref[0] = _dot(a, w_ref[0].astype(BF16)) + b_ref[0]


def _mod_vectors(cc, w_mod, b_mod):
    depth, d, n = w_mod.shape
    tn = 1536
    return _pallas_call(
        _mod_kernel,
        out_shape=jax.ShapeDtypeStruct((depth, 8, n), F32),
        grid=(depth, n // tn),
        in_specs=[pl.BlockSpec((8, d), lambda l, j: (0, 0)),
                  pl.BlockSpec((1, d, tn), lambda l, j: (l, 0, j)),
                  pl.BlockSpec((1, 1, tn), lambda l, j: (l, 0, j))],
        out_specs=pl.BlockSpec((1, 8, tn), lambda l, j: (l, 0, j)),
        semantics=("parallel", "parallel"),
        name="mod_vectors",
    )(cc, w_mod, b_mod.reshape(depth, 1, n))


def _modulated_norm(x, g, mod_lat, mod_ctx, row0, n_ctx):
    rows = x.shape[0]
    is_ctx = (row0 + lax.broadcasted_iota(jnp.int32, (rows, 1), 0)) < n_ctx
    shift = jnp.where(is_ctx, mod_ctx[0:1, :], mod_lat[0:1, :])
    scale = jnp.where(is_ctx, mod_ctx[1:2, :], mod_lat[1:2, :])
    return _rms_rows(x, x.shape[1]) * g * (1.0 + scale) + shift


def _in_proj_kernel(x_ref, g_ref, ml_ref, mc_ref, w_ref, o_ref, h_ref, *, n_ctx, tm):
    @pl.when(pl.program_id(1) == 0)
    def _():
        h = _modulated_norm(x_ref[...], g_ref[...], ml_ref[...], mc_ref[...], pl.program_id(0) * tm, n_ctx)
        h_ref[...] = h.astype(BF16)

    o_ref[...] = _dot(h_ref[...], w_ref[...])


def _in_proj(x_all, g, mod_lat, mod_ctx, w, layer, n_ctx):
    t, d = x_all.shape
    n = w.shape[2]
    tm = _row_tile(t, 1280)
    return _pallas_call(
        functools.partial(_in_proj_kernel, n_ctx=n_ctx, tm=tm),
        out_shape=jax.ShapeDtypeStruct((t, n), F32),
        grid=(t // tm, n // U_TILE_N),
        in_specs=[pl.BlockSpec((tm, d), lambda i, j: (i, 0)),
                  pl.BlockSpec((1, d), lambda i, j: (0, 0)),
                  pl.BlockSpec((2, d), lambda i, j: (0, 0)),
                  pl.BlockSpec((2, d), lambda i, j: (0, 0)),
                  pl.BlockSpec((None, d, U_TILE_N), lambda i, j: (layer, 0, j))],
        out_specs=pl.BlockSpec((tm, U_TILE_N), lambda i, j: (i, j)),
        scratch_shapes=[pltpu.VMEM((tm, d), BF16)],
        semantics=("parallel", "arbitrary"),
        name="in_proj",
    )(x_all, g, mod_lat, mod_ctx, w)


def _rope_tables(seq_len, n_ctx, rot_dim, lane0):
    n_rows = seq_len // GRID_W
    row = jnp.repeat(jnp.arange(n_rows), GRID_W).astype(F32)
    col = jnp.tile(jnp.arange(GRID_W), n_rows).astype(F32)
    axis_dim = rot_dim // 2
    half = axis_dim // 2
    inv = ROPE_THETA ** (-jnp.arange(0, axis_dim, 2, dtype=F32) / axis_dim)
    ang_r = row[:, None] * inv
    ang_c = col[:, None] * inv
    zeros = jnp.zeros((seq_len, half), F32)
    cos = jnp.concatenate([jnp.cos(ang_r), jnp.cos(ang_r), jnp.cos(ang_c), jnp.cos(ang_c)], axis=1)
    s1 = jnp.concatenate([zeros, jnp.sin(ang_r), zeros, jnp.sin(ang_c)], axis=1)
    s2 = jnp.concatenate([-jnp.sin(ang_r), zeros, -jnp.sin(ang_c), zeros], axis=1)

    def place(tab, fill):
        full = jnp.full((seq_len, LANES), fill, F32).at[:, lane0:lane0 + rot_dim].set(tab)
        ctx = jnp.full((n_ctx, LANES), fill, F32)
        return jnp.concatenate([ctx, full], axis=0)

    return place(cos, 1.0), place(s1, 0.0), place(s2, 0.0)


def _norm_rope_blocks(blocks, gains, n, cos, s1, s2, half):
    sums = [jnp.sum(x * x, axis=-1, keepdims=True) for x in blocks]
    inv = [lax.rsqrt(s * (1.0 / n) + EPS) for s in sums]
    normed = [x * r * g for x, r, g in zip(blocks, inv, gains)]
    fwd = [pltpu.roll(y, half, 1) for y in normed]
    bwd = [pltpu.roll(y, LANES - half, 1) for y in normed]
    return [y * cos + a * s1 + b * s2 for y, a, b in zip(normed, fwd, bwd)]


def _mla_prep_kernel(u_ref, gq_ref, wuq_ref, gkv_ref, wk_ref, wv_ref, gqn_ref, gkn_ref,
                     cos_ref, s1_ref, s2_ref, qt_ref, k_ref, vt_ref):
    u = u_ref[...]
    cq = u[:, :MLA_Q_RANK]
    ckv = u[:, MLA_Q_RANK:MLA_Q_RANK + MLA_KV_RANK]
    pe = u[:, MLA_Q_RANK + MLA_KV_RANK:]
    q = _dot((_rms_rows(cq, MLA_Q_RANK) * gq_ref[...]).astype(BF16), wuq_ref[...])
    kv_in = (_rms_rows(ckv, MLA_KV_RANK) * gkv_ref[...]).astype(BF16)
    kn = _dot(kv_in, wk_ref[...])
    v = _dot(kv_in, wv_ref[...])
    cos, s1, s2 = cos_ref[...], s1_ref[...], s2_ref[...]
    half = MLA_ROPE // 4
    q_scale = MLA_QK ** -0.5 * LOG2E
    n = MLA_HEADS
    blocks = ([q[:, LANES * h:LANES * (h + 1)] for h in range(n)]
              + [kn[:, LANES * h:LANES * (h + 1)] + pe for h in range(n)])
    roped = _norm_rope_blocks(blocks, [gqn_ref[...]] * n + [gkn_ref[...]] * n, MLA_QK, cos, s1, s2, half)
    for h in range(n):
        qt_ref[h] = (roped[h] * q_scale).T.astype(BF16)
        k_ref[h, 0] = roped[n + h].astype(BF16)
    tm = v.shape[0]
    vt_ref[:, 0] = _with_sum_rows(v.T.reshape(MLA_HEADS, MLA_V, tm)).astype(BF16)


def _mla_prep(u, gq, wuq, gkv, wk, wv, gqn, gkn, tabs):
    t = u.shape[0]
    tm = ROW_TILE
    nt = t // tm
    const = lambda i: (0, 0)
    rows = lambda i: (i, 0)
    return _Stage(
        kernel=_mla_prep_kernel,
        out_shape=(jax.ShapeDtypeStruct((MLA_HEADS, LANES, t), BF16),
                   jax.ShapeDtypeStruct((MLA_HEADS, nt, tm, LANES), BF16),
                   jax.ShapeDtypeStruct((MLA_HEADS, nt, MLA_V + SUM_ROWS, tm), BF16)),
        in_specs=[pl.BlockSpec((tm, 512), lambda i: (i, U_MLA // 512)),
                  pl.BlockSpec((1, MLA_Q_RANK), const),
                  pl.BlockSpec(wuq.shape, const),
                  pl.BlockSpec((1, MLA_KV_RANK), const),
                  pl.BlockSpec(wk.shape, const),
                  pl.BlockSpec(wv.shape, const),
                  pl.BlockSpec((1, LANES), const),
                  pl.BlockSpec((1, LANES), const),
                  pl.BlockSpec((tm, LANES), rows),
                  pl.BlockSpec((tm, LANES), rows),
                  pl.BlockSpec((tm, LANES), rows)],
        out_specs=(pl.BlockSpec((MLA_HEADS, LANES, tm), lambda i: (0, 0, i)),
                   pl.BlockSpec((MLA_HEADS, 1, tm, LANES), lambda i: (0, i, 0, 0)),
                   pl.BlockSpec((MLA_HEADS, 1, MLA_V + SUM_ROWS, tm), lambda i: (0, i, 0, 0))),
        scratch_shapes=(),
        args=(u, gq, wuq, gkv, wk, wv, gqn, gkn, *tabs))


def _dif_prep_kernel(q_ref, k_ref, v_ref, gq_ref, gk_ref, shift_ref, cos_ref, s1_ref, s2_ref,
                     qt_out, k_out, vt_out):
    cos, s1, s2 = cos_ref[...], s1_ref[...], s2_ref[...]
    half = DIF_HEAD_DIM // 4
    q_scale = DIF_HEAD_DIM ** -0.5 * LOG2E
    n = 2 * DIF_HEADS
    blocks = ([q_ref[:, LANES * a:LANES * (a + 1)] for a in range(n)]
              + [k_ref[:, LANES * a:LANES * (a + 1)] for a in range(n)])
    roped = _norm_rope_blocks(blocks, [gq_ref[...]] * n + [gk_ref[...]] * n, DIF_HEAD_DIM, cos, s1, s2, half)
    spare = lax.broadcasted_iota(jnp.int32, roped[0].shape, 1) == DIF_HEAD_DIM
    for a in range(n):
        qt_out[a] = jnp.where(spare, -shift_ref[...], roped[a] * q_scale).T.astype(BF16)
        k_out[a, 0] = jnp.where(spare, 1.0, roped[n + a]).astype(BF16)
    v = v_ref[...]
    vt_out[:, 0] = _with_sum_rows(v.T.reshape(DIF_HEADS, 2 * DIF_HEAD_DIM, v.shape[0])).astype(BF16)


def _dif_prep(u, gq, gk, shift, tabs):
    t = u.shape[0]
    tm = ROW_TILE
    nt = t // tm
    nsub = 2 * DIF_HEADS
    const = lambda i: (0, 0)
    rows = lambda i: (i, 0)
    return _Stage(
        kernel=_dif_prep_kernel,
        out_shape=(jax.ShapeDtypeStruct((nsub, LANES, t), BF16),
                   jax.ShapeDtypeStruct((nsub, nt, tm, LANES), BF16),
                   jax.ShapeDtypeStruct((DIF_HEADS, nt, 2 * DIF_HEAD_DIM + SUM_ROWS, tm), BF16)),
        in_specs=[pl.BlockSpec((tm, 1024), lambda i: (i, U_DQ // 1024)),
                  pl.BlockSpec((tm, 1024), lambda i: (i, U_DK // 1024)),
                  pl.BlockSpec((tm, 512), lambda i: (i, U_DV // 512)),
                  pl.BlockSpec((1, LANES), const),
                  pl.BlockSpec((1, LANES), const),
                  pl.BlockSpec((1, LANES), const),
                  pl.BlockSpec((tm, LANES), rows),
                  pl.BlockSpec((tm, LANES), rows),
                  pl.BlockSpec((tm, LANES), rows)],
        out_specs=(pl.BlockSpec((nsub, LANES, tm), lambda i: (0, 0, i)),
                   pl.BlockSpec((nsub, 1, tm, LANES), lambda i: (0, i, 0, 0)),
                   pl.BlockSpec((DIF_HEADS, 1, 2 * DIF_HEAD_DIM + SUM_ROWS, tm), lambda i: (0, i, 0, 0))),
        scratch_shapes=(),
        args=(u, u, u, gq, gk, shift, *tabs))


ATTN_GROUP = 2
ATTN_TRIP = 64
SUM_ROWS = 16


def _with_sum_rows(vt):
    heads, _, keys = vt.shape
    row = lax.broadcasted_iota(jnp.int32, (heads, SUM_ROWS, keys), 1)
    return jnp.concatenate([vt, jnp.where(row == 0, 1.0, 0.0).astype(vt.dtype)], axis=1)


def _attn_group(qt_ref, k_ref, vt_ref, v_of_sub, s_buf, p_buf, acc_ref, n_chunks):
    tq = qt_ref.shape[2]
    last = n_chunks - 1

    def scores(c):
        out = []
        for a in range(ATTN_GROUP):
            s = _dot(k_ref[a, c], qt_ref[a])
            out.append((s, jnp.max(s, axis=0, keepdims=True)))
        return out

    def stash(sc, slot):
        for a in range(ATTN_GROUP):
            s_buf[slot, a] = sc[a][0]
        return tuple(mx for (_, mx) in sc)

    def softmax(s_of, mx, slot, ms):
        new_ms, alphas = [], []
        for a in range(ATTN_GROUP):
            m_new = jnp.maximum(ms[a], mx[a])
            alphas.append(jnp.exp2(ms[a] - m_new))
            p_buf[slot, a] = jnp.exp2(s_of(a) - m_new).astype(BF16)
            new_ms.append(m_new)
        return tuple(new_ms), tuple(alphas)

    def values(c, slot, alphas):
        for a in range(ATTN_GROUP):
            acc_ref[a] = alphas[a] * acc_ref[a] + _dot(vt_ref[v_of_sub[a], c], p_buf[slot, a])

    per_trip = math.gcd(last, ATTN_TRIP)

    def trip(t, state):
        ms, alphas, mx0 = state
        c = per_trip * t + 1
        for pair in range(per_trip // 2):
            cur, nxt = pair % 2, 1 - pair % 2
            even = scores(c + 1)
            values(c - 1, 0, alphas)
            ms, alphas = softmax(lambda a: s_buf[cur, a], mx0, 1, ms)
            mx0 = stash(scores(jnp.minimum(c + 2, last)), nxt)
            values(c, 1, alphas)
            ms, alphas = softmax(lambda a: even[a][0], [mx for (_, mx) in even], 0, ms)
            c = c + 2
        return ms, alphas, mx0

    acc_ref[...] = jnp.zeros(acc_ref.shape, F32)
    ms = tuple(jnp.full((1, tq), -jnp.inf, F32) for _ in range(ATTN_GROUP))
    first = scores(0)
    ms, alphas = softmax(lambda a: first[a][0], [mx for (_, mx) in first], 0, ms)
    mx0 = stash(scores(1), 0)
    n_trips = jnp.where(pl.program_id(1) > 0, last // per_trip, 0)
    ms, alphas, _ = lax.fori_loop(0, n_trips, trip, (ms, alphas, mx0))
    values(per_trip * n_trips, 0, alphas)


def _normalised(acc_ref, a):
    dv = acc_ref.shape[1] - SUM_ROWS
    return acc_ref[a, :dv] * (1.0 / acc_ref[a, dv:dv + 1])


PV_SPAN = 32
MAX_SHIFT = 60.0


def _attn_group_shifted(qt_ref, k_ref, vt_ref, v_of_sub, p_buf, acc_ref, n_chunks):
    span = math.gcd(n_chunks - 1, PV_SPAN)

    def accumulate(c0, n, slot):
        for a in range(ATTN_GROUP):
            for i in range(n):
                s = _dot(k_ref[a, c0 + i], qt_ref[a])
                p_buf[slot, a, KEY_CHUNK * i:KEY_CHUNK * (i + 1)] = jnp.exp2(s).astype(BF16)
            vt = jnp.concatenate([vt_ref[v_of_sub[a], c0 + i] for i in range(n)], axis=1)
            acc_ref[a] = acc_ref[a] + _dot(vt, p_buf[slot, a, :KEY_CHUNK * n])

    acc_ref[...] = jnp.zeros(acc_ref.shape, F32)
    accumulate(0, 1, 0)

    @pl.when(pl.program_id(1) > 0)
    def _():
        for j in range((n_chunks - 1) // span):
            accumulate(1 + span * j, span, j % 2)


def _attend(shifted, qt_ref, k_ref, vt_ref, v_of_sub, scratch, n_chunks):
    if shifted:
        p_buf, acc_ref = scratch
        _attn_group_shifted(qt_ref, k_ref, vt_ref, v_of_sub, p_buf, acc_ref, n_chunks)
    else:
        s_buf, p_buf, acc_ref = scratch
        _attn_group(qt_ref, k_ref, vt_ref, v_of_sub, s_buf, p_buf, acc_ref, n_chunks)
    return acc_ref


def _mla_attn_kernel(qt_ref, k_ref, vt_ref, o_ref, *scratch, n_chunks, shifted):
    acc_ref = _attend(shifted, qt_ref, k_ref, vt_ref, tuple(range(ATTN_GROUP)), scratch, n_chunks)
    o = jnp.concatenate([_normalised(acc_ref, a) for a in range(ATTN_GROUP)], axis=0)
    o_ref[...] = o.T


def _dif_attn_kernel(lam_ref, gsub_ref, qt_ref, k_ref, vt_ref, o_ref, *scratch, n_chunks, shifted, lam_init):
    acc_ref = _attend(shifted, qt_ref, k_ref, vt_ref, tuple(a // 2 for a in range(ATTN_GROUP)), scratch,
                      n_chunks)
    lp = lam_ref[...]
    lam = (jnp.exp(jnp.sum(lp[0:1] * lp[1:2], axis=-1, keepdims=True))
           - jnp.exp(jnp.sum(lp[2:3] * lp[3:4], axis=-1, keepdims=True)) + lam_init)
    outs = []
    for h in range(ATTN_GROUP // 2):
        o = _normalised(acc_ref, 2 * h) - lam * _normalised(acc_ref, 2 * h + 1)
        outs.append(o * lax.rsqrt(jnp.mean(o * o, axis=0, keepdims=True) + EPS))
    gsub = gsub_ref[...] * (1.0 - lam_init)
    o_ref[...] = jnp.concatenate(outs, axis=0).T * jnp.concatenate([gsub] * len(outs), axis=1)


def _attention(qt, k, vt, *, shifted=False, lam=None, gsub=None, lam_init=None):
    nsub, _, t = qt.shape
    n_chunks = k.shape[1]
    assert (n_chunks - 1) % 4 == 0 and nsub % ATTN_GROUP == 0
    tq = KEY_CHUNK
    dv = vt.shape[2] - SUM_ROWS
    groups = nsub // ATTN_GROUP
    n_v = vt.shape[0] // groups
    out_w = n_v * dv
    resident = dict(pipeline_mode=pl.Buffered(1))
    specs = [pl.BlockSpec((ATTN_GROUP, LANES, tq), lambda g, i: (g, 0, i)),
             pl.BlockSpec((ATTN_GROUP, n_chunks, KEY_CHUNK, LANES), lambda g, i: (g, 0, 0, 0), **resident),
             pl.BlockSpec((n_v, n_chunks, dv + SUM_ROWS, KEY_CHUNK), lambda g, i: (g, 0, 0, 0), **resident)]
    if lam is None:
        body = functools.partial(_mla_attn_kernel, n_chunks=n_chunks, shifted=shifted)
        args = (qt, k, vt)
    else:
        body = functools.partial(_dif_attn_kernel, n_chunks=n_chunks, shifted=shifted, lam_init=lam_init)
        specs = [pl.BlockSpec((8, LANES), lambda g, i: (0, 0)),
                 pl.BlockSpec((1, LANES), lambda g, i: (0, 0))] + specs
        args = (lam, gsub, qt, k, vt)
    acc = pltpu.VMEM((ATTN_GROUP, dv + SUM_ROWS, tq), F32)
    if shifted:
        span = math.gcd(n_chunks - 1, PV_SPAN)
        scratch = [pltpu.VMEM((2, ATTN_GROUP, span * KEY_CHUNK, tq), BF16), acc]
    else:
        scratch = [pltpu.VMEM((2, ATTN_GROUP, KEY_CHUNK, tq), F32),
                   pltpu.VMEM((2, ATTN_GROUP, KEY_CHUNK, tq), BF16), acc]
    return _pallas_call(
        body,
        out_shape=jax.ShapeDtypeStruct((t, groups * out_w), F32),
        grid=(groups, t // tq),
        in_specs=specs,
        out_specs=pl.BlockSpec((tq, out_w), lambda g, i: (i, g)),
        scratch_shapes=scratch,
        semantics=("parallel", "arbitrary"),
        name=("mla_attention" if lam is None else "dif_attention") + ("_shifted" if shifted else ""),
    )(*args)


def _ssm_prep_kernel(x_ref, prev_ref, next_ref, dt_ref, w_ref, b_ref, dtb_ref, xo_ref, dto_ref, e_ref,
                     *, n_ctx, n_tok, tm):
    row0 = pl.program_id(0) * tm
    pad = SSM_CONV // 2
    has_prev = jnp.logical_and(row0 != 0, row0 != n_ctx)
    has_next = jnp.logical_and(row0 + tm != n_ctx, row0 + tm != n_tok)
    e_ref[0:8] = jnp.where(has_prev, prev_ref[...], 0.0)
    e_ref[8:8 + tm] = x_ref[...]
    e_ref[8 + tm:16 + tm] = jnp.where(has_next, next_ref[...], 0.0)
    acc = jnp.zeros(x_ref.shape, F32) + b_ref[...]
    for k in range(SSM_CONV):
        acc = acc + w_ref[k:k + 1, :] * e_ref[pl.ds(8 - pad + k, tm), :]
    xo_ref[...] = _silu(acc)
    d = dt_ref[...] + dtb_ref[...]
    dto_ref[...] = jnp.maximum(d, 0.0) + jnp.log1p(jnp.exp(-jnp.abs(d)))


def _ssm_prep(u, conv_w, conv_b, dt_bias, n_ctx):
    t = u.shape[0]
    tm = ROW_TILE
    nt = t // tm
    cb = U_XBC // SSM_XBC
    const = lambda i: (0, 0)
    return _Stage(
        kernel=functools.partial(_ssm_prep_kernel, n_ctx=n_ctx, n_tok=t, tm=tm),
        out_shape=(jax.ShapeDtypeStruct((t, SSM_XBC), F32), jax.ShapeDtypeStruct((t, LANES), F32)),
        in_specs=[pl.BlockSpec((tm, SSM_XBC), lambda i: (i, cb)),
                  pl.BlockSpec((8, SSM_XBC), lambda i: (jnp.maximum(i * (tm // 8) - 1, 0), cb)),
                  pl.BlockSpec((8, SSM_XBC), lambda i: (jnp.minimum((i + 1) * (tm // 8), t // 8 - 1), cb)),
                  pl.BlockSpec((tm, LANES), lambda i: (i, U_DT // LANES)),
                  pl.BlockSpec((8, SSM_XBC), const),
                  pl.BlockSpec((1, SSM_XBC), const),
                  pl.BlockSpec((1, LANES), const)],
        out_specs=(pl.BlockSpec((tm, SSM_XBC), lambda i: (i, 0)),
                   pl.BlockSpec((tm, LANES), lambda i: (i, 0))),
        scratch_shapes=(pltpu.VMEM((tm + 16, SSM_XBC), F32),),
        args=(u, u, u, u, conv_w, conv_b, dt_bias))


def _ssd_kernel(xf_ref, dtf_ref, xr_ref, dtr_ref, alog_ref, of_ref, or_ref, hf_ref, hr_ref):
    @pl.when(pl.program_id(0) == 0)
    def _():
        hf_ref[...] = jnp.zeros(hf_ref.shape, F32)
        hr_ref[...] = jnp.zeros(hr_ref.shape, F32)

    lc = SSM_CHUNK
    a_row = -jnp.exp(alog_ref[...])
    n_sub = xf_ref.shape[0] // lc
    fwd, bwd = [None] * n_sub, [None] * n_sub
    for j in range(n_sub):
        rows = slice(lc * j, lc * (j + 1))
        fwd[j] = _ssd_chunk(xf_ref[rows, :], dtf_ref[rows, :], a_row, hf_ref, 0)
        jr = n_sub - 1 - j
        rows = slice(lc * jr, lc * (jr + 1))
        bwd[jr] = _ssd_chunk(xr_ref[rows, :], dtr_ref[rows, :], a_row, hr_ref, 1)
    of_ref[...] = jnp.concatenate(fwd, axis=0)
    or_ref[...] = jnp.concatenate(bwd, axis=0)


def _ssd_chunk(xbc, dt, a_row, h_ref, direction):
    lc = SSM_CHUNK
    hd, per_group = SSM_HEAD_DIM, SSM_HEADS // SSM_GROUPS
    dta = dt * a_row
    r = lax.broadcasted_iota(jnp.int32, (lc, lc), 0)
    c = lax.broadcasted_iota(jnp.int32, (lc, lc), 1)
    keep = (r >= c) if direction == 0 else (r <= c)
    tri = jnp.where(keep, 1.0, 0.0).astype(F32)
    cum = jnp.dot(tri, dta, preferred_element_type=F32, precision=lax.Precision.HIGHEST)
    cum_t = cum.T
    dt_t = dt.T
    total = jnp.sum(dta, axis=0, keepdims=True)
    chunk_decay = jnp.exp(total)
    w_t = (dt * jnp.exp(total - cum)).T
    outs = []
    for g in range(SSM_GROUPS):
        b_f32 = xbc[:, SSM_WIDTH + SSM_STATE * g:SSM_WIDTH + SSM_STATE * (g + 1)]
        c_g = xbc[:, SSM_WIDTH + SSM_STATE * (SSM_GROUPS + g):
                  SSM_WIDTH + SSM_STATE * (SSM_GROUPS + g + 1)].astype(BF16)
        cb = _dot_nt(c_g, b_f32.astype(BF16))
        b_t = b_f32.T
        for h in range(per_group * g, per_group * (g + 1)):
            col = direction * SSM_HEADS + h
            cum_b = jnp.broadcast_to(cum[:, col:col + 1], (lc, lc))
            seg = jnp.exp(jnp.where(keep, cum_b - cum_t[col:col + 1, :], -jnp.inf))
            x_h = xbc[:, hd * h:hd * (h + 1)].astype(BF16)
            y = _dot((cb * seg * dt_t[col:col + 1, :]).astype(BF16), x_h)
            state = h_ref[h]
            y_off = _dot(c_g, state.astype(BF16))
            outs.append(y + y_off * jnp.exp(cum_b[:, :hd]))
            upd = _dot((b_t * w_t[col:col + 1, :]).astype(BF16), x_h)
            h_ref[h] = state * chunk_decay[:, col:col + 1] + upd
    return jnp.concatenate(outs, axis=1)


SSD_STEP = 2 * SSM_CHUNK


def _ssd(xbc, dt, a_log, n_ctx):
    t = xbc.shape[0]
    lc = SSD_STEP
    assert n_ctx % lc == 0 and t % lc == 0
    nc = t // lc
    ncc = n_ctx // lc
    fwd = lambda s: (s, 0)
    bwd = lambda s: (jnp.where(s < ncc, ncc - 1 - s, nc - 1 - (s - ncc)), 0)
    state = pltpu.VMEM((SSM_HEADS, SSM_STATE, SSM_HEAD_DIM), F32)
    out = jax.ShapeDtypeStruct((t, SSM_WIDTH), F32)
    return _pallas_call(
        _ssd_kernel,
        out_shape=(out, out),
        grid=(nc,),
        in_specs=[pl.BlockSpec((lc, SSM_XBC), fwd), pl.BlockSpec((lc, LANES), fwd),
                  pl.BlockSpec((lc, SSM_XBC), bwd), pl.BlockSpec((lc, LANES), bwd),
                  pl.BlockSpec((1, LANES), lambda s: (0, 0))],
        out_specs=(pl.BlockSpec((lc, SSM_WIDTH), fwd), pl.BlockSpec((lc, SSM_WIDTH), bwd)),
        scratch_shapes=[state, state],
        semantics=("arbitrary",),
        name="ssd_scan",
    )(xbc, dt, xbc, dt, a_log)


def _merge_kernel(x_ref, gate_ref, ya_ref, yb_ref, yf_ref, yr_ref, xs_ref, z_ref,
                  bg_ref, dskip_ref, gssm_ref, wa_ref, wb_ref, wc_ref, wo_ref,
                  ml_ref, mc_ref, g2_ref, wr_ref, br_ref,
                  xo_ref, f_ref, lg_ref, *, n_ctx, tm):
    y = (yf_ref[...] + yr_ref[...] + dskip_ref[...] * xs_ref[...]) * _silu(z_ref[...])
    gw = SSM_WIDTH // SSM_GROUPS
    yc = jnp.concatenate([_rms_rows(y[:, gw * g:gw * (g + 1)], gw) for g in range(SSM_GROUPS)], axis=1)
    yc = yc * gssm_ref[...]
    gate = _sigmoid(gate_ref[...] + bg_ref[...])
    m = (gate[:, :D_MODEL] * _dot(ya_ref[...].astype(BF16), wa_ref[...])
         + gate[:, D_MODEL:2 * D_MODEL] * _dot(yb_ref[...].astype(BF16), wb_ref[...])
         + gate[:, 2 * D_MODEL:] * _dot(yc.astype(BF16), wc_ref[...]))
    out = _dot(m.astype(BF16), wo_ref[...])
    row0 = pl.program_id(0) * tm
    is_ctx = (row0 + lax.broadcasted_iota(jnp.int32, (tm, 1), 0)) < n_ctx
    gt1 = jnp.where(is_ctx, mc_ref[2:3, :], ml_ref[2:3, :])
    x_new = x_ref[...] + gt1 * out
    xo_ref[...] = x_new
    f = _modulated_norm(x_new, g2_ref[...], ml_ref[3:5, :], mc_ref[3:5, :], row0, n_ctx)
    f_ref[...] = f
    lg_ref[...] = _dot(f.astype(BF16), wr_ref[...]) + br_ref[...]


def _merge(x_all, u, ya, yb, yf, yr, xbc, b_gate, dskip, g_ssm, wa, wb, wc, wo, mod_lat, mod_ctx,
           g2, w_router, b_router, n_ctx):
    t, d = x_all.shape
    tm = ROW_TILE
    const = lambda i: (0, 0)
    rows = lambda i: (i, 0)
    full = lambda a: pl.BlockSpec(a.shape, const)
    return _pallas_call(
        functools.partial(_merge_kernel, n_ctx=n_ctx, tm=tm),
        out_shape=(jax.ShapeDtypeStruct((t, d), F32), jax.ShapeDtypeStruct((t, d), F32),
                   jax.ShapeDtypeStruct((t, LANES), F32)),
        grid=(t // tm,),
        in_specs=[pl.BlockSpec((tm, d), rows),
                  pl.BlockSpec((tm, GATE_COLS), lambda i: (i, U_GATE // GATE_COLS)),
                  pl.BlockSpec((tm, MLA_WIDTH), rows),
                  pl.BlockSpec((tm, DIF_WIDTH), rows),
                  pl.BlockSpec((tm, SSM_WIDTH), rows),
                  pl.BlockSpec((tm, SSM_WIDTH), rows),
                  pl.BlockSpec((tm, SSM_WIDTH), rows),
                  pl.BlockSpec((tm, SSM_WIDTH), lambda i: (i, U_Z // SSM_WIDTH)),
                  full(b_gate), full(dskip), full(g_ssm), full(wa), full(wb), full(wc), full(wo),
                  full(mod_lat), full(mod_ctx), full(g2), full(w_router), full(b_router)],
        out_specs=(pl.BlockSpec((tm, d), rows), pl.BlockSpec((tm, d), rows),
                   pl.BlockSpec((tm, LANES), rows)),
        semantics=("parallel",),
        name="merge",
    )(x_all, u, ya, yb, yf, yr, xbc, u, b_gate, dskip, g_ssm, wa, wb, wc, wo, mod_lat, mod_ctx,
      g2, w_router, b_router)


def _moe_kernel(be_ref, nb_ref, x_ref, wgu_ref, bgu_ref, wd_ref, bd_ref, o_ref, wgu_s, wd_s):
    b = pl.program_id(0)
    prev = be_ref[jnp.maximum(b - 1, 0)]
    fresh = jnp.logical_or(b == 0, be_ref[b] != prev)

    @pl.when(fresh)
    def _():
        wgu_s[...] = wgu_ref[...].astype(BF16)
        wd_s[...] = wd_ref[...].astype(BF16)

    @pl.when(b < nb_ref[0])
    def _():
        gu = _dot(x_ref[...].astype(BF16), wgu_s[...]) + bgu_ref[...]
        glu = jnp.minimum(gu[:, :D_FF], SWIGLU_LIMIT)
        lin = jnp.clip(gu[:, D_FF:], -SWIGLU_LIMIT, SWIGLU_LIMIT)
        act = glu * _sigmoid(SWIGLU_ALPHA * glu) * (lin + 1.0)
        o_ref[...] = _dot(act.astype(BF16), wd_s[...]) + bd_ref[...]

    @pl.when(b >= nb_ref[0])
    def _():
        o_ref[...] = jnp.zeros(o_ref.shape, F32)


def _moe_experts(block_e, n_used, x_sorted, w_gu, b_gu, w_down, b_down, layer):
    n_slots, d = x_sorted.shape
    n_blocks = n_slots // MOE_BLOCK
    by_expert = lambda b, be, nb: (layer, be[b], 0, 0)
    return _pallas_call(
        _moe_kernel,
        out_shape=jax.ShapeDtypeStruct((n_slots, d), F32),
        num_scalar_prefetch=2,
        grid=(n_blocks,),
        in_specs=[pl.BlockSpec((MOE_BLOCK, d), lambda b, be, nb: (b, 0)),
                  pl.BlockSpec((None, None, d, 2 * D_FF), by_expert),
                  pl.BlockSpec((None, None, 1, 2 * D_FF), by_expert),
                  pl.BlockSpec((None, None, D_FF, d), by_expert),
                  pl.BlockSpec((None, None, 1, d), by_expert)],
        out_specs=pl.BlockSpec((MOE_BLOCK, d), lambda b, be, nb: (b, 0)),
        scratch_shapes=[pltpu.VMEM((d, 2 * D_FF), BF16), pltpu.VMEM((D_FF, d), BF16)],
        semantics=("arbitrary",),
        name="moe_experts",
    )(block_e, n_used, x_sorted, w_gu, b_gu, w_down, b_down)


def _combine_kernel(x_ref, g_ref, ml_ref, mc_ref, *rest, n_ctx, tm):
    y_refs, o_ref = rest[:TOP_K], rest[TOP_K]
    is_ctx = (pl.program_id(0) * tm + lax.broadcasted_iota(jnp.int32, (tm, 1), 0)) < n_ctx
    gate2 = jnp.where(is_ctx, mc_ref[...], ml_ref[...])
    g = g_ref[...]
    y = g[:, 0:1] * y_refs[0][...]
    for k in range(1, TOP_K):
        y = y + g[:, k:k + 1] * y_refs[k][...]
    o_ref[...] = x_ref[...] + gate2 * y


def _combine(x_all, y_rows, gates, gate2_lat, gate2_ctx, n_ctx):
    t, d = x_all.shape
    tm = ROW_TILE
    rows = lambda i: (i, 0)
    const = lambda i: (0, 0)
    return _pallas_call(
        functools.partial(_combine_kernel, n_ctx=n_ctx, tm=tm),
        out_shape=jax.ShapeDtypeStruct((t, d), F32),
        grid=(t // tm,),
        in_specs=[pl.BlockSpec((tm, d), rows), pl.BlockSpec((tm, LANES), rows),
                  pl.BlockSpec((1, d), const), pl.BlockSpec((1, d), const)]
                 + [pl.BlockSpec((tm, d), rows)] * TOP_K,
        out_specs=pl.BlockSpec((tm, d), rows),
        semantics=("parallel",),
        name="moe_combine",
    )(x_all, gates, gate2_lat, gate2_ctx, *y_rows)


ROUTE_IDX, ROUTE_GATE, ROUTE_RANK = 0, TOP_K, 2 * TOP_K


def _router_kernel(lg_ref, o_ref, cnt_ref, run_ref, *, tm):
    @pl.when(pl.program_id(0) == 0)
    def _():
        run_ref[...] = jnp.zeros(run_ref.shape, F32)

    lane = lax.broadcasted_iota(jnp.int32, (tm, LANES), 1)
    lane_f = lane.astype(F32)
    lg = jnp.where(lane < N_EXPERTS, lg_ref[...], -jnp.inf)
    hots, vals = [], []
    for _ in range(TOP_K):
        mx = jnp.max(lg, axis=-1, keepdims=True)
        idx = jnp.min(jnp.where(lg == mx, lane_f, float(LANES)), axis=-1, keepdims=True)
        hot = lane_f == idx
        lg = jnp.where(hot, -jnp.inf, lg)
        hots.append((hot, idx))
        vals.append(mx)
    exps = [jnp.exp(v - vals[0]) for v in vals]
    inv = 1.0 / sum(exps[1:], exps[0])
    chosen = jnp.zeros((tm, LANES), F32)
    for hot, _ in hots:
        chosen = jnp.where(hot, 1.0, chosen)
    r = lax.broadcasted_iota(jnp.int32, (tm, tm), 0)
    c = lax.broadcasted_iota(jnp.int32, (tm, tm), 1)
    earlier = jnp.where(r > c, 1.0, 0.0).astype(BF16)
    before = _dot(earlier, chosen.astype(BF16)) + run_ref[0:1, :]
    out = jnp.zeros((tm, LANES), F32)
    for k, (hot, idx) in enumerate(hots):
        rank = jnp.sum(jnp.where(hot, before, 0.0), axis=-1, keepdims=True)
        out = jnp.where(lane == ROUTE_IDX + k, idx, out)
        out = jnp.where(lane == ROUTE_GATE + k, exps[k] * inv, out)
        out = jnp.where(lane == ROUTE_RANK + k, rank, out)
    o_ref[...] = out
    run_ref[...] = run_ref[...] + jnp.sum(chosen, axis=0, keepdims=True)
    cnt_ref[...] = run_ref[...]


def _router(logits):
    t = logits.shape[0]
    tm = ROW_TILE
    return _pallas_call(
        functools.partial(_router_kernel, tm=tm),
        out_shape=(jax.ShapeDtypeStruct((t, LANES), F32), jax.ShapeDtypeStruct((8, LANES), F32)),
        grid=(t // tm,),
        in_specs=[pl.BlockSpec((tm, LANES), lambda i: (i, 0))],
        out_specs=(pl.BlockSpec((tm, LANES), lambda i: (i, 0)), pl.BlockSpec((8, LANES), lambda i: (0, 0))),
        scratch_shapes=[pltpu.VMEM((8, LANES), F32)],
        semantics=("arbitrary",),
        name="moe_router",
    )(logits)


def _route(logits, n_tok):
    routed, counts = _router(logits)
    top_idx = routed[:, ROUTE_IDX:ROUTE_IDX + TOP_K].astype(jnp.int32)
    gates = routed[:, ROUTE_GATE:ROUTE_GATE + TOP_K]
    rank = routed[:, ROUTE_RANK:ROUTE_RANK + TOP_K].astype(jnp.int32)
    n_assign = n_tok * TOP_K
    counts = counts[0, :N_EXPERTS].astype(jnp.int32)
    padded = (counts + MOE_BLOCK - 1) // MOE_BLOCK * MOE_BLOCK
    pad_end = jnp.cumsum(padded)
    pad_start = pad_end - padded
    slot_of = pad_start[top_idx] + rank
    n_blocks = -(-(n_assign + N_EXPERTS * (MOE_BLOCK - 1)) // MOE_BLOCK)
    n_slots = n_blocks * MOE_BLOCK
    token = jnp.arange(n_assign, dtype=jnp.int32) // TOP_K
    slot_tok = jnp.zeros((n_slots,), jnp.int32).at[slot_of.reshape(-1)].set(token, unique_indices=True)
    block_start = jnp.arange(n_blocks, dtype=jnp.int32) * MOE_BLOCK
    block_e = jnp.minimum(jnp.searchsorted(pad_end, block_start, side='right'), N_EXPERTS - 1)
    n_used = (pad_end[-1] // MOE_BLOCK).reshape(1)
    return gates, slot_tok, slot_of, block_e.astype(jnp.int32), n_used.astype(jnp.int32)


def kernel(x, c, ctx, c_ctx, w_mod, b_mod, g_norm1, g_norm2, w_in, b_gate, mla_g_q, mla_w_uq, mla_g_kv, mla_w_ukv, mla_g_qn, mla_g_kn, dif_g_qn, dif_g_kn, dif_lambda, dif_g_sub, ssm_conv_w, ssm_conv_b, ssm_dt_bias, ssm_a_log, ssm_d, ssm_g_norm, w_up_mla, w_up_dif, w_up_ssm, w_out, moe_w_router, moe_b_router, moe_w_gu, moe_b_gu, moe_w_down, moe_b_down):
    assert x.shape[0] == 1 and ctx.shape[0] == 1
    depth = w_in.shape[0]
    seq = x.shape[1]
    n_ctx = ctx.shape[1]
    n_tok = n_ctx + seq
    d = D_MODEL
    assert n_ctx == KEY_CHUNK and n_tok % ROW_TILE == 0 and seq % GRID_W == 0

    x_all = jnp.concatenate([ctx[0], x[0]], axis=0)
    cc = jnp.zeros((8, d), F32).at[0].set(c[0]).at[1].set(c_ctx)
    mod = _mod_vectors(cc, w_mod, b_mod)
    mod = mod[:, :2].reshape(depth, 2, 6, d)

    rope_mla = _rope_tables(seq, n_ctx, MLA_ROPE, MLA_NOPE)
    rope_dif = _rope_tables(seq, n_ctx, DIF_HEAD_DIM, 0)
    w_in_all = _take_columns(w_in, _in_proj_columns()).astype(BF16)
    w_uq_all = _take_columns(mla_w_uq, _head_columns(MLA_HEADS, MLA_QK, 0, MLA_QK, LANES)).astype(BF16)
    w_uk_all = _take_columns(mla_w_ukv, _head_columns(MLA_HEADS, MLA_NOPE + MLA_V, 0, MLA_NOPE, LANES)).astype(BF16)
    w_uv_all = _take_columns(mla_w_ukv, _head_columns(MLA_HEADS, MLA_NOPE + MLA_V, MLA_NOPE, MLA_V, MLA_V)).astype(BF16)
    w_up_all = [w.astype(BF16) for w in (w_up_mla, w_up_dif, w_up_ssm, w_out)]

    for i in range(depth):
        lam_init = 0.8 - 0.6 * math.exp(-0.3 * i)
        mod_lat, mod_ctx = mod[i, 0], mod[i, 1]
        u = _in_proj(x_all, g_norm1[i][None], mod_lat[0:2], mod_ctx[0:2], w_in_all, i, n_ctx)

        bound = (DIF_HEAD_DIM ** 0.5 * LOG2E) * jnp.max(jnp.abs(dif_g_qn[i])) * jnp.max(jnp.abs(dif_g_kn[i]))
        conv_w = jnp.zeros((8, SSM_XBC), F32).at[:SSM_CONV].set(ssm_conv_w[i])
        mla_qkv, dif_qkv, (xbc, dt) = _run_stages("branch_prep", (n_tok // ROW_TILE,), [
            _mla_prep(u, mla_g_q[i][None], w_uq_all[i], mla_g_kv[i][None], w_uk_all[i], w_uv_all[i],
                      _pad_lanes(mla_g_qn[i][None]), _pad_lanes(mla_g_kn[i][None]), rope_mla),
            _dif_prep(u, _pad_lanes(dif_g_qn[i][None]), _pad_lanes(dif_g_kn[i][None]),
                      jnp.full((1, LANES), bound, F32), rope_dif),
            _ssm_prep(u, conv_w, ssm_conv_b[i][None], _pad_lanes(ssm_dt_bias[i].reshape(1, -1)), n_ctx)])
        ya = _attention(*mla_qkv)

        lam_rows = jnp.zeros((8, LANES), F32).at[:4, :DIF_HEAD_DIM].set(dif_lambda[i])
        dif_attention = functools.partial(_attention, lam=lam_rows, gsub=dif_g_sub[i][None], lam_init=lam_init)
        yb = lax.cond(bound <= MAX_SHIFT, functools.partial(dif_attention, shifted=True),
                      functools.partial(dif_attention, shifted=False), *dif_qkv)

        a_log = _pad_lanes(ssm_a_log[i].reshape(1, -1))
        yf, yr = _ssd(xbc, dt, a_log, n_ctx)

        dskip = jnp.repeat(ssm_d[i, 0] + ssm_d[i, 1], SSM_HEAD_DIM)[None]
        w_router = jnp.zeros((d, LANES), BF16).at[:, :N_EXPERTS].set(moe_w_router[i].astype(BF16))
        b_router = jnp.zeros((1, LANES), F32).at[0, :N_EXPERTS].set(moe_b_router[i])
        x_all, f, logits = _merge(
            x_all, u, ya, yb, yf, yr, xbc, b_gate[i][None], dskip, ssm_g_norm[i][None],
            w_up_all[0][i], w_up_all[1][i], w_up_all[2][i], w_up_all[3][i],
            mod_lat[0:5], mod_ctx[0:5], g_norm2[i][None],
            w_router, b_router, n_ctx)

        gates, slot_tok, slot_of, block_e, n_used = _route(logits, n_tok)
        y_slots = _moe_experts(block_e, n_used, f[slot_tok], moe_w_gu,
                               moe_b_gu.reshape(depth, N_EXPERTS, 1, 2 * D_FF), moe_w_down,
                               moe_b_down.reshape(depth, N_EXPERTS, 1, d), i)
        y_rows = [y_slots[slot_of[:, k]] for k in range(TOP_K)]
        x_all = _combine(x_all, y_rows, _pad_lanes(gates), mod_lat[5:6], mod_ctx[5:6], n_ctx)
    return x_all[n_ctx:][None]
```

```python
import collections
import functools
import math

import numpy as np
import jax
import jax.numpy as jnp
from jax import lax
from jax.experimental import pallas as pl
from jax.experimental.pallas import tpu as pltpu

F32 = jnp.float32
BF16 = jnp.bfloat16
LANES = 128
VMEM_REQUEST_CAP = 56 * 1024 * 1024
VMEM_TEMPORARIES = 16 * 1024 * 1024

D_MODEL = 1024
EPS = 1e-6
ROPE_THETA = 10000.0
GRID_W = 64
N_BRANCH = 3

MLA_HEADS = 8
MLA_Q_RANK = 256
MLA_KV_RANK = 128
MLA_NOPE = 64
MLA_ROPE = 32
MLA_V = 64
MLA_QK = MLA_NOPE + MLA_ROPE
MLA_WIDTH = MLA_HEADS * MLA_V

DIF_HEADS = 4
DIF_HEAD_DIM = 64
DIF_WIDTH = DIF_HEADS * 2 * DIF_HEAD_DIM

SSM_HEADS = 8
SSM_HEAD_DIM = 64
SSM_WIDTH = SSM_HEADS * SSM_HEAD_DIM
SSM_GROUPS = 2
SSM_STATE = 128
SSM_CONV = 5
SSM_CHUNK = 128
SSM_XBC = SSM_WIDTH + 2 * SSM_GROUPS * SSM_STATE

N_EXPERTS = 32
TOP_K = 4
D_FF = 1024
SWIGLU_LIMIT = 7.0
SWIGLU_ALPHA = 1.702
MOE_BLOCK = 512

MLA_COLS = MLA_Q_RANK + MLA_KV_RANK + MLA_ROPE
DIF_COLS = 3 * DIF_WIDTH
SSM_COLS = SSM_WIDTH + SSM_XBC + 2 * SSM_HEADS
GATE_COLS = N_BRANCH * D_MODEL

ROW_TILE = 256
KEY_CHUNK = 256
LOG2E = 1.4426950408889634

U_GATE, U_DQ, U_DK, U_XBC, U_MLA, U_DV, U_Z, U_DT = 0, 3072, 4096, 5120, 6144, 6656, 7168, 7680
U_COLS = 8192
U_TILE_N = 2048


def _in_proj_columns():
    src = np.full((U_COLS,), -1, np.int64)
    dif0 = MLA_COLS
    ssm0 = MLA_COLS + DIF_COLS
    gate0 = ssm0 + SSM_COLS
    src[U_GATE:U_GATE + GATE_COLS] = gate0 + np.arange(GATE_COLS)
    for a in range(2 * DIF_HEADS):
        src[U_DQ + LANES * a:U_DQ + LANES * a + DIF_HEAD_DIM] = dif0 + DIF_HEAD_DIM * a + np.arange(DIF_HEAD_DIM)
        src[U_DK + LANES * a:U_DK + LANES * a + DIF_HEAD_DIM] = (dif0 + DIF_WIDTH + DIF_HEAD_DIM * a
                                                                  + np.arange(DIF_HEAD_DIM))
    src[U_DV:U_DV + DIF_WIDTH] = dif0 + 2 * DIF_WIDTH + np.arange(DIF_WIDTH)
    src[U_MLA:U_MLA + MLA_Q_RANK + MLA_KV_RANK] = np.arange(MLA_Q_RANK + MLA_KV_RANK)
    pe0 = U_MLA + MLA_Q_RANK + MLA_KV_RANK + MLA_NOPE
    src[pe0:pe0 + MLA_ROPE] = MLA_Q_RANK + MLA_KV_RANK + np.arange(MLA_ROPE)
    src[U_Z:U_Z + SSM_WIDTH] = ssm0 + np.arange(SSM_WIDTH)
    src[U_XBC:U_XBC + SSM_XBC] = ssm0 + SSM_WIDTH + np.arange(SSM_XBC)
    src[U_DT:U_DT + 2 * SSM_HEADS] = ssm0 + SSM_WIDTH + SSM_XBC + np.arange(2 * SSM_HEADS)
    return src


def _take_columns(w, src):
    pieces, start = [], 0
    for stop in range(1, len(src) + 1):
        same_run = stop < len(src) and (src[stop] == src[stop - 1] + 1 if src[stop - 1] >= 0 else src[stop] < 0)
        if not same_run:
            if src[start] < 0:
                pieces.append(jnp.zeros(w.shape[:-1] + (stop - start,), w.dtype))
            else:
                pieces.append(w[..., src[start]:src[start] + stop - start])
            start = stop
    return jnp.concatenate(pieces, axis=-1)


def _head_columns(n_heads, src_stride, src_off, width, dst_stride):
    src = np.full((n_heads * dst_stride,), -1, np.int64)
    for h in range(n_heads):
        src[h * dst_stride:h * dst_stride + width] = h * src_stride + src_off + np.arange(width)
    return src


def _pad_lanes(v, n=LANES):
    return jnp.pad(v, [(0, 0)] * (v.ndim - 1) + [(0, n - v.shape[-1])])


def _row_tile(n, cap):
    best = 8
    for t in range(8, cap + 1, 8):
        if n % t == 0:
            best = t
    return best


def _dot(a, b):
    return jnp.dot(a, b, preferred_element_type=F32)


def _dot_nt(a, b):
    return lax.dot_general(a, b, (((1,), (1,)), ((), ())), preferred_element_type=F32)


def _sigmoid(x):
    return 1.0 / (1.0 + jnp.exp(-x))


def _silu(x):
    return x * _sigmoid(x)


def _rms_rows(x, n):
    return x * lax.rsqrt(jnp.sum(x * x, axis=-1, keepdims=True) * (1.0 / n) + EPS)


def _as_tuple(x):
    return tuple(x) if isinstance(x, (tuple, list)) else (x,)


def _window_bytes(spec, dtype):
    if spec.block_shape is None:
        return 0
    dims = [1 if n is None else n for n in spec.block_shape]
    copies = 2 if spec.pipeline_mode is None else spec.pipeline_mode.buffer_count
    return math.prod(dims) * jnp.dtype(dtype).itemsize * copies


def _pallas_call(kernel, *, name, out_shape, grid, in_specs, out_specs, semantics, scratch_shapes=(),
                 num_scalar_prefetch=0):
    def call(*args):
        need = VMEM_TEMPORARIES
        need += sum(_window_bytes(s, a.dtype) for s, a in zip(in_specs, args[num_scalar_prefetch:]))
        need += sum(_window_bytes(s, o.dtype) for s, o in zip(_as_tuple(out_specs), _as_tuple(out_shape)))
        need += sum(math.prod(b.shape) * jnp.dtype(b.dtype).itemsize for b in scratch_shapes)
        params = pltpu.CompilerParams(dimension_semantics=semantics,
                                      vmem_limit_bytes=min(need, VMEM_REQUEST_CAP))
        grid_spec = pltpu.PrefetchScalarGridSpec(
            num_scalar_prefetch=num_scalar_prefetch, grid=grid, in_specs=list(in_specs), out_specs=out_specs,
            scratch_shapes=list(scratch_shapes))
        launch = pl.pallas_call(kernel, out_shape=out_shape, grid_spec=grid_spec, compiler_params=params, name=name)
        return launch(*args)
    return call


_Stage = collections.namedtuple("_Stage", "kernel out_shape in_specs out_specs scratch_shapes args")


def _run_stages(name, grid, stages):
    n_in = [len(s.in_specs) for s in stages]
    n_out = [len(s.out_shape) for s in stages]
    n_scr = [len(s.scratch_shapes) for s in stages]

    def body(*refs):
        ins, outs, scr = refs[:sum(n_in)], refs[sum(n_in):sum(n_in) + sum(n_out)], refs[sum(n_in) + sum(n_out):]
        for j, stage in enumerate(stages):
            stage.kernel(*ins[sum(n_in[:j]):sum(n_in[:j + 1])], *outs[sum(n_out[:j]):sum(n_out[:j + 1])],
                         *scr[sum(n_scr[:j]):sum(n_scr[:j + 1])])

    flat = lambda field: [x for s in stages for x in getattr(s, field)]
    outs = _pallas_call(body, name=name, grid=grid, out_shape=tuple(flat("out_shape")),
                        in_specs=flat("in_specs"), out_specs=tuple(flat("out_specs")),
                        scratch_shapes=flat("scratch_shapes"), semantics=("parallel",))(*flat("args"))
    return [outs[sum(n_out[:j]):sum(n_out[:j + 1])] for j in range(len(stages))]


def _mod_kernel(a_ref, w_ref, b_ref, o_ref):
    a = _silu(a_ref[...]).astype(BF16)
    o_ref[0] = _dot(a, w_ref[0].astype(BF16)) + b_ref[0]


def _mod_vectors(cc, w_mod, b_mod):
    depth, d, n = w_mod.shape
    tn = 1536
    return _pallas_call(
        _mod_kernel,
        out_shape=jax.ShapeDtypeStruct((depth, 8, n), F32),
        grid=(depth, n // tn),
        in_specs=[pl.BlockSpec((8, d), lambda l, j: (0, 0)),
                  pl.BlockSpec((1, d, tn), lambda l, j: (l, 0, j)),
                  pl.BlockSpec((1, 1, tn), lambda l, j: (l, 0, j))],
        out_specs=pl.BlockSpec((1, 8, tn), lambda l, j: (l, 0, j)),
        semantics=("parallel", "parallel"),
        name="mod_vectors",
    )(cc, w_mod, b_mod.reshape(depth, 1, n))


def _modulated_norm(x, g, mod_lat, mod_ctx, row0, n_ctx):
    rows = x.shape[0]
    is_ctx = (row0 + lax.broadcasted_iota(jnp.int32, (rows, 1), 0)) < n_ctx
    shift = jnp.where(is_ctx, mod_ctx[0:1, :], mod_lat[0:1, :])
    scale = jnp.where(is_ctx, mod_ctx[1:2, :], mod_lat[1:2, :])
    return _rms_rows(x, x.shape[1]) * g * (1.0 + scale) + shift


def _in_proj_kernel(x_ref, g_ref, ml_ref, mc_ref, w_ref, o_ref, h_ref, *, n_ctx, tm):
    @pl.when(pl.program_id(1) == 0)
    def _():
        h = _modulated_norm(x_ref[...], g_ref[...], ml_ref[...], mc_ref[...], pl.program_id(0) * tm, n_ctx)
        h_ref[...] = h.astype(BF16)

    o_ref[...] = _dot(h_ref[...], w_ref[...])


def _in_proj(x_all, g, mod_lat, mod_ctx, w, layer, n_ctx):
    t, d = x_all.shape
    n = w.shape[2]
    tm = _row_tile(t, 1280)
    return _pallas_call(
        functools.partial(_in_proj_kernel, n_ctx=n_ctx, tm=tm),
        out_shape=jax.ShapeDtypeStruct((t, n), F32),
        grid=(t // tm, n // U_TILE_N),
        in_specs=[pl.BlockSpec((tm, d), lambda i, j: (i, 0)),
                  pl.BlockSpec((1, d), lambda i, j: (0, 0)),
                  pl.BlockSpec((2, d), lambda i, j: (0, 0)),
                  pl.BlockSpec((2, d), lambda i, j: (0, 0)),
                  pl.BlockSpec((None, d, U_TILE_N), lambda i, j: (layer, 0, j))],
        out_specs=pl.BlockSpec((tm, U_TILE_N), lambda i, j: (i, j)),
        scratch_shapes=[pltpu.VMEM((tm, d), BF16)],
        semantics=("parallel", "arbitrary"),
        name="in_proj",
    )(x_all, g, mod_lat, mod_ctx, w)


def _rope_tables(seq_len, n_ctx, rot_dim, lane0):
    n_rows = seq_len // GRID_W
    row = jnp.repeat(jnp.arange(n_rows), GRID_W).astype(F32)
    col = jnp.tile(jnp.arange(GRID_W), n_rows).astype(F32)
    axis_dim = rot_dim // 2
    half = axis_dim // 2
    inv = ROPE_THETA ** (-jnp.arange(0, axis_dim, 2, dtype=F32) / axis_dim)
    ang_r = row[:, None] * inv
    ang_c = col[:, None] * inv
    zeros = jnp.zeros((seq_len, half), F32)
    cos = jnp.concatenate([jnp.cos(ang_r), jnp.cos(ang_r), jnp.cos(ang_c), jnp.cos(ang_c)], axis=1)
    s1 = jnp.concatenate([zeros, jnp.sin(ang_r), zeros, jnp.sin(ang_c)], axis=1)
    s2 = jnp.concatenate([-jnp.sin(ang_r), zeros, -jnp.sin(ang_c), zeros], axis=1)

    def place(tab, fill):
        full = jnp.full((seq_len, LANES), fill, F32).at[:, lane0:lane0 + rot_dim].set(tab)
        ctx = jnp.full((n_ctx, LANES), fill, F32)
        return jnp.concatenate([ctx, full], axis=0)

    return place(cos, 1.0), place(s1, 0.0), place(s2, 0.0)


def _norm_rope_blocks(blocks, gains, n, cos, s1, s2, half):
    sums = [jnp.sum(x * x, axis=-1, keepdims=True) for x in blocks]
    inv = [lax.rsqrt(s * (1.0 / n) + EPS) for s in sums]
    normed = [x * r * g for x, r, g in zip(blocks, inv, gains)]
    fwd = [pltpu.roll(y, half, 1) for y in normed]
    bwd = [pltpu.roll(y, LANES - half, 1) for y in normed]
    return [y * cos + a * s1 + b * s2 for y, a, b in zip(normed, fwd, bwd)]


def _mla_prep_kernel(u_ref, gq_ref, wuq_ref, gkv_ref, wk_ref, wv_ref, gqn_ref, gkn_ref,
                     cos_ref, s1_ref, s2_ref, qt_ref, k_ref, vt_ref):
    u = u_ref[...]
    cq = u[:, :MLA_Q_RANK]
    ckv = u[:, MLA_Q_RANK:MLA_Q_RANK + MLA_KV_RANK]
    pe = u[:, MLA_Q_RANK + MLA_KV_RANK:]
    q = _dot((_rms_rows(cq, MLA_Q_RANK) * gq_ref[...]).astype(BF16), wuq_ref[...])
    kv_in = (_rms_rows(ckv, MLA_KV_RANK) * gkv_ref[...]).astype(BF16)
    kn = _dot(kv_in, wk_ref[...])
    v = _dot(kv_in, wv_ref[...])
    cos, s1, s2 = cos_ref[...], s1_ref[...], s2_ref[...]
    half = MLA_ROPE // 4
    q_scale = MLA_QK ** -0.5 * LOG2E
    n = MLA_HEADS
    blocks = ([q[:, LANES * h:LANES * (h + 1)] for h in range(n)]
              + [kn[:, LANES * h:LANES * (h + 1)] + pe for h in range(n)])
    roped = _norm_rope_blocks(blocks, [gqn_ref[...]] * n + [gkn_ref[...]] * n, MLA_QK, cos, s1, s2, half)
    for h in range(n):
        qt_ref[h] = (roped[h] * q_scale).T.astype(BF16)
        k_ref[h, 0] = roped[n + h].astype(BF16)
    tm = v.shape[0]
    vt_ref[:, 0] = _with_sum_rows(v.T.reshape(MLA_HEADS, MLA_V, tm)).astype(BF16)


def _mla_prep(u, gq, wuq, gkv, wk, wv, gqn, gkn, tabs):
    t = u.shape[0]
    tm = ROW_TILE
    nt = t // tm
    const = lambda i: (0, 0)
    rows = lambda i: (i, 0)
    return _Stage(
        kernel=_mla_prep_kernel,
        out_shape=(jax.ShapeDtypeStruct((MLA_HEADS, LANES, t), BF16),
                   jax.ShapeDtypeStruct((MLA_HEADS, nt, tm, LANES), BF16),
                   jax.ShapeDtypeStruct((MLA_HEADS, nt, MLA_V + SUM_ROWS, tm), BF16)),
        in_specs=[pl.BlockSpec((tm, 512), lambda i: (i, U_MLA // 512)),
                  pl.BlockSpec((1, MLA_Q_RANK), const),
                  pl.BlockSpec(wuq.shape, const),
                  pl.BlockSpec((1, MLA_KV_RANK), const),
                  pl.BlockSpec(wk.shape, const),
                  pl.BlockSpec(wv.shape, const),
                  pl.BlockSpec((1, LANES), const),
                  pl.BlockSpec((1, LANES), const),
                  pl.BlockSpec((tm, LANES), rows),
                  pl.BlockSpec((tm, LANES), rows),
                  pl.BlockSpec((tm, LANES), rows)],
        out_specs=(pl.BlockSpec((MLA_HEADS, LANES, tm), lambda i: (0, 0, i)),
                   pl.BlockSpec((MLA_HEADS, 1, tm, LANES), lambda i: (0, i, 0, 0)),
                   pl.BlockSpec((MLA_HEADS, 1, MLA_V + SUM_ROWS, tm), lambda i: (0, i, 0, 0))),
        scratch_shapes=(),
        args=(u, gq, wuq, gkv, wk, wv, gqn, gkn, *tabs))


def _dif_prep_kernel(q_ref, k_ref, v_ref, gq_ref, gk_ref, shift_ref, cos_ref, s1_ref, s2_ref,
                     qt_out, k_out, vt_out):
    cos, s1, s2 = cos_ref[...], s1_ref[...], s2_ref[...]
    half = DIF_HEAD_DIM // 4
    q_scale = DIF_HEAD_DIM ** -0.5 * LOG2E
    n = 2 * DIF_HEADS
    blocks = ([q_ref[:, LANES * a:LANES * (a + 1)] for a in range(n)]
              + [k_ref[:, LANES * a:LANES * (a + 1)] for a in range(n)])
    roped = _norm_rope_blocks(blocks, [gq_ref[...]] * n + [gk_ref[...]] * n, DIF_HEAD_DIM, cos, s1, s2, half)
    spare = lax.broadcasted_iota(jnp.int32, roped[0].shape, 1) == DIF_HEAD_DIM
    for a in range(n):
        qt_out[a] = jnp.where(spare, -shift_ref[...], roped[a] * q_scale).T.astype(BF16)
        k_out[a, 0] = jnp.where(spare, 1.0, roped[n + a]).astype(BF16)
    v = v_ref[...]
    vt_out[:, 0] = _with_sum_rows(v.T.reshape(DIF_HEADS, 2 * DIF_HEAD_DIM, v.shape[0])).astype(BF16)


def _dif_prep(u, gq, gk, shift, tabs):
    t = u.shape[0]
    tm = ROW_TILE
    nt = t // tm
    nsub = 2 * DIF_HEADS
    const = lambda i: (0, 0)
    rows = lambda i: (i, 0)
    return _Stage(
        kernel=_dif_prep_kernel,
        out_shape=(jax.ShapeDtypeStruct((nsub, LANES, t), BF16),
                   jax.ShapeDtypeStruct((nsub, nt, tm, LANES), BF16),
                   jax.ShapeDtypeStruct((DIF_HEADS, nt, 2 * DIF_HEAD_DIM + SUM_ROWS, tm), BF16)),
        in_specs=[pl.BlockSpec((tm, 1024), lambda i: (i, U_DQ // 1024)),
                  pl.BlockSpec((tm, 1024), lambda i: (i, U_DK // 1024)),
                  pl.BlockSpec((tm, 512), lambda i: (i, U_DV // 512)),
                  pl.BlockSpec((1, LANES), const),
                  pl.BlockSpec((1, LANES), const),
                  pl.BlockSpec((1, LANES), const),
                  pl.BlockSpec((tm, LANES), rows),
                  pl.BlockSpec((tm, LANES), rows),
                  pl.BlockSpec((tm, LANES), rows)],
        out_specs=(pl.BlockSpec((nsub, LANES, tm), lambda i: (0, 0, i)),
                   pl.BlockSpec((nsub, 1, tm, LANES), lambda i: (0, i, 0, 0)),
                   pl.BlockSpec((DIF_HEADS, 1, 2 * DIF_HEAD_DIM + SUM_ROWS, tm), lambda i: (0, i, 0, 0))),
        scratch_shapes=(),
        args=(u, u, u, gq, gk, shift, *tabs))


ATTN_GROUP = 2
ATTN_TRIP = 64
SUM_ROWS = 16


def _with_sum_rows(vt):
    heads, _, keys = vt.shape
    row = lax.broadcasted_iota(jnp.int32, (heads, SUM_ROWS, keys), 1)
    return jnp.concatenate([vt, jnp.where(row == 0, 1.0, 0.0).astype(vt.dtype)], axis=1)


def _attn_group(qt_ref, k_ref, vt_ref, v_of_sub, s_buf, p_buf, acc_ref, n_chunks):
    tq = qt_ref.shape[2]
    last = n_chunks - 1

    def scores(c):
        out = []
        for a in range(ATTN_GROUP):
            s = _dot(k_ref[a, c], qt_ref[a])
            out.append((s, jnp.max(s, axis=0, keepdims=True)))
        return out

    def stash(sc, slot):
        for a in range(ATTN_GROUP):
            s_buf[slot, a] = sc[a][0]
        return tuple(mx for (_, mx) in sc)

    def softmax(s_of, mx, slot, ms):
        new_ms, alphas = [], []
        for a in range(ATTN_GROUP):
            m_new = jnp.maximum(ms[a], mx[a])
            alphas.append(jnp.exp2(ms[a] - m_new))
            p_buf[slot, a] = jnp.exp2(s_of(a) - m_new).astype(BF16)
            new_ms.append(m_new)
        return tuple(new_ms), tuple(alphas)

    def values(c, slot, alphas):
        for a in range(ATTN_GROUP):
            acc_ref[a] = alphas[a] * acc_ref[a] + _dot(vt_ref[v_of_sub[a], c], p_buf[slot, a])

    per_trip = math.gcd(last, ATTN_TRIP)

    def trip(t, state):
        ms, alphas, mx0 = state
        c = per_trip * t + 1
        for pair in range(per_trip // 2):
            cur, nxt = pair % 2, 1 - pair % 2
            even = scores(c + 1)
            values(c - 1, 0, alphas)
            ms, alphas = softmax(lambda a: s_buf[cur, a], mx0, 1, ms)
            mx0 = stash(scores(jnp.minimum(c + 2, last)), nxt)
            values(c, 1, alphas)
            ms, alphas = softmax(lambda a: even[a][0], [mx for (_, mx) in even], 0, ms)
            c = c + 2
        return ms, alphas, mx0

    acc_ref[...] = jnp.zeros(acc_ref.shape, F32)
    ms = tuple(jnp.full((1, tq), -jnp.inf, F32) for _ in range(ATTN_GROUP))
    first = scores(0)
    ms, alphas = softmax(lambda a: first[a][0], [mx for (_, mx) in first], 0, ms)
    mx0 = stash(scores(1), 0)
    n_trips = jnp.where(pl.program_id(1) > 0, last // per_trip, 0)
    ms, alphas, _ = lax.fori_loop(0, n_trips, trip, (ms, alphas, mx0))
    values(per_trip * n_trips, 0, alphas)


def _normalised(acc_ref, a):
    dv = acc_ref.shape[1] - SUM_ROWS
    return acc_ref[a, :dv] * (1.0 / acc_ref[a, dv:dv + 1])


PV_SPAN = 32
MAX_SHIFT = 60.0


def _attn_group_shifted(qt_ref, k_ref, vt_ref, v_of_sub, p_buf, acc_ref, n_chunks):
    span = math.gcd(n_chunks - 1, PV_SPAN)

    def accumulate(c0, n, slot):
        for a in range(ATTN_GROUP):
            for i in range(n):
                s = _dot(k_ref[a, c0 + i], qt_ref[a])
                p_buf[slot, a, KEY_CHUNK * i:KEY_CHUNK * (i + 1)] = jnp.exp2(s).astype(BF16)
            vt = jnp.concatenate([vt_ref[v_of_sub[a], c0 + i] for i in range(n)], axis=1)
            acc_ref[a] = acc_ref[a] + _dot(vt, p_buf[slot, a, :KEY_CHUNK * n])

    acc_ref[...] = jnp.zeros(acc_ref.shape, F32)
    accumulate(0, 1, 0)

    @pl.when(pl.program_id(1) > 0)
    def _():
        for j in range((n_chunks - 1) // span):
            accumulate(1 + span * j, span, j % 2)


def _attend(shifted, qt_ref, k_ref, vt_ref, v_of_sub, scratch, n_chunks):
    if shifted:
        p_buf, acc_ref = scratch
        _attn_group_shifted(qt_ref, k_ref, vt_ref, v_of_sub, p_buf, acc_ref, n_chunks)
    else:
        s_buf, p_buf, acc_ref = scratch
        _attn_group(qt_ref, k_ref, vt_ref, v_of_sub, s_buf, p_buf, acc_ref, n_chunks)
    return acc_ref


def _mla_attn_kernel(qt_ref, k_ref, vt_ref, o_ref, *scratch, n_chunks, shifted):
    acc_ref = _attend(shifted, qt_ref, k_ref, vt_ref, tuple(range(ATTN_GROUP)), scratch, n_chunks)
    o = jnp.concatenate([_normalised(acc_ref, a) for a in range(ATTN_GROUP)], axis=0)
    o_ref[...] = o.T


def _dif_attn_kernel(lam_ref, gsub_ref, qt_ref, k_ref, vt_ref, o_ref, *scratch, n_chunks, shifted, lam_init):
    acc_ref = _attend(shifted, qt_ref, k_ref, vt_ref, tuple(a // 2 for a in range(ATTN_GROUP)), scratch,
                      n_chunks)
    lp = lam_ref[...]
    lam = (jnp.exp(jnp.sum(lp[0:1] * lp[1:2], axis=-1, keepdims=True))
           - jnp.exp(jnp.sum(lp[2:3] * lp[3:4], axis=-1, keepdims=True)) + lam_init)
    outs = []
    for h in range(ATTN_GROUP // 2):
        o = _normalised(acc_ref, 2 * h) - lam * _normalised(acc_ref, 2 * h + 1)
        outs.append(o * lax.rsqrt(jnp.mean(o * o, axis=0, keepdims=True) + EPS))
    gsub = gsub_ref[...] * (1.0 - lam_init)
    o_ref[...] = jnp.concatenate(outs, axis=0).T * jnp.concatenate([gsub] * len(outs), axis=1)


def _attention(qt, k, vt, *, shifted=False, lam=None, gsub=None, lam_init=None):
    nsub, _, t = qt.shape
    n_chunks = k.shape[1]
    assert (n_chunks - 1) % 4 == 0 and nsub % ATTN_GROUP == 0
    tq = KEY_CHUNK
    dv = vt.shape[2] - SUM_ROWS
    groups = nsub // ATTN_GROUP
    n_v = vt.shape[0] // groups
    out_w = n_v * dv
    resident = dict(pipeline_mode=pl.Buffered(1))
    specs = [pl.BlockSpec((ATTN_GROUP, LANES, tq), lambda g, i: (g, 0, i)),
             pl.BlockSpec((ATTN_GROUP, n_chunks, KEY_CHUNK, LANES), lambda g, i: (g, 0, 0, 0), **resident),
             pl.BlockSpec((n_v, n_chunks, dv + SUM_ROWS, KEY_CHUNK), lambda g, i: (g, 0, 0, 0), **resident)]
    if lam is None:
        body = functools.partial(_mla_attn_kernel, n_chunks=n_chunks, shifted=shifted)
        args = (qt, k, vt)
    else:
        body = functools.partial(_dif_attn_kernel, n_chunks=n_chunks, shifted=shifted, lam_init=lam_init)
        specs = [pl.BlockSpec((8, LANES), lambda g, i: (0, 0)),
                 pl.BlockSpec((1, LANES), lambda g, i: (0, 0))] + specs
        args = (lam, gsub, qt, k, vt)
    acc = pltpu.VMEM((ATTN_GROUP, dv + SUM_ROWS, tq), F32)
    if shifted:
        span = math.gcd(n_chunks - 1, PV_SPAN)
        scratch = [pltpu.VMEM((2, ATTN_GROUP, span * KEY_CHUNK, tq), BF16), acc]
    else:
        scratch = [pltpu.VMEM((2, ATTN_GROUP, KEY_CHUNK, tq), F32),
                   pltpu.VMEM((2, ATTN_GROUP, KEY_CHUNK, tq), BF16), acc]
    return _pallas_call(
        body,
        out_shape=jax.ShapeDtypeStruct((t, groups * out_w), F32),
        grid=(groups, t // tq),
        in_specs=specs,
        out_specs=pl.BlockSpec((tq, out_w), lambda g, i: (i, g)),
        scratch_shapes=scratch,
        semantics=("parallel", "arbitrary"),
        name=("mla_attention" if lam is None else "dif_attention") + ("_shifted" if shifted else ""),
    )(*args)


def _ssm_prep_kernel(x_ref, prev_ref, next_ref, dt_ref, w_ref, b_ref, dtb_ref, xo_ref, dto_ref, e_ref,
                     *, n_ctx, n_tok, tm):
    row0 = pl.program_id(0) * tm
    pad = SSM_CONV // 2
    has_prev = jnp.logical_and(row0 != 0, row0 != n_ctx)
    has_next = jnp.logical_and(row0 + tm != n_ctx, row0 + tm != n_tok)
    e_ref[0:8] = jnp.where(has_prev, prev_ref[...], 0.0)
    e_ref[8:8 + tm] = x_ref[...]
    e_ref[8 + tm:16 + tm] = jnp.where(has_next, next_ref[...], 0.0)
    acc = jnp.zeros(x_ref.shape, F32) + b_ref[...]
    for k in range(SSM_CONV):
        acc = acc + w_ref[k:k + 1, :] * e_ref[pl.ds(8 - pad + k, tm), :]
    xo_ref[...] = _silu(acc)
    d = dt_ref[...] + dtb_ref[...]
    dto_ref[...] = jnp.maximum(d, 0.0) + jnp.log1p(jnp.exp(-jnp.abs(d)))


def _ssm_prep(u, conv_w, conv_b, dt_bias, n_ctx):
    t = u.shape[0]
    tm = ROW_TILE
    nt = t // tm
    cb = U_XBC // SSM_XBC
    const = lambda i: (0, 0)
    return _Stage(
        kernel=functools.partial(_ssm_prep_kernel, n_ctx=n_ctx, n_tok=t, tm=tm),
        out_shape=(jax.ShapeDtypeStruct((t, SSM_XBC), F32), jax.ShapeDtypeStruct((t, LANES), F32)),
        in_specs=[pl.BlockSpec((tm, SSM_XBC), lambda i: (i, cb)),
                  pl.BlockSpec((8, SSM_XBC), lambda i: (jnp.maximum(i * (tm // 8) - 1, 0), cb)),
                  pl.BlockSpec((8, SSM_XBC), lambda i: (jnp.minimum((i + 1) * (tm // 8), t // 8 - 1), cb)),
                  pl.BlockSpec((tm, LANES), lambda i: (i, U_DT // LANES)),
                  pl.BlockSpec((8, SSM_XBC), const),
                  pl.BlockSpec((1, SSM_XBC), const),
                  pl.BlockSpec((1, LANES), const)],
        out_specs=(pl.BlockSpec((tm, SSM_XBC), lambda i: (i, 0)),
                   pl.BlockSpec((tm, LANES), lambda i: (i, 0))),
        scratch_shapes=(pltpu.VMEM((tm + 16, SSM_XBC), F32),),
        args=(u, u, u, u, conv_w, conv_b, dt_bias))


def _ssd_kernel(xf_ref, dtf_ref, xr_ref, dtr_ref, alog_ref, of_ref, or_ref, hf_ref, hr_ref):
    @pl.when(pl.program_id(0) == 0)
    def _():
        hf_ref[...] = jnp.zeros(hf_ref.shape, F32)
        hr_ref[...] = jnp.zeros(hr_ref.shape, F32)

    lc = SSM_CHUNK
    a_row = -jnp.exp(alog_ref[...])
    n_sub = xf_ref.shape[0] // lc
    fwd = [_ssd_local(xf_ref[lc * j:lc * (j + 1), :], dtf_ref[lc * j:lc * (j + 1), :], a_row, 0)
           for j in range(n_sub)]
    bwd = [_ssd_local(xr_ref[lc * j:lc * (j + 1), :], dtr_ref[lc * j:lc * (j + 1), :], a_row, 1)
           for j in range(n_sub)]
    of_ref[...] = jnp.concatenate([_ssd_carry(part, hf_ref) for part in fwd], axis=0)
    or_ref[...] = jnp.concatenate([_ssd_carry(part, hr_ref) for part in reversed(bwd)][::-1], axis=0)


def _ssd_carry(local, h_ref):
    outs = []
    for h, (y, c_g, from_start, decay, upd) in enumerate(local):
        state = h_ref[h]
        outs.append(y + _dot(c_g, state.astype(BF16)) * from_start)
        h_ref[h] = state * decay + upd
    return jnp.concatenate(outs, axis=1)


def _ssd_local(xbc, dt, a_row, direction):
    lc = SSM_CHUNK
    hd, per_group = SSM_HEAD_DIM, SSM_HEADS // SSM_GROUPS
    dta = dt * a_row
    r = lax.broadcasted_iota(jnp.int32, (lc, lc), 0)
    c = lax.broadcasted_iota(jnp.int32, (lc, lc), 1)
    keep = (r >= c) if direction == 0 else (r <= c)
    tri = jnp.where(keep, 1.0, 0.0).astype(F32)
    cum = jnp.dot(tri, dta, preferred_element_type=F32, precision=lax.Precision.HIGHEST)
    cum_t = cum.T
    dt_t = dt.T
    total = jnp.sum(dta, axis=0, keepdims=True)
    chunk_decay = jnp.exp(total)
    w_t = (dt * jnp.exp(total - cum)).T
    b_f32 = [xbc[:, SSM_WIDTH + SSM_STATE * g:SSM_WIDTH + SSM_STATE * (g + 1)] for g in range(SSM_GROUPS)]
    c_bf = [xbc[:, SSM_WIDTH + SSM_STATE * (SSM_GROUPS + g):SSM_WIDTH + SSM_STATE * (SSM_GROUPS + g + 1)].astype(BF16)
            for g in range(SSM_GROUPS)]
    cb = [_dot_nt(c, b.astype(BF16)) for c, b in zip(c_bf, b_f32)]
    b_t = [b.T for b in b_f32]
    heads = range(SSM_HEADS)
    cols = [direction * SSM_HEADS + h for h in heads]
    cum_b = [jnp.broadcast_to(cum[:, c:c + 1], (lc, lc)) for c in cols]
    seg = [jnp.exp(jnp.where(keep, cb_ - cum_t[c:c + 1, :], -jnp.inf)) for cb_, c in zip(cum_b, cols)]
    x_h = [xbc[:, hd * h:hd * (h + 1)].astype(BF16) for h in heads]
    scores = [(cb[h // per_group] * s * dt_t[c:c + 1, :]).astype(BF16) for h, s, c in zip(heads, seg, cols)]
    weighted_b = [(b_t[h // per_group] * w_t[c:c + 1, :]).astype(BF16) for h, c in zip(heads, cols)]
    y = [_dot(s, x) for s, x in zip(scores, x_h)]
    upd = [_dot(wb, x) for wb, x in zip(weighted_b, x_h)]
    return [(y[h], c_bf[h // per_group], jnp.exp(cum_b[h][:, :hd]), chunk_decay[:, cols[h]:cols[h] + 1], upd[h])
            for h in heads]


SSD_STEP = 2 * SSM_CHUNK


def _ssd(xbc, dt, a_log, n_ctx):
    t = xbc.shape[0]
    lc = SSD_STEP
    assert n_ctx % lc == 0 and t % lc == 0
    nc = t // lc
    ncc = n_ctx // lc
    fwd = lambda s: (s, 0)
    bwd = lambda s: (jnp.where(s < ncc, ncc - 1 - s, nc - 1 - (s - ncc)), 0)
    state = pltpu.VMEM((SSM_HEADS, SSM_STATE, SSM_HEAD_DIM), F32)
    out = jax.ShapeDtypeStruct((t, SSM_WIDTH), F32)
    return _pallas_call(
        _ssd_kernel,
        out_shape=(out, out),
        grid=(nc,),
        in_specs=[pl.BlockSpec((lc, SSM_XBC), fwd), pl.BlockSpec((lc, LANES), fwd),
                  pl.BlockSpec((lc, SSM_XBC), bwd), pl.BlockSpec((lc, LANES), bwd),
                  pl.BlockSpec((1, LANES), lambda s: (0, 0))],
        out_specs=(pl.BlockSpec((lc, SSM_WIDTH), fwd), pl.BlockSpec((lc, SSM_WIDTH), bwd)),
        scratch_shapes=[state, state],
        semantics=("arbitrary",),
        name="ssd_scan",
    )(xbc, dt, xbc, dt, a_log)


def _merge_kernel(x_ref, gate_ref, ya_ref, yb_ref, yf_ref, yr_ref, xs_ref, z_ref,
                  bg_ref, dskip_ref, gssm_ref, wa_ref, wb_ref, wc_ref, wo_ref,
                  ml_ref, mc_ref, g2_ref, wr_ref, br_ref,
                  xo_ref, f_ref, lg_ref, *, n_ctx, tm):
    y = (yf_ref[...] + yr_ref[...] + dskip_ref[...] * xs_ref[...]) * _silu(z_ref[...])
    gw = SSM_WIDTH // SSM_GROUPS
    yc = jnp.concatenate([_rms_rows(y[:, gw * g:gw * (g + 1)], gw) for g in range(SSM_GROUPS)], axis=1)
    yc = yc * gssm_ref[...]
    gate = _sigmoid(gate_ref[...] + bg_ref[...])
    m = (gate[:, :D_MODEL] * _dot(ya_ref[...].astype(BF16), wa_ref[...])
         + gate[:, D_MODEL:2 * D_MODEL] * _dot(yb_ref[...].astype(BF16), wb_ref[...])
         + gate[:, 2 * D_MODEL:] * _dot(yc.astype(BF16), wc_ref[...]))
    out = _dot(m.astype(BF16), wo_ref[...])
    row0 = pl.program_id(0) * tm
    is_ctx = (row0 + lax.broadcasted_iota(jnp.int32, (tm, 1), 0)) < n_ctx
    gt1 = jnp.where(is_ctx, mc_ref[2:3, :], ml_ref[2:3, :])
    x_new = x_ref[...] + gt1 * out
    xo_ref[...] = x_new
    f = _modulated_norm(x_new, g2_ref[...], ml_ref[3:5, :], mc_ref[3:5, :], row0, n_ctx)
    f_ref[...] = f
    lg_ref[...] = _dot(f.astype(BF16), wr_ref[...]) + br_ref[...]


def _merge(x_all, u, ya, yb, yf, yr, xbc, b_gate, dskip, g_ssm, wa, wb, wc, wo, mod_lat, mod_ctx,
           g2, w_router, b_router, n_ctx):
    t, d = x_all.shape
    tm = ROW_TILE
    const = lambda i: (0, 0)
    rows = lambda i: (i, 0)
    full = lambda a: pl.BlockSpec(a.shape, const)
    return _pallas_call(
        functools.partial(_merge_kernel, n_ctx=n_ctx, tm=tm),
        out_shape=(jax.ShapeDtypeStruct((t, d), F32), jax.ShapeDtypeStruct((t, d), F32),
                   jax.ShapeDtypeStruct((t, LANES), F32)),
        grid=(t // tm,),
        in_specs=[pl.BlockSpec((tm, d), rows),
                  pl.BlockSpec((tm, GATE_COLS), lambda i: (i, U_GATE // GATE_COLS)),
                  pl.BlockSpec((tm, MLA_WIDTH), rows),
                  pl.BlockSpec((tm, DIF_WIDTH), rows),
                  pl.BlockSpec((tm, SSM_WIDTH), rows),
                  pl.BlockSpec((tm, SSM_WIDTH), rows),
                  pl.BlockSpec((tm, SSM_WIDTH), rows),
                  pl.BlockSpec((tm, SSM_WIDTH), lambda i: (i, U_Z // SSM_WIDTH)),
                  full(b_gate), full(dskip), full(g_ssm), full(wa), full(wb), full(wc), full(wo),
                  full(mod_lat), full(mod_ctx), full(g2), full(w_router), full(b_router)],
        out_specs=(pl.BlockSpec((tm, d), rows), pl.BlockSpec((tm, d), rows),
                   pl.BlockSpec((tm, LANES), rows)),
        semantics=("parallel",),
        name="merge",
    )(x_all, u, ya, yb, yf, yr, xbc, u, b_gate, dskip, g_ssm, wa, wb, wc, wo, mod_lat, mod_ctx,
      g2, w_router, b_router)


def _moe_kernel(be_ref, nb_ref, x_ref, wgu_ref, bgu_ref, wd_ref, bd_ref, o_ref, wgu_s, wd_s):
    b = pl.program_id(0)
    prev = be_ref[jnp.maximum(b - 1, 0)]
    fresh = jnp.logical_or(b == 0, be_ref[b] != prev)

    @pl.when(fresh)
    def _():
        wgu_s[...] = wgu_ref[...].astype(BF16)
        wd_s[...] = wd_ref[...].astype(BF16)

    @pl.when(b < nb_ref[0])
    def _():
        gu = _dot(x_ref[...].astype(BF16), wgu_s[...]) + bgu_ref[...]
        glu = jnp.minimum(gu[:, :D_FF], SWIGLU_LIMIT)
        lin = jnp.clip(gu[:, D_FF:], -SWIGLU_LIMIT, SWIGLU_LIMIT)
        act = glu * _sigmoid(SWIGLU_ALPHA * glu) * (lin + 1.0)
        o_ref[...] = _dot(act.astype(BF16), wd_s[...]) + bd_ref[...]

    @pl.when(b >= nb_ref[0])
    def _():
        o_ref[...] = jnp.zeros(o_ref.shape, F32)


def _moe_experts(block_e, n_used, x_sorted, w_gu, b_gu, w_down, b_down, layer):
    n_slots, d = x_sorted.shape
    n_blocks = n_slots // MOE_BLOCK
    by_expert = lambda b, be, nb: (layer, be[b], 0, 0)
    return _pallas_call(
        _moe_kernel,
        out_shape=jax.ShapeDtypeStruct((n_slots, d), F32),
        num_scalar_prefetch=2,
        grid=(n_blocks,),
        in_specs=[pl.BlockSpec((MOE_BLOCK, d), lambda b, be, nb: (b, 0)),
                  pl.BlockSpec((None, None, d, 2 * D_FF), by_expert),
                  pl.BlockSpec((None, None, 1, 2 * D_FF), by_expert),
                  pl.BlockSpec((None, None, D_FF, d), by_expert),
                  pl.BlockSpec((None, None, 1, d), by_expert)],
        out_specs=pl.BlockSpec((MOE_BLOCK, d), lambda b, be, nb: (b, 0)),
        scratch_shapes=[pltpu.VMEM((d, 2 * D_FF), BF16), pltpu.VMEM((D_FF, d), BF16)],
        semantics=("arbitrary",),
        name="moe_experts",
    )(block_e, n_used, x_sorted, w_gu, b_gu, w_down, b_down)


def _combine_kernel(x_ref, g_ref, ml_ref, mc_ref, *rest, n_ctx, tm):
    y_refs, o_ref = rest[:TOP_K], rest[TOP_K]
    is_ctx = (pl.program_id(0) * tm + lax.broadcasted_iota(jnp.int32, (tm, 1), 0)) < n_ctx
    gate2 = jnp.where(is_ctx, mc_ref[...], ml_ref[...])
    g = g_ref[...]
    y = g[:, 0:1] * y_refs[0][...]
    for k in range(1, TOP_K):
        y = y + g[:, k:k + 1] * y_refs[k][...]
    o_ref[...] = x_ref[...] + gate2 * y


def _combine(x_all, y_rows, gates, gate2_lat, gate2_ctx, n_ctx):
    t, d = x_all.shape
    tm = ROW_TILE
    rows = lambda i: (i, 0)
    const = lambda i: (0, 0)
    return _pallas_call(
        functools.partial(_combine_kernel, n_ctx=n_ctx, tm=tm),
        out_shape=jax.ShapeDtypeStruct((t, d), F32),
        grid=(t // tm,),
        in_specs=[pl.BlockSpec((tm, d), rows), pl.BlockSpec((tm, LANES), rows),
                  pl.BlockSpec((1, d), const), pl.BlockSpec((1, d), const)]
                 + [pl.BlockSpec((tm, d), rows)] * TOP_K,
        out_specs=pl.BlockSpec((tm, d), rows),
        semantics=("parallel",),
        name="moe_combine",
    )(x_all, gates, gate2_lat, gate2_ctx, *y_rows)


ROUTE_IDX, ROUTE_GATE, ROUTE_RANK = 0, TOP_K, 2 * TOP_K


def _router_kernel(lg_ref, o_ref, cnt_ref, run_ref, *, tm):
    @pl.when(pl.program_id(0) == 0)
    def _():
        run_ref[...] = jnp.zeros(run_ref.shape, F32)

    lane = lax.broadcasted_iota(jnp.int32, (tm, LANES), 1)
    lane_f = lane.astype(F32)
    lg = jnp.where(lane < N_EXPERTS, lg_ref[...], -jnp.inf)
    hots, vals = [], []
    for _ in range(TOP_K):
        mx = jnp.max(lg, axis=-1, keepdims=True)
        idx = jnp.min(jnp.where(lg == mx, lane_f, float(LANES)), axis=-1, keepdims=True)
        hot = lane_f == idx
        lg = jnp.where(hot, -jnp.inf, lg)
        hots.append((hot, idx))
        vals.append(mx)
    exps = [jnp.exp(v - vals[0]) for v in vals]
    inv = 1.0 / sum(exps[1:], exps[0])
    chosen = jnp.zeros((tm, LANES), F32)
    for hot, _ in hots:
        chosen = jnp.where(hot, 1.0, chosen)
    r = lax.broadcasted_iota(jnp.int32, (tm, tm), 0)
    c = lax.broadcasted_iota(jnp.int32, (tm, tm), 1)
    earlier = jnp.where(r > c, 1.0, 0.0).astype(BF16)
    before = _dot(earlier, chosen.astype(BF16)) + run_ref[0:1, :]
    out = jnp.zeros((tm, LANES), F32)
    for k, (hot, idx) in enumerate(hots):
        rank = jnp.sum(jnp.where(hot, before, 0.0), axis=-1, keepdims=True)
        out = jnp.where(lane == ROUTE_IDX + k, idx, out)
        out = jnp.where(lane == ROUTE_GATE + k, exps[k] * inv, out)
        out = jnp.where(lane == ROUTE_RANK + k, rank, out)
    o_ref[...] = out
    run_ref[...] = run_ref[...] + jnp.sum(chosen, axis=0, keepdims=True)
    cnt_ref[...] = run_ref[...]


def _router(logits):
    t = logits.shape[0]
    tm = ROW_TILE
    return _pallas_call(
        functools.partial(_router_kernel, tm=tm),
        out_shape=(jax.ShapeDtypeStruct((t, LANES), F32), jax.ShapeDtypeStruct((8, LANES), F32)),
        grid=(t // tm,),
        in_specs=[pl.BlockSpec((tm, LANES), lambda i: (i, 0))],
        out_specs=(pl.BlockSpec((tm, LANES), lambda i: (i, 0)), pl.BlockSpec((8, LANES), lambda i: (0, 0))),
        scratch_shapes=[pltpu.VMEM((8, LANES), F32)],
        semantics=("arbitrary",),
        name="moe_router",
    )(logits)


def _route(logits, n_tok):
    routed, counts = _router(logits)
    top_idx = routed[:, ROUTE_IDX:ROUTE_IDX + TOP_K].astype(jnp.int32)
    gates = routed[:, ROUTE_GATE:ROUTE_GATE + TOP_K]
    rank = routed[:, ROUTE_RANK:ROUTE_RANK + TOP_K].astype(jnp.int32)
    n_assign = n_tok * TOP_K
    counts = counts[0, :N_EXPERTS].astype(jnp.int32)
    padded = (counts + MOE_BLOCK - 1) // MOE_BLOCK * MOE_BLOCK
    pad_end = jnp.cumsum(padded)
    pad_start = pad_end - padded
    slot_of = pad_start[top_idx] + rank
    n_blocks = -(-(n_assign + N_EXPERTS * (MOE_BLOCK - 1)) // MOE_BLOCK)
    n_slots = n_blocks * MOE_BLOCK
    token = jnp.arange(n_assign, dtype=jnp.int32) // TOP_K
    slot_tok = jnp.zeros((n_slots,), jnp.int32).at[slot_of.reshape(-1)].set(token, unique_indices=True)
    block_start = jnp.arange(n_blocks, dtype=jnp.int32) * MOE_BLOCK
    block_e = jnp.minimum(jnp.searchsorted(pad_end, block_start, side='right'), N_EXPERTS - 1)
    n_used = (pad_end[-1] // MOE_BLOCK).reshape(1)
    return gates, slot_tok, slot_of, block_e.astype(jnp.int32), n_used.astype(jnp.int32)


def kernel(x, c, ctx, c_ctx, w_mod, b_mod, g_norm1, g_norm2, w_in, b_gate, mla_g_q, mla_w_uq, mla_g_kv, mla_w_ukv, mla_g_qn, mla_g_kn, dif_g_qn, dif_g_kn, dif_lambda, dif_g_sub, ssm_conv_w, ssm_conv_b, ssm_dt_bias, ssm_a_log, ssm_d, ssm_g_norm, w_up_mla, w_up_dif, w_up_ssm, w_out, moe_w_router, moe_b_router, moe_w_gu, moe_b_gu, moe_w_down, moe_b_down):
    assert x.shape[0] == 1 and ctx.shape[0] == 1
    depth = w_in.shape[0]
    seq = x.shape[1]
    n_ctx = ctx.shape[1]
    n_tok = n_ctx + seq
    d = D_MODEL
    assert n_ctx == KEY_CHUNK and n_tok % ROW_TILE == 0 and seq % GRID_W == 0

    x_all = jnp.concatenate([ctx[0], x[0]], axis=0)
    cc = jnp.zeros((8, d), F32).at[0].set(c[0]).at[1].set(c_ctx)
    mod = _mod_vectors(cc, w_mod, b_mod)
    mod = mod[:, :2].reshape(depth, 2, 6, d)

    rope_mla = _rope_tables(seq, n_ctx, MLA_ROPE, MLA_NOPE)
    rope_dif = _rope_tables(seq, n_ctx, DIF_HEAD_DIM, 0)
    w_in_all = _take_columns(w_in, _in_proj_columns()).astype(BF16)
    w_uq_all = _take_columns(mla_w_uq, _head_columns(MLA_HEADS, MLA_QK, 0, MLA_QK, LANES)).astype(BF16)
    w_uk_all = _take_columns(mla_w_ukv, _head_columns(MLA_HEADS, MLA_NOPE + MLA_V, 0, MLA_NOPE, LANES)).astype(BF16)
    w_uv_all = _take_columns(mla_w_ukv, _head_columns(MLA_HEADS, MLA_NOPE + MLA_V, MLA_NOPE, MLA_V, MLA_V)).astype(BF16)
    w_up_all = [w.astype(BF16) for w in (w_up_mla, w_up_dif, w_up_ssm, w_out)]

    for i in range(depth):
        lam_init = 0.8 - 0.6 * math.exp(-0.3 * i)
        mod_lat, mod_ctx = mod[i, 0], mod[i, 1]
        u = _in_proj(x_all, g_norm1[i][None], mod_lat[0:2], mod_ctx[0:2], w_in_all, i, n_ctx)

        bound = (DIF_HEAD_DIM ** 0.5 * LOG2E) * jnp.max(jnp.abs(dif_g_qn[i])) * jnp.max(jnp.abs(dif_g_kn[i]))
        conv_w = jnp.zeros((8, SSM_XBC), F32).at[:SSM_CONV].set(ssm_conv_w[i])
        mla_qkv, dif_qkv, (xbc, dt) = _run_stages("branch_prep", (n_tok // ROW_TILE,), [
            _mla_prep(u, mla_g_q[i][None], w_uq_all[i], mla_g_kv[i][None], w_uk_all[i], w_uv_all[i],
                      _pad_lanes(mla_g_qn[i][None]), _pad_lanes(mla_g_kn[i][None]), rope_mla),
            _dif_prep(u, _pad_lanes(dif_g_qn[i][None]), _pad_lanes(dif_g_kn[i][None]),
                      jnp.full((1, LANES), bound, F32), rope_dif),
            _ssm_prep(u, conv_w, ssm_conv_b[i][None], _pad_lanes(ssm_dt_bias[i].reshape(1, -1)), n_ctx)])
        ya = _attention(*mla_qkv)

        lam_rows = jnp.zeros((8, LANES), F32).at[:4, :DIF_HEAD_DIM].set(dif_lambda[i])
        dif_attention = functools.partial(_attention, lam=lam_rows, gsub=dif_g_sub[i][None], lam_init=lam_init)
        yb = lax.cond(bound <= MAX_SHIFT, functools.partial(dif_attention, shifted=True),
                      functools.partial(dif_attention, shifted=False), *dif_qkv)

        a_log = _pad_lanes(ssm_a_log[i].reshape(1, -1))
        yf, yr = _ssd(xbc, dt, a_log, n_ctx)

        dskip = jnp.repeat(ssm_d[i, 0] + ssm_d[i, 1], SSM_HEAD_DIM)[None]
        w_router = jnp.zeros((d, LANES), BF16).at[:, :N_EXPERTS].set(moe_w_router[i].astype(BF16))
        b_router = jnp.zeros((1, LANES), F32).at[0, :N_EXPERTS].set(moe_b_router[i])
        x_all, f, logits = _merge(
            x_all, u, ya, yb, yf, yr, xbc, b_gate[i][None], dskip, ssm_g_norm[i][None],
            w_up_all[0][i], w_up_all[1][i], w_up_all[2][i], w_up_all[3][i],
            mod_lat[0:5], mod_ctx[0:5], g_norm2[i][None],
            w_router, b_router, n_ctx)

        gates, slot_tok, slot_of, block_e, n_used = _route(logits, n_tok)
        y_slots = _moe_experts(block_e, n_used, f[slot_tok], moe_w_gu,
                               moe_b_gu.reshape(depth, N_EXPERTS, 1, 2 * D_FF), moe_w_down,
                               moe_b_down.reshape(depth, N_EXPERTS, 1, d), i)
        y_rows = [y_slots[slot_of[:, k]] for k in range(TOP_K)]
        x_all = _combine(x_all, y_rows, _pad_lanes(gates), mod_lat[5:6], mod_ctx[5:6], n_ctx)
    return x_all[n_ctx:][None]
```

```python
import collections
import functools
import math

import numpy as np
import jax
import jax.numpy as jnp
from jax import lax
from jax.experimental import pallas as pl
from jax.experimental.pallas import tpu as pltpu

F32 = jnp.float32
BF16 = jnp.bfloat16
LANES = 128
VMEM_REQUEST_CAP = 56 * 1024 * 1024
VMEM_TEMPORARIES = 16 * 1024 * 1024

D_MODEL = 1024
EPS = 1e-6
ROPE_THETA = 10000.0
GRID_W = 64
N_BRANCH = 3

MLA_HEADS = 8
MLA_Q_RANK = 256
MLA_KV_RANK = 128
MLA_NOPE = 64
MLA_ROPE = 32
MLA_V = 64
MLA_QK = MLA_NOPE + MLA_ROPE
MLA_WIDTH = MLA_HEADS * MLA_V

DIF_HEADS = 4
DIF_HEAD_DIM = 64
DIF_WIDTH = DIF_HEADS * 2 * DIF_HEAD_DIM

SSM_HEADS = 8
SSM_HEAD_DIM = 64
SSM_WIDTH = SSM_HEADS * SSM_HEAD_DIM
SSM_GROUPS = 2
SSM_STATE = 128
SSM_CONV = 5
SSM_CHUNK = 128
SSM_XBC = SSM_WIDTH + 2 * SSM_GROUPS * SSM_STATE

N_EXPERTS = 32
TOP_K = 4
D_FF = 1024
SWIGLU_LIMIT = 7.0
SWIGLU_ALPHA = 1.702
MOE_BLOCK = 512

MLA_COLS = MLA_Q_RANK + MLA_KV_RANK + MLA_ROPE
DIF_COLS = 3 * DIF_WIDTH
SSM_COLS = SSM_WIDTH + SSM_XBC + 2 * SSM_HEADS
GATE_COLS = N_BRANCH * D_MODEL

ROW_TILE = 256
KEY_CHUNK = 256
LOG2E = 1.4426950408889634

U_GATE, U_DQ, U_DK, U_XBC, U_MLA, U_DV, U_Z, U_DT = 0, 3072, 4096, 5120, 6144, 6656, 7168, 7680
U_COLS = 8192
U_TILE_N = 2048


def _in_proj_columns():
    src = np.full((U_COLS,), -1, np.int64)
    dif0 = MLA_COLS
    ssm0 = MLA_COLS + DIF_COLS
    gate0 = ssm0 + SSM_COLS
    src[U_GATE:U_GATE + GATE_COLS] = gate0 + np.arange(GATE_COLS)
    for a in range(2 * DIF_HEADS):
        src[U_DQ + LANES * a:U_DQ + LANES * a + DIF_HEAD_DIM] = dif0 + DIF_HEAD_DIM * a + np.arange(DIF_HEAD_DIM)
        src[U_DK + LANES * a:U_DK + LANES * a + DIF_HEAD_DIM] = (dif0 + DIF_WIDTH + DIF_HEAD_DIM * a
                                                                  + np.arange(DIF_HEAD_DIM))
    src[U_DV:U_DV + DIF_WIDTH] = dif0 + 2 * DIF_WIDTH + np.arange(DIF_WIDTH)
    src[U_MLA:U_MLA + MLA_Q_RANK + MLA_KV_RANK] = np.arange(MLA_Q_RANK + MLA_KV_RANK)
    pe0 = U_MLA + MLA_Q_RANK + MLA_KV_RANK + MLA_NOPE
    src[pe0:pe0 + MLA_ROPE] = MLA_Q_RANK + MLA_KV_RANK + np.arange(MLA_ROPE)
    src[U_Z:U_Z + SSM_WIDTH] = ssm0 + np.arange(SSM_WIDTH)
    src[U_XBC:U_XBC + SSM_XBC] = ssm0 + SSM_WIDTH + np.arange(SSM_XBC)
    src[U_DT:U_DT + 2 * SSM_HEADS] = ssm0 + SSM_WIDTH + SSM_XBC + np.arange(2 * SSM_HEADS)
    return src


def _take_columns(w, src):
    pieces, start = [], 0
    for stop in range(1, len(src) + 1):
        same_run = stop < len(src) and (src[stop] == src[stop - 1] + 1 if src[stop - 1] >= 0 else src[stop] < 0)
        if not same_run:
            if src[start] < 0:
                pieces.append(jnp.zeros(w.shape[:-1] + (stop - start,), w.dtype))
            else:
                pieces.append(w[..., src[start]:src[start] + stop - start])
            start = stop
    return jnp.concatenate(pieces, axis=-1)


def _head_columns(n_heads, src_stride, src_off, width, dst_stride):
    src = np.full((n_heads * dst_stride,), -1, np.int64)
    for h in range(n_heads):
        src[h * dst_stride:h * dst_stride + width] = h * src_stride + src_off + np.arange(width)
    return src


def _pad_lanes(v, n=LANES):
    return jnp.pad(v, [(0, 0)] * (v.ndim - 1) + [(0, n - v.shape[-1])])


def _row_tile(n, cap):
    best = 8
    for t in range(8, cap + 1, 8):
        if n % t == 0:
            best = t
    return best


def _dot(a, b):
    return jnp.dot(a, b, preferred_element_type=F32)


def _dot_nt(a, b):
    return lax.dot_general(a, b, (((1,), (1,)), ((), ())), preferred_element_type=F32)


def _sigmoid(x):
    return 1.0 / (1.0 + jnp.exp(-x))


def _silu(x):
    return x * _sigmoid(x)


def _rms_rows(x, n):
    return x * lax.rsqrt(jnp.sum(x * x, axis=-1, keepdims=True) * (1.0 / n) + EPS)


def _as_tuple(x):
    return tuple(x) if isinstance(x, (tuple, list)) else (x,)


def _window_bytes(spec, dtype):
    if spec.block_shape is None:
        return 0
    dims = [1 if n is None else n for n in spec.block_shape]
    copies = 2 if spec.pipeline_mode is None else spec.pipeline_mode.buffer_count
    return math.prod(dims) * jnp.dtype(dtype).itemsize * copies


def _pallas_call(kernel, *, name, out_shape, grid, in_specs, out_specs, semantics, scratch_shapes=(),
                 num_scalar_prefetch=0):
    def call(*args):
        need = VMEM_TEMPORARIES
        need += sum(_window_bytes(s, a.dtype) for s, a in zip(in_specs, args[num_scalar_prefetch:]))
        need += sum(_window_bytes(s, o.dtype) for s, o in zip(_as_tuple(out_specs), _as_tuple(out_shape)))
        need += sum(math.prod(b.shape) * jnp.dtype(b.dtype).itemsize for b in scratch_shapes)
        params = pltpu.CompilerParams(dimension_semantics=semantics,
                                      vmem_limit_bytes=min(need, VMEM_REQUEST_CAP))
        grid_spec = pltpu.PrefetchScalarGridSpec(
            num_scalar_prefetch=num_scalar_prefetch, grid=grid, in_specs=list(in_specs), out_specs=out_specs,
            scratch_shapes=list(scratch_shapes))
        launch = pl.pallas_call(kernel, out_shape=out_shape, grid_spec=grid_spec, compiler_params=params, name=name)
        return launch(*args)
    return call


_Stage = collections.namedtuple("_Stage", "kernel out_shape in_specs out_specs scratch_shapes args")


def _run_stages(name, grid, stages):
    n_in = [len(s.in_specs) for s in stages]
    n_out = [len(s.out_shape) for s in stages]
    n_scr = [len(s.scratch_shapes) for s in stages]

    def body(*refs):
        ins, outs, scr = refs[:sum(n_in)], refs[sum(n_in):sum(n_in) + sum(n_out)], refs[sum(n_in) + sum(n_out):]
        for j, stage in enumerate(stages):
            stage.kernel(*ins[sum(n_in[:j]):sum(n_in[:j + 1])], *outs[sum(n_out[:j]):sum(n_out[:j + 1])],
                         *scr[sum(n_scr[:j]):sum(n_scr[:j + 1])])

    flat = lambda field: [x for s in stages for x in getattr(s, field)]
    outs = _pallas_call(body, name=name, grid=grid, out_shape=tuple(flat("out_shape")),
                        in_specs=flat("in_specs"), out_specs=tuple(flat("out_specs")),
                        scratch_shapes=flat("scratch_shapes"), semantics=("parallel",))(*flat("args"))
    return [outs[sum(n_out[:j]):sum(n_out[:j + 1])] for j in range(len(stages))]


def _mod_kernel(a_ref, w_ref, b_ref, o_ref):
    a = _silu(a_ref[...]).astype(BF16)
    o_ref[0] = _dot(a, w_ref[0].astype(BF16)) + b_ref[0]


def _mod_vectors(cc, w_mod, b_mod):
    depth, d, n = w_mod.shape
    tn = 1536
    return _pallas_call(
        _mod_kernel,
        out_shape=jax.ShapeDtypeStruct((depth, 8, n), F32),
        grid=(depth, n // tn),
        in_specs=[pl.BlockSpec((8, d), lambda l, j: (0, 0)),
                  pl.BlockSpec((1, d, tn), lambda l, j: (l, 0, j)),
                  pl.BlockSpec((1, 1, tn), lambda l, j: (l, 0, j))],
        out_specs=pl.BlockSpec((1, 8, tn), lambda l, j: (l, 0, j)),
        semantics=("parallel", "parallel"),
        name="mod_vectors",
    )(cc, w_mod, b_mod.reshape(depth, 1, n))


def _modulated_norm(x, g, mod_lat, mod_ctx, row0, n_ctx):
    rows = x.shape[0]
    is_ctx = (row0 + lax.broadcasted_iota(jnp.int32, (rows, 1), 0)) < n_ctx
    shift = jnp.where(is_ctx, mod_ctx[0:1, :], mod_lat[0:1, :])
    scale = jnp.where(is_ctx, mod_ctx[1:2, :], mod_lat[1:2, :])
    return _rms_rows(x, x.shape[1]) * g * (1.0 + scale) + shift


def _in_proj_kernel(x_ref, g_ref, ml_ref, mc_ref, w_ref, o_ref, h_ref, *, n_ctx, tm):
    @pl.when(pl.program_id(1) == 0)
    def _():
        h = _modulated_norm(x_ref[...], g_ref[...], ml_ref[...], mc_ref[...], pl.program_id(0) * tm, n_ctx)
        h_ref[...] = h.astype(BF16)

    o_ref[...] = _dot(h_ref[...], w_ref[...])


def _in_proj(x_all, g, mod_lat, mod_ctx, w, layer, n_ctx):
    t, d = x_all.shape
    n = w.shape[2]
    tm = _row_tile(t, 1280)
    return _pallas_call(
        functools.partial(_in_proj_kernel, n_ctx=n_ctx, tm=tm),
        out_shape=jax.ShapeDtypeStruct((t, n), F32),
        grid=(t // tm, n // U_TILE_N),
        in_specs=[pl.BlockSpec((tm, d), lambda i, j: (i, 0)),
                  pl.BlockSpec((1, d), lambda i, j: (0, 0)),
                  pl.BlockSpec((2, d), lambda i, j: (0, 0)),
                  pl.BlockSpec((2, d), lambda i, j: (0, 0)),
                  pl.BlockSpec((None, d, U_TILE_N), lambda i, j: (layer, 0, j))],
        out_specs=pl.BlockSpec((tm, U_TILE_N), lambda i, j: (i, j)),
        scratch_shapes=[pltpu.VMEM((tm, d), BF16)],
        semantics=("parallel", "arbitrary"),
        name="in_proj",
    )(x_all, g, mod_lat, mod_ctx, w)


def _rope_tables(seq_len, n_ctx, rot_dim, lane0):
    n_rows = seq_len // GRID_W
    row = jnp.repeat(jnp.arange(n_rows), GRID_W).astype(F32)
    col = jnp.tile(jnp.arange(GRID_W), n_rows).astype(F32)
    axis_dim = rot_dim // 2
    half = axis_dim // 2
    inv = ROPE_THETA ** (-jnp.arange(0, axis_dim, 2, dtype=F32) / axis_dim)
    ang_r = row[:, None] * inv
    ang_c = col[:, None] * inv
    zeros = jnp.zeros((seq_len, half), F32)
    cos = jnp.concatenate([jnp.cos(ang_r), jnp.cos(ang_r), jnp.cos(ang_c), jnp.cos(ang_c)], axis=1)
    s1 = jnp.concatenate([zeros, jnp.sin(ang_r), zeros, jnp.sin(ang_c)], axis=1)
    s2 = jnp.concatenate([-jnp.sin(ang_r), zeros, -jnp.sin(ang_c), zeros], axis=1)

    def place(tab, fill):
        full = jnp.full((seq_len, LANES), fill, F32).at[:, lane0:lane0 + rot_dim].set(tab)
        ctx = jnp.full((n_ctx, LANES), fill, F32)
        return jnp.concatenate([ctx, full], axis=0)

    return place(cos, 1.0), place(s1, 0.0), place(s2, 0.0)


def _norm_rope_blocks(blocks, gains, n, cos, s1, s2, half):
    sums = [jnp.sum(x * x, axis=-1, keepdims=True) for x in blocks]
    inv = [lax.rsqrt(s * (1.0 / n) + EPS) for s in sums]
    normed = [x * r * g for x, r, g in zip(blocks, inv, gains)]
    fwd = [pltpu.roll(y, half, 1) for y in normed]
    bwd = [pltpu.roll(y, LANES - half, 1) for y in normed]
    return [y * cos + a * s1 + b * s2 for y, a, b in zip(normed, fwd, bwd)]


def _mla_prep_kernel(u_ref, gq_ref, wuq_ref, gkv_ref, wk_ref, wv_ref, gqn_ref, gkn_ref,
                     cos_ref, s1_ref, s2_ref, qt_ref, k_ref, vt_ref):
    u = u_ref[...]
    cq = u[:, :MLA_Q_RANK]
    ckv = u[:, MLA_Q_RANK:MLA_Q_RANK + MLA_KV_RANK]
    pe = u[:, MLA_Q_RANK + MLA_KV_RANK:]
    q = _dot((_rms_rows(cq, MLA_Q_RANK) * gq_ref[...]).astype(BF16), wuq_ref[...])
    kv_in = (_rms_rows(ckv, MLA_KV_RANK) * gkv_ref[...]).astype(BF16)
    kn = _dot(kv_in, wk_ref[...])
    v = _dot(kv_in, wv_ref[...])
    cos, s1, s2 = cos_ref[...], s1_ref[...], s2_ref[...]
    half = MLA_ROPE // 4
    q_scale = MLA_QK ** -0.5 * LOG2E
    n = MLA_HEADS
    blocks = ([q[:, LANES * h:LANES * (h + 1)] for h in range(n)]
              + [kn[:, LANES * h:LANES * (h + 1)] + pe for h in range(n)])
    roped = _norm_rope_blocks(blocks, [gqn_ref[...]] * n + [gkn_ref[...]] * n, MLA_QK, cos, s1, s2, half)
    for h in range(n):
        qt_ref[h] = (roped[h] * q_scale).T.astype(BF16)
        k_ref[h, 0] = roped[n + h].astype(BF16)
    tm = v.shape[0]
    vt_ref[:, 0] = _with_sum_rows(v.T.reshape(MLA_HEADS, MLA_V, tm)).astype(BF16)


def _mla_prep(u, gq, wuq, gkv, wk, wv, gqn, gkn, tabs):
    t = u.shape[0]
    tm = ROW_TILE
    nt = t // tm
    const = lambda i: (0, 0)
    rows = lambda i: (i, 0)
    return _Stage(
        kernel=_mla_prep_kernel,
        out_shape=(jax.ShapeDtypeStruct((MLA_HEADS, LANES, t), BF16),
                   jax.ShapeDtypeStruct((MLA_HEADS, nt, tm, LANES), BF16),
                   jax.ShapeDtypeStruct((MLA_HEADS, nt, MLA_V + SUM_ROWS, tm), BF16)),
        in_specs=[pl.BlockSpec((tm, 512), lambda i: (i, U_MLA // 512)),
                  pl.BlockSpec((1, MLA_Q_RANK), const),
                  pl.BlockSpec(wuq.shape, const),
                  pl.BlockSpec((1, MLA_KV_RANK), const),
                  pl.BlockSpec(wk.shape, const),
                  pl.BlockSpec(wv.shape, const),
                  pl.BlockSpec((1, LANES), const),
                  pl.BlockSpec((1, LANES), const),
                  pl.BlockSpec((tm, LANES), rows),
                  pl.BlockSpec((tm, LANES), rows),
                  pl.BlockSpec((tm, LANES), rows)],
        out_specs=(pl.BlockSpec((MLA_HEADS, LANES, tm), lambda i: (0, 0, i)),
                   pl.BlockSpec((MLA_HEADS, 1, tm, LANES), lambda i: (0, i, 0, 0)),
                   pl.BlockSpec((MLA_HEADS, 1, MLA_V + SUM_ROWS, tm), lambda i: (0, i, 0, 0))),
        scratch_shapes=(),
        args=(u, gq, wuq, gkv, wk, wv, gqn, gkn, *tabs))


def _dif_prep_kernel(q_ref, k_ref, v_ref, gq_ref, gk_ref, shift_ref, cos_ref, s1_ref, s2_ref,
                     qt_out, k_out, vt_out):
    cos, s1, s2 = cos_ref[...], s1_ref[...], s2_ref[...]
    half = DIF_HEAD_DIM // 4
    q_scale = DIF_HEAD_DIM ** -0.5 * LOG2E
    n = 2 * DIF_HEADS
    blocks = ([q_ref[:, LANES * a:LANES * (a + 1)] for a in range(n)]
              + [k_ref[:, LANES * a:LANES * (a + 1)] for a in range(n)])
    roped = _norm_rope_blocks(blocks, [gq_ref[...]] * n + [gk_ref[...]] * n, DIF_HEAD_DIM, cos, s1, s2, half)
    spare = lax.broadcasted_iota(jnp.int32, roped[0].shape, 1) == DIF_HEAD_DIM
    for a in range(n):
        qt_out[a] = jnp.where(spare, -shift_ref[...], roped[a] * q_scale).T.astype(BF16)
        k_out[a, 0] = jnp.where(spare, 1.0, roped[n + a]).astype(BF16)
    v = v_ref[...]
    vt_out[:, 0] = _with_sum_rows(v.T.reshape(DIF_HEADS, 2 * DIF_HEAD_DIM, v.shape[0])).astype(BF16)


def _dif_prep(u, gq, gk, shift, tabs):
    t = u.shape[0]
    tm = ROW_TILE
    nt = t // tm
    nsub = 2 * DIF_HEADS
    const = lambda i: (0, 0)
    rows = lambda i: (i, 0)
    return _Stage(
        kernel=_dif_prep_kernel,
        out_shape=(jax.ShapeDtypeStruct((nsub, LANES, t), BF16),
                   jax.ShapeDtypeStruct((nsub, nt, tm, LANES), BF16),
                   jax.ShapeDtypeStruct((DIF_HEADS, nt, 2 * DIF_HEAD_DIM + SUM_ROWS, tm), BF16)),
        in_specs=[pl.BlockSpec((tm, 1024), lambda i: (i, U_DQ // 1024)),
                  pl.BlockSpec((tm, 1024), lambda i: (i, U_DK // 1024)),
                  pl.BlockSpec((tm, 512), lambda i: (i, U_DV // 512)),
                  pl.BlockSpec((1, LANES), const),
                  pl.BlockSpec((1, LANES), const),
                  pl.BlockSpec((1, LANES), const),
                  pl.BlockSpec((tm, LANES), rows),
                  pl.BlockSpec((tm, LANES), rows),
                  pl.BlockSpec((tm, LANES), rows)],
        out_specs=(pl.BlockSpec((nsub, LANES, tm), lambda i: (0, 0, i)),
                   pl.BlockSpec((nsub, 1, tm, LANES), lambda i: (0, i, 0, 0)),
                   pl.BlockSpec((DIF_HEADS, 1, 2 * DIF_HEAD_DIM + SUM_ROWS, tm), lambda i: (0, i, 0, 0))),
        scratch_shapes=(),
        args=(u, u, u, gq, gk, shift, *tabs))


ATTN_GROUP = 2
ATTN_TRIP = 64
SUM_ROWS = 16


def _with_sum_rows(vt):
    heads, _, keys = vt.shape
    row = lax.broadcasted_iota(jnp.int32, (heads, SUM_ROWS, keys), 1)
    return jnp.concatenate([vt, jnp.where(row == 0, 1.0, 0.0).astype(vt.dtype)], axis=1)


def _attn_group(qt_ref, k_ref, vt_ref, v_of_sub, s_buf, p_buf, acc_ref, n_chunks):
    tq = qt_ref.shape[2]
    last = n_chunks - 1

    def scores(c):
        out = []
        for a in range(ATTN_GROUP):
            s = _dot(k_ref[a, c], qt_ref[a])
            out.append((s, jnp.max(s, axis=0, keepdims=True)))
        return out

    def stash(sc, slot):
        for a in range(ATTN_GROUP):
            s_buf[slot, a] = sc[a][0]
        return tuple(mx for (_, mx) in sc)

    def softmax(s_of, mx, slot, ms):
        new_ms, alphas = [], []
        for a in range(ATTN_GROUP):
            m_new = jnp.maximum(ms[a], mx[a])
            alphas.append(jnp.exp2(ms[a] - m_new))
            p_buf[slot, a] = jnp.exp2(s_of(a) - m_new).astype(BF16)
            new_ms.append(m_new)
        return tuple(new_ms), tuple(alphas)

    def values(c, slot, alphas):
        for a in range(ATTN_GROUP):
            acc_ref[a] = alphas[a] * acc_ref[a] + _dot(vt_ref[v_of_sub[a], c], p_buf[slot, a])

    per_trip = math.gcd(last, ATTN_TRIP)

    def trip(t, state):
        ms, alphas, mx0 = state
        c = per_trip * t + 1
        for pair in range(per_trip // 2):
            cur, nxt = pair % 2, 1 - pair % 2
            even = scores(c + 1)
            values(c - 1, 0, alphas)
            ms, alphas = softmax(lambda a: s_buf[cur, a], mx0, 1, ms)
            mx0 = stash(scores(jnp.minimum(c + 2, last)), nxt)
            values(c, 1, alphas)
            ms, alphas = softmax(lambda a: even[a][0], [mx for (_, mx) in even], 0, ms)
            c = c + 2
        return ms, alphas, mx0

    def context_chunk():
        ms = tuple(jnp.full((1, tq), -jnp.inf, F32) for _ in range(ATTN_GROUP))
        first = scores(0)
        return softmax(lambda a: first[a][0], [mx for (_, mx) in first], 0, ms)

    acc_ref[...] = jnp.zeros(acc_ref.shape, F32)

    @pl.when(pl.program_id(1) == 0)
    def _():
        _, alphas = context_chunk()
        values(0, 0, alphas)

    @pl.when(pl.program_id(1) > 0)
    def _():
        ms, alphas = context_chunk()
        state = (ms, alphas, stash(scores(1), 0))
        if per_trip == last:
            _, alphas, _ = trip(0, state)
        else:
            _, alphas, _ = lax.fori_loop(0, last // per_trip, trip, state)
        values(last, 0, alphas)


def _normalised(acc_ref, a):
    dv = acc_ref.shape[1] - SUM_ROWS
    return acc_ref[a, :dv] * (1.0 / acc_ref[a, dv:dv + 1])


PV_SPAN = 32
MAX_SHIFT = 60.0


def _attn_group_shifted(qt_ref, k_ref, vt_ref, v_of_sub, p_buf, acc_ref, n_chunks):
    span = math.gcd(n_chunks - 1, PV_SPAN)

    def accumulate(c0, n, slot):
        for a in range(ATTN_GROUP):
            for i in range(n):
                s = _dot(k_ref[a, c0 + i], qt_ref[a])
                p_buf[slot, a, KEY_CHUNK * i:KEY_CHUNK * (i + 1)] = jnp.exp2(s).astype(BF16)
            vt = jnp.concatenate([vt_ref[v_of_sub[a], c0 + i] for i in range(n)], axis=1)
            acc_ref[a] = acc_ref[a] + _dot(vt, p_buf[slot, a, :KEY_CHUNK * n])

    acc_ref[...] = jnp.zeros(acc_ref.shape, F32)

    @pl.when(pl.program_id(1) == 0)
    def _():
        accumulate(0, 1, 0)

    @pl.when(pl.program_id(1) > 0)
    def _():
        accumulate(0, span + 1, 0)
        for j in range(1, (n_chunks - 1) // span):
            accumulate(1 + span * j, span, j % 2)


def _attend(shifted, qt_ref, k_ref, vt_ref, v_of_sub, scratch, n_chunks):
    if shifted:
        p_buf, acc_ref = scratch
        _attn_group_shifted(qt_ref, k_ref, vt_ref, v_of_sub, p_buf, acc_ref, n_chunks)
    else:
        s_buf, p_buf, acc_ref = scratch
        _attn_group(qt_ref, k_ref, vt_ref, v_of_sub, s_buf, p_buf, acc_ref, n_chunks)
    return acc_ref


def _mla_attn_kernel(qt_ref, k_ref, vt_ref, o_ref, *scratch, n_chunks, shifted):
    acc_ref = _attend(shifted, qt_ref, k_ref, vt_ref, tuple(range(ATTN_GROUP)), scratch, n_chunks)
    o = jnp.concatenate([_normalised(acc_ref, a) for a in range(ATTN_GROUP)], axis=0)
    o_ref[...] = o.T


def _dif_attn_kernel(lam_ref, gsub_ref, qt_ref, k_ref, vt_ref, o_ref, *scratch, n_chunks, shifted, lam_init):
    acc_ref = _attend(shifted, qt_ref, k_ref, vt_ref, tuple(a // 2 for a in range(ATTN_GROUP)), scratch,
                      n_chunks)
    lp = lam_ref[...]
    lam = (jnp.exp(jnp.sum(lp[0:1] * lp[1:2], axis=-1, keepdims=True))
           - jnp.exp(jnp.sum(lp[2:3] * lp[3:4], axis=-1, keepdims=True)) + lam_init)
    outs = []
    for h in range(ATTN_GROUP // 2):
        o = _normalised(acc_ref, 2 * h) - lam * _normalised(acc_ref, 2 * h + 1)
        outs.append(o * lax.rsqrt(jnp.mean(o * o, axis=0, keepdims=True) + EPS))
    gsub = gsub_ref[...] * (1.0 - lam_init)
    o_ref[...] = jnp.concatenate(outs, axis=0).T * jnp.concatenate([gsub] * len(outs), axis=1)


def _attention(qt, k, vt, *, shifted=False, lam=None, gsub=None, lam_init=None):
    nsub, _, t = qt.shape
    n_chunks = k.shape[1]
    assert (n_chunks - 1) % 4 == 0 and nsub % ATTN_GROUP == 0
    tq = KEY_CHUNK
    dv = vt.shape[2] - SUM_ROWS
    groups = nsub // ATTN_GROUP
    n_v = vt.shape[0] // groups
    out_w = n_v * dv
    resident = dict(pipeline_mode=pl.Buffered(1))
    specs = [pl.BlockSpec((ATTN_GROUP, LANES, tq), lambda g, i: (g, 0, i)),
             pl.BlockSpec((ATTN_GROUP, n_chunks, KEY_CHUNK, LANES), lambda g, i: (g, 0, 0, 0), **resident),
             pl.BlockSpec((n_v, n_chunks, dv + SUM_ROWS, KEY_CHUNK), lambda g, i: (g, 0, 0, 0), **resident)]
    if lam is None:
        body = functools.partial(_mla_attn_kernel, n_chunks=n_chunks, shifted=shifted)
        args = (qt, k, vt)
    else:
        body = functools.partial(_dif_attn_kernel, n_chunks=n_chunks, shifted=shifted, lam_init=lam_init)
        specs = [pl.BlockSpec((8, LANES), lambda g, i: (0, 0)),
                 pl.BlockSpec((1, LANES), lambda g, i: (0, 0))] + specs
        args = (lam, gsub, qt, k, vt)
    acc = pltpu.VMEM((ATTN_GROUP, dv + SUM_ROWS, tq), F32)
    if shifted:
        span = math.gcd(n_chunks - 1, PV_SPAN)
        scratch = [pltpu.VMEM((2, ATTN_GROUP, (span + 1) * KEY_CHUNK, tq), BF16), acc]
    else:
        scratch = [pltpu.VMEM((2, ATTN_GROUP, KEY_CHUNK, tq), F32),
                   pltpu.VMEM((2, ATTN_GROUP, KEY_CHUNK, tq), BF16), acc]
    return _pallas_call(
        body,
        out_shape=jax.ShapeDtypeStruct((t, groups * out_w), F32),
        grid=(groups, t // tq),
        in_specs=specs,
        out_specs=pl.BlockSpec((tq, out_w), lambda g, i: (i, g)),
        scratch_shapes=scratch,
        semantics=("parallel", "arbitrary"),
        name=("mla_attention" if lam is None else "dif_attention") + ("_shifted" if shifted else ""),
    )(*args)


def _ssm_prep_kernel(x_ref, prev_ref, next_ref, dt_ref, w_ref, b_ref, dtb_ref, xo_ref, dto_ref, e_ref,
                     *, n_ctx, n_tok, tm):
    row0 = pl.program_id(0) * tm
    pad = SSM_CONV // 2
    has_prev = jnp.logical_and(row0 != 0, row0 != n_ctx)
    has_next = jnp.logical_and(row0 + tm != n_ctx, row0 + tm != n_tok)
    e_ref[0:8] = jnp.where(has_prev, prev_ref[...], 0.0)
    e_ref[8:8 + tm] = x_ref[...]
    e_ref[8 + tm:16 + tm] = jnp.where(has_next, next_ref[...], 0.0)
    acc = jnp.zeros(x_ref.shape, F32) + b_ref[...]
    for k in range(SSM_CONV):
        acc = acc + w_ref[k:k + 1, :] * e_ref[pl.ds(8 - pad + k, tm), :]
    xo_ref[...] = _silu(acc)
    d = dt_ref[...] + dtb_ref[...]
    dto_ref[...] = jnp.maximum(d, 0.0) + jnp.log1p(jnp.exp(-jnp.abs(d)))


def _ssm_prep(u, conv_w, conv_b, dt_bias, n_ctx):
    t = u.shape[0]
    tm = ROW_TILE
    nt = t // tm
    cb = U_XBC // SSM_XBC
    const = lambda i: (0, 0)
    return _Stage(
        kernel=functools.partial(_ssm_prep_kernel, n_ctx=n_ctx, n_tok=t, tm=tm),
        out_shape=(jax.ShapeDtypeStruct((t, SSM_XBC), F32), jax.ShapeDtypeStruct((t, LANES), F32)),
        in_specs=[pl.BlockSpec((tm, SSM_XBC), lambda i: (i, cb)),
                  pl.BlockSpec((8, SSM_XBC), lambda i: (jnp.maximum(i * (tm // 8) - 1, 0), cb)),
                  pl.BlockSpec((8, SSM_XBC), lambda i: (jnp.minimum((i + 1) * (tm // 8), t // 8 - 1), cb)),
                  pl.BlockSpec((tm, LANES), lambda i: (i, U_DT // LANES)),
                  pl.BlockSpec((8, SSM_XBC), const),
                  pl.BlockSpec((1, SSM_XBC), const),
                  pl.BlockSpec((1, LANES), const)],
        out_specs=(pl.BlockSpec((tm, SSM_XBC), lambda i: (i, 0)),
                   pl.BlockSpec((tm, LANES), lambda i: (i, 0))),
        scratch_shapes=(pltpu.VMEM((tm + 16, SSM_XBC), F32),),
        args=(u, u, u, u, conv_w, conv_b, dt_bias))


def _ssd_kernel(xf_ref, dtf_ref, xr_ref, dtr_ref, alog_ref, of_ref, or_ref, hf_ref, hr_ref):
    @pl.when(pl.program_id(0) == 0)
    def _():
        hf_ref[...] = jnp.zeros(hf_ref.shape, F32)
        hr_ref[...] = jnp.zeros(hr_ref.shape, F32)

    lc = SSM_CHUNK
    a_row = -jnp.exp(alog_ref[...])
    n_sub = xf_ref.shape[0] // lc
    fwd = [_ssd_local(xf_ref[lc * j:lc * (j + 1), :], dtf_ref[lc * j:lc * (j + 1), :], a_row, 0)
           for j in range(n_sub)]
    bwd = [_ssd_local(xr_ref[lc * j:lc * (j + 1), :], dtr_ref[lc * j:lc * (j + 1), :], a_row, 1)
           for j in range(n_sub)]
    of_ref[...] = jnp.concatenate([_ssd_carry(part, hf_ref) for part in fwd], axis=0)
    or_ref[...] = jnp.concatenate([_ssd_carry(part, hr_ref) for part in reversed(bwd)][::-1], axis=0)


def _ssd_carry(local, h_ref):
    outs = []
    for h, (y, c_g, from_start, decay, upd) in enumerate(local):
        state = h_ref[h]
        outs.append(y + _dot(c_g, state.astype(BF16)) * from_start)
        h_ref[h] = state * decay + upd
    return jnp.concatenate(outs, axis=1)


def _ssd_local(xbc, dt, a_row, direction):
    lc = SSM_CHUNK
    hd, per_group = SSM_HEAD_DIM, SSM_HEADS // SSM_GROUPS
    dta = dt * a_row
    r = lax.broadcasted_iota(jnp.int32, (lc, lc), 0)
    c = lax.broadcasted_iota(jnp.int32, (lc, lc), 1)
    keep = (r >= c) if direction == 0 else (r <= c)
    tri = jnp.where(keep, 1.0, 0.0).astype(F32)
    cum = jnp.dot(tri, dta, preferred_element_type=F32, precision=lax.Precision.HIGHEST)
    cum_t = cum.T
    dt_t = dt.T
    total = jnp.sum(dta, axis=0, keepdims=True)
    chunk_decay = jnp.exp(total)
    w_t = (dt * jnp.exp(total - cum)).T
    b_f32 = [xbc[:, SSM_WIDTH + SSM_STATE * g:SSM_WIDTH + SSM_STATE * (g + 1)] for g in range(SSM_GROUPS)]
    c_bf = [xbc[:, SSM_WIDTH + SSM_STATE * (SSM_GROUPS + g):SSM_WIDTH + SSM_STATE * (SSM_GROUPS + g + 1)].astype(BF16)
            for g in range(SSM_GROUPS)]
    cb = [_dot_nt(c, b.astype(BF16)) for c, b in zip(c_bf, b_f32)]
    b_t = [b.T for b in b_f32]
    heads = range(SSM_HEADS)
    cols = [direction * SSM_HEADS + h for h in heads]
    cum_b = [jnp.broadcast_to(cum[:, c:c + 1], (lc, lc)) for c in cols]
    seg = [jnp.exp(jnp.where(keep, cb_ - cum_t[c:c + 1, :], -jnp.inf)) for cb_, c in zip(cum_b, cols)]
    x_h = [xbc[:, hd * h:hd * (h + 1)].astype(BF16) for h in heads]
    scores = [(cb[h // per_group] * s * dt_t[c:c + 1, :]).astype(BF16) for h, s, c in zip(heads, seg, cols)]
    weighted_b = [(b_t[h // per_group] * w_t[c:c + 1, :]).astype(BF16) for h, c in zip(heads, cols)]
    y = [_dot(s, x) for s, x in zip(scores, x_h)]
    upd = [_dot(wb, x) for wb, x in zip(weighted_b, x_h)]
    return [(y[h], c_bf[h // per_group], jnp.exp(cum_b[h][:, :hd]), chunk_decay[:, cols[h]:cols[h] + 1], upd[h])
            for h in heads]


SSD_STEP = 2 * SSM_CHUNK


def _ssd(xbc, dt, a_log, n_ctx):
    t = xbc.shape[0]
    lc = SSD_STEP
    assert n_ctx % lc == 0 and t % lc == 0
    nc = t // lc
    ncc = n_ctx // lc
    fwd = lambda s: (s, 0)
    bwd = lambda s: (jnp.where(s < ncc, ncc - 1 - s, nc - 1 - (s - ncc)), 0)
    state = pltpu.VMEM((SSM_HEADS, SSM_STATE, SSM_HEAD_DIM), F32)
    out = jax.ShapeDtypeStruct((t, SSM_WIDTH), F32)
    return _pallas_call(
        _ssd_kernel,
        out_shape=(out, out),
        grid=(nc,),
        in_specs=[pl.BlockSpec((lc, SSM_XBC), fwd), pl.BlockSpec((lc, LANES), fwd),
                  pl.BlockSpec((lc, SSM_XBC), bwd), pl.BlockSpec((lc, LANES), bwd),
                  pl.BlockSpec((1, LANES), lambda s: (0, 0))],
        out_specs=(pl.BlockSpec((lc, SSM_WIDTH), fwd), pl.BlockSpec((lc, SSM_WIDTH), bwd)),
        scratch_shapes=[state, state],
        semantics=("arbitrary",),
        name="ssd_scan",
    )(xbc, dt, xbc, dt, a_log)


def _merge_kernel(x_ref, gate_ref, ya_ref, yb_ref, yf_ref, yr_ref, xs_ref, z_ref,
                  bg_ref, dskip_ref, gssm_ref, wa_ref, wb_ref, wc_ref, wo_ref,
                  ml_ref, mc_ref, g2_ref, wr_ref, br_ref,
                  xo_ref, f_ref, lg_ref, *, n_ctx, tm):
    y = (yf_ref[...] + yr_ref[...] + dskip_ref[...] * xs_ref[...]) * _silu(z_ref[...])
    gw = SSM_WIDTH // SSM_GROUPS
    yc = jnp.concatenate([_rms_rows(y[:, gw * g:gw * (g + 1)], gw) for g in range(SSM_GROUPS)], axis=1)
    yc = yc * gssm_ref[...]
    gate = _sigmoid(gate_ref[...] + bg_ref[...])
    m = (gate[:, :D_MODEL] * _dot(ya_ref[...].astype(BF16), wa_ref[...])
         + gate[:, D_MODEL:2 * D_MODEL] * _dot(yb_ref[...].astype(BF16), wb_ref[...])
         + gate[:, 2 * D_MODEL:] * _dot(yc.astype(BF16), wc_ref[...]))
    out = _dot(m.astype(BF16), wo_ref[...])
    row0 = pl.program_id(0) * tm
    is_ctx = (row0 + lax.broadcasted_iota(jnp.int32, (tm, 1), 0)) < n_ctx
    gt1 = jnp.where(is_ctx, mc_ref[2:3, :], ml_ref[2:3, :])
    x_new = x_ref[...] + gt1 * out
    xo_ref[...] = x_new
    f = _modulated_norm(x_new, g2_ref[...], ml_ref[3:5, :], mc_ref[3:5, :], row0, n_ctx)
    f_ref[...] = f
    lg_ref[...] = _dot(f.astype(BF16), wr_ref[...]) + br_ref[...]


def _merge(x_all, u, ya, yb, yf, yr, xbc, b_gate, dskip, g_ssm, wa, wb, wc, wo, mod_lat, mod_ctx,
           g2, w_router, b_router, n_ctx):
    t, d = x_all.shape
    tm = ROW_TILE
    const = lambda i: (0, 0)
    rows = lambda i: (i, 0)
    full = lambda a: pl.BlockSpec(a.shape, const)
    return _pallas_call(
        functools.partial(_merge_kernel, n_ctx=n_ctx, tm=tm),
        out_shape=(jax.ShapeDtypeStruct((t, d), F32), jax.ShapeDtypeStruct((t, d), F32),
                   jax.ShapeDtypeStruct((t, LANES), F32)),
        grid=(t // tm,),
        in_specs=[pl.BlockSpec((tm, d), rows),
                  pl.BlockSpec((tm, GATE_COLS), lambda i: (i, U_GATE // GATE_COLS)),
                  pl.BlockSpec((tm, MLA_WIDTH), rows),
                  pl.BlockSpec((tm, DIF_WIDTH), rows),
                  pl.BlockSpec((tm, SSM_WIDTH), rows),
                  pl.BlockSpec((tm, SSM_WIDTH), rows),
                  pl.BlockSpec((tm, SSM_WIDTH), rows),
                  pl.BlockSpec((tm, SSM_WIDTH), lambda i: (i, U_Z // SSM_WIDTH)),
                  full(b_gate), full(dskip), full(g_ssm), full(wa), full(wb), full(wc), full(wo),
                  full(mod_lat), full(mod_ctx), full(g2), full(w_router), full(b_router)],
        out_specs=(pl.BlockSpec((tm, d), rows), pl.BlockSpec((tm, d), rows),
                   pl.BlockSpec((tm, LANES), rows)),
        semantics=("parallel",),
        name="merge",
    )(x_all, u, ya, yb, yf, yr, xbc, u, b_gate, dskip, g_ssm, wa, wb, wc, wo, mod_lat, mod_ctx,
      g2, w_router, b_router)


def _moe_kernel(be_ref, nb_ref, x_ref, wgu_ref, bgu_ref, wd_ref, bd_ref, o_ref, wgu_s, wd_s):
    b = pl.program_id(0)
    prev = be_ref[jnp.maximum(b - 1, 0)]
    fresh = jnp.logical_or(b == 0, be_ref[b] != prev)

    @pl.when(fresh)
    def _():
        wgu_s[...] = wgu_ref[...].astype(BF16)
        wd_s[...] = wd_ref[...].astype(BF16)

    @pl.when(b < nb_ref[0])
    def _():
        gu = _dot(x_ref[...].astype(BF16), wgu_s[...]) + bgu_ref[...]
        glu = jnp.minimum(gu[:, :D_FF], SWIGLU_LIMIT)
        lin = jnp.clip(gu[:, D_FF:], -SWIGLU_LIMIT, SWIGLU_LIMIT)
        act = glu * _sigmoid(SWIGLU_ALPHA * glu) * (lin + 1.0)
        o_ref[...] = _dot(act.astype(BF16), wd_s[...]) + bd_ref[...]

    @pl.when(b >= nb_ref[0])
    def _():
        o_ref[...] = jnp.zeros(o_ref.shape, F32)


def _moe_experts(block_e, n_used, x_sorted, w_gu, b_gu, w_down, b_down, layer):
    n_slots, d = x_sorted.shape
    n_blocks = n_slots // MOE_BLOCK
    by_expert = lambda b, be, nb: (layer, be[b], 0, 0)
    return _pallas_call(
        _moe_kernel,
        out_shape=jax.ShapeDtypeStruct((n_slots, d), F32),
        num_scalar_prefetch=2,
        grid=(n_blocks,),
        in_specs=[pl.BlockSpec((MOE_BLOCK, d), lambda b, be, nb: (b, 0)),
                  pl.BlockSpec((None, None, d, 2 * D_FF), by_expert),
                  pl.BlockSpec((None, None, 1, 2 * D_FF), by_expert),
                  pl.BlockSpec((None, None, D_FF, d), by_expert),
                  pl.BlockSpec((None, None, 1, d), by_expert)],
        out_specs=pl.BlockSpec((MOE_BLOCK, d), lambda b, be, nb: (b, 0)),
        scratch_shapes=[pltpu.VMEM((d, 2 * D_FF), BF16), pltpu.VMEM((D_FF, d), BF16)],
        semantics=("arbitrary",),
        name="moe_experts",
    )(block_e, n_used, x_sorted, w_gu, b_gu, w_down, b_down)


def _combine_kernel(x_ref, g_ref, ml_ref, mc_ref, *rest, n_ctx, tm):
    y_refs, o_ref = rest[:TOP_K], rest[TOP_K]
    is_ctx = (pl.program_id(0) * tm + lax.broadcasted_iota(jnp.int32, (tm, 1), 0)) < n_ctx
    gate2 = jnp.where(is_ctx, mc_ref[...], ml_ref[...])
    g = g_ref[...]
    y = g[:, 0:1] * y_refs[0][...]
    for k in range(1, TOP_K):
        y = y + g[:, k:k + 1] * y_refs[k][...]
    o_ref[...] = x_ref[...] + gate2 * y


def _combine(x_all, y_rows, gates, gate2_lat, gate2_ctx, n_ctx):
    t, d = x_all.shape
    tm = ROW_TILE
    rows = lambda i: (i, 0)
    const = lambda i: (0, 0)
    return _pallas_call(
        functools.partial(_combine_kernel, n_ctx=n_ctx, tm=tm),
        out_shape=jax.ShapeDtypeStruct((t, d), F32),
        grid=(t // tm,),
        in_specs=[pl.BlockSpec((tm, d), rows), pl.BlockSpec((tm, LANES), rows),
                  pl.BlockSpec((1, d), const), pl.BlockSpec((1, d), const)]
                 + [pl.BlockSpec((tm, d), rows)] * TOP_K,
        out_specs=pl.BlockSpec((tm, d), rows),
        semantics=("parallel",),
        name="moe_combine",
    )(x_all, gates, gate2_lat, gate2_ctx, *y_rows)


ROUTE_IDX, ROUTE_GATE, ROUTE_RANK = 0, TOP_K, 2 * TOP_K


def _router_kernel(lg_ref, o_ref, cnt_ref, run_ref, *, tm):
    @pl.when(pl.program_id(0) == 0)
    def _():
        run_ref[...] = jnp.zeros(run_ref.shape, F32)

    lane = lax.broadcasted_iota(jnp.int32, (tm, LANES), 1)
    lane_f = lane.astype(F32)
    lg = jnp.where(lane < N_EXPERTS, lg_ref[...], -jnp.inf)
    hots, vals = [], []
    for _ in range(TOP_K):
        mx = jnp.max(lg, axis=-1, keepdims=True)
        idx = jnp.min(jnp.where(lg == mx, lane_f, float(LANES)), axis=-1, keepdims=True)
        hot = lane_f == idx
        lg = jnp.where(hot, -jnp.inf, lg)
        hots.append((hot, idx))
        vals.append(mx)
    exps = [jnp.exp(v - vals[0]) for v in vals]
    inv = 1.0 / sum(exps[1:], exps[0])
    chosen = jnp.zeros((tm, LANES), F32)
    for hot, _ in hots:
        chosen = jnp.where(hot, 1.0, chosen)
    r = lax.broadcasted_iota(jnp.int32, (tm, tm), 0)
    c = lax.broadcasted_iota(jnp.int32, (tm, tm), 1)
    earlier = jnp.where(r > c, 1.0, 0.0).astype(BF16)
    before = _dot(earlier, chosen.astype(BF16)) + run_ref[0:1, :]
    out = jnp.zeros((tm, LANES), F32)
    for k, (hot, idx) in enumerate(hots):
        rank = jnp.sum(jnp.where(hot, before, 0.0), axis=-1, keepdims=True)
        out = jnp.where(lane == ROUTE_IDX + k, idx, out)
        out = jnp.where(lane == ROUTE_GATE + k, exps[k] * inv, out)
        out = jnp.where(lane == ROUTE_RANK + k, rank, out)
    o_ref[...] = out
    run_ref[...] = run_ref[...] + jnp.sum(chosen, axis=0, keepdims=True)
    cnt_ref[...] = run_ref[...]


def _router(logits):
    t = logits.shape[0]
    tm = ROW_TILE
    return _pallas_call(
        functools.partial(_router_kernel, tm=tm),
        out_shape=(jax.ShapeDtypeStruct((t, LANES), F32), jax.ShapeDtypeStruct((8, LANES), F32)),
        grid=(t // tm,),
        in_specs=[pl.BlockSpec((tm, LANES), lambda i: (i, 0))],
        out_specs=(pl.BlockSpec((tm, LANES), lambda i: (i, 0)), pl.BlockSpec((8, LANES), lambda i: (0, 0))),
        scratch_shapes=[pltpu.VMEM((8, LANES), F32)],
        semantics=("arbitrary",),
        name="moe_router",
    )(logits)


def _route(logits, n_tok):
    routed, counts = _router(logits)
    top_idx = routed[:, ROUTE_IDX:ROUTE_IDX + TOP_K].astype(jnp.int32)
    gates = routed[:, ROUTE_GATE:ROUTE_GATE + TOP_K]
    rank = routed[:, ROUTE_RANK:ROUTE_RANK + TOP_K].astype(jnp.int32)
    n_assign = n_tok * TOP_K
    counts = counts[0, :N_EXPERTS].astype(jnp.int32)
    padded = (counts + MOE_BLOCK - 1) // MOE_BLOCK * MOE_BLOCK
    pad_end = jnp.cumsum(padded)
    pad_start = pad_end - padded
    slot_of = pad_start[top_idx] + rank
    n_blocks = -(-(n_assign + N_EXPERTS * (MOE_BLOCK - 1)) // MOE_BLOCK)
    n_slots = n_blocks * MOE_BLOCK
    token = jnp.arange(n_assign, dtype=jnp.int32) // TOP_K
    slot_tok = jnp.zeros((n_slots,), jnp.int32).at[slot_of.reshape(-1)].set(token, unique_indices=True)
    block_start = jnp.arange(n_blocks, dtype=jnp.int32) * MOE_BLOCK
    block_e = jnp.minimum(jnp.searchsorted(pad_end, block_start, side='right'), N_EXPERTS - 1)
    n_used = (pad_end[-1] // MOE_BLOCK).reshape(1)
    return gates, slot_tok, slot_of, block_e.astype(jnp.int32), n_used.astype(jnp.int32)


def kernel(x, c, ctx, c_ctx, w_mod, b_mod, g_norm1, g_norm2, w_in, b_gate, mla_g_q, mla_w_uq, mla_g_kv, mla_w_ukv, mla_g_qn, mla_g_kn, dif_g_qn, dif_g_kn, dif_lambda, dif_g_sub, ssm_conv_w, ssm_conv_b, ssm_dt_bias, ssm_a_log, ssm_d, ssm_g_norm, w_up_mla, w_up_dif, w_up_ssm, w_out, moe_w_router, moe_b_router, moe_w_gu, moe_b_gu, moe_w_down, moe_b_down):
    assert x.shape[0] == 1 and ctx.shape[0] == 1
    depth = w_in.shape[0]
    seq = x.shape[1]
    n_ctx = ctx.shape[1]
    n_tok = n_ctx + seq
    d = D_MODEL
    assert n_ctx == KEY_CHUNK and n_tok % ROW_TILE == 0 and seq % GRID_W == 0

    x_all = jnp.concatenate([ctx[0], x[0]], axis=0)
    cc = jnp.zeros((8, d), F32).at[0].set(c[0]).at[1].set(c_ctx)
    mod = _mod_vectors(cc, w_mod, b_mod)
    mod = mod[:, :2].reshape(depth, 2, 6, d)

    rope_mla = _rope_tables(seq, n_ctx, MLA_ROPE, MLA_NOPE)
    rope_dif = _rope_tables(seq, n_ctx, DIF_HEAD_DIM, 0)
    w_in_all = _take_columns(w_in, _in_proj_columns()).astype(BF16)
    w_uq_all = _take_columns(mla_w_uq, _head_columns(MLA_HEADS, MLA_QK, 0, MLA_QK, LANES)).astype(BF16)
    w_uk_all = _take_columns(mla_w_ukv, _head_columns(MLA_HEADS, MLA_NOPE + MLA_V, 0, MLA_NOPE, LANES)).astype(BF16)
    w_uv_all = _take_columns(mla_w_ukv, _head_columns(MLA_HEADS, MLA_NOPE + MLA_V, MLA_NOPE, MLA_V, MLA_V)).astype(BF16)
    w_up_all = [w.astype(BF16) for w in (w_up_mla, w_up_dif, w_up_ssm, w_out)]

    for i in range(depth):
        lam_init = 0.8 - 0.6 * math.exp(-0.3 * i)
        mod_lat, mod_ctx = mod[i, 0], mod[i, 1]
        u = _in_proj(x_all, g_norm1[i][None], mod_lat[0:2], mod_ctx[0:2], w_in_all, i, n_ctx)

        bound = (DIF_HEAD_DIM ** 0.5 * LOG2E) * jnp.max(jnp.abs(dif_g_qn[i])) * jnp.max(jnp.abs(dif_g_kn[i]))
        conv_w = jnp.zeros((8, SSM_XBC), F32).at[:SSM_CONV].set(ssm_conv_w[i])
        mla_qkv, dif_qkv, (xbc, dt) = _run_stages("branch_prep", (n_tok // ROW_TILE,), [
            _mla_prep(u, mla_g_q[i][None], w_uq_all[i], mla_g_kv[i][None], w_uk_all[i], w_uv_all[i],
                      _pad_lanes(mla_g_qn[i][None]), _pad_lanes(mla_g_kn[i][None]), rope_mla),
            _dif_prep(u, _pad_lanes(dif_g_qn[i][None]), _pad_lanes(dif_g_kn[i][None]),
                      jnp.full((1, LANES), bound, F32), rope_dif),
            _ssm_prep(u, conv_w, ssm_conv_b[i][None], _pad_lanes(ssm_dt_bias[i].reshape(1, -1)), n_ctx)])
        ya = _attention(*mla_qkv)

        lam_rows = jnp.zeros((8, LANES), F32).at[:4, :DIF_HEAD_DIM].set(dif_lambda[i])
        dif_attention = functools.partial(_attention, lam=lam_rows, gsub=dif_g_sub[i][None], lam_init=lam_init)
        yb = lax.cond(bound <= MAX_SHIFT, functools.partial(dif_attention, shifted=True),
                      functools.partial(dif_attention, shifted=False), *dif_qkv)

        a_log = _pad_lanes(ssm_a_log[i].reshape(1, -1))
        yf, yr = _ssd(xbc, dt, a_log, n_ctx)

        dskip = jnp.repeat(ssm_d[i, 0] + ssm_d[i, 1], SSM_HEAD_DIM)[None]
        w_router = jnp.zeros((d, LANES), BF16).at[:, :N_EXPERTS].set(moe_w_router[i].astype(BF16))
        b_router = jnp.zeros((1, LANES), F32).at[0, :N_EXPERTS].set(moe_b_router[i])
        x_all, f, logits = _merge(
            x_all, u, ya, yb, yf, yr, xbc, b_gate[i][None], dskip, ssm_g_norm[i][None],
            w_up_all[0][i], w_up_all[1][i], w_up_all[2][i], w_up_all[3][i],
            mod_lat[0:5], mod_ctx[0:5], g_norm2[i][None],
            w_router, b_router, n_ctx)

        gates, slot_tok, slot_of, block_e, n_used = _route(logits, n_tok)
        y_slots = _moe_experts(block_e, n_used, f[slot_tok], moe_w_gu,
                               moe_b_gu.reshape(depth, N_EXPERTS, 1, 2 * D_FF), moe_w_down,
                               moe_b_down.reshape(depth, N_EXPERTS, 1, d), i)
        y_rows = [y_slots[slot_of[:, k]] for k in range(TOP_K)]
        x_all = _combine(x_all, y_rows, _pad_lanes(gates), mod_lat[5:6], mod_ctx[5:6], n_ctx)
    return x_all[n_ctx:][None]
```

```python
import collections
import functools
import math

import numpy as np
import jax
import jax.numpy as jnp
from jax import lax
from jax.experimental import pallas as pl
from jax.experimental.pallas import tpu as pltpu

F32 = jnp.float32
BF16 = jnp.bfloat16
LANES = 128
VMEM_REQUEST_CAP = 56 * 1024 * 1024
VMEM_TEMPORARIES = 16 * 1024 * 1024

D_MODEL = 1024
EPS = 1e-6
ROPE_THETA = 10000.0
GRID_W = 64
N_BRANCH = 3

MLA_HEADS = 8
MLA_Q_RANK = 256
MLA_KV_RANK = 128
MLA_NOPE = 64
MLA_ROPE = 32
MLA_V = 64
MLA_QK = MLA_NOPE + MLA_ROPE
MLA_WIDTH = MLA_HEADS * MLA_V

DIF_HEADS = 4
DIF_HEAD_DIM = 64
DIF_WIDTH = DIF_HEADS * 2 * DIF_HEAD_DIM

SSM_HEADS = 8
SSM_HEAD_DIM = 64
SSM_WIDTH = SSM_HEADS * SSM_HEAD_DIM
SSM_GROUPS = 2
SSM_STATE = 128
SSM_CONV = 5
SSM_CHUNK = 128
SSM_XBC = SSM_WIDTH + 2 * SSM_GROUPS * SSM_STATE

N_EXPERTS = 32
TOP_K = 4
D_FF = 1024
SWIGLU_LIMIT = 7.0
SWIGLU_ALPHA = 1.702
MOE_BLOCK = 512

MLA_COLS = MLA_Q_RANK + MLA_KV_RANK + MLA_ROPE
DIF_COLS = 3 * DIF_WIDTH
SSM_COLS = SSM_WIDTH + SSM_XBC + 2 * SSM_HEADS
GATE_COLS = N_BRANCH * D_MODEL

ROW_TILE = 256
KEY_CHUNK = 256
LOG2E = 1.4426950408889634

U_GATE, U_DQ, U_DK, U_XBC, U_MLA, U_DV, U_Z, U_DT = 0, 3072, 4096, 5120, 6144, 6656, 7168, 7680
U_COLS = 8192
U_TILE_N = 2048


def _in_proj_columns():
    src = np.full((U_COLS,), -1, np.int64)
    dif0 = MLA_COLS
    ssm0 = MLA_COLS + DIF_COLS
    gate0 = ssm0 + SSM_COLS
    src[U_GATE:U_GATE + GATE_COLS] = gate0 + np.arange(GATE_COLS)
    for a in range(2 * DIF_HEADS):
        src[U_DQ + LANES * a:U_DQ + LANES * a + DIF_HEAD_DIM] = dif0 + DIF_HEAD_DIM * a + np.arange(DIF_HEAD_DIM)
        src[U_DK + LANES * a:U_DK + LANES * a + DIF_HEAD_DIM] = (dif0 + DIF_WIDTH + DIF_HEAD_DIM * a
                                                                  + np.arange(DIF_HEAD_DIM))
    src[U_DV:U_DV + DIF_WIDTH] = dif0 + 2 * DIF_WIDTH + np.arange(DIF_WIDTH)
    src[U_MLA:U_MLA + MLA_Q_RANK + MLA_KV_RANK] = np.arange(MLA_Q_RANK + MLA_KV_RANK)
    pe0 = U_MLA + MLA_Q_RANK + MLA_KV_RANK + MLA_NOPE
    src[pe0:pe0 + MLA_ROPE] = MLA_Q_RANK + MLA_KV_RANK + np.arange(MLA_ROPE)
    src[U_Z:U_Z + SSM_WIDTH] = ssm0 + np.arange(SSM_WIDTH)
    src[U_XBC:U_XBC + SSM_XBC] = ssm0 + SSM_WIDTH + np.arange(SSM_XBC)
    src[U_DT:U_DT + 2 * SSM_HEADS] = ssm0 + SSM_WIDTH + SSM_XBC + np.arange(2 * SSM_HEADS)
    return src


def _take_columns(w, src):
    pieces, start = [], 0
    for stop in range(1, len(src) + 1):
        same_run = stop < len(src) and (src[stop] == src[stop - 1] + 1 if src[stop - 1] >= 0 else src[stop] < 0)
        if not same_run:
            if src[start] < 0:
                pieces.append(jnp.zeros(w.shape[:-1] + (stop - start,), w.dtype))
            else:
                pieces.append(w[..., src[start]:src[start] + stop - start])
            start = stop
    return jnp.concatenate(pieces, axis=-1)


def _head_columns(n_heads, src_stride, src_off, width, dst_stride):
    src = np.full((n_heads * dst_stride,), -1, np.int64)
    for h in range(n_heads):
        src[h * dst_stride:h * dst_stride + width] = h * src_stride + src_off + np.arange(width)
    return src


def _pad_lanes(v, n=LANES):
    return jnp.pad(v, [(0, 0)] * (v.ndim - 1) + [(0, n - v.shape[-1])])


def _row_tile(n, cap):
    best = 8
    for t in range(8, cap + 1, 8):
        if n % t == 0:
            best = t
    return best


def _dot(a, b):
    return jnp.dot(a, b, preferred_element_type=F32)


def _dot_nt(a, b):
    return lax.dot_general(a, b, (((1,), (1,)), ((), ())), preferred_element_type=F32)


def _sigmoid(x):
    return 1.0 / (1.0 + jnp.exp(-x))


def _silu(x):
    return x * _sigmoid(x)


def _rms_rows(x, n):
    return x * lax.rsqrt(jnp.sum(x * x, axis=-1, keepdims=True) * (1.0 / n) + EPS)


def _as_tuple(x):
    return tuple(x) if isinstance(x, (tuple, list)) else (x,)


def _window_bytes(spec, dtype):
    if spec.block_shape is None:
        return 0
    dims = [1 if n is None else n for n in spec.block_shape]
    copies = 2 if spec.pipeline_mode is None else spec.pipeline_mode.buffer_count
    return math.prod(dims) * jnp.dtype(dtype).itemsize * copies


def _pallas_call(kernel, *, name, out_shape, grid, in_specs, out_specs, semantics, scratch_shapes=(),
                 num_scalar_prefetch=0):
    def call(*args):
        need = VMEM_TEMPORARIES
        need += sum(_window_bytes(s, a.dtype) for s, a in zip(in_specs, args[num_scalar_prefetch:]))
        need += sum(_window_bytes(s, o.dtype) for s, o in zip(_as_tuple(out_specs), _as_tuple(out_shape)))
        need += sum(math.prod(b.shape) * jnp.dtype(b.dtype).itemsize for b in scratch_shapes)
        params = pltpu.CompilerParams(dimension_semantics=semantics,
                                      vmem_limit_bytes=min(need, VMEM_REQUEST_CAP))
        grid_spec = pltpu.PrefetchScalarGridSpec(
            num_scalar_prefetch=num_scalar_prefetch, grid=grid, in_specs=list(in_specs), out_specs=out_specs,
            scratch_shapes=list(scratch_shapes))
        launch = pl.pallas_call(kernel, out_shape=out_shape, grid_spec=grid_spec, compiler_params=params, name=name)
        return launch(*args)
    return call


_Stage = collections.namedtuple("_Stage", "kernel out_shape in_specs out_specs scratch_shapes args")


def _run_stages(name, grid, stages):
    n_in = [len(s.in_specs) for s in stages]
    n_out = [len(s.out_shape) for s in stages]
    n_scr = [len(s.scratch_shapes) for s in stages]

    def body(*refs):
        ins, outs, scr = refs[:sum(n_in)], refs[sum(n_in):sum(n_in) + sum(n_out)], refs[sum(n_in) + sum(n_out):]
        for j, stage in enumerate(stages):
            stage.kernel(*ins[sum(n_in[:j]):sum(n_in[:j + 1])], *outs[sum(n_out[:j]):sum(n_out[:j + 1])],
                         *scr[sum(n_scr[:j]):sum(n_scr[:j + 1])])

    flat = lambda field: [x for s in stages for x in getattr(s, field)]
    outs = _pallas_call(body, name=name, grid=grid, out_shape=tuple(flat("out_shape")),
                        in_specs=flat("in_specs"), out_specs=tuple(flat("out_specs")),
                        scratch_shapes=flat("scratch_shapes"), semantics=("parallel",))(*flat("args"))
    return [outs[sum(n_out[:j]):sum(n_out[:j + 1])] for j in range(len(stages))]


def _mod_kernel(a_ref, w_ref, b_ref, o_ref):
    a = _silu(a_ref[...]).astype(BF16)
    o_ref[0] = _dot(a, w_ref[0].astype(BF16)) + b_ref[0]


def _mod_vectors(cc, w_mod, b_mod):
    depth, d, n = w_mod.shape
    tn = 1536
    return _pallas_call(
        _mod_kernel,
        out_shape=jax.ShapeDtypeStruct((depth, 8, n), F32),
        grid=(depth, n // tn),
        in_specs=[pl.BlockSpec((8, d), lambda l, j: (0, 0)),
                  pl.BlockSpec((1, d, tn), lambda l, j: (l, 0, j)),
                  pl.BlockSpec((1, 1, tn), lambda l, j: (l, 0, j))],
        out_specs=pl.BlockSpec((1, 8, tn), lambda l, j: (l, 0, j)),
        semantics=("parallel", "parallel"),
        name="mod_vectors",
    )(cc, w_mod, b_mod.reshape(depth, 1, n))


def _modulated_norm(x, g, mod_lat, mod_ctx, row0, n_ctx):
    rows = x.shape[0]
    is_ctx = (row0 + lax.broadcasted_iota(jnp.int32, (rows, 1), 0)) < n_ctx
    shift = jnp.where(is_ctx, mod_ctx[0:1, :], mod_lat[0:1, :])
    scale = jnp.where(is_ctx, mod_ctx[1:2, :], mod_lat[1:2, :])
    return _rms_rows(x, x.shape[1]) * g * (1.0 + scale) + shift


def _in_proj_kernel(x_ref, g_ref, ml_ref, mc_ref, w_ref, o_ref, h_ref, *, n_ctx, tm):
    @pl.when(pl.program_id(1) == 0)
    def _():
        h = _modulated_norm(x_ref[...], g_ref[...], ml_ref[...], mc_ref[...], pl.program_id(0) * tm, n_ctx)
        h_ref[...] = h.astype(BF16)

    o_ref[...] = _dot(h_ref[...], w_ref[...])


def _in_proj(x_all, g, mod_lat, mod_ctx, w, layer, n_ctx):
    t, d = x_all.shape
    n = w.shape[2]
    tm = _row_tile(t, 1280)
    return _pallas_call(
        functools.partial(_in_proj_kernel, n_ctx=n_ctx, tm=tm),
        out_shape=jax.ShapeDtypeStruct((t, n), F32),
        grid=(t // tm, n // U_TILE_N),
        in_specs=[pl.BlockSpec((tm, d), lambda i, j: (i, 0)),
                  pl.BlockSpec((1, d), lambda i, j: (0, 0)),
                  pl.BlockSpec((2, d), lambda i, j: (0, 0)),
                  pl.BlockSpec((2, d), lambda i, j: (0, 0)),
                  pl.BlockSpec((None, d, U_TILE_N), lambda i, j: (layer, 0, j))],
        out_specs=pl.BlockSpec((tm, U_TILE_N), lambda i, j: (i, j)),
        scratch_shapes=[pltpu.VMEM((tm, d), BF16)],
        semantics=("parallel", "arbitrary"),
        name="in_proj",
    )(x_all, g, mod_lat, mod_ctx, w)


def _rope_tables(seq_len, n_ctx, rot_dim, lane0):
    n_rows = seq_len // GRID_W
    row = jnp.repeat(jnp.arange(n_rows), GRID_W).astype(F32)
    col = jnp.tile(jnp.arange(GRID_W), n_rows).astype(F32)
    axis_dim = rot_dim // 2
    half = axis_dim // 2
    inv = ROPE_THETA ** (-jnp.arange(0, axis_dim, 2, dtype=F32) / axis_dim)
    ang_r = row[:, None] * inv
    ang_c = col[:, None] * inv
    zeros = jnp.zeros((seq_len, half), F32)
    cos = jnp.concatenate([jnp.cos(ang_r), jnp.cos(ang_r), jnp.cos(ang_c), jnp.cos(ang_c)], axis=1)
    s1 = jnp.concatenate([zeros, jnp.sin(ang_r), zeros, jnp.sin(ang_c)], axis=1)
    s2 = jnp.concatenate([-jnp.sin(ang_r), zeros, -jnp.sin(ang_c), zeros], axis=1)

    def place(tab, fill):
        full = jnp.full((seq_len, LANES), fill, F32).at[:, lane0:lane0 + rot_dim].set(tab)
        ctx = jnp.full((n_ctx, LANES), fill, F32)
        return jnp.concatenate([ctx, full], axis=0)

    return place(cos, 1.0), place(s1, 0.0), place(s2, 0.0)


def _norm_rope_blocks(blocks, gains, n, cos, s1, s2, half):
    sums = [jnp.sum(x * x, axis=-1, keepdims=True) for x in blocks]
    inv = [lax.rsqrt(s * (1.0 / n) + EPS) for s in sums]
    normed = [x * r * g for x, r, g in zip(blocks, inv, gains)]
    fwd = [pltpu.roll(y, half, 1) for y in normed]
    bwd = [pltpu.roll(y, LANES - half, 1) for y in normed]
    return [y * cos + a * s1 + b * s2 for y, a, b in zip(normed, fwd, bwd)]


def _mla_prep_kernel(u_ref, gq_ref, wuq_ref, gkv_ref, wk_ref, wv_ref, gqn_ref, gkn_ref,
                     cos_ref, s1_ref, s2_ref, qt_ref, k_ref, vt_ref):
    u = u_ref[...]
    cq = u[:, :MLA_Q_RANK]
    ckv = u[:, MLA_Q_RANK:MLA_Q_RANK + MLA_KV_RANK]
    pe = u[:, MLA_Q_RANK + MLA_KV_RANK:]
    q = _dot((_rms_rows(cq, MLA_Q_RANK) * gq_ref[...]).astype(BF16), wuq_ref[...])
    kv_in = (_rms_rows(ckv, MLA_KV_RANK) * gkv_ref[...]).astype(BF16)
    kn = _dot(kv_in, wk_ref[...])
    v = _dot(kv_in, wv_ref[...])
    cos, s1, s2 = cos_ref[...], s1_ref[...], s2_ref[...]
    half = MLA_ROPE // 4
    q_scale = MLA_QK ** -0.5 * LOG2E
    n = MLA_HEADS
    blocks = ([q[:, LANES * h:LANES * (h + 1)] for h in range(n)]
              + [kn[:, LANES * h:LANES * (h + 1)] + pe for h in range(n)])
    roped = _norm_rope_blocks(blocks, [gqn_ref[...]] * n + [gkn_ref[...]] * n, MLA_QK, cos, s1, s2, half)
    for h in range(n):
        qt_ref[h] = (roped[h] * q_scale).T.astype(BF16)
        k_ref[h, 0] = roped[n + h].astype(BF16)
    tm = v.shape[0]
    vt_ref[:, 0] = _with_sum_rows(v.T.reshape(MLA_HEADS, MLA_V, tm)).astype(BF16)


def _mla_prep(u, gq, wuq, gkv, wk, wv, gqn, gkn, tabs):
    t = u.shape[0]
    tm = ROW_TILE
    nt = t // tm
    const = lambda i: (0, 0)
    rows = lambda i: (i, 0)
    return _Stage(
        kernel=_mla_prep_kernel,
        out_shape=(jax.ShapeDtypeStruct((MLA_HEADS, LANES, t), BF16),
                   jax.ShapeDtypeStruct((MLA_HEADS, nt, tm, LANES), BF16),
                   jax.ShapeDtypeStruct((MLA_HEADS, nt, MLA_V + SUM_ROWS, tm), BF16)),
        in_specs=[pl.BlockSpec((tm, 512), lambda i: (i, U_MLA // 512)),
                  pl.BlockSpec((1, MLA_Q_RANK), const),
                  pl.BlockSpec(wuq.shape, const),
                  pl.BlockSpec((1, MLA_KV_RANK), const),
                  pl.BlockSpec(wk.shape, const),
                  pl.BlockSpec(wv.shape, const),
                  pl.BlockSpec((1, LANES), const),
                  pl.BlockSpec((1, LANES), const),
                  pl.BlockSpec((tm, LANES), rows),
                  pl.BlockSpec((tm, LANES), rows),
                  pl.BlockSpec((tm, LANES), rows)],
        out_specs=(pl.BlockSpec((MLA_HEADS, LANES, tm), lambda i: (0, 0, i)),
                   pl.BlockSpec((MLA_HEADS, 1, tm, LANES), lambda i: (0, i, 0, 0)),
                   pl.BlockSpec((MLA_HEADS, 1, MLA_V + SUM_ROWS, tm), lambda i: (0, i, 0, 0))),
        scratch_shapes=(),
        args=(u, gq, wuq, gkv, wk, wv, gqn, gkn, *tabs))


def _dif_prep_kernel(q_ref, k_ref, v_ref, gq_ref, gk_ref, shift_ref, cos_ref, s1_ref, s2_ref,
                     qt_out, k_out, vt_out):
    cos, s1, s2 = cos_ref[...], s1_ref[...], s2_ref[...]
    half = DIF_HEAD_DIM // 4
    q_scale = DIF_HEAD_DIM ** -0.5 * LOG2E
    n = 2 * DIF_HEADS
    blocks = ([q_ref[:, LANES * a:LANES * (a + 1)] for a in range(n)]
              + [k_ref[:, LANES * a:LANES * (a + 1)] for a in range(n)])
    roped = _norm_rope_blocks(blocks, [gq_ref[...]] * n + [gk_ref[...]] * n, DIF_HEAD_DIM, cos, s1, s2, half)
    spare = lax.broadcasted_iota(jnp.int32, roped[0].shape, 1) == DIF_HEAD_DIM
    for a in range(n):
        qt_out[a] = jnp.where(spare, -shift_ref[...], roped[a] * q_scale).T.astype(BF16)
        k_out[a, 0] = jnp.where(spare, 1.0, roped[n + a]).astype(BF16)
    v = v_ref[...]
    vt_out[:, 0] = _with_sum_rows(v.T.reshape(DIF_HEADS, 2 * DIF_HEAD_DIM, v.shape[0])).astype(BF16)


def _dif_prep(u, gq, gk, shift, tabs):
    t = u.shape[0]
    tm = ROW_TILE
    nt = t // tm
    nsub = 2 * DIF_HEADS
    const = lambda i: (0, 0)
    rows = lambda i: (i, 0)
    return _Stage(
        kernel=_dif_prep_kernel,
        out_shape=(jax.ShapeDtypeStruct((nsub, LANES, t), BF16),
                   jax.ShapeDtypeStruct((nsub, nt, tm, LANES), BF16),
                   jax.ShapeDtypeStruct((DIF_HEADS, nt, 2 * DIF_HEAD_DIM + SUM_ROWS, tm), BF16)),
        in_specs=[pl.BlockSpec((tm, 1024), lambda i: (i, U_DQ // 1024)),
                  pl.BlockSpec((tm, 1024), lambda i: (i, U_DK // 1024)),
                  pl.BlockSpec((tm, 512), lambda i: (i, U_DV // 512)),
                  pl.BlockSpec((1, LANES), const),
                  pl.BlockSpec((1, LANES), const),
                  pl.BlockSpec((1, LANES), const),
                  pl.BlockSpec((tm, LANES), rows),
                  pl.BlockSpec((tm, LANES), rows),
                  pl.BlockSpec((tm, LANES), rows)],
        out_specs=(pl.BlockSpec((nsub, LANES, tm), lambda i: (0, 0, i)),
                   pl.BlockSpec((nsub, 1, tm, LANES), lambda i: (0, i, 0, 0)),
                   pl.BlockSpec((DIF_HEADS, 1, 2 * DIF_HEAD_DIM + SUM_ROWS, tm), lambda i: (0, i, 0, 0))),
        scratch_shapes=(),
        args=(u, u, u, gq, gk, shift, *tabs))


ATTN_GROUP = 2
ATTN_TRIP = 64
SUM_ROWS = 16


def _with_sum_rows(vt):
    heads, _, keys = vt.shape
    row = lax.broadcasted_iota(jnp.int32, (heads, SUM_ROWS, keys), 1)
    return jnp.concatenate([vt, jnp.where(row == 0, 1.0, 0.0).astype(vt.dtype)], axis=1)


def _attn_group(qt_ref, k_ref, vt_ref, v_of_sub, s_buf, p_buf, acc_ref, n_chunks):
    group, _, tq = qt_ref.shape
    last = n_chunks - 1

    def scores(c):
        out = []
        for a in range(group):
            s = _dot(k_ref[a, c], qt_ref[a])
            out.append((s, jnp.max(s, axis=0, keepdims=True)))
        return out

    def stash(sc, slot):
        for a in range(group):
            s_buf[slot, a] = sc[a][0]
        return tuple(mx for (_, mx) in sc)

    def softmax(s_of, mx, slot, ms):
        new_ms, alphas = [], []
        for a in range(group):
            m_new = jnp.maximum(ms[a], mx[a])
            alphas.append(jnp.exp2(ms[a] - m_new))
            p_buf[slot, a] = jnp.exp2(s_of(a) - m_new).astype(BF16)
            new_ms.append(m_new)
        return tuple(new_ms), tuple(alphas)

    def values(c, slot, alphas):
        for a in range(group):
            acc_ref[a] = alphas[a] * acc_ref[a] + _dot(vt_ref[v_of_sub[a], c], p_buf[slot, a])

    per_trip = math.gcd(last, ATTN_TRIP)

    def trip(t, state):
        ms, alphas, mx0 = state
        c = per_trip * t + 1
        for pair in range(per_trip // 2):
            cur, nxt = pair % 2, 1 - pair % 2
            even = scores(c + 1)
            values(c - 1, 0, alphas)
            ms, alphas = softmax(lambda a: s_buf[cur, a], mx0, 1, ms)
            mx0 = stash(scores(jnp.minimum(c + 2, last)), nxt)
            values(c, 1, alphas)
            ms, alphas = softmax(lambda a: even[a][0], [mx for (_, mx) in even], 0, ms)
            c = c + 2
        return ms, alphas, mx0

    def context_chunk():
        ms = tuple(jnp.full((1, tq), -jnp.inf, F32) for _ in range(group))
        first = scores(0)
        return softmax(lambda a: first[a][0], [mx for (_, mx) in first], 0, ms)

    acc_ref[...] = jnp.zeros(acc_ref.shape, F32)

    @pl.when(pl.program_id(1) == 0)
    def _():
        _, alphas = context_chunk()
        values(0, 0, alphas)

    @pl.when(pl.program_id(1) > 0)
    def _():
        ms, alphas = context_chunk()
        state = (ms, alphas, stash(scores(1), 0))
        if per_trip == last:
            _, alphas, _ = trip(0, state)
        else:
            _, alphas, _ = lax.fori_loop(0, last // per_trip, trip, state)
        values(last, 0, alphas)


def _normalised(acc_ref, a):
    dv = acc_ref.shape[1] - SUM_ROWS
    return acc_ref[a, :dv] * (1.0 / acc_ref[a, dv:dv + 1])


PV_SPAN_WORK = 64
SHIFTED_GROUP = 4
MAX_SHIFT = 60.0


def _attn_group_shifted(qt_ref, k_ref, vt_ref, v_of_sub, p_buf, acc_ref, n_chunks):
    group = qt_ref.shape[0]
    span = _value_span(n_chunks, group)

    def accumulate(c0, n, slot):
        for a in range(group):
            for i in range(n):
                s = _dot(k_ref[a, c0 + i], qt_ref[a])
                p_buf[slot, a, KEY_CHUNK * i:KEY_CHUNK * (i + 1)] = jnp.exp2(s).astype(BF16)
            vt = jnp.concatenate([vt_ref[v_of_sub[a], c0 + i] for i in range(n)], axis=1)
            acc_ref[a] = acc_ref[a] + _dot(vt, p_buf[slot, a, :KEY_CHUNK * n])

    acc_ref[...] = jnp.zeros(acc_ref.shape, F32)

    @pl.when(pl.program_id(1) == 0)
    def _():
        accumulate(0, 1, 0)

    @pl.when(pl.program_id(1) > 0)
    def _():
        accumulate(0, span + 1, 0)
        for j in range(1, (n_chunks - 1) // span):
            accumulate(1 + span * j, span, j % 2)


def _attend(shifted, qt_ref, k_ref, vt_ref, v_of_sub, scratch, n_chunks):
    if shifted:
        p_buf, acc_ref = scratch
        _attn_group_shifted(qt_ref, k_ref, vt_ref, v_of_sub, p_buf, acc_ref, n_chunks)
    else:
        s_buf, p_buf, acc_ref = scratch
        _attn_group(qt_ref, k_ref, vt_ref, v_of_sub, s_buf, p_buf, acc_ref, n_chunks)
    return acc_ref


def _value_span(n_chunks, group):
    return math.gcd(n_chunks - 1, PV_SPAN_WORK // group)


def _mla_attn_kernel(qt_ref, k_ref, vt_ref, o_ref, *scratch, n_chunks, shifted):
    group = qt_ref.shape[0]
    acc_ref = _attend(shifted, qt_ref, k_ref, vt_ref, tuple(range(group)), scratch, n_chunks)
    o = jnp.concatenate([_normalised(acc_ref, a) for a in range(group)], axis=0)
    o_ref[...] = o.T


def _dif_attn_kernel(lam_ref, gsub_ref, qt_ref, k_ref, vt_ref, o_ref, *scratch, n_chunks, shifted, lam_init):
    group = qt_ref.shape[0]
    acc_ref = _attend(shifted, qt_ref, k_ref, vt_ref, tuple(a // 2 for a in range(group)), scratch,
                      n_chunks)
    lp = lam_ref[...]
    lam = (jnp.exp(jnp.sum(lp[0:1] * lp[1:2], axis=-1, keepdims=True))
           - jnp.exp(jnp.sum(lp[2:3] * lp[3:4], axis=-1, keepdims=True)) + lam_init)
    outs = []
    for h in range(group // 2):
        o = _normalised(acc_ref, 2 * h) - lam * _normalised(acc_ref, 2 * h + 1)
        outs.append(o * lax.rsqrt(jnp.mean(o * o, axis=0, keepdims=True) + EPS))
    gsub = gsub_ref[...] * (1.0 - lam_init)
    o_ref[...] = jnp.concatenate(outs, axis=0).T * jnp.concatenate([gsub] * len(outs), axis=1)


def _attention(qt, k, vt, *, group=ATTN_GROUP, shifted=False, lam=None, gsub=None, lam_init=None):
    nsub, _, t = qt.shape
    n_chunks = k.shape[1]
    assert (n_chunks - 1) % 4 == 0 and nsub % group == 0
    tq = KEY_CHUNK
    dv = vt.shape[2] - SUM_ROWS
    groups = nsub // group
    n_v = vt.shape[0] // groups
    out_w = n_v * dv
    resident = dict(pipeline_mode=pl.Buffered(1))
    specs = [pl.BlockSpec((group, LANES, tq), lambda g, i: (g, 0, i)),
             pl.BlockSpec((group, n_chunks, KEY_CHUNK, LANES), lambda g, i: (g, 0, 0, 0), **resident),
             pl.BlockSpec((n_v, n_chunks, dv + SUM_ROWS, KEY_CHUNK), lambda g, i: (g, 0, 0, 0), **resident)]
    if lam is None:
        body = functools.partial(_mla_attn_kernel, n_chunks=n_chunks, shifted=shifted)
        args = (qt, k, vt)
    else:
        body = functools.partial(_dif_attn_kernel, n_chunks=n_chunks, shifted=shifted, lam_init=lam_init)
        specs = [pl.BlockSpec((8, LANES), lambda g, i: (0, 0)),
                 pl.BlockSpec((1, LANES), lambda g, i: (0, 0))] + specs
        args = (lam, gsub, qt, k, vt)
    acc = pltpu.VMEM((group, dv + SUM_ROWS, tq), F32)
    if shifted:
        scratch = [pltpu.VMEM((2, group, (_value_span(n_chunks, group) + 1) * KEY_CHUNK, tq), BF16), acc]
    else:
        scratch = [pltpu.VMEM((2, group, KEY_CHUNK, tq), F32), pltpu.VMEM((2, group, KEY_CHUNK, tq), BF16), acc]
    return _pallas_call(
        body,
        out_shape=jax.ShapeDtypeStruct((t, groups * out_w), F32),
        grid=(groups, t // tq),
        in_specs=specs,
        out_specs=pl.BlockSpec((tq, out_w), lambda g, i: (i, g)),
        scratch_shapes=scratch,
        semantics=("parallel", "arbitrary"),
        name=("mla_attention" if lam is None else "dif_attention") + ("_shifted" if shifted else ""),
    )(*args)


def _ssm_prep_kernel(x_ref, prev_ref, next_ref, dt_ref, w_ref, b_ref, dtb_ref, xo_ref, dto_ref, e_ref,
                     *, n_ctx, n_tok, tm):
    row0 = pl.program_id(0) * tm
    pad = SSM_CONV // 2
    has_prev = jnp.logical_and(row0 != 0, row0 != n_ctx)
    has_next = jnp.logical_and(row0 + tm != n_ctx, row0 + tm != n_tok)
    e_ref[0:8] = jnp.where(has_prev, prev_ref[...], 0.0)
    e_ref[8:8 + tm] = x_ref[...]
    e_ref[8 + tm:16 + tm] = jnp.where(has_next, next_ref[...], 0.0)
    acc = jnp.zeros(x_ref.shape, F32) + b_ref[...]
    for k in range(SSM_CONV):
        acc = acc + w_ref[k:k + 1, :] * e_ref[pl.ds(8 - pad + k, tm), :]
    xo_ref[...] = _silu(acc)
    d = dt_ref[...] + dtb_ref[...]
    dto_ref[...] = jnp.maximum(d, 0.0) + jnp.log1p(jnp.exp(-jnp.abs(d)))


def _ssm_prep(u, conv_w, conv_b, dt_bias, n_ctx):
    t = u.shape[0]
    tm = ROW_TILE
    nt = t // tm
    cb = U_XBC // SSM_XBC
    const = lambda i: (0, 0)
    return _Stage(
        kernel=functools.partial(_ssm_prep_kernel, n_ctx=n_ctx, n_tok=t, tm=tm),
        out_shape=(jax.ShapeDtypeStruct((t, SSM_XBC), F32), jax.ShapeDtypeStruct((t, LANES), F32)),
        in_specs=[pl.BlockSpec((tm, SSM_XBC), lambda i: (i, cb)),
                  pl.BlockSpec((8, SSM_XBC), lambda i: (jnp.maximum(i * (tm // 8) - 1, 0), cb)),
                  pl.BlockSpec((8, SSM_XBC), lambda i: (jnp.minimum((i + 1) * (tm // 8), t // 8 - 1), cb)),
                  pl.BlockSpec((tm, LANES), lambda i: (i, U_DT // LANES)),
                  pl.BlockSpec((8, SSM_XBC), const),
                  pl.BlockSpec((1, SSM_XBC), const),
                  pl.BlockSpec((1, LANES), const)],
        out_specs=(pl.BlockSpec((tm, SSM_XBC), lambda i: (i, 0)),
                   pl.BlockSpec((tm, LANES), lambda i: (i, 0))),
        scratch_shapes=(pltpu.VMEM((tm + 16, SSM_XBC), F32),),
        args=(u, u, u, u, conv_w, conv_b, dt_bias))


def _ssd_kernel(xf_ref, dtf_ref, xr_ref, dtr_ref, alog_ref, of_ref, or_ref, hf_ref, hr_ref):
    @pl.when(pl.program_id(0) == 0)
    def _():
        hf_ref[...] = jnp.zeros(hf_ref.shape, F32)
        hr_ref[...] = jnp.zeros(hr_ref.shape, F32)

    lc = SSM_CHUNK
    a_row = -jnp.exp(alog_ref[...])
    n_sub = xf_ref.shape[0] // lc
    fwd = [_ssd_local(xf_ref[lc * j:lc * (j + 1), :], dtf_ref[lc * j:lc * (j + 1), :], a_row, 0)
           for j in range(n_sub)]
    bwd = [_ssd_local(xr_ref[lc * j:lc * (j + 1), :], dtr_ref[lc * j:lc * (j + 1), :], a_row, 1)
           for j in range(n_sub)]
    of_ref[...] = jnp.concatenate([_ssd_carry(part, hf_ref) for part in fwd], axis=0)
    or_ref[...] = jnp.concatenate([_ssd_carry(part, hr_ref) for part in reversed(bwd)][::-1], axis=0)


def _ssd_carry(local, h_ref):
    outs = []
    for h, (y, c_g, from_start, decay, upd) in enumerate(local):
        state = h_ref[h]
        outs.append(y + _dot(c_g, state.astype(BF16)) * from_start)
        h_ref[h] = state * decay + upd
    return jnp.concatenate(outs, axis=1)


def _ssd_local(xbc, dt, a_row, direction):
    lc = SSM_CHUNK
    hd, per_group = SSM_HEAD_DIM, SSM_HEADS // SSM_GROUPS
    dta = dt * a_row
    r = lax.broadcasted_iota(jnp.int32, (lc, lc), 0)
    c = lax.broadcasted_iota(jnp.int32, (lc, lc), 1)
    keep = (r >= c) if direction == 0 else (r <= c)
    tri = jnp.where(keep, 1.0, 0.0).astype(F32)
    cum = jnp.dot(tri, dta, preferred_element_type=F32, precision=lax.Precision.HIGHEST)
    cum_t = cum.T
    dt_t = dt.T
    total = jnp.sum(dta, axis=0, keepdims=True)
    chunk_decay = jnp.exp(total)
    w_t = (dt * jnp.exp(total - cum)).T
    b_f32 = [xbc[:, SSM_WIDTH + SSM_STATE * g:SSM_WIDTH + SSM_STATE * (g + 1)] for g in range(SSM_GROUPS)]
    c_bf = [xbc[:, SSM_WIDTH + SSM_STATE * (SSM_GROUPS + g):SSM_WIDTH + SSM_STATE * (SSM_GROUPS + g + 1)].astype(BF16)
            for g in range(SSM_GROUPS)]
    cb = [_dot_nt(c, b.astype(BF16)) for c, b in zip(c_bf, b_f32)]
    b_t = [b.T for b in b_f32]
    heads = range(SSM_HEADS)
    cols = [direction * SSM_HEADS + h for h in heads]
    cum_b = [jnp.broadcast_to(cum[:, c:c + 1], (lc, lc)) for c in cols]
    seg = [jnp.exp(jnp.where(keep, cb_ - cum_t[c:c + 1, :], -jnp.inf)) for cb_, c in zip(cum_b, cols)]
    x_h = [xbc[:, hd * h:hd * (h + 1)].astype(BF16) for h in heads]
    scores = [(cb[h // per_group] * s * dt_t[c:c + 1, :]).astype(BF16) for h, s, c in zip(heads, seg, cols)]
    weighted_b = [(b_t[h // per_group] * w_t[c:c + 1, :]).astype(BF16) for h, c in zip(heads, cols)]
    y = [_dot(s, x) for s, x in zip(scores, x_h)]
    upd = [_dot(wb, x) for wb, x in zip(weighted_b, x_h)]
    return [(y[h], c_bf[h // per_group], jnp.exp(cum_b[h][:, :hd]), chunk_decay[:, cols[h]:cols[h] + 1], upd[h])
            for h in heads]


SSD_STEP = 2 * SSM_CHUNK


def _ssd(xbc, dt, a_log, n_ctx):
    t = xbc.shape[0]
    lc = SSD_STEP
    assert n_ctx % lc == 0 and t % lc == 0
    nc = t // lc
    ncc = n_ctx // lc
    fwd = lambda s: (s, 0)
    bwd = lambda s: (jnp.where(s < ncc, ncc - 1 - s, nc - 1 - (s - ncc)), 0)
    state = pltpu.VMEM((SSM_HEADS, SSM_STATE, SSM_HEAD_DIM), F32)
    out = jax.ShapeDtypeStruct((t, SSM_WIDTH), F32)
    return _pallas_call(
        _ssd_kernel,
        out_shape=(out, out),
        grid=(nc,),
        in_specs=[pl.BlockSpec((lc, SSM_XBC), fwd), pl.BlockSpec((lc, LANES), fwd),
                  pl.BlockSpec((lc, SSM_XBC), bwd), pl.BlockSpec((lc, LANES), bwd),
                  pl.BlockSpec((1, LANES), lambda s: (0, 0))],
        out_specs=(pl.BlockSpec((lc, SSM_WIDTH), fwd), pl.BlockSpec((lc, SSM_WIDTH), bwd)),
        scratch_shapes=[state, state],
        semantics=("arbitrary",),
        name="ssd_scan",
    )(xbc, dt, xbc, dt, a_log)


def _merge_kernel(x_ref, gate_ref, ya_ref, yb_ref, yf_ref, yr_ref, xs_ref, z_ref,
                  bg_ref, dskip_ref, gssm_ref, wa_ref, wb_ref, wc_ref, wo_ref,
                  ml_ref, mc_ref, g2_ref, wr_ref, br_ref,
                  xo_ref, f_ref, lg_ref, *, n_ctx, tm):
    y = (yf_ref[...] + yr_ref[...] + dskip_ref[...] * xs_ref[...]) * _silu(z_ref[...])
    gw = SSM_WIDTH // SSM_GROUPS
    yc = jnp.concatenate([_rms_rows(y[:, gw * g:gw * (g + 1)], gw) for g in range(SSM_GROUPS)], axis=1)
    yc = yc * gssm_ref[...]
    gate = _sigmoid(gate_ref[...] + bg_ref[...])
    m = (gate[:, :D_MODEL] * _dot(ya_ref[...].astype(BF16), wa_ref[...])
         + gate[:, D_MODEL:2 * D_MODEL] * _dot(yb_ref[...].astype(BF16), wb_ref[...])
         + gate[:, 2 * D_MODEL:] * _dot(yc.astype(BF16), wc_ref[...]))
    out = _dot(m.astype(BF16), wo_ref[...])
    row0 = pl.program_id(0) * tm
    is_ctx = (row0 + lax.broadcasted_iota(jnp.int32, (tm, 1), 0)) < n_ctx
    gt1 = jnp.where(is_ctx, mc_ref[2:3, :], ml_ref[2:3, :])
    x_new = x_ref[...] + gt1 * out
    xo_ref[...] = x_new
    f = _modulated_norm(x_new, g2_ref[...], ml_ref[3:5, :], mc_ref[3:5, :], row0, n_ctx)
    f_ref[...] = f
    lg_ref[...] = _dot(f.astype(BF16), wr_ref[...]) + br_ref[...]


def _merge(x_all, u, ya, yb, yf, yr, xbc, b_gate, dskip, g_ssm, wa, wb, wc, wo, mod_lat, mod_ctx,
           g2, w_router, b_router, n_ctx):
    t, d = x_all.shape
    tm = ROW_TILE
    const = lambda i: (0, 0)
    rows = lambda i: (i, 0)
    full = lambda a: pl.BlockSpec(a.shape, const)
    return _pallas_call(
        functools.partial(_merge_kernel, n_ctx=n_ctx, tm=tm),
        out_shape=(jax.ShapeDtypeStruct((t, d), F32), jax.ShapeDtypeStruct((t, d), F32),
                   jax.ShapeDtypeStruct((t, LANES), F32)),
        grid=(t // tm,),
        in_specs=[pl.BlockSpec((tm, d), rows),
                  pl.BlockSpec((tm, GATE_COLS), lambda i: (i, U_GATE // GATE_COLS)),
                  pl.BlockSpec((tm, MLA_WIDTH), rows),
                  pl.BlockSpec((tm, DIF_WIDTH), rows),
                  pl.BlockSpec((tm, SSM_WIDTH), rows),
                  pl.BlockSpec((tm, SSM_WIDTH), rows),
                  pl.BlockSpec((tm, SSM_WIDTH), rows),
                  pl.BlockSpec((tm, SSM_WIDTH), lambda i: (i, U_Z // SSM_WIDTH)),
                  full(b_gate), full(dskip), full(g_ssm), full(wa), full(wb), full(wc), full(wo),
                  full(mod_lat), full(mod_ctx), full(g2), full(w_router), full(b_router)],
        out_specs=(pl.BlockSpec((tm, d), rows), pl.BlockSpec((tm, d), rows),
                   pl.BlockSpec((tm, LANES), rows)),
        semantics=("parallel",),
        name="merge",
    )(x_all, u, ya, yb, yf, yr, xbc, u, b_gate, dskip, g_ssm, wa, wb, wc, wo, mod_lat, mod_ctx,
      g2, w_router, b_router)


def _moe_kernel(be_ref, nb_ref, x_ref, wgu_ref, bgu_ref, wd_ref, bd_ref, o_ref, wgu_s, wd_s):
    b = pl.program_id(0)
    prev = be_ref[jnp.maximum(b - 1, 0)]
    fresh = jnp.logical_or(b == 0, be_ref[b] != prev)

    @pl.when(fresh)
    def _():
        wgu_s[...] = wgu_ref[...].astype(BF16)
        wd_s[...] = wd_ref[...].astype(BF16)

    @pl.when(b < nb_ref[0])
    def _():
        gu = _dot(x_ref[...].astype(BF16), wgu_s[...]) + bgu_ref[...]
        glu = jnp.minimum(gu[:, :D_FF], SWIGLU_LIMIT)
        lin = jnp.clip(gu[:, D_FF:], -SWIGLU_LIMIT, SWIGLU_LIMIT)
        act = glu * _sigmoid(SWIGLU_ALPHA * glu) * (lin + 1.0)
        o_ref[...] = _dot(act.astype(BF16), wd_s[...]) + bd_ref[...]

    @pl.when(b >= nb_ref[0])
    def _():
        o_ref[...] = jnp.zeros(o_ref.shape, F32)


def _moe_experts(block_e, n_used, x_sorted, w_gu, b_gu, w_down, b_down, layer):
    n_slots, d = x_sorted.shape
    n_blocks = n_slots // MOE_BLOCK
    by_expert = lambda b, be, nb: (layer, be[b], 0, 0)
    return _pallas_call(
        _moe_kernel,
        out_shape=jax.ShapeDtypeStruct((n_slots, d), F32),
        num_scalar_prefetch=2,
        grid=(n_blocks,),
        in_specs=[pl.BlockSpec((MOE_BLOCK, d), lambda b, be, nb: (b, 0)),
                  pl.BlockSpec((None, None, d, 2 * D_FF), by_expert),
                  pl.BlockSpec((None, None, 1, 2 * D_FF), by_expert),
                  pl.BlockSpec((None, None, D_FF, d), by_expert),
                  pl.BlockSpec((None, None, 1, d), by_expert)],
        out_specs=pl.BlockSpec((MOE_BLOCK, d), lambda b, be, nb: (b, 0)),
        scratch_shapes=[pltpu.VMEM((d, 2 * D_FF), BF16), pltpu.VMEM((D_FF, d), BF16)],
        semantics=("arbitrary",),
        name="moe_experts",
    )(block_e, n_used, x_sorted, w_gu, b_gu, w_down, b_down)


def _combine_kernel(x_ref, g_ref, ml_ref, mc_ref, *rest, n_ctx, tm):
    y_refs, o_ref = rest[:TOP_K], rest[TOP_K]
    is_ctx = (pl.program_id(0) * tm + lax.broadcasted_iota(jnp.int32, (tm, 1), 0)) < n_ctx
    gate2 = jnp.where(is_ctx, mc_ref[...], ml_ref[...])
    g = g_ref[...]
    y = g[:, 0:1] * y_refs[0][...]
    for k in range(1, TOP_K):
        y = y + g[:, k:k + 1] * y_refs[k][...]
    o_ref[...] = x_ref[...] + gate2 * y


def _combine(x_all, y_rows, gates, gate2_lat, gate2_ctx, n_ctx):
    t, d = x_all.shape
    tm = ROW_TILE
    rows = lambda i: (i, 0)
    const = lambda i: (0, 0)
    return _pallas_call(
        functools.partial(_combine_kernel, n_ctx=n_ctx, tm=tm),
        out_shape=jax.ShapeDtypeStruct((t, d), F32),
        grid=(t // tm,),
        in_specs=[pl.BlockSpec((tm, d), rows), pl.BlockSpec((tm, LANES), rows),
                  pl.BlockSpec((1, d), const), pl.BlockSpec((1, d), const)]
                 + [pl.BlockSpec((tm, d), rows)] * TOP_K,
        out_specs=pl.BlockSpec((tm, d), rows),
        semantics=("parallel",),
        name="moe_combine",
    )(x_all, gates, gate2_lat, gate2_ctx, *y_rows)


ROUTE_IDX, ROUTE_GATE, ROUTE_RANK = 0, TOP_K, 2 * TOP_K


def _router_kernel(lg_ref, o_ref, cnt_ref, run_ref, *, tm):
    @pl.when(pl.program_id(0) == 0)
    def _():
        run_ref[...] = jnp.zeros(run_ref.shape, F32)

    lane = lax.broadcasted_iota(jnp.int32, (tm, LANES), 1)
    lane_f = lane.astype(F32)
    lg = jnp.where(lane < N_EXPERTS, lg_ref[...], -jnp.inf)
    hots, vals = [], []
    for _ in range(TOP_K):
        mx = jnp.max(lg, axis=-1, keepdims=True)
        idx = jnp.min(jnp.where(lg == mx, lane_f, float(LANES)), axis=-1, keepdims=True)
        hot = lane_f == idx
        lg = jnp.where(hot, -jnp.inf, lg)
        hots.append((hot, idx))
        vals.append(mx)
    exps = [jnp.exp(v - vals[0]) for v in vals]
    inv = 1.0 / sum(exps[1:], exps[0])
    chosen = jnp.zeros((tm, LANES), F32)
    for hot, _ in hots:
        chosen = jnp.where(hot, 1.0, chosen)
    r = lax.broadcasted_iota(jnp.int32, (tm, tm), 0)
    c = lax.broadcasted_iota(jnp.int32, (tm, tm), 1)
    earlier = jnp.where(r > c, 1.0, 0.0).astype(BF16)
    before = _dot(earlier, chosen.astype(BF16)) + run_ref[0:1, :]
    out = jnp.zeros((tm, LANES), F32)
    for k, (hot, idx) in enumerate(hots):
        rank = jnp.sum(jnp.where(hot, before, 0.0), axis=-1, keepdims=True)
        out = jnp.where(lane == ROUTE_IDX + k, idx, out)
        out = jnp.where(lane == ROUTE_GATE + k, exps[k] * inv, out)
        out = jnp.where(lane == ROUTE_RANK + k, rank, out)
    o_ref[...] = out
    run_ref[...] = run_ref[...] + jnp.sum(chosen, axis=0, keepdims=True)
    cnt_ref[...] = run_ref[...]


def _router(logits):
    t = logits.shape[0]
    tm = ROW_TILE
    return _pallas_call(
        functools.partial(_router_kernel, tm=tm),
        out_shape=(jax.ShapeDtypeStruct((t, LANES), F32), jax.ShapeDtypeStruct((8, LANES), F32)),
        grid=(t // tm,),
        in_specs=[pl.BlockSpec((tm, LANES), lambda i: (i, 0))],
        out_specs=(pl.BlockSpec((tm, LANES), lambda i: (i, 0)), pl.BlockSpec((8, LANES), lambda i: (0, 0))),
        scratch_shapes=[pltpu.VMEM((8, LANES), F32)],
        semantics=("arbitrary",),
        name="moe_router",
    )(logits)


def _route(logits, n_tok):
    routed, counts = _router(logits)
    top_idx = routed[:, ROUTE_IDX:ROUTE_IDX + TOP_K].astype(jnp.int32)
    gates = routed[:, ROUTE_GATE:ROUTE_GATE + TOP_K]
    rank = routed[:, ROUTE_RANK:ROUTE_RANK + TOP_K].astype(jnp.int32)
    n_assign = n_tok * TOP_K
    counts = counts[0, :N_EXPERTS].astype(jnp.int32)
    padded = (counts + MOE_BLOCK - 1) // MOE_BLOCK * MOE_BLOCK
    pad_end = jnp.cumsum(padded)
    pad_start = pad_end - padded
    slot_of = pad_start[top_idx] + rank
    n_blocks = -(-(n_assign + N_EXPERTS * (MOE_BLOCK - 1)) // MOE_BLOCK)
    n_slots = n_blocks * MOE_BLOCK
    token = jnp.arange(n_assign, dtype=jnp.int32) // TOP_K
    slot_tok = jnp.zeros((n_slots,), jnp.int32).at[slot_of.reshape(-1)].set(token, unique_indices=True)
    block_start = jnp.arange(n_blocks, dtype=jnp.int32) * MOE_BLOCK
    block_e = jnp.minimum(jnp.searchsorted(pad_end, block_start, side='right'), N_EXPERTS - 1)
    n_used = (pad_end[-1] // MOE_BLOCK).reshape(1)
    return gates, slot_tok, slot_of, block_e.astype(jnp.int32), n_used.astype(jnp.int32)


def kernel(x, c, ctx, c_ctx, w_mod, b_mod, g_norm1, g_norm2, w_in, b_gate, mla_g_q, mla_w_uq, mla_g_kv, mla_w_ukv, mla_g_qn, mla_g_kn, dif_g_qn, dif_g_kn, dif_lambda, dif_g_sub, ssm_conv_w, ssm_conv_b, ssm_dt_bias, ssm_a_log, ssm_d, ssm_g_norm, w_up_mla, w_up_dif, w_up_ssm, w_out, moe_w_router, moe_b_router, moe_w_gu, moe_b_gu, moe_w_down, moe_b_down):
    assert x.shape[0] == 1 and ctx.shape[0] == 1
    depth = w_in.shape[0]
    seq = x.shape[1]
    n_ctx = ctx.shape[1]
    n_tok = n_ctx + seq
    d = D_MODEL
    assert n_ctx == KEY_CHUNK and n_tok % ROW_TILE == 0 and seq % GRID_W == 0

    x_all = jnp.concatenate([ctx[0], x[0]], axis=0)
    cc = jnp.zeros((8, d), F32).at[0].set(c[0]).at[1].set(c_ctx)
    mod = _mod_vectors(cc, w_mod, b_mod)
    mod = mod[:, :2].reshape(depth, 2, 6, d)

    rope_mla = _rope_tables(seq, n_ctx, MLA_ROPE, MLA_NOPE)
    rope_dif = _rope_tables(seq, n_ctx, DIF_HEAD_DIM, 0)
    w_in_all = _take_columns(w_in, _in_proj_columns()).astype(BF16)
    w_uq_all = _take_columns(mla_w_uq, _head_columns(MLA_HEADS, MLA_QK, 0, MLA_QK, LANES)).astype(BF16)
    w_uk_all = _take_columns(mla_w_ukv, _head_columns(MLA_HEADS, MLA_NOPE + MLA_V, 0, MLA_NOPE, LANES)).astype(BF16)
    w_uv_all = _take_columns(mla_w_ukv, _head_columns(MLA_HEADS, MLA_NOPE + MLA_V, MLA_NOPE, MLA_V, MLA_V)).astype(BF16)
    w_up_all = [w.astype(BF16) for w in (w_up_mla, w_up_dif, w_up_ssm, w_out)]

    for i in range(depth):
        lam_init = 0.8 - 0.6 * math.exp(-0.3 * i)
        mod_lat, mod_ctx = mod[i, 0], mod[i, 1]
        u = _in_proj(x_all, g_norm1[i][None], mod_lat[0:2], mod_ctx[0:2], w_in_all, i, n_ctx)

        bound = (DIF_HEAD_DIM ** 0.5 * LOG2E) * jnp.max(jnp.abs(dif_g_qn[i])) * jnp.max(jnp.abs(dif_g_kn[i]))
        conv_w = jnp.zeros((8, SSM_XBC), F32).at[:SSM_CONV].set(ssm_conv_w[i])
        mla_qkv, dif_qkv, (xbc, dt) = _run_stages("branch_prep", (n_tok // ROW_TILE,), [
            _mla_prep(u, mla_g_q[i][None], w_uq_all[i], mla_g_kv[i][None], w_uk_all[i], w_uv_all[i],
                      _pad_lanes(mla_g_qn[i][None]), _pad_lanes(mla_g_kn[i][None]), rope_mla),
            _dif_prep(u, _pad_lanes(dif_g_qn[i][None]), _pad_lanes(dif_g_kn[i][None]),
                      jnp.full((1, LANES), bound, F32), rope_dif),
            _ssm_prep(u, conv_w, ssm_conv_b[i][None], _pad_lanes(ssm_dt_bias[i].reshape(1, -1)), n_ctx)])
        ya = _attention(*mla_qkv)

        lam_rows = jnp.zeros((8, LANES), F32).at[:4, :DIF_HEAD_DIM].set(dif_lambda[i])
        dif_attention = functools.partial(_attention, lam=lam_rows, gsub=dif_g_sub[i][None], lam_init=lam_init)
        yb = lax.cond(bound <= MAX_SHIFT, functools.partial(dif_attention, shifted=True, group=SHIFTED_GROUP),
                      functools.partial(dif_attention, shifted=False), *dif_qkv)

        a_log = _pad_lanes(ssm_a_log[i].reshape(1, -1))
        yf, yr = _ssd(xbc, dt, a_log, n_ctx)

        dskip = jnp.repeat(ssm_d[i, 0] + ssm_d[i, 1], SSM_HEAD_DIM)[None]
        w_router = jnp.zeros((d, LANES), BF16).at[:, :N_EXPERTS].set(moe_w_router[i].astype(BF16))
        b_router = jnp.zeros((1, LANES), F32).at[0, :N_EXPERTS].set(moe_b_router[i])
        x_all, f, logits = _merge(
            x_all, u, ya, yb, yf, yr, xbc, b_gate[i][None], dskip, ssm_g_norm[i][None],
            w_up_all[0][i], w_up_all[1][i], w_up_all[2][i], w_up_all[3][i],
            mod_lat[0:5], mod_ctx[0:5], g_norm2[i][None],
            w_router, b_router, n_ctx)

        gates, slot_tok, slot_of, block_e, n_used = _route(logits, n_tok)
        y_slots = _moe_experts(block_e, n_used, f[slot_tok], moe_w_gu,
                               moe_b_gu.reshape(depth, N_EXPERTS, 1, 2 * D_FF), moe_w_down,
                               moe_b_down.reshape(depth, N_EXPERTS, 1, d), i)
        y_rows = [y_slots[slot_of[:, k]] for k in range(TOP_K)]
        x_all = _combine(x_all, y_rows, _pad_lanes(gates), mod_lat[5:6], mod_ctx[5:6], n_ctx)
    return x_all[n_ctx:][None]
```

```python
import collections
import functools
import math

import numpy as np
import jax
import jax.numpy as jnp
from jax import lax
from jax.experimental import pallas as pl
from jax.experimental.pallas import tpu as pltpu

F32 = jnp.float32
BF16 = jnp.bfloat16
LANES = 128
VMEM_REQUEST_CAP = 56 * 1024 * 1024
VMEM_TEMPORARIES = 16 * 1024 * 1024

D_MODEL = 1024
EPS = 1e-6
ROPE_THETA = 10000.0
GRID_W = 64
N_BRANCH = 3

MLA_HEADS = 8
MLA_Q_RANK = 256
MLA_KV_RANK = 128
MLA_NOPE = 64
MLA_ROPE = 32
MLA_V = 64
MLA_QK = MLA_NOPE + MLA_ROPE
MLA_WIDTH = MLA_HEADS * MLA_V

DIF_HEADS = 4
DIF_HEAD_DIM = 64
DIF_WIDTH = DIF_HEADS * 2 * DIF_HEAD_DIM

SSM_HEADS = 8
SSM_HEAD_DIM = 64
SSM_WIDTH = SSM_HEADS * SSM_HEAD_DIM
SSM_GROUPS = 2
SSM_STATE = 128
SSM_CONV = 5
SSM_CHUNK = 128
SSM_XBC = SSM_WIDTH + 2 * SSM_GROUPS * SSM_STATE

N_EXPERTS = 32
TOP_K = 4
D_FF = 1024
SWIGLU_LIMIT = 7.0
SWIGLU_ALPHA = 1.702
MOE_BLOCK = 512

MLA_COLS = MLA_Q_RANK + MLA_KV_RANK + MLA_ROPE
DIF_COLS = 3 * DIF_WIDTH
SSM_COLS = SSM_WIDTH + SSM_XBC + 2 * SSM_HEADS
GATE_COLS = N_BRANCH * D_MODEL

ROW_TILE = 256
KEY_CHUNK = 256
LOG2E = 1.4426950408889634

U_GATE, U_DQ, U_DK, U_XBC, U_MLA, U_DV, U_Z, U_DT = 0, 3072, 4096, 5120, 6144, 6656, 7168, 7680
U_COLS = 8192
U_TILE_N = 2048


def _in_proj_columns():
    src = np.full((U_COLS,), -1, np.int64)
    dif0 = MLA_COLS
    ssm0 = MLA_COLS + DIF_COLS
    gate0 = ssm0 + SSM_COLS
    src[U_GATE:U_GATE + GATE_COLS] = gate0 + np.arange(GATE_COLS)
    for a in range(2 * DIF_HEADS):
        src[U_DQ + LANES * a:U_DQ + LANES * a + DIF_HEAD_DIM] = dif0 + DIF_HEAD_DIM * a + np.arange(DIF_HEAD_DIM)
        src[U_DK + LANES * a:U_DK + LANES * a + DIF_HEAD_DIM] = (dif0 + DIF_WIDTH + DIF_HEAD_DIM * a
                                                                  + np.arange(DIF_HEAD_DIM))
    src[U_DV:U_DV + DIF_WIDTH] = dif0 + 2 * DIF_WIDTH + np.arange(DIF_WIDTH)
    src[U_MLA:U_MLA + MLA_Q_RANK + MLA_KV_RANK] = np.arange(MLA_Q_RANK + MLA_KV_RANK)
    pe0 = U_MLA + MLA_Q_RANK + MLA_KV_RANK + MLA_NOPE
    src[pe0:pe0 + MLA_ROPE] = MLA_Q_RANK + MLA_KV_RANK + np.arange(MLA_ROPE)
    src[U_Z:U_Z + SSM_WIDTH] = ssm0 + np.arange(SSM_WIDTH)
    src[U_XBC:U_XBC + SSM_XBC] = ssm0 + SSM_WIDTH + np.arange(SSM_XBC)
    src[U_DT:U_DT + 2 * SSM_HEADS] = ssm0 + SSM_WIDTH + SSM_XBC + np.arange(2 * SSM_HEADS)
    return src


def _take_columns(w, src):
    pieces, start = [], 0
    for stop in range(1, len(src) + 1):
        same_run = stop < len(src) and (src[stop] == src[stop - 1] + 1 if src[stop - 1] >= 0 else src[stop] < 0)
        if not same_run:
            if src[start] < 0:
                pieces.append(jnp.zeros(w.shape[:-1] + (stop - start,), w.dtype))
            else:
                pieces.append(w[..., src[start]:src[start] + stop - start])
            start = stop
    return jnp.concatenate(pieces, axis=-1)


def _head_columns(n_heads, src_stride, src_off, width, dst_stride):
    src = np.full((n_heads * dst_stride,), -1, np.int64)
    for h in range(n_heads):
        src[h * dst_stride:h * dst_stride + width] = h * src_stride + src_off + np.arange(width)
    return src


def _pad_lanes(v, n=LANES):
    return jnp.pad(v, [(0, 0)] * (v.ndim - 1) + [(0, n - v.shape[-1])])


def _row_tile(n, cap):
    best = 8
    for t in range(8, cap + 1, 8):
        if n % t == 0:
            best = t
    return best


def _dot(a, b):
    return jnp.dot(a, b, preferred_element_type=F32)


def _dot_nt(a, b):
    return lax.dot_general(a, b, (((1,), (1,)), ((), ())), preferred_element_type=F32)


def _sigmoid(x):
    return 1.0 / (1.0 + jnp.exp(-x))


def _silu(x):
    return x * _sigmoid(x)


def _rms_rows(x, n):
    return x * lax.rsqrt(jnp.sum(x * x, axis=-1, keepdims=True) * (1.0 / n) + EPS)


def _as_tuple(x):
    return tuple(x) if isinstance(x, (tuple, list)) else (x,)


def _window_bytes(spec, dtype):
    if spec.block_shape is None:
        return 0
    dims = [1 if n is None else n for n in spec.block_shape]
    copies = 2 if spec.pipeline_mode is None else spec.pipeline_mode.buffer_count
    return math.prod(dims) * jnp.dtype(dtype).itemsize * copies


def _pallas_call(kernel, *, name, out_shape, grid, in_specs, out_specs, semantics, scratch_shapes=(),
                 num_scalar_prefetch=0):
    def call(*args):
        need = VMEM_TEMPORARIES
        need += sum(_window_bytes(s, a.dtype) for s, a in zip(in_specs, args[num_scalar_prefetch:]))
        need += sum(_window_bytes(s, o.dtype) for s, o in zip(_as_tuple(out_specs), _as_tuple(out_shape)))
        need += sum(math.prod(b.shape) * jnp.dtype(b.dtype).itemsize for b in scratch_shapes)
        params = pltpu.CompilerParams(dimension_semantics=semantics,
                                      vmem_limit_bytes=min(need, VMEM_REQUEST_CAP))
        grid_spec = pltpu.PrefetchScalarGridSpec(
            num_scalar_prefetch=num_scalar_prefetch, grid=grid, in_specs=list(in_specs), out_specs=out_specs,
            scratch_shapes=list(scratch_shapes))
        launch = pl.pallas_call(kernel, out_shape=out_shape, grid_spec=grid_spec, compiler_params=params, name=name)
        return launch(*args)
    return call


_Stage = collections.namedtuple("_Stage", "kernel out_shape in_specs out_specs scratch_shapes args")


def _run_stages(name, grid, stages):
    n_in = [len(s.in_specs) for s in stages]
    n_out = [len(s.out_shape) for s in stages]
    n_scr = [len(s.scratch_shapes) for s in stages]

    def body(*refs):
        ins, outs, scr = refs[:sum(n_in)], refs[sum(n_in):sum(n_in) + sum(n_out)], refs[sum(n_in) + sum(n_out):]
        for j, stage in enumerate(stages):
            stage.kernel(*ins[sum(n_in[:j]):sum(n_in[:j + 1])], *outs[sum(n_out[:j]):sum(n_out[:j + 1])],
                         *scr[sum(n_scr[:j]):sum(n_scr[:j + 1])])

    flat = lambda field: [x for s in stages for x in getattr(s, field)]
    outs = _pallas_call(body, name=name, grid=grid, out_shape=tuple(flat("out_shape")),
                        in_specs=flat("in_specs"), out_specs=tuple(flat("out_specs")),
                        scratch_shapes=flat("scratch_shapes"), semantics=("parallel",))(*flat("args"))
    return [outs[sum(n_out[:j]):sum(n_out[:j + 1])] for j in range(len(stages))]


def _mod_kernel(a_ref, w_ref, b_ref, o_ref):
    a = _silu(a_ref[...]).astype(BF16)
    o_ref[0] = _dot(a, w_ref[0].astype(BF16)) + b_ref[0]


def _mod_vectors(cc, w_mod, b_mod):
    depth, d, n = w_mod.shape
    tn = 1536
    return _pallas_call(
        _mod_kernel,
        out_shape=jax.ShapeDtypeStruct((depth, 8, n), F32),
        grid=(depth, n // tn),
        in_specs=[pl.BlockSpec((8, d), lambda l, j: (0, 0)),
                  pl.BlockSpec((1, d, tn), lambda l, j: (l, 0, j)),
                  pl.BlockSpec((1, 1, tn), lambda l, j: (l, 0, j))],
        out_specs=pl.BlockSpec((1, 8, tn), lambda l, j: (l, 0, j)),
        semantics=("parallel", "parallel"),
        name="mod_vectors",
    )(cc, w_mod, b_mod.reshape(depth, 1, n))


def _modulated_norm(x, g, mod_lat, mod_ctx, row0, n_ctx):
    rows = x.shape[0]
    is_ctx = (row0 + lax.broadcasted_iota(jnp.int32, (rows, 1), 0)) < n_ctx
    shift = jnp.where(is_ctx, mod_ctx[0:1, :], mod_lat[0:1, :])
    scale = jnp.where(is_ctx, mod_ctx[1:2, :], mod_lat[1:2, :])
    return _rms_rows(x, x.shape[1]) * g * (1.0 + scale) + shift


def _in_proj_kernel(x_ref, g_ref, ml_ref, mc_ref, w_ref, o_ref, h_ref, *, n_ctx, tm):
    @pl.when(pl.program_id(1) == 0)
    def _():
        h = _modulated_norm(x_ref[...], g_ref[...], ml_ref[...], mc_ref[...], pl.program_id(0) * tm, n_ctx)
        h_ref[...] = h.astype(BF16)

    o_ref[...] = _dot(h_ref[...], w_ref[...])


def _in_proj(x_all, g, mod_lat, mod_ctx, w, layer, n_ctx):
    t, d = x_all.shape
    n = w.shape[2]
    tm = _row_tile(t, 1280)
    return _pallas_call(
        functools.partial(_in_proj_kernel, n_ctx=n_ctx, tm=tm),
        out_shape=jax.ShapeDtypeStruct((t, n), F32),
        grid=(t // tm, n // U_TILE_N),
        in_specs=[pl.BlockSpec((tm, d), lambda i, j: (i, 0)),
                  pl.BlockSpec((1, d), lambda i, j: (0, 0)),
                  pl.BlockSpec((2, d), lambda i, j: (0, 0)),
                  pl.BlockSpec((2, d), lambda i, j: (0, 0)),
                  pl.BlockSpec((None, d, U_TILE_N), lambda i, j: (layer, 0, j))],
        out_specs=pl.BlockSpec((tm, U_TILE_N), lambda i, j: (i, j)),
        scratch_shapes=[pltpu.VMEM((tm, d), BF16)],
        semantics=("parallel", "arbitrary"),
        name="in_proj",
    )(x_all, g, mod_lat, mod_ctx, w)


def _rope_tables(seq_len, n_ctx, rot_dim, lane0):
    n_rows = seq_len // GRID_W
    row = jnp.repeat(jnp.arange(n_rows), GRID_W).astype(F32)
    col = jnp.tile(jnp.arange(GRID_W), n_rows).astype(F32)
    axis_dim = rot_dim // 2
    half = axis_dim // 2
    inv = ROPE_THETA ** (-jnp.arange(0, axis_dim, 2, dtype=F32) / axis_dim)
    ang_r = row[:, None] * inv
    ang_c = col[:, None] * inv
    zeros = jnp.zeros((seq_len, half), F32)
    cos = jnp.concatenate([jnp.cos(ang_r), jnp.cos(ang_r), jnp.cos(ang_c), jnp.cos(ang_c)], axis=1)
    s1 = jnp.concatenate([zeros, jnp.sin(ang_r), zeros, jnp.sin(ang_c)], axis=1)
    s2 = jnp.concatenate([-jnp.sin(ang_r), zeros, -jnp.sin(ang_c), zeros], axis=1)

    def place(tab, fill):
        full = jnp.full((seq_len, LANES), fill, F32).at[:, lane0:lane0 + rot_dim].set(tab)
        ctx = jnp.full((n_ctx, LANES), fill, F32)
        return jnp.concatenate([ctx, full], axis=0)

    return place(cos, 1.0), place(s1, 0.0), place(s2, 0.0)


def _norm_rope_blocks(blocks, gains, n, cos, s1, s2, half):
    sums = [jnp.sum(x * x, axis=-1, keepdims=True) for x in blocks]
    inv = [lax.rsqrt(s * (1.0 / n) + EPS) for s in sums]
    normed = [x * r * g for x, r, g in zip(blocks, inv, gains)]
    fwd = [pltpu.roll(y, half, 1) for y in normed]
    bwd = [pltpu.roll(y, LANES - half, 1) for y in normed]
    return [y * cos + a * s1 + b * s2 for y, a, b in zip(normed, fwd, bwd)]


def _mla_prep_kernel(u_ref, gq_ref, wuq_ref, gkv_ref, wk_ref, wv_ref, gqn_ref, gkn_ref, shift_ref,
                     cos_ref, s1_ref, s2_ref, qt_ref, k_ref, vt_ref):
    u = u_ref[...]
    cq = u[:, :MLA_Q_RANK]
    ckv = u[:, MLA_Q_RANK:MLA_Q_RANK + MLA_KV_RANK]
    pe = u[:, MLA_Q_RANK + MLA_KV_RANK:]
    q = _dot((_rms_rows(cq, MLA_Q_RANK) * gq_ref[...]).astype(BF16), wuq_ref[...])
    kv_in = (_rms_rows(ckv, MLA_KV_RANK) * gkv_ref[...]).astype(BF16)
    kn = _dot(kv_in, wk_ref[...])
    v = _dot(kv_in, wv_ref[...])
    cos, s1, s2 = cos_ref[...], s1_ref[...], s2_ref[...]
    half = MLA_ROPE // 4
    q_scale = MLA_QK ** -0.5 * LOG2E
    n = MLA_HEADS
    blocks = ([q[:, LANES * h:LANES * (h + 1)] for h in range(n)]
              + [kn[:, LANES * h:LANES * (h + 1)] + pe for h in range(n)])
    roped = _norm_rope_blocks(blocks, [gqn_ref[...]] * n + [gkn_ref[...]] * n, MLA_QK, cos, s1, s2, half)
    spare = lax.broadcasted_iota(jnp.int32, roped[0].shape, 1) == MLA_QK
    for h in range(n):
        qt_ref[h] = jnp.where(spare, -shift_ref[...], roped[h] * q_scale).T.astype(BF16)
        k_ref[h, 0] = jnp.where(spare, 1.0, roped[n + h]).astype(BF16)
    tm = v.shape[0]
    vt_ref[:, 0] = _with_sum_rows(v.T.reshape(MLA_HEADS, MLA_V, tm)).astype(BF16)


def _mla_prep(u, gq, wuq, gkv, wk, wv, gqn, gkn, shift, tabs):
    t = u.shape[0]
    tm = ROW_TILE
    nt = t // tm
    const = lambda i: (0, 0)
    rows = lambda i: (i, 0)
    return _Stage(
        kernel=_mla_prep_kernel,
        out_shape=(jax.ShapeDtypeStruct((MLA_HEADS, LANES, t), BF16),
                   jax.ShapeDtypeStruct((MLA_HEADS, nt, tm, LANES), BF16),
                   jax.ShapeDtypeStruct((MLA_HEADS, nt, MLA_V + SUM_ROWS, tm), BF16)),
        in_specs=[pl.BlockSpec((tm, 512), lambda i: (i, U_MLA // 512)),
                  pl.BlockSpec((1, MLA_Q_RANK), const),
                  pl.BlockSpec(wuq.shape, const),
                  pl.BlockSpec((1, MLA_KV_RANK), const),
                  pl.BlockSpec(wk.shape, const),
                  pl.BlockSpec(wv.shape, const),
                  pl.BlockSpec((1, LANES), const),
                  pl.BlockSpec((1, LANES), const),
                  pl.BlockSpec((1, LANES), const),
                  pl.BlockSpec((tm, LANES), rows),
                  pl.BlockSpec((tm, LANES), rows),
                  pl.BlockSpec((tm, LANES), rows)],
        out_specs=(pl.BlockSpec((MLA_HEADS, LANES, tm), lambda i: (0, 0, i)),
                   pl.BlockSpec((MLA_HEADS, 1, tm, LANES), lambda i: (0, i, 0, 0)),
                   pl.BlockSpec((MLA_HEADS, 1, MLA_V + SUM_ROWS, tm), lambda i: (0, i, 0, 0))),
        scratch_shapes=(),
        args=(u, gq, wuq, gkv, wk, wv, gqn, gkn, shift, *tabs))


def _dif_prep_kernel(q_ref, k_ref, v_ref, gq_ref, gk_ref, shift_ref, cos_ref, s1_ref, s2_ref,
                     qt_out, k_out, vt_out):
    cos, s1, s2 = cos_ref[...], s1_ref[...], s2_ref[...]
    half = DIF_HEAD_DIM // 4
    q_scale = DIF_HEAD_DIM ** -0.5 * LOG2E
    n = 2 * DIF_HEADS
    blocks = ([q_ref[:, LANES * a:LANES * (a + 1)] for a in range(n)]
              + [k_ref[:, LANES * a:LANES * (a + 1)] for a in range(n)])
    roped = _norm_rope_blocks(blocks, [gq_ref[...]] * n + [gk_ref[...]] * n, DIF_HEAD_DIM, cos, s1, s2, half)
    spare = lax.broadcasted_iota(jnp.int32, roped[0].shape, 1) == DIF_HEAD_DIM
    for a in range(n):
        qt_out[a] = jnp.where(spare, -shift_ref[...], roped[a] * q_scale).T.astype(BF16)
        k_out[a, 0] = jnp.where(spare, 1.0, roped[n + a]).astype(BF16)
    v = v_ref[...]
    vt_out[:, 0] = _with_sum_rows(v.T.reshape(DIF_HEADS, 2 * DIF_HEAD_DIM, v.shape[0])).astype(BF16)


def _dif_prep(u, gq, gk, shift, tabs):
    t = u.shape[0]
    tm = ROW_TILE
    nt = t // tm
    nsub = 2 * DIF_HEADS
    const = lambda i: (0, 0)
    rows = lambda i: (i, 0)
    return _Stage(
        kernel=_dif_prep_kernel,
        out_shape=(jax.ShapeDtypeStruct((nsub, LANES, t), BF16),
                   jax.ShapeDtypeStruct((nsub, nt, tm, LANES), BF16),
                   jax.ShapeDtypeStruct((DIF_HEADS, nt, 2 * DIF_HEAD_DIM + SUM_ROWS, tm), BF16)),
        in_specs=[pl.BlockSpec((tm, 1024), lambda i: (i, U_DQ // 1024)),
                  pl.BlockSpec((tm, 1024), lambda i: (i, U_DK // 1024)),
                  pl.BlockSpec((tm, 512), lambda i: (i, U_DV // 512)),
                  pl.BlockSpec((1, LANES), const),
                  pl.BlockSpec((1, LANES), const),
                  pl.BlockSpec((1, LANES), const),
                  pl.BlockSpec((tm, LANES), rows),
                  pl.BlockSpec((tm, LANES), rows),
                  pl.BlockSpec((tm, LANES), rows)],
        out_specs=(pl.BlockSpec((nsub, LANES, tm), lambda i: (0, 0, i)),
                   pl.BlockSpec((nsub, 1, tm, LANES), lambda i: (0, i, 0, 0)),
                   pl.BlockSpec((DIF_HEADS, 1, 2 * DIF_HEAD_DIM + SUM_ROWS, tm), lambda i: (0, i, 0, 0))),
        scratch_shapes=(),
        args=(u, u, u, gq, gk, shift, *tabs))


ATTN_GROUP = 2
ATTN_TRIP = 64
SUM_ROWS = 16


def _with_sum_rows(vt):
    heads, _, keys = vt.shape
    row = lax.broadcasted_iota(jnp.int32, (heads, SUM_ROWS, keys), 1)
    return jnp.concatenate([vt, jnp.where(row == 0, 1.0, 0.0).astype(vt.dtype)], axis=1)


def _attn_group(qt_ref, k_ref, vt_ref, v_of_sub, s_buf, p_buf, acc_ref, n_chunks):
    group, _, tq = qt_ref.shape
    last = n_chunks - 1

    def scores(c):
        out = []
        for a in range(group):
            s = _dot(k_ref[a, c], qt_ref[a])
            out.append((s, jnp.max(s, axis=0, keepdims=True)))
        return out

    def stash(sc, slot):
        for a in range(group):
            s_buf[slot, a] = sc[a][0]
        return tuple(mx for (_, mx) in sc)

    def softmax(s_of, mx, slot, ms):
        new_ms, alphas = [], []
        for a in range(group):
            m_new = jnp.maximum(ms[a], mx[a])
            alphas.append(jnp.exp2(ms[a] - m_new))
            p_buf[slot, a] = jnp.exp2(s_of(a) - m_new).astype(BF16)
            new_ms.append(m_new)
        return tuple(new_ms), tuple(alphas)

    def values(c, slot, alphas):
        for a in range(group):
            acc_ref[a] = alphas[a] * acc_ref[a] + _dot(vt_ref[v_of_sub[a], c], p_buf[slot, a])

    per_trip = math.gcd(last, ATTN_TRIP)

    def trip(t, state):
        ms, alphas, mx0 = state
        c = per_trip * t + 1
        for pair in range(per_trip // 2):
            cur, nxt = pair % 2, 1 - pair % 2
            even = scores(c + 1)
            values(c - 1, 0, alphas)
            ms, alphas = softmax(lambda a: s_buf[cur, a], mx0, 1, ms)
            mx0 = stash(scores(jnp.minimum(c + 2, last)), nxt)
            values(c, 1, alphas)
            ms, alphas = softmax(lambda a: even[a][0], [mx for (_, mx) in even], 0, ms)
            c = c + 2
        return ms, alphas, mx0

    def context_chunk():
        ms = tuple(jnp.full((1, tq), -jnp.inf, F32) for _ in range(group))
        first = scores(0)
        return softmax(lambda a: first[a][0], [mx for (_, mx) in first], 0, ms)

    acc_ref[...] = jnp.zeros(acc_ref.shape, F32)

    @pl.when(pl.program_id(1) == 0)
    def _():
        _, alphas = context_chunk()
        values(0, 0, alphas)

    @pl.when(pl.program_id(1) > 0)
    def _():
        ms, alphas = context_chunk()
        state = (ms, alphas, stash(scores(1), 0))
        if per_trip == last:
            _, alphas, _ = trip(0, state)
        else:
            _, alphas, _ = lax.fori_loop(0, last // per_trip, trip, state)
        values(last, 0, alphas)


def _normalised(acc_ref, a):
    dv = acc_ref.shape[1] - SUM_ROWS
    return acc_ref[a, :dv] * (1.0 / acc_ref[a, dv:dv + 1])


PV_SPAN_WORK = 64
SHIFTED_GROUP = 4
MAX_SHIFT = 60.0


def _attn_group_shifted(qt_ref, k_ref, vt_ref, v_of_sub, p_buf, acc_ref, n_chunks):
    group = qt_ref.shape[0]
    span = _value_span(n_chunks, group)

    def accumulate(c0, n, slot):
        for a in range(group):
            for i in range(n):
                s = _dot(k_ref[a, c0 + i], qt_ref[a])
                p_buf[slot, a, KEY_CHUNK * i:KEY_CHUNK * (i + 1)] = jnp.exp2(s).astype(BF16)
            vt = jnp.concatenate([vt_ref[v_of_sub[a], c0 + i] for i in range(n)], axis=1)
            acc_ref[a] = acc_ref[a] + _dot(vt, p_buf[slot, a, :KEY_CHUNK * n])

    acc_ref[...] = jnp.zeros(acc_ref.shape, F32)

    @pl.when(pl.program_id(1) == 0)
    def _():
        accumulate(0, 1, 0)

    @pl.when(pl.program_id(1) > 0)
    def _():
        accumulate(0, span + 1, 0)
        for j in range(1, (n_chunks - 1) // span):
            accumulate(1 + span * j, span, j % 2)


def _attend(shifted, qt_ref, k_ref, vt_ref, v_of_sub, scratch, n_chunks):
    if shifted:
        p_buf, acc_ref = scratch
        _attn_group_shifted(qt_ref, k_ref, vt_ref, v_of_sub, p_buf, acc_ref, n_chunks)
    else:
        s_buf, p_buf, acc_ref = scratch
        _attn_group(qt_ref, k_ref, vt_ref, v_of_sub, s_buf, p_buf, acc_ref, n_chunks)
    return acc_ref


def _value_span(n_chunks, group):
    return math.gcd(n_chunks - 1, PV_SPAN_WORK // group)


def _mla_attn_kernel(qt_ref, k_ref, vt_ref, o_ref, *scratch, n_chunks, shifted):
    group = qt_ref.shape[0]
    acc_ref = _attend(shifted, qt_ref, k_ref, vt_ref, tuple(range(group)), scratch, n_chunks)
    o = jnp.concatenate([_normalised(acc_ref, a) for a in range(group)], axis=0)
    o_ref[...] = o.T


def _dif_attn_kernel(lam_ref, gsub_ref, qt_ref, k_ref, vt_ref, o_ref, *scratch, n_chunks, shifted, lam_init):
    group = qt_ref.shape[0]
    acc_ref = _attend(shifted, qt_ref, k_ref, vt_ref, tuple(a // 2 for a in range(group)), scratch,
                      n_chunks)
    lp = lam_ref[...]
    lam = (jnp.exp(jnp.sum(lp[0:1] * lp[1:2], axis=-1, keepdims=True))
           - jnp.exp(jnp.sum(lp[2:3] * lp[3:4], axis=-1, keepdims=True)) + lam_init)
    outs = []
    for h in range(group // 2):
        o = _normalised(acc_ref, 2 * h) - lam * _normalised(acc_ref, 2 * h + 1)
        outs.append(o * lax.rsqrt(jnp.mean(o * o, axis=0, keepdims=True) + EPS))
    gsub = gsub_ref[...] * (1.0 - lam_init)
    o_ref[...] = jnp.concatenate(outs, axis=0).T * jnp.concatenate([gsub] * len(outs), axis=1)


def _attention(qt, k, vt, *, group=ATTN_GROUP, shifted=False, lam=None, gsub=None, lam_init=None):
    nsub, _, t = qt.shape
    n_chunks = k.shape[1]
    assert (n_chunks - 1) % 4 == 0 and nsub % group == 0
    tq = KEY_CHUNK
    dv = vt.shape[2] - SUM_ROWS
    groups = nsub // group
    n_v = vt.shape[0] // groups
    out_w = n_v * dv
    resident = dict(pipeline_mode=pl.Buffered(1))
    specs = [pl.BlockSpec((group, LANES, tq), lambda g, i: (g, 0, i)),
             pl.BlockSpec((group, n_chunks, KEY_CHUNK, LANES), lambda g, i: (g, 0, 0, 0), **resident),
             pl.BlockSpec((n_v, n_chunks, dv + SUM_ROWS, KEY_CHUNK), lambda g, i: (g, 0, 0, 0), **resident)]
    if lam is None:
        body = functools.partial(_mla_attn_kernel, n_chunks=n_chunks, shifted=shifted)
        args = (qt, k, vt)
    else:
        body = functools.partial(_dif_attn_kernel, n_chunks=n_chunks, shifted=shifted, lam_init=lam_init)
        specs = [pl.BlockSpec((8, LANES), lambda g, i: (0, 0)),
                 pl.BlockSpec((1, LANES), lambda g, i: (0, 0))] + specs
        args = (lam, gsub, qt, k, vt)
    acc = pltpu.VMEM((group, dv + SUM_ROWS, tq), F32)
    if shifted:
        scratch = [pltpu.VMEM((2, group, (_value_span(n_chunks, group) + 1) * KEY_CHUNK, tq), BF16), acc]
    else:
        scratch = [pltpu.VMEM((2, group, KEY_CHUNK, tq), F32), pltpu.VMEM((2, group, KEY_CHUNK, tq), BF16), acc]
    return _pallas_call(
        body,
        out_shape=jax.ShapeDtypeStruct((t, groups * out_w), F32),
        grid=(groups, t // tq),
        in_specs=specs,
        out_specs=pl.BlockSpec((tq, out_w), lambda g, i: (i, g)),
        scratch_shapes=scratch,
        semantics=("parallel", "arbitrary"),
        name=("mla_attention" if lam is None else "dif_attention") + ("_shifted" if shifted else ""),
    )(*args)


def _ssm_prep_kernel(x_ref, prev_ref, next_ref, dt_ref, w_ref, b_ref, dtb_ref, xo_ref, dto_ref, e_ref,
                     *, n_ctx, n_tok, tm):
    row0 = pl.program_id(0) * tm
    pad = SSM_CONV // 2
    has_prev = jnp.logical_and(row0 != 0, row0 != n_ctx)
    has_next = jnp.logical_and(row0 + tm != n_ctx, row0 + tm != n_tok)
    e_ref[0:8] = jnp.where(has_prev, prev_ref[...], 0.0)
    e_ref[8:8 + tm] = x_ref[...]
    e_ref[8 + tm:16 + tm] = jnp.where(has_next, next_ref[...], 0.0)
    acc = jnp.zeros(x_ref.shape, F32) + b_ref[...]
    for k in range(SSM_CONV):
        acc = acc + w_ref[k:k + 1, :] * e_ref[pl.ds(8 - pad + k, tm), :]
    xo_ref[...] = _silu(acc)
    d = dt_ref[...] + dtb_ref[...]
    dto_ref[...] = jnp.maximum(d, 0.0) + jnp.log1p(jnp.exp(-jnp.abs(d)))


def _ssm_prep(u, conv_w, conv_b, dt_bias, n_ctx):
    t = u.shape[0]
    tm = ROW_TILE
    nt = t // tm
    cb = U_XBC // SSM_XBC
    const = lambda i: (0, 0)
    return _Stage(
        kernel=functools.partial(_ssm_prep_kernel, n_ctx=n_ctx, n_tok=t, tm=tm),
        out_shape=(jax.ShapeDtypeStruct((t, SSM_XBC), F32), jax.ShapeDtypeStruct((t, LANES), F32)),
        in_specs=[pl.BlockSpec((tm, SSM_XBC), lambda i: (i, cb)),
                  pl.BlockSpec((8, SSM_XBC), lambda i: (jnp.maximum(i * (tm // 8) - 1, 0), cb)),
                  pl.BlockSpec((8, SSM_XBC), lambda i: (jnp.minimum((i + 1) * (tm // 8), t // 8 - 1), cb)),
                  pl.BlockSpec((tm, LANES), lambda i: (i, U_DT // LANES)),
                  pl.BlockSpec((8, SSM_XBC), const),
                  pl.BlockSpec((1, SSM_XBC), const),
                  pl.BlockSpec((1, LANES), const)],
        out_specs=(pl.BlockSpec((tm, SSM_XBC), lambda i: (i, 0)),
                   pl.BlockSpec((tm, LANES), lambda i: (i, 0))),
        scratch_shapes=(pltpu.VMEM((tm + 16, SSM_XBC), F32),),
        args=(u, u, u, u, conv_w, conv_b, dt_bias))


def _ssd_kernel(xf_ref, dtf_ref, xr_ref, dtr_ref, alog_ref, of_ref, or_ref, hf_ref, hr_ref):
    @pl.when(pl.program_id(0) == 0)
    def _():
        hf_ref[...] = jnp.zeros(hf_ref.shape, F32)
        hr_ref[...] = jnp.zeros(hr_ref.shape, F32)

    lc = SSM_CHUNK
    a_row = -jnp.exp(alog_ref[...])
    n_sub = xf_ref.shape[0] // lc
    fwd = [_ssd_local(xf_ref[lc * j:lc * (j + 1), :], dtf_ref[lc * j:lc * (j + 1), :], a_row, 0)
           for j in range(n_sub)]
    bwd = [_ssd_local(xr_ref[lc * j:lc * (j + 1), :], dtr_ref[lc * j:lc * (j + 1), :], a_row, 1)
           for j in range(n_sub)]
    of_ref[...] = jnp.concatenate([_ssd_carry(part, hf_ref) for part in fwd], axis=0)
    or_ref[...] = jnp.concatenate([_ssd_carry(part, hr_ref) for part in reversed(bwd)][::-1], axis=0)


def _ssd_carry(local, h_ref):
    outs = []
    for h, (y, c_g, from_start, decay, upd) in enumerate(local):
        state = h_ref[h]
        outs.append(y + _dot(c_g, state.astype(BF16)) * from_start)
        h_ref[h] = state * decay + upd
    return jnp.concatenate(outs, axis=1)


def _ssd_local(xbc, dt, a_row, direction):
    lc = SSM_CHUNK
    hd, per_group = SSM_HEAD_DIM, SSM_HEADS // SSM_GROUPS
    dta = dt * a_row
    r = lax.broadcasted_iota(jnp.int32, (lc, lc), 0)
    c = lax.broadcasted_iota(jnp.int32, (lc, lc), 1)
    keep = (r >= c) if direction == 0 else (r <= c)
    tri = jnp.where(keep, 1.0, 0.0).astype(F32)
    cum = jnp.dot(tri, dta, preferred_element_type=F32, precision=lax.Precision.HIGHEST)
    cum_t = cum.T
    dt_t = dt.T
    total = jnp.sum(dta, axis=0, keepdims=True)
    chunk_decay = jnp.exp(total)
    w_t = (dt * jnp.exp(total - cum)).T
    b_f32 = [xbc[:, SSM_WIDTH + SSM_STATE * g:SSM_WIDTH + SSM_STATE * (g + 1)] for g in range(SSM_GROUPS)]
    c_bf = [xbc[:, SSM_WIDTH + SSM_STATE * (SSM_GROUPS + g):SSM_WIDTH + SSM_STATE * (SSM_GROUPS + g + 1)].astype(BF16)
            for g in range(SSM_GROUPS)]
    cb = [_dot_nt(c, b.astype(BF16)) for c, b in zip(c_bf, b_f32)]
    b_t = [b.T for b in b_f32]
    heads = range(SSM_HEADS)
    cols = [direction * SSM_HEADS + h for h in heads]
    cum_b = [jnp.broadcast_to(cum[:, c:c + 1], (lc, lc)) for c in cols]
    seg = [jnp.exp(jnp.where(keep, cb_ - cum_t[c:c + 1, :], -jnp.inf)) for cb_, c in zip(cum_b, cols)]
    x_h = [xbc[:, hd * h:hd * (h + 1)].astype(BF16) for h in heads]
    scores = [(cb[h // per_group] * s * dt_t[c:c + 1, :]).astype(BF16) for h, s, c in zip(heads, seg, cols)]
    weighted_b = [(b_t[h // per_group] * w_t[c:c + 1, :]).astype(BF16) for h, c in zip(heads, cols)]
    y = [_dot(s, x) for s, x in zip(scores, x_h)]
    upd = [_dot(wb, x) for wb, x in zip(weighted_b, x_h)]
    return [(y[h], c_bf[h // per_group], jnp.exp(cum_b[h][:, :hd]), chunk_decay[:, cols[h]:cols[h] + 1], upd[h])
            for h in heads]


SSD_STEP = 2 * SSM_CHUNK


def _ssd(xbc, dt, a_log, n_ctx):
    t = xbc.shape[0]
    lc = SSD_STEP
    assert n_ctx % lc == 0 and t % lc == 0
    nc = t // lc
    ncc = n_ctx // lc
    fwd = lambda s: (s, 0)
    bwd = lambda s: (jnp.where(s < ncc, ncc - 1 - s, nc - 1 - (s - ncc)), 0)
    state = pltpu.VMEM((SSM_HEADS, SSM_STATE, SSM_HEAD_DIM), F32)
    out = jax.ShapeDtypeStruct((t, SSM_WIDTH), F32)
    return _pallas_call(
        _ssd_kernel,
        out_shape=(out, out),
        grid=(nc,),
        in_specs=[pl.BlockSpec((lc, SSM_XBC), fwd), pl.BlockSpec((lc, LANES), fwd),
                  pl.BlockSpec((lc, SSM_XBC), bwd), pl.BlockSpec((lc, LANES), bwd),
                  pl.BlockSpec((1, LANES), lambda s: (0, 0))],
        out_specs=(pl.BlockSpec((lc, SSM_WIDTH), fwd), pl.BlockSpec((lc, SSM_WIDTH), bwd)),
        scratch_shapes=[state, state],
        semantics=("arbitrary",),
        name="ssd_scan",
    )(xbc, dt, xbc, dt, a_log)


def _merge_kernel(x_ref, gate_ref, ya_ref, yb_ref, yf_ref, yr_ref, xs_ref, z_ref,
                  bg_ref, dskip_ref, gssm_ref, wa_ref, wb_ref, wc_ref, wo_ref,
                  ml_ref, mc_ref, g2_ref, wr_ref, br_ref,
                  xo_ref, f_ref, lg_ref, *, n_ctx, tm):
    y = (yf_ref[...] + yr_ref[...] + dskip_ref[...] * xs_ref[...]) * _silu(z_ref[...])
    gw = SSM_WIDTH // SSM_GROUPS
    yc = jnp.concatenate([_rms_rows(y[:, gw * g:gw * (g + 1)], gw) for g in range(SSM_GROUPS)], axis=1)
    yc = yc * gssm_ref[...]
    gate = _sigmoid(gate_ref[...] + bg_ref[...])
    m = (gate[:, :D_MODEL] * _dot(ya_ref[...].astype(BF16), wa_ref[...])
         + gate[:, D_MODEL:2 * D_MODEL] * _dot(yb_ref[...].astype(BF16), wb_ref[...])
         + gate[:, 2 * D_MODEL:] * _dot(yc.astype(BF16), wc_ref[...]))
    out = _dot(m.astype(BF16), wo_ref[...])
    row0 = pl.program_id(0) * tm
    is_ctx = (row0 + lax.broadcasted_iota(jnp.int32, (tm, 1), 0)) < n_ctx
    gt1 = jnp.where(is_ctx, mc_ref[2:3, :], ml_ref[2:3, :])
    x_new = x_ref[...] + gt1 * out
    xo_ref[...] = x_new
    f = _modulated_norm(x_new, g2_ref[...], ml_ref[3:5, :], mc_ref[3:5, :], row0, n_ctx)
    f_ref[...] = f
    lg_ref[...] = _dot(f.astype(BF16), wr_ref[...]) + br_ref[...]


def _merge(x_all, u, ya, yb, yf, yr, xbc, b_gate, dskip, g_ssm, wa, wb, wc, wo, mod_lat, mod_ctx,
           g2, w_router, b_router, n_ctx):
    t, d = x_all.shape
    tm = ROW_TILE
    const = lambda i: (0, 0)
    rows = lambda i: (i, 0)
    full = lambda a: pl.BlockSpec(a.shape, const)
    return _pallas_call(
        functools.partial(_merge_kernel, n_ctx=n_ctx, tm=tm),
        out_shape=(jax.ShapeDtypeStruct((t, d), F32), jax.ShapeDtypeStruct((t, d), F32),
                   jax.ShapeDtypeStruct((t, LANES), F32)),
        grid=(t // tm,),
        in_specs=[pl.BlockSpec((tm, d), rows),
                  pl.BlockSpec((tm, GATE_COLS), lambda i: (i, U_GATE // GATE_COLS)),
                  pl.BlockSpec((tm, MLA_WIDTH), rows),
                  pl.BlockSpec((tm, DIF_WIDTH), rows),
                  pl.BlockSpec((tm, SSM_WIDTH), rows),
                  pl.BlockSpec((tm, SSM_WIDTH), rows),
                  pl.BlockSpec((tm, SSM_WIDTH), rows),
                  pl.BlockSpec((tm, SSM_WIDTH), lambda i: (i, U_Z // SSM_WIDTH)),
                  full(b_gate), full(dskip), full(g_ssm), full(wa), full(wb), full(wc), full(wo),
                  full(mod_lat), full(mod_ctx), full(g2), full(w_router), full(b_router)],
        out_specs=(pl.BlockSpec((tm, d), rows), pl.BlockSpec((tm, d), rows),
                   pl.BlockSpec((tm, LANES), rows)),
        semantics=("parallel",),
        name="merge",
    )(x_all, u, ya, yb, yf, yr, xbc, u, b_gate, dskip, g_ssm, wa, wb, wc, wo, mod_lat, mod_ctx,
      g2, w_router, b_router)


def _moe_kernel(be_ref, nb_ref, x_ref, wgu_ref, bgu_ref, wd_ref, bd_ref, o_ref, wgu_s, wd_s):
    b = pl.program_id(0)
    prev = be_ref[jnp.maximum(b - 1, 0)]
    fresh = jnp.logical_or(b == 0, be_ref[b] != prev)

    @pl.when(fresh)
    def _():
        wgu_s[...] = wgu_ref[...].astype(BF16)
        wd_s[...] = wd_ref[...].astype(BF16)

    @pl.when(b < nb_ref[0])
    def _():
        gu = _dot(x_ref[...].astype(BF16), wgu_s[...]) + bgu_ref[...]
        glu = jnp.minimum(gu[:, :D_FF], SWIGLU_LIMIT)
        lin = jnp.clip(gu[:, D_FF:], -SWIGLU_LIMIT, SWIGLU_LIMIT)
        act = glu * _sigmoid(SWIGLU_ALPHA * glu) * (lin + 1.0)
        o_ref[...] = _dot(act.astype(BF16), wd_s[...]) + bd_ref[...]

    @pl.when(b >= nb_ref[0])
    def _():
        o_ref[...] = jnp.zeros(o_ref.shape, F32)


def _moe_experts(block_e, n_used, x_sorted, w_gu, b_gu, w_down, b_down, layer):
    n_slots, d = x_sorted.shape
    n_blocks = n_slots // MOE_BLOCK
    by_expert = lambda b, be, nb: (layer, be[b], 0, 0)
    return _pallas_call(
        _moe_kernel,
        out_shape=jax.ShapeDtypeStruct((n_slots, d), F32),
        num_scalar_prefetch=2,
        grid=(n_blocks,),
        in_specs=[pl.BlockSpec((MOE_BLOCK, d), lambda b, be, nb: (b, 0)),
                  pl.BlockSpec((None, None, d, 2 * D_FF), by_expert),
                  pl.BlockSpec((None, None, 1, 2 * D_FF), by_expert),
                  pl.BlockSpec((None, None, D_FF, d), by_expert),
                  pl.BlockSpec((None, None, 1, d), by_expert)],
        out_specs=pl.BlockSpec((MOE_BLOCK, d), lambda b, be, nb: (b, 0)),
        scratch_shapes=[pltpu.VMEM((d, 2 * D_FF), BF16), pltpu.VMEM((D_FF, d), BF16)],
        semantics=("arbitrary",),
        name="moe_experts",
    )(block_e, n_used, x_sorted, w_gu, b_gu, w_down, b_down)


def _combine_kernel(x_ref, g_ref, ml_ref, mc_ref, *rest, n_ctx, tm):
    y_refs, o_ref = rest[:TOP_K], rest[TOP_K]
    is_ctx = (pl.program_id(0) * tm + lax.broadcasted_iota(jnp.int32, (tm, 1), 0)) < n_ctx
    gate2 = jnp.where(is_ctx, mc_ref[...], ml_ref[...])
    g = g_ref[...]
    y = g[:, 0:1] * y_refs[0][...]
    for k in range(1, TOP_K):
        y = y + g[:, k:k + 1] * y_refs[k][...]
    o_ref[...] = x_ref[...] + gate2 * y


def _combine(x_all, y_rows, gates, gate2_lat, gate2_ctx, n_ctx):
    t, d = x_all.shape
    tm = ROW_TILE
    rows = lambda i: (i, 0)
    const = lambda i: (0, 0)
    return _pallas_call(
        functools.partial(_combine_kernel, n_ctx=n_ctx, tm=tm),
        out_shape=jax.ShapeDtypeStruct((t, d), F32),
        grid=(t // tm,),
        in_specs=[pl.BlockSpec((tm, d), rows), pl.BlockSpec((tm, LANES), rows),
                  pl.BlockSpec((1, d), const), pl.BlockSpec((1, d), const)]
                 + [pl.BlockSpec((tm, d), rows)] * TOP_K,
        out_specs=pl.BlockSpec((tm, d), rows),
        semantics=("parallel",),
        name="moe_combine",
    )(x_all, gates, gate2_lat, gate2_ctx, *y_rows)


ROUTE_IDX, ROUTE_GATE, ROUTE_RANK = 0, TOP_K, 2 * TOP_K


def _router_kernel(lg_ref, o_ref, cnt_ref, run_ref, *, tm):
    @pl.when(pl.program_id(0) == 0)
    def _():
        run_ref[...] = jnp.zeros(run_ref.shape, F32)

    lane = lax.broadcasted_iota(jnp.int32, (tm, LANES), 1)
    lane_f = lane.astype(F32)
    lg = jnp.where(lane < N_EXPERTS, lg_ref[...], -jnp.inf)
    hots, vals = [], []
    for _ in range(TOP_K):
        mx = jnp.max(lg, axis=-1, keepdims=True)
        idx = jnp.min(jnp.where(lg == mx, lane_f, float(LANES)), axis=-1, keepdims=True)
        hot = lane_f == idx
        lg = jnp.where(hot, -jnp.inf, lg)
        hots.append((hot, idx))
        vals.append(mx)
    exps = [jnp.exp(v - vals[0]) for v in vals]
    inv = 1.0 / sum(exps[1:], exps[0])
    chosen = jnp.zeros((tm, LANES), F32)
    for hot, _ in hots:
        chosen = jnp.where(hot, 1.0, chosen)
    r = lax.broadcasted_iota(jnp.int32, (tm, tm), 0)
    c = lax.broadcasted_iota(jnp.int32, (tm, tm), 1)
    earlier = jnp.where(r > c, 1.0, 0.0).astype(BF16)
    before = _dot(earlier, chosen.astype(BF16)) + run_ref[0:1, :]
    out = jnp.zeros((tm, LANES), F32)
    for k, (hot, idx) in enumerate(hots):
        rank = jnp.sum(jnp.where(hot, before, 0.0), axis=-1, keepdims=True)
        out = jnp.where(lane == ROUTE_IDX + k, idx, out)
        out = jnp.where(lane == ROUTE_GATE + k, exps[k] * inv, out)
        out = jnp.where(lane == ROUTE_RANK + k, rank, out)
    o_ref[...] = out
    run_ref[...] = run_ref[...] + jnp.sum(chosen, axis=0, keepdims=True)
    cnt_ref[...] = run_ref[...]


def _router(logits):
    t = logits.shape[0]
    tm = ROW_TILE
    return _pallas_call(
        functools.partial(_router_kernel, tm=tm),
        out_shape=(jax.ShapeDtypeStruct((t, LANES), F32), jax.ShapeDtypeStruct((8, LANES), F32)),
        grid=(t // tm,),
        in_specs=[pl.BlockSpec((tm, LANES), lambda i: (i, 0))],
        out_specs=(pl.BlockSpec((tm, LANES), lambda i: (i, 0)), pl.BlockSpec((8, LANES), lambda i: (0, 0))),
        scratch_shapes=[pltpu.VMEM((8, LANES), F32)],
        semantics=("arbitrary",),
        name="moe_router",
    )(logits)


def _route(logits, n_tok):
    routed, counts = _router(logits)
    top_idx = routed[:, ROUTE_IDX:ROUTE_IDX + TOP_K].astype(jnp.int32)
    gates = routed[:, ROUTE_GATE:ROUTE_GATE + TOP_K]
    rank = routed[:, ROUTE_RANK:ROUTE_RANK + TOP_K].astype(jnp.int32)
    n_assign = n_tok * TOP_K
    counts = counts[0, :N_EXPERTS].astype(jnp.int32)
    padded = (counts + MOE_BLOCK - 1) // MOE_BLOCK * MOE_BLOCK
    pad_end = jnp.cumsum(padded)
    pad_start = pad_end - padded
    slot_of = pad_start[top_idx] + rank
    n_blocks = -(-(n_assign + N_EXPERTS * (MOE_BLOCK - 1)) // MOE_BLOCK)
    n_slots = n_blocks * MOE_BLOCK
    token = jnp.arange(n_assign, dtype=jnp.int32) // TOP_K
    slot_tok = jnp.zeros((n_slots,), jnp.int32).at[slot_of.reshape(-1)].set(token, unique_indices=True)
    block_start = jnp.arange(n_blocks, dtype=jnp.int32) * MOE_BLOCK
    block_e = jnp.minimum(jnp.searchsorted(pad_end, block_start, side='right'), N_EXPERTS - 1)
    n_used = (pad_end[-1] // MOE_BLOCK).reshape(1)
    return gates, slot_tok, slot_of, block_e.astype(jnp.int32), n_used.astype(jnp.int32)


def kernel(x, c, ctx, c_ctx, w_mod, b_mod, g_norm1, g_norm2, w_in, b_gate, mla_g_q, mla_w_uq, mla_g_kv, mla_w_ukv, mla_g_qn, mla_g_kn, dif_g_qn, dif_g_kn, dif_lambda, dif_g_sub, ssm_conv_w, ssm_conv_b, ssm_dt_bias, ssm_a_log, ssm_d, ssm_g_norm, w_up_mla, w_up_dif, w_up_ssm, w_out, moe_w_router, moe_b_router, moe_w_gu, moe_b_gu, moe_w_down, moe_b_down):
    assert x.shape[0] == 1 and ctx.shape[0] == 1
    depth = w_in.shape[0]
    seq = x.shape[1]
    n_ctx = ctx.shape[1]
    n_tok = n_ctx + seq
    d = D_MODEL
    assert n_ctx == KEY_CHUNK and n_tok % ROW_TILE == 0 and seq % GRID_W == 0

    x_all = jnp.concatenate([ctx[0], x[0]], axis=0)
    cc = jnp.zeros((8, d), F32).at[0].set(c[0]).at[1].set(c_ctx)
    mod = _mod_vectors(cc, w_mod, b_mod)
    mod = mod[:, :2].reshape(depth, 2, 6, d)

    rope_mla = _rope_tables(seq, n_ctx, MLA_ROPE, MLA_NOPE)
    rope_dif = _rope_tables(seq, n_ctx, DIF_HEAD_DIM, 0)
    w_in_all = _take_columns(w_in, _in_proj_columns()).astype(BF16)
    w_uq_all = _take_columns(mla_w_uq, _head_columns(MLA_HEADS, MLA_QK, 0, MLA_QK, LANES)).astype(BF16)
    w_uk_all = _take_columns(mla_w_ukv, _head_columns(MLA_HEADS, MLA_NOPE + MLA_V, 0, MLA_NOPE, LANES)).astype(BF16)
    w_uv_all = _take_columns(mla_w_ukv, _head_columns(MLA_HEADS, MLA_NOPE + MLA_V, MLA_NOPE, MLA_V, MLA_V)).astype(BF16)
    w_up_all = [w.astype(BF16) for w in (w_up_mla, w_up_dif, w_up_ssm, w_out)]

    for i in range(depth):
        lam_init = 0.8 - 0.6 * math.exp(-0.3 * i)
        mod_lat, mod_ctx = mod[i, 0], mod[i, 1]
        u = _in_proj(x_all, g_norm1[i][None], mod_lat[0:2], mod_ctx[0:2], w_in_all, i, n_ctx)

        bound_mla = (MLA_QK ** 0.5 * LOG2E) * jnp.max(jnp.abs(mla_g_qn[i])) * jnp.max(jnp.abs(mla_g_kn[i]))
        bound_dif = (DIF_HEAD_DIM ** 0.5 * LOG2E) * jnp.max(jnp.abs(dif_g_qn[i])) * jnp.max(jnp.abs(dif_g_kn[i]))
        conv_w = jnp.zeros((8, SSM_XBC), F32).at[:SSM_CONV].set(ssm_conv_w[i])
        mla_qkv, dif_qkv, (xbc, dt) = _run_stages("branch_prep", (n_tok // ROW_TILE,), [
            _mla_prep(u, mla_g_q[i][None], w_uq_all[i], mla_g_kv[i][None], w_uk_all[i], w_uv_all[i],
                      _pad_lanes(mla_g_qn[i][None]), _pad_lanes(mla_g_kn[i][None]),
                      jnp.full((1, LANES), bound_mla, F32), rope_mla),
            _dif_prep(u, _pad_lanes(dif_g_qn[i][None]), _pad_lanes(dif_g_kn[i][None]),
                      jnp.full((1, LANES), bound_dif, F32), rope_dif),
            _ssm_prep(u, conv_w, ssm_conv_b[i][None], _pad_lanes(ssm_dt_bias[i].reshape(1, -1)), n_ctx)])
        ya = lax.cond(bound_mla <= MAX_SHIFT, functools.partial(_attention, shifted=True, group=SHIFTED_GROUP),
                      functools.partial(_attention, shifted=False), *mla_qkv)

        lam_rows = jnp.zeros((8, LANES), F32).at[:4, :DIF_HEAD_DIM].set(dif_lambda[i])
        dif_attention = functools.partial(_attention, lam=lam_rows, gsub=dif_g_sub[i][None], lam_init=lam_init)
        yb = lax.cond(bound_dif <= MAX_SHIFT, functools.partial(dif_attention, shifted=True, group=SHIFTED_GROUP),
                      functools.partial(dif_attention, shifted=False), *dif_qkv)

        a_log = _pad_lanes(ssm_a_log[i].reshape(1, -1))
        yf, yr = _ssd(xbc, dt, a_log, n_ctx)

        dskip = jnp.repeat(ssm_d[i, 0] + ssm_d[i, 1], SSM_HEAD_DIM)[None]
        w_router = jnp.zeros((d, LANES), BF16).at[:, :N_EXPERTS].set(moe_w_router[i].astype(BF16))
        b_router = jnp.zeros((1, LANES), F32).at[0, :N_EXPERTS].set(moe_b_router[i])
        x_all, f, logits = _merge(
            x_all, u, ya, yb, yf, yr, xbc, b_gate[i][None], dskip, ssm_g_norm[i][None],
            w_up_all[0][i], w_up_all[1][i], w_up_all[2][i], w_up_all[3][i],
            mod_lat[0:5], mod_ctx[0:5], g_norm2[i][None],
            w_router, b_router, n_ctx)

        gates, slot_tok, slot_of, block_e, n_used = _route(logits, n_tok)
        y_slots = _moe_experts(block_e, n_used, f[slot_tok], moe_w_gu,
                               moe_b_gu.reshape(depth, N_EXPERTS, 1, 2 * D_FF), moe_w_down,
                               moe_b_down.reshape(depth, N_EXPERTS, 1, d), i)
        y_rows = [y_slots[slot_of[:, k]] for k in range(TOP_K)]
        x_all = _combine(x_all, y_rows, _pad_lanes(gates), mod_lat[5:6], mod_ctx[5:6], n_ctx)
    return x_all[n_ctx:][None]
```

```python
import collections
import functools
import math

import numpy as np
import jax
import jax.numpy as jnp
from jax import lax
from jax.experimental import pallas as pl
from jax.experimental.pallas import tpu as pltpu

F32 = jnp.float32
BF16 = jnp.bfloat16
LANES = 128
VMEM_REQUEST_CAP = 56 * 1024 * 1024
VMEM_TEMPORARIES = 16 * 1024 * 1024

D_MODEL = 1024
EPS = 1e-6
ROPE_THETA = 10000.0
GRID_W = 64
N_BRANCH = 3

MLA_HEADS = 8
MLA_Q_RANK = 256
MLA_KV_RANK = 128
MLA_NOPE = 64
MLA_ROPE = 32
MLA_V = 64
MLA_QK = MLA_NOPE + MLA_ROPE
MLA_WIDTH = MLA_HEADS * MLA_V

DIF_HEADS = 4
DIF_HEAD_DIM = 64
DIF_WIDTH = DIF_HEADS * 2 * DIF_HEAD_DIM

SSM_HEADS = 8
SSM_HEAD_DIM = 64
SSM_WIDTH = SSM_HEADS * SSM_HEAD_DIM
SSM_GROUPS = 2
SSM_STATE = 128
SSM_CONV = 5
SSM_CHUNK = 128
SSM_XBC = SSM_WIDTH + 2 * SSM_GROUPS * SSM_STATE

N_EXPERTS = 32
TOP_K = 4
D_FF = 1024
SWIGLU_LIMIT = 7.0
SWIGLU_ALPHA = 1.702
MOE_BLOCK = 512

MLA_COLS = MLA_Q_RANK + MLA_KV_RANK + MLA_ROPE
DIF_COLS = 3 * DIF_WIDTH
SSM_COLS = SSM_WIDTH + SSM_XBC + 2 * SSM_HEADS
GATE_COLS = N_BRANCH * D_MODEL

ROW_TILE = 256
KEY_CHUNK = 256
LOG2E = 1.4426950408889634

U_GATE, U_DQ, U_DK, U_XBC, U_MLA, U_DV, U_Z, U_DT = 0, 3072, 4096, 5120, 6144, 6656, 7168, 7680
U_COLS = 8192
U_TILE_N = 2048


def _in_proj_columns():
    src = np.full((U_COLS,), -1, np.int64)
    dif0 = MLA_COLS
    ssm0 = MLA_COLS + DIF_COLS
    gate0 = ssm0 + SSM_COLS
    src[U_GATE:U_GATE + GATE_COLS] = gate0 + np.arange(GATE_COLS)
    for a in range(2 * DIF_HEADS):
        src[U_DQ + LANES * a:U_DQ + LANES * a + DIF_HEAD_DIM] = dif0 + DIF_HEAD_DIM * a + np.arange(DIF_HEAD_DIM)
        src[U_DK + LANES * a:U_DK + LANES * a + DIF_HEAD_DIM] = (dif0 + DIF_WIDTH + DIF_HEAD_DIM * a
                                                                  + np.arange(DIF_HEAD_DIM))
    src[U_DV:U_DV + DIF_WIDTH] = dif0 + 2 * DIF_WIDTH + np.arange(DIF_WIDTH)
    src[U_MLA:U_MLA + MLA_Q_RANK + MLA_KV_RANK] = np.arange(MLA_Q_RANK + MLA_KV_RANK)
    pe0 = U_MLA + MLA_Q_RANK + MLA_KV_RANK + MLA_NOPE
    src[pe0:pe0 + MLA_ROPE] = MLA_Q_RANK + MLA_KV_RANK + np.arange(MLA_ROPE)
    src[U_Z:U_Z + SSM_WIDTH] = ssm0 + np.arange(SSM_WIDTH)
    src[U_XBC:U_XBC + SSM_XBC] = ssm0 + SSM_WIDTH + np.arange(SSM_XBC)
    src[U_DT:U_DT + 2 * SSM_HEADS] = ssm0 + SSM_WIDTH + SSM_XBC + np.arange(2 * SSM_HEADS)
    return src


def _take_columns(w, src):
    pieces, start = [], 0
    for stop in range(1, len(src) + 1):
        same_run = stop < len(src) and (src[stop] == src[stop - 1] + 1 if src[stop - 1] >= 0 else src[stop] < 0)
        if not same_run:
            if src[start] < 0:
                pieces.append(jnp.zeros(w.shape[:-1] + (stop - start,), w.dtype))
            else:
                pieces.append(w[..., src[start]:src[start] + stop - start])
            start = stop
    return jnp.concatenate(pieces, axis=-1)


def _head_columns(n_heads, src_stride, src_off, width, dst_stride):
    src = np.full((n_heads * dst_stride,), -1, np.int64)
    for h in range(n_heads):
        src[h * dst_stride:h * dst_stride + width] = h * src_stride + src_off + np.arange(width)
    return src


def _pad_lanes(v, n=LANES):
    return jnp.pad(v, [(0, 0)] * (v.ndim - 1) + [(0, n - v.shape[-1])])


def _row_tile(n, cap):
    best = 8
    for t in range(8, cap + 1, 8):
        if n % t == 0:
            best = t
    return best


def _dot(a, b):
    return jnp.dot(a, b, preferred_element_type=F32)


def _dot_nt(a, b):
    return lax.dot_general(a, b, (((1,), (1,)), ((), ())), preferred_element_type=F32)


def _sigmoid(x):
    return 1.0 / (1.0 + jnp.exp(-x))


def _silu(x):
    return x * _sigmoid(x)


def _rms_rows(x, n):
    return x * lax.rsqrt(jnp.sum(x * x, axis=-1, keepdims=True) * (1.0 / n) + EPS)


def _as_tuple(x):
    return tuple(x) if isinstance(x, (tuple, list)) else (x,)


def _window_bytes(spec, dtype):
    if spec.block_shape is None:
        return 0
    dims = [1 if n is None else n for n in spec.block_shape]
    copies = 2 if spec.pipeline_mode is None else spec.pipeline_mode.buffer_count
    return math.prod(dims) * jnp.dtype(dtype).itemsize * copies


def _pallas_call(kernel, *, name, out_shape, grid, in_specs, out_specs, semantics, scratch_shapes=(),
                 num_scalar_prefetch=0):
    def call(*args):
        need = VMEM_TEMPORARIES
        need += sum(_window_bytes(s, a.dtype) for s, a in zip(in_specs, args[num_scalar_prefetch:]))
        need += sum(_window_bytes(s, o.dtype) for s, o in zip(_as_tuple(out_specs), _as_tuple(out_shape)))
        need += sum(math.prod(b.shape) * jnp.dtype(b.dtype).itemsize for b in scratch_shapes)
        params = pltpu.CompilerParams(dimension_semantics=semantics,
                                      vmem_limit_bytes=min(need, VMEM_REQUEST_CAP))
        grid_spec = pltpu.PrefetchScalarGridSpec(
            num_scalar_prefetch=num_scalar_prefetch, grid=grid, in_specs=list(in_specs), out_specs=out_specs,
            scratch_shapes=list(scratch_shapes))
        launch = pl.pallas_call(kernel, out_shape=out_shape, grid_spec=grid_spec, compiler_params=params, name=name)
        return launch(*args)
    return call


_Stage = collections.namedtuple("_Stage", "kernel out_shape in_specs out_specs scratch_shapes args")


def _run_stages(name, grid, stages):
    n_in = [len(s.in_specs) for s in stages]
    n_out = [len(s.out_shape) for s in stages]
    n_scr = [len(s.scratch_shapes) for s in stages]

    def body(*refs):
        ins, outs, scr = refs[:sum(n_in)], refs[sum(n_in):sum(n_in) + sum(n_out)], refs[sum(n_in) + sum(n_out):]
        for j, stage in enumerate(stages):
            stage.kernel(*ins[sum(n_in[:j]):sum(n_in[:j + 1])], *outs[sum(n_out[:j]):sum(n_out[:j + 1])],
                         *scr[sum(n_scr[:j]):sum(n_scr[:j + 1])])

    flat = lambda field: [x for s in stages for x in getattr(s, field)]
    outs = _pallas_call(body, name=name, grid=grid, out_shape=tuple(flat("out_shape")),
                        in_specs=flat("in_specs"), out_specs=tuple(flat("out_specs")),
                        scratch_shapes=flat("scratch_shapes"), semantics=("parallel",))(*flat("args"))
    return [outs[sum(n_out[:j]):sum(n_out[:j + 1])] for j in range(len(stages))]


def _mod_kernel(a_ref, w_ref, b_ref, o_ref):
    a = _silu(a_ref[...]).astype(BF16)
    o_ref[0] = _dot(a, w_ref[0].astype(BF16)) + b_ref[0]


def _mod_vectors(cc, w_mod, b_mod):
    depth, d, n = w_mod.shape
    tn = 1536
    return _pallas_call(
        _mod_kernel,
        out_shape=jax.ShapeDtypeStruct((depth, 8, n), F32),
        grid=(depth, n // tn),
        in_specs=[pl.BlockSpec((8, d), lambda l, j: (0, 0)),
                  pl.BlockSpec((1, d, tn), lambda l, j: (l, 0, j)),
                  pl.BlockSpec((1, 1, tn), lambda l, j: (l, 0, j))],
        out_specs=pl.BlockSpec((1, 8, tn), lambda l, j: (l, 0, j)),
        semantics=("parallel", "parallel"),
        name="mod_vectors",
    )(cc, w_mod, b_mod.reshape(depth, 1, n))


def _modulated_norm(x, g, mod_lat, mod_ctx, row0, n_ctx):
    rows = x.shape[0]
    is_ctx = (row0 + lax.broadcasted_iota(jnp.int32, (rows, 1), 0)) < n_ctx
    shift = jnp.where(is_ctx, mod_ctx[0:1, :], mod_lat[0:1, :])
    scale = jnp.where(is_ctx, mod_ctx[1:2, :], mod_lat[1:2, :])
    return _rms_rows(x, x.shape[1]) * g * (1.0 + scale) + shift


def _in_proj_kernel(x_ref, g_ref, ml_ref, mc_ref, w_ref, o_ref, h_ref, *, n_ctx, tm):
    @pl.when(pl.program_id(1) == 0)
    def _():
        h = _modulated_norm(x_ref[...], g_ref[...], ml_ref[...], mc_ref[...], pl.program_id(0) * tm, n_ctx)
        h_ref[...] = h.astype(BF16)

    o_ref[...] = _dot(h_ref[...], w_ref[...])


def _in_proj(x_all, g, mod_lat, mod_ctx, w, layer, n_ctx):
    t, d = x_all.shape
    n = w.shape[2]
    tm = _row_tile(t, 1280)
    return _pallas_call(
        functools.partial(_in_proj_kernel, n_ctx=n_ctx, tm=tm),
        out_shape=jax.ShapeDtypeStruct((t, n), F32),
        grid=(t // tm, n // U_TILE_N),
        in_specs=[pl.BlockSpec((tm, d), lambda i, j: (i, 0)),
                  pl.BlockSpec((1, d), lambda i, j: (0, 0)),
                  pl.BlockSpec((2, d), lambda i, j: (0, 0)),
                  pl.BlockSpec((2, d), lambda i, j: (0, 0)),
                  pl.BlockSpec((None, d, U_TILE_N), lambda i, j: (layer, 0, j))],
        out_specs=pl.BlockSpec((tm, U_TILE_N), lambda i, j: (i, j)),
        scratch_shapes=[pltpu.VMEM((tm, d), BF16)],
        semantics=("parallel", "arbitrary"),
        name="in_proj",
    )(x_all, g, mod_lat, mod_ctx, w)


def _rope_tables(seq_len, n_ctx, rot_dim, lane0):
    n_rows = seq_len // GRID_W
    row = jnp.repeat(jnp.arange(n_rows), GRID_W).astype(F32)
    col = jnp.tile(jnp.arange(GRID_W), n_rows).astype(F32)
    axis_dim = rot_dim // 2
    half = axis_dim // 2
    inv = ROPE_THETA ** (-jnp.arange(0, axis_dim, 2, dtype=F32) / axis_dim)
    ang_r = row[:, None] * inv
    ang_c = col[:, None] * inv
    zeros = jnp.zeros((seq_len, half), F32)
    cos = jnp.concatenate([jnp.cos(ang_r), jnp.cos(ang_r), jnp.cos(ang_c), jnp.cos(ang_c)], axis=1)
    s1 = jnp.concatenate([zeros, jnp.sin(ang_r), zeros, jnp.sin(ang_c)], axis=1)
    s2 = jnp.concatenate([-jnp.sin(ang_r), zeros, -jnp.sin(ang_c), zeros], axis=1)

    def place(tab, fill):
        full = jnp.full((seq_len, LANES), fill, F32).at[:, lane0:lane0 + rot_dim].set(tab)
        ctx = jnp.full((n_ctx, LANES), fill, F32)
        return jnp.concatenate([ctx, full], axis=0)

    return place(cos, 1.0), place(s1, 0.0), place(s2, 0.0)


def _norm_rope_blocks(blocks, gains, n, cos, s1, s2, half):
    sums = [jnp.sum(x * x, axis=-1, keepdims=True) for x in blocks]
    inv = [lax.rsqrt(s * (1.0 / n) + EPS) for s in sums]
    normed = [x * r * g for x, r, g in zip(blocks, inv, gains)]
    fwd = [pltpu.roll(y, half, 1) for y in normed]
    bwd = [pltpu.roll(y, LANES - half, 1) for y in normed]
    return [y * cos + a * s1 + b * s2 for y, a, b in zip(normed, fwd, bwd)]


def _mla_prep_kernel(u_ref, gq_ref, wuq_ref, gkv_ref, wk_ref, wv_ref, gqn_ref, gkn_ref, shift_ref,
                     cos_ref, s1_ref, s2_ref, qt_ref, k_ref, vt_ref):
    u = u_ref[...]
    cq = u[:, :MLA_Q_RANK]
    ckv = u[:, MLA_Q_RANK:MLA_Q_RANK + MLA_KV_RANK]
    pe = u[:, MLA_Q_RANK + MLA_KV_RANK:]
    q = _dot((_rms_rows(cq, MLA_Q_RANK) * gq_ref[...]).astype(BF16), wuq_ref[...])
    kv_in = (_rms_rows(ckv, MLA_KV_RANK) * gkv_ref[...]).astype(BF16)
    kn = _dot(kv_in, wk_ref[...])
    v = _dot(kv_in, wv_ref[...])
    cos, s1, s2 = cos_ref[...], s1_ref[...], s2_ref[...]
    half = MLA_ROPE // 4
    q_scale = MLA_QK ** -0.5 * LOG2E
    n = MLA_HEADS
    blocks = ([q[:, LANES * h:LANES * (h + 1)] for h in range(n)]
              + [kn[:, LANES * h:LANES * (h + 1)] + pe for h in range(n)])
    roped = _norm_rope_blocks(blocks, [gqn_ref[...]] * n + [gkn_ref[...]] * n, MLA_QK, cos, s1, s2, half)
    spare = lax.broadcasted_iota(jnp.int32, roped[0].shape, 1) == MLA_QK
    for h in range(n):
        qt_ref[h] = jnp.where(spare, -shift_ref[...], roped[h] * q_scale).T.astype(BF16)
        k_ref[h, 0] = jnp.where(spare, 1.0, roped[n + h]).astype(BF16)
    tm = v.shape[0]
    vt_ref[:, 0] = _with_sum_rows(v.T.reshape(MLA_HEADS, MLA_V, tm)).astype(BF16)


def _mla_prep(u, gq, wuq, gkv, wk, wv, gqn, gkn, shift, tabs):
    t = u.shape[0]
    tm = ROW_TILE
    nt = t // tm
    const = lambda i: (0, 0)
    rows = lambda i: (i, 0)
    return _Stage(
        kernel=_mla_prep_kernel,
        out_shape=(jax.ShapeDtypeStruct((MLA_HEADS, LANES, t), BF16),
                   jax.ShapeDtypeStruct((MLA_HEADS, nt, tm, LANES), BF16),
                   jax.ShapeDtypeStruct((MLA_HEADS, nt, MLA_V + SUM_ROWS, tm), BF16)),
        in_specs=[pl.BlockSpec((tm, 512), lambda i: (i, U_MLA // 512)),
                  pl.BlockSpec((1, MLA_Q_RANK), const),
                  pl.BlockSpec(wuq.shape, const),
                  pl.BlockSpec((1, MLA_KV_RANK), const),
                  pl.BlockSpec(wk.shape, const),
                  pl.BlockSpec(wv.shape, const),
                  pl.BlockSpec((1, LANES), const),
                  pl.BlockSpec((1, LANES), const),
                  pl.BlockSpec((1, LANES), const),
                  pl.BlockSpec((tm, LANES), rows),
                  pl.BlockSpec((tm, LANES), rows),
                  pl.BlockSpec((tm, LANES), rows)],
        out_specs=(pl.BlockSpec((MLA_HEADS, LANES, tm), lambda i: (0, 0, i)),
                   pl.BlockSpec((MLA_HEADS, 1, tm, LANES), lambda i: (0, i, 0, 0)),
                   pl.BlockSpec((MLA_HEADS, 1, MLA_V + SUM_ROWS, tm), lambda i: (0, i, 0, 0))),
        scratch_shapes=(),
        args=(u, gq, wuq, gkv, wk, wv, gqn, gkn, shift, *tabs))


def _dif_prep_kernel(q_ref, k_ref, v_ref, gq_ref, gk_ref, shift_ref, cos_ref, s1_ref, s2_ref,
                     qt_out, k_out, vt_out):
    cos, s1, s2 = cos_ref[...], s1_ref[...], s2_ref[...]
    half = DIF_HEAD_DIM // 4
    q_scale = DIF_HEAD_DIM ** -0.5 * LOG2E
    n = 2 * DIF_HEADS
    blocks = ([q_ref[:, LANES * a:LANES * (a + 1)] for a in range(n)]
              + [k_ref[:, LANES * a:LANES * (a + 1)] for a in range(n)])
    roped = _norm_rope_blocks(blocks, [gq_ref[...]] * n + [gk_ref[...]] * n, DIF_HEAD_DIM, cos, s1, s2, half)
    spare = lax.broadcasted_iota(jnp.int32, roped[0].shape, 1) == DIF_HEAD_DIM
    for a in range(n):
        qt_out[a] = jnp.where(spare, -shift_ref[...], roped[a] * q_scale).T.astype(BF16)
        k_out[a, 0] = jnp.where(spare, 1.0, roped[n + a]).astype(BF16)
    v = v_ref[...]
    vt_out[:, 0] = _with_sum_rows(v.T.reshape(DIF_HEADS, 2 * DIF_HEAD_DIM, v.shape[0])).astype(BF16)


def _dif_prep(u, gq, gk, shift, tabs):
    t = u.shape[0]
    tm = ROW_TILE
    nt = t // tm
    nsub = 2 * DIF_HEADS
    const = lambda i: (0, 0)
    rows = lambda i: (i, 0)
    return _Stage(
        kernel=_dif_prep_kernel,
        out_shape=(jax.ShapeDtypeStruct((nsub, LANES, t), BF16),
                   jax.ShapeDtypeStruct((nsub, nt, tm, LANES), BF16),
                   jax.ShapeDtypeStruct((DIF_HEADS, nt, 2 * DIF_HEAD_DIM + SUM_ROWS, tm), BF16)),
        in_specs=[pl.BlockSpec((tm, 1024), lambda i: (i, U_DQ // 1024)),
                  pl.BlockSpec((tm, 1024), lambda i: (i, U_DK // 1024)),
                  pl.BlockSpec((tm, 512), lambda i: (i, U_DV // 512)),
                  pl.BlockSpec((1, LANES), const),
                  pl.BlockSpec((1, LANES), const),
                  pl.BlockSpec((1, LANES), const),
                  pl.BlockSpec((tm, LANES), rows),
                  pl.BlockSpec((tm, LANES), rows),
                  pl.BlockSpec((tm, LANES), rows)],
        out_specs=(pl.BlockSpec((nsub, LANES, tm), lambda i: (0, 0, i)),
                   pl.BlockSpec((nsub, 1, tm, LANES), lambda i: (0, i, 0, 0)),
                   pl.BlockSpec((DIF_HEADS, 1, 2 * DIF_HEAD_DIM + SUM_ROWS, tm), lambda i: (0, i, 0, 0))),
        scratch_shapes=(),
        args=(u, u, u, gq, gk, shift, *tabs))


ATTN_GROUP = 2
ATTN_TRIP = 64
SUM_ROWS = 16


def _with_sum_rows(vt):
    heads, _, keys = vt.shape
    row = lax.broadcasted_iota(jnp.int32, (heads, SUM_ROWS, keys), 1)
    return jnp.concatenate([vt, jnp.where(row == 0, 1.0, 0.0).astype(vt.dtype)], axis=1)


def _attn_group(qt_ref, k_ref, vt_ref, v_of_sub, s_buf, p_buf, acc_ref, n_chunks):
    group, _, tq = qt_ref.shape
    last = n_chunks - 1

    def scores(c):
        out = []
        for a in range(group):
            s = _dot(k_ref[a, c], qt_ref[a])
            out.append((s, jnp.max(s, axis=0, keepdims=True)))
        return out

    def stash(sc, slot):
        for a in range(group):
            s_buf[slot, a] = sc[a][0]
        return tuple(mx for (_, mx) in sc)

    def softmax(s_of, mx, slot, ms):
        new_ms, alphas = [], []
        for a in range(group):
            m_new = jnp.maximum(ms[a], mx[a])
            alphas.append(jnp.exp2(ms[a] - m_new))
            p_buf[slot, a] = jnp.exp2(s_of(a) - m_new).astype(BF16)
            new_ms.append(m_new)
        return tuple(new_ms), tuple(alphas)

    def values(c, slot, alphas):
        for a in range(group):
            acc_ref[a] = alphas[a] * acc_ref[a] + _dot(vt_ref[v_of_sub[a], c], p_buf[slot, a])

    per_trip = math.gcd(last, ATTN_TRIP)

    def trip(t, state):
        ms, alphas, mx0 = state
        c = per_trip * t + 1
        for pair in range(per_trip // 2):
            cur, nxt = pair % 2, 1 - pair % 2
            even = scores(c + 1)
            values(c - 1, 0, alphas)
            ms, alphas = softmax(lambda a: s_buf[cur, a], mx0, 1, ms)
            mx0 = stash(scores(jnp.minimum(c + 2, last)), nxt)
            values(c, 1, alphas)
            ms, alphas = softmax(lambda a: even[a][0], [mx for (_, mx) in even], 0, ms)
            c = c + 2
        return ms, alphas, mx0

    def context_chunk():
        ms = tuple(jnp.full((1, tq), -jnp.inf, F32) for _ in range(group))
        first = scores(0)
        return softmax(lambda a: first[a][0], [mx for (_, mx) in first], 0, ms)

    acc_ref[...] = jnp.zeros(acc_ref.shape, F32)

    @pl.when(pl.program_id(1) == 0)
    def _():
        _, alphas = context_chunk()
        values(0, 0, alphas)

    @pl.when(pl.program_id(1) > 0)
    def _():
        ms, alphas = context_chunk()
        state = (ms, alphas, stash(scores(1), 0))
        if per_trip == last:
            _, alphas, _ = trip(0, state)
        else:
            _, alphas, _ = lax.fori_loop(0, last // per_trip, trip, state)
        values(last, 0, alphas)


def _normalised(acc_ref, a):
    dv = acc_ref.shape[1] - SUM_ROWS
    return acc_ref[a, :dv] * (1.0 / acc_ref[a, dv:dv + 1])


PV_SPAN_WORK = 64
SHIFTED_GROUP = 4
MAX_SHIFT = 60.0


def _attn_group_shifted(qt_ref, k_ref, vt_ref, v_of_sub, p_buf, acc_ref, n_chunks):
    group = qt_ref.shape[0]
    span = _value_span(n_chunks, group)

    def accumulate(c0, n, slot):
        for a in range(group):
            for i in range(n):
                s = _dot(k_ref[a, c0 + i], qt_ref[a])
                p_buf[slot, a, KEY_CHUNK * i:KEY_CHUNK * (i + 1)] = jnp.exp2(s).astype(BF16)
            vt = jnp.concatenate([vt_ref[v_of_sub[a], c0 + i] for i in range(n)], axis=1)
            acc_ref[a] = acc_ref[a] + _dot(vt, p_buf[slot, a, :KEY_CHUNK * n])

    acc_ref[...] = jnp.zeros(acc_ref.shape, F32)

    @pl.when(pl.program_id(1) == 0)
    def _():
        accumulate(0, 1, 0)

    @pl.when(pl.program_id(1) > 0)
    def _():
        accumulate(0, span + 1, 0)
        for j in range(1, (n_chunks - 1) // span):
            accumulate(1 + span * j, span, j % 2)


def _attend(shifted, qt_ref, k_ref, vt_ref, v_of_sub, scratch, n_chunks):
    if shifted:
        p_buf, acc_ref = scratch
        _attn_group_shifted(qt_ref, k_ref, vt_ref, v_of_sub, p_buf, acc_ref, n_chunks)
    else:
        s_buf, p_buf, acc_ref = scratch
        _attn_group(qt_ref, k_ref, vt_ref, v_of_sub, s_buf, p_buf, acc_ref, n_chunks)
    return acc_ref


def _value_span(n_chunks, group):
    return math.gcd(n_chunks - 1, PV_SPAN_WORK // group)


def _mla_attn_kernel(qt_ref, k_ref, vt_ref, o_ref, *scratch, n_chunks, shifted):
    group = qt_ref.shape[0]
    acc_ref = _attend(shifted, qt_ref, k_ref, vt_ref, tuple(range(group)), scratch, n_chunks)
    o = jnp.concatenate([_normalised(acc_ref, a) for a in range(group)], axis=0)
    o_ref[...] = o.T


def _dif_attn_kernel(lam_ref, gsub_ref, qt_ref, k_ref, vt_ref, o_ref, *scratch, n_chunks, shifted, lam_init):
    group = qt_ref.shape[0]
    acc_ref = _attend(shifted, qt_ref, k_ref, vt_ref, tuple(a // 2 for a in range(group)), scratch,
                      n_chunks)
    lp = lam_ref[...]
    lam = (jnp.exp(jnp.sum(lp[0:1] * lp[1:2], axis=-1, keepdims=True))
           - jnp.exp(jnp.sum(lp[2:3] * lp[3:4], axis=-1, keepdims=True)) + lam_init)
    outs = []
    for h in range(group // 2):
        o = _normalised(acc_ref, 2 * h) - lam * _normalised(acc_ref, 2 * h + 1)
        outs.append(o * lax.rsqrt(jnp.mean(o * o, axis=0, keepdims=True) + EPS))
    gsub = gsub_ref[...] * (1.0 - lam_init)
    o_ref[...] = jnp.concatenate(outs, axis=0).T * jnp.concatenate([gsub] * len(outs), axis=1)


def _attention(qt, k, vt, *, group=ATTN_GROUP, shifted=False, lam=None, gsub=None, lam_init=None):
    nsub, _, t = qt.shape
    n_chunks = k.shape[1]
    assert (n_chunks - 1) % 4 == 0 and nsub % group == 0
    tq = KEY_CHUNK
    dv = vt.shape[2] - SUM_ROWS
    groups = nsub // group
    n_v = vt.shape[0] // groups
    out_w = n_v * dv
    resident = dict(pipeline_mode=pl.Buffered(1))
    specs = [pl.BlockSpec((group, LANES, tq), lambda g, i: (g, 0, i)),
             pl.BlockSpec((group, n_chunks, KEY_CHUNK, LANES), lambda g, i: (g, 0, 0, 0), **resident),
             pl.BlockSpec((n_v, n_chunks, dv + SUM_ROWS, KEY_CHUNK), lambda g, i: (g, 0, 0, 0), **resident)]
    if lam is None:
        body = functools.partial(_mla_attn_kernel, n_chunks=n_chunks, shifted=shifted)
        args = (qt, k, vt)
    else:
        body = functools.partial(_dif_attn_kernel, n_chunks=n_chunks, shifted=shifted, lam_init=lam_init)
        specs = [pl.BlockSpec((8, LANES), lambda g, i: (0, 0)),
                 pl.BlockSpec((1, LANES), lambda g, i: (0, 0))] + specs
        args = (lam, gsub, qt, k, vt)
    acc = pltpu.VMEM((group, dv + SUM_ROWS, tq), F32)
    if shifted:
        scratch = [pltpu.VMEM((2, group, (_value_span(n_chunks, group) + 1) * KEY_CHUNK, tq), BF16), acc]
    else:
        scratch = [pltpu.VMEM((2, group, KEY_CHUNK, tq), F32), pltpu.VMEM((2, group, KEY_CHUNK, tq), BF16), acc]
    return _pallas_call(
        body,
        out_shape=jax.ShapeDtypeStruct((t, groups * out_w), F32),
        grid=(groups, t // tq),
        in_specs=specs,
        out_specs=pl.BlockSpec((tq, out_w), lambda g, i: (i, g)),
        scratch_shapes=scratch,
        semantics=("parallel", "arbitrary"),
        name=("mla_attention" if lam is None else "dif_attention") + ("_shifted" if shifted else ""),
    )(*args)


def _ssm_prep_kernel(x_ref, prev_ref, next_ref, dt_ref, w_ref, b_ref, dtb_ref, xo_ref, dto_ref, e_ref,
                     *, n_ctx, n_tok, tm):
    row0 = pl.program_id(0) * tm
    pad = SSM_CONV // 2
    has_prev = jnp.logical_and(row0 != 0, row0 != n_ctx)
    has_next = jnp.logical_and(row0 + tm != n_ctx, row0 + tm != n_tok)
    e_ref[0:8] = jnp.where(has_prev, prev_ref[...], 0.0)
    e_ref[8:8 + tm] = x_ref[...]
    e_ref[8 + tm:16 + tm] = jnp.where(has_next, next_ref[...], 0.0)
    acc = jnp.zeros(x_ref.shape, F32) + b_ref[...]
    for k in range(SSM_CONV):
        acc = acc + w_ref[k:k + 1, :] * e_ref[pl.ds(8 - pad + k, tm), :]
    xo_ref[...] = _silu(acc)
    d = dt_ref[...] + dtb_ref[...]
    dto_ref[...] = jnp.maximum(d, 0.0) + jnp.log1p(jnp.exp(-jnp.abs(d)))


def _ssm_prep(u, conv_w, conv_b, dt_bias, n_ctx):
    t = u.shape[0]
    tm = ROW_TILE
    nt = t // tm
    cb = U_XBC // SSM_XBC
    const = lambda i: (0, 0)
    return _Stage(
        kernel=functools.partial(_ssm_prep_kernel, n_ctx=n_ctx, n_tok=t, tm=tm),
        out_shape=(jax.ShapeDtypeStruct((t, SSM_XBC), F32), jax.ShapeDtypeStruct((t, LANES), F32)),
        in_specs=[pl.BlockSpec((tm, SSM_XBC), lambda i: (i, cb)),
                  pl.BlockSpec((8, SSM_XBC), lambda i: (jnp.maximum(i * (tm // 8) - 1, 0), cb)),
                  pl.BlockSpec((8, SSM_XBC), lambda i: (jnp.minimum((i + 1) * (tm // 8), t // 8 - 1), cb)),
                  pl.BlockSpec((tm, LANES), lambda i: (i, U_DT // LANES)),
                  pl.BlockSpec((8, SSM_XBC), const),
                  pl.BlockSpec((1, SSM_XBC), const),
                  pl.BlockSpec((1, LANES), const)],
        out_specs=(pl.BlockSpec((tm, SSM_XBC), lambda i: (i, 0)),
                   pl.BlockSpec((tm, LANES), lambda i: (i, 0))),
        scratch_shapes=(pltpu.VMEM((tm + 16, SSM_XBC), F32),),
        args=(u, u, u, u, conv_w, conv_b, dt_bias))


def _ssd_kernel(xf_ref, dtf_ref, xr_ref, dtr_ref, alog_ref, of_ref, or_ref, hf_ref, hr_ref):
    @pl.when(pl.program_id(0) == 0)
    def _():
        hf_ref[...] = jnp.zeros(hf_ref.shape, F32)
        hr_ref[...] = jnp.zeros(hr_ref.shape, F32)

    lc = SSM_CHUNK
    a_row = -jnp.exp(alog_ref[...])
    n_sub = xf_ref.shape[0] // lc
    fwd = [_ssd_local(xf_ref[lc * j:lc * (j + 1), :], dtf_ref[lc * j:lc * (j + 1), :], a_row, 0)
           for j in range(n_sub)]
    bwd = [_ssd_local(xr_ref[lc * j:lc * (j + 1), :], dtr_ref[lc * j:lc * (j + 1), :], a_row, 1)
           for j in range(n_sub)]
    of_ref[...] = jnp.concatenate([_ssd_carry(part, hf_ref) for part in fwd], axis=0)
    or_ref[...] = jnp.concatenate([_ssd_carry(part, hr_ref) for part in reversed(bwd)][::-1], axis=0)


def _ssd_carry(local, h_ref):
    outs = []
    for h, (y, c_g, from_start, decay, upd) in enumerate(local):
        state = h_ref[h]
        outs.append(y + _dot(c_g, state.astype(BF16)) * from_start)
        h_ref[h] = state * decay + upd
    return jnp.concatenate(outs, axis=1)


def _ssd_local(xbc, dt, a_row, direction):
    lc = SSM_CHUNK
    hd, per_group = SSM_HEAD_DIM, SSM_HEADS // SSM_GROUPS
    dta = dt * a_row
    r = lax.broadcasted_iota(jnp.int32, (lc, lc), 0)
    c = lax.broadcasted_iota(jnp.int32, (lc, lc), 1)
    keep = (r >= c) if direction == 0 else (r <= c)
    tri = jnp.where(keep, 1.0, 0.0).astype(F32)
    cum = jnp.dot(tri, dta, preferred_element_type=F32, precision=lax.Precision.HIGHEST)
    cum_t = cum.T
    dt_t = dt.T
    total = jnp.sum(dta, axis=0, keepdims=True)
    chunk_decay = jnp.exp(total)
    w_t = (dt * jnp.exp(total - cum)).T
    b_f32 = [xbc[:, SSM_WIDTH + SSM_STATE * g:SSM_WIDTH + SSM_STATE * (g + 1)] for g in range(SSM_GROUPS)]
    c_bf = [xbc[:, SSM_WIDTH + SSM_STATE * (SSM_GROUPS + g):SSM_WIDTH + SSM_STATE * (SSM_GROUPS + g + 1)].astype(BF16)
            for g in range(SSM_GROUPS)]
    cb = [_dot_nt(c, b.astype(BF16)) for c, b in zip(c_bf, b_f32)]
    b_t = [b.T for b in b_f32]
    heads = range(SSM_HEADS)
    cols = [direction * SSM_HEADS + h for h in heads]
    cum_b = [jnp.broadcast_to(cum[:, c:c + 1], (lc, lc)) for c in cols]
    seg = [jnp.exp(jnp.where(keep, cb_ - cum_t[c:c + 1, :], -jnp.inf)) for cb_, c in zip(cum_b, cols)]
    x_h = [xbc[:, hd * h:hd * (h + 1)].astype(BF16) for h in heads]
    scores = [(cb[h // per_group] * s * dt_t[c:c + 1, :]).astype(BF16) for h, s, c in zip(heads, seg, cols)]
    weighted_b = [(b_t[h // per_group] * w_t[c:c + 1, :]).astype(BF16) for h, c in zip(heads, cols)]
    y = [_dot(s, x) for s, x in zip(scores, x_h)]
    upd = [_dot(wb, x) for wb, x in zip(weighted_b, x_h)]
    return [(y[h], c_bf[h // per_group], jnp.exp(cum_b[h][:, :hd]), chunk_decay[:, cols[h]:cols[h] + 1], upd[h])
            for h in heads]


SSD_STEP = 2 * SSM_CHUNK


def _ssd(xbc, dt, a_log, n_ctx):
    t = xbc.shape[0]
    lc = SSD_STEP
    assert n_ctx % lc == 0 and t % lc == 0
    nc = t // lc
    ncc = n_ctx // lc
    fwd = lambda s: (s, 0)
    bwd = lambda s: (jnp.where(s < ncc, ncc - 1 - s, nc - 1 - (s - ncc)), 0)
    state = pltpu.VMEM((SSM_HEADS, SSM_STATE, SSM_HEAD_DIM), F32)
    out = jax.ShapeDtypeStruct((t, SSM_WIDTH), F32)
    return _pallas_call(
        _ssd_kernel,
        out_shape=(out, out),
        grid=(nc,),
        in_specs=[pl.BlockSpec((lc, SSM_XBC), fwd), pl.BlockSpec((lc, LANES), fwd),
                  pl.BlockSpec((lc, SSM_XBC), bwd), pl.BlockSpec((lc, LANES), bwd),
                  pl.BlockSpec((1, LANES), lambda s: (0, 0))],
        out_specs=(pl.BlockSpec((lc, SSM_WIDTH), fwd), pl.BlockSpec((lc, SSM_WIDTH), bwd)),
        scratch_shapes=[state, state],
        semantics=("arbitrary",),
        name="ssd_scan",
    )(xbc, dt, xbc, dt, a_log)


def _merge_kernel(x_ref, gate_ref, ya_ref, yb_ref, yf_ref, yr_ref, xs_ref, z_ref,
                  bg_ref, dskip_ref, gssm_ref, wa_ref, wb_ref, wc_ref, wo_ref,
                  ml_ref, mc_ref, g2_ref, wr_ref, br_ref,
                  xo_ref, f_ref, routed_ref, cnt_ref, run_ref, *, n_ctx, tm):
    y = (yf_ref[...] + yr_ref[...] + dskip_ref[...] * xs_ref[...]) * _silu(z_ref[...])
    gw = SSM_WIDTH // SSM_GROUPS
    yc = jnp.concatenate([_rms_rows(y[:, gw * g:gw * (g + 1)], gw) for g in range(SSM_GROUPS)], axis=1)
    yc = yc * gssm_ref[...]
    gate = _sigmoid(gate_ref[...] + bg_ref[...])
    m = (gate[:, :D_MODEL] * _dot(ya_ref[...].astype(BF16), wa_ref[...])
         + gate[:, D_MODEL:2 * D_MODEL] * _dot(yb_ref[...].astype(BF16), wb_ref[...])
         + gate[:, 2 * D_MODEL:] * _dot(yc.astype(BF16), wc_ref[...]))
    out = _dot(m.astype(BF16), wo_ref[...])
    row0 = pl.program_id(0) * tm
    is_ctx = (row0 + lax.broadcasted_iota(jnp.int32, (tm, 1), 0)) < n_ctx
    gt1 = jnp.where(is_ctx, mc_ref[2:3, :], ml_ref[2:3, :])
    x_new = x_ref[...] + gt1 * out
    xo_ref[...] = x_new
    f = _modulated_norm(x_new, g2_ref[...], ml_ref[3:5, :], mc_ref[3:5, :], row0, n_ctx)
    f_ref[...] = f
    _route_rows(_dot(f.astype(BF16), wr_ref[...]) + br_ref[...], routed_ref, cnt_ref, run_ref, tm)


def _merge(x_all, u, ya, yb, yf, yr, xbc, b_gate, dskip, g_ssm, wa, wb, wc, wo, mod_lat, mod_ctx,
           g2, w_router, b_router, n_ctx):
    t, d = x_all.shape
    tm = ROW_TILE
    const = lambda i: (0, 0)
    rows = lambda i: (i, 0)
    full = lambda a: pl.BlockSpec(a.shape, const)
    return _pallas_call(
        functools.partial(_merge_kernel, n_ctx=n_ctx, tm=tm),
        out_shape=(jax.ShapeDtypeStruct((t, d), F32), jax.ShapeDtypeStruct((t, d), F32),
                   jax.ShapeDtypeStruct((t, LANES), F32), jax.ShapeDtypeStruct((8, LANES), F32)),
        grid=(t // tm,),
        scratch_shapes=[pltpu.VMEM((8, LANES), F32)],
        in_specs=[pl.BlockSpec((tm, d), rows),
                  pl.BlockSpec((tm, GATE_COLS), lambda i: (i, U_GATE // GATE_COLS)),
                  pl.BlockSpec((tm, MLA_WIDTH), rows),
                  pl.BlockSpec((tm, DIF_WIDTH), rows),
                  pl.BlockSpec((tm, SSM_WIDTH), rows),
                  pl.BlockSpec((tm, SSM_WIDTH), rows),
                  pl.BlockSpec((tm, SSM_WIDTH), rows),
                  pl.BlockSpec((tm, SSM_WIDTH), lambda i: (i, U_Z // SSM_WIDTH)),
                  full(b_gate), full(dskip), full(g_ssm), full(wa), full(wb), full(wc), full(wo),
                  full(mod_lat), full(mod_ctx), full(g2), full(w_router), full(b_router)],
        out_specs=(pl.BlockSpec((tm, d), rows), pl.BlockSpec((tm, d), rows),
                   pl.BlockSpec((tm, LANES), rows), pl.BlockSpec((8, LANES), const)),
        semantics=("arbitrary",),
        name="merge",
    )(x_all, u, ya, yb, yf, yr, xbc, u, b_gate, dskip, g_ssm, wa, wb, wc, wo, mod_lat, mod_ctx,
      g2, w_router, b_router)


def _moe_kernel(be_ref, nb_ref, x_ref, wgu_ref, bgu_ref, wd_ref, bd_ref, o_ref, wgu_s, wd_s):
    b = pl.program_id(0)
    prev = be_ref[jnp.maximum(b - 1, 0)]
    fresh = jnp.logical_or(b == 0, be_ref[b] != prev)

    @pl.when(fresh)
    def _():
        wgu_s[...] = wgu_ref[...].astype(BF16)
        wd_s[...] = wd_ref[...].astype(BF16)

    @pl.when(b < nb_ref[0])
    def _():
        gu = _dot(x_ref[...].astype(BF16), wgu_s[...]) + bgu_ref[...]
        glu = jnp.minimum(gu[:, :D_FF], SWIGLU_LIMIT)
        lin = jnp.clip(gu[:, D_FF:], -SWIGLU_LIMIT, SWIGLU_LIMIT)
        act = glu * _sigmoid(SWIGLU_ALPHA * glu) * (lin + 1.0)
        o_ref[...] = _dot(act.astype(BF16), wd_s[...]) + bd_ref[...]

    @pl.when(b >= nb_ref[0])
    def _():
        o_ref[...] = jnp.zeros(o_ref.shape, F32)


def _moe_experts(block_e, n_used, x_sorted, w_gu, b_gu, w_down, b_down, layer):
    n_slots, d = x_sorted.shape
    n_blocks = n_slots // MOE_BLOCK
    by_expert = lambda b, be, nb: (layer, be[b], 0, 0)
    return _pallas_call(
        _moe_kernel,
        out_shape=jax.ShapeDtypeStruct((n_slots, d), F32),
        num_scalar_prefetch=2,
        grid=(n_blocks,),
        in_specs=[pl.BlockSpec((MOE_BLOCK, d), lambda b, be, nb: (b, 0)),
                  pl.BlockSpec((None, None, d, 2 * D_FF), by_expert),
                  pl.BlockSpec((None, None, 1, 2 * D_FF), by_expert),
                  pl.BlockSpec((None, None, D_FF, d), by_expert),
                  pl.BlockSpec((None, None, 1, d), by_expert)],
        out_specs=pl.BlockSpec((MOE_BLOCK, d), lambda b, be, nb: (b, 0)),
        scratch_shapes=[pltpu.VMEM((d, 2 * D_FF), BF16), pltpu.VMEM((D_FF, d), BF16)],
        semantics=("arbitrary",),
        name="moe_experts",
    )(block_e, n_used, x_sorted, w_gu, b_gu, w_down, b_down)


def _combine_kernel(x_ref, g_ref, ml_ref, mc_ref, *rest, n_ctx, tm):
    y_refs, o_ref = rest[:TOP_K], rest[TOP_K]
    is_ctx = (pl.program_id(0) * tm + lax.broadcasted_iota(jnp.int32, (tm, 1), 0)) < n_ctx
    gate2 = jnp.where(is_ctx, mc_ref[...], ml_ref[...])
    g = g_ref[...]
    y = g[:, 0:1] * y_refs[0][...]
    for k in range(1, TOP_K):
        y = y + g[:, k:k + 1] * y_refs[k][...]
    o_ref[...] = x_ref[...] + gate2 * y


def _combine(x_all, y_rows, gates, gate2_lat, gate2_ctx, n_ctx):
    t, d = x_all.shape
    tm = ROW_TILE
    rows = lambda i: (i, 0)
    const = lambda i: (0, 0)
    return _pallas_call(
        functools.partial(_combine_kernel, n_ctx=n_ctx, tm=tm),
        out_shape=jax.ShapeDtypeStruct((t, d), F32),
        grid=(t // tm,),
        in_specs=[pl.BlockSpec((tm, d), rows), pl.BlockSpec((tm, LANES), rows),
                  pl.BlockSpec((1, d), const), pl.BlockSpec((1, d), const)]
                 + [pl.BlockSpec((tm, d), rows)] * TOP_K,
        out_specs=pl.BlockSpec((tm, d), rows),
        semantics=("parallel",),
        name="moe_combine",
    )(x_all, gates, gate2_lat, gate2_ctx, *y_rows)


ROUTE_IDX, ROUTE_GATE, ROUTE_RANK = 0, TOP_K, 2 * TOP_K


def _route_rows(logits, o_ref, cnt_ref, run_ref, tm):
    @pl.when(pl.program_id(0) == 0)
    def _():
        run_ref[...] = jnp.zeros(run_ref.shape, F32)

    lane = lax.broadcasted_iota(jnp.int32, (tm, LANES), 1)
    lane_f = lane.astype(F32)
    lg = jnp.where(lane < N_EXPERTS, logits, -jnp.inf)
    hots, vals = [], []
    for _ in range(TOP_K):
        mx = jnp.max(lg, axis=-1, keepdims=True)
        idx = jnp.min(jnp.where(lg == mx, lane_f, float(LANES)), axis=-1, keepdims=True)
        hot = lane_f == idx
        lg = jnp.where(hot, -jnp.inf, lg)
        hots.append((hot, idx))
        vals.append(mx)
    exps = [jnp.exp(v - vals[0]) for v in vals]
    inv = 1.0 / sum(exps[1:], exps[0])
    chosen = jnp.zeros((tm, LANES), F32)
    for hot, _ in hots:
        chosen = jnp.where(hot, 1.0, chosen)
    r = lax.broadcasted_iota(jnp.int32, (tm, tm), 0)
    c = lax.broadcasted_iota(jnp.int32, (tm, tm), 1)
    earlier = jnp.where(r > c, 1.0, 0.0).astype(BF16)
    before = _dot(earlier, chosen.astype(BF16)) + run_ref[0:1, :]
    out = jnp.zeros((tm, LANES), F32)
    for k, (hot, idx) in enumerate(hots):
        rank = jnp.sum(jnp.where(hot, before, 0.0), axis=-1, keepdims=True)
        out = jnp.where(lane == ROUTE_IDX + k, idx, out)
        out = jnp.where(lane == ROUTE_GATE + k, exps[k] * inv, out)
        out = jnp.where(lane == ROUTE_RANK + k, rank, out)
    o_ref[...] = out
    run_ref[...] = run_ref[...] + jnp.sum(chosen, axis=0, keepdims=True)
    cnt_ref[...] = run_ref[...]


def _route(routed, counts, n_tok):
    top_idx = routed[:, ROUTE_IDX:ROUTE_IDX + TOP_K].astype(jnp.int32)
    gates = routed[:, ROUTE_GATE:ROUTE_GATE + TOP_K]
    rank = routed[:, ROUTE_RANK:ROUTE_RANK + TOP_K].astype(jnp.int32)
    n_assign = n_tok * TOP_K
    counts = counts[0, :N_EXPERTS].astype(jnp.int32)
    padded = (counts + MOE_BLOCK - 1) // MOE_BLOCK * MOE_BLOCK
    pad_end = jnp.cumsum(padded)
    pad_start = pad_end - padded
    slot_of = pad_start[top_idx] + rank
    n_blocks = -(-(n_assign + N_EXPERTS * (MOE_BLOCK - 1)) // MOE_BLOCK)
    n_slots = n_blocks * MOE_BLOCK
    token = jnp.arange(n_assign, dtype=jnp.int32) // TOP_K
    slot_tok = jnp.zeros((n_slots,), jnp.int32).at[slot_of.reshape(-1)].set(token, unique_indices=True)
    block_start = jnp.arange(n_blocks, dtype=jnp.int32) * MOE_BLOCK
    block_e = jnp.minimum(jnp.searchsorted(pad_end, block_start, side='right'), N_EXPERTS - 1)
    n_used = (pad_end[-1] // MOE_BLOCK).reshape(1)
    return gates, slot_tok, slot_of, block_e.astype(jnp.int32), n_used.astype(jnp.int32)


def kernel(x, c, ctx, c_ctx, w_mod, b_mod, g_norm1, g_norm2, w_in, b_gate, mla_g_q, mla_w_uq, mla_g_kv, mla_w_ukv, mla_g_qn, mla_g_kn, dif_g_qn, dif_g_kn, dif_lambda, dif_g_sub, ssm_conv_w, ssm_conv_b, ssm_dt_bias, ssm_a_log, ssm_d, ssm_g_norm, w_up_mla, w_up_dif, w_up_ssm, w_out, moe_w_router, moe_b_router, moe_w_gu, moe_b_gu, moe_w_down, moe_b_down):
    assert x.shape[0] == 1 and ctx.shape[0] == 1
    depth = w_in.shape[0]
    seq = x.shape[1]
    n_ctx = ctx.shape[1]
    n_tok = n_ctx + seq
    d = D_MODEL
    assert n_ctx == KEY_CHUNK and n_tok % ROW_TILE == 0 and seq % GRID_W == 0

    x_all = jnp.concatenate([ctx[0], x[0]], axis=0)
    cc = jnp.zeros((8, d), F32).at[0].set(c[0]).at[1].set(c_ctx)
    mod = _mod_vectors(cc, w_mod, b_mod)
    mod = mod[:, :2].reshape(depth, 2, 6, d)

    rope_mla = _rope_tables(seq, n_ctx, MLA_ROPE, MLA_NOPE)
    rope_dif = _rope_tables(seq, n_ctx, DIF_HEAD_DIM, 0)
    w_in_all = _take_columns(w_in, _in_proj_columns()).astype(BF16)
    w_uq_all = _take_columns(mla_w_uq, _head_columns(MLA_HEADS, MLA_QK, 0, MLA_QK, LANES)).astype(BF16)
    w_uk_all = _take_columns(mla_w_ukv, _head_columns(MLA_HEADS, MLA_NOPE + MLA_V, 0, MLA_NOPE, LANES)).astype(BF16)
    w_uv_all = _take_columns(mla_w_ukv, _head_columns(MLA_HEADS, MLA_NOPE + MLA_V, MLA_NOPE, MLA_V, MLA_V)).astype(BF16)
    w_up_all = [w.astype(BF16) for w in (w_up_mla, w_up_dif, w_up_ssm, w_out)]

    for i in range(depth):
        lam_init = 0.8 - 0.6 * math.exp(-0.3 * i)
        mod_lat, mod_ctx = mod[i, 0], mod[i, 1]
        u = _in_proj(x_all, g_norm1[i][None], mod_lat[0:2], mod_ctx[0:2], w_in_all, i, n_ctx)

        bound_mla = (MLA_QK ** 0.5 * LOG2E) * jnp.max(jnp.abs(mla_g_qn[i])) * jnp.max(jnp.abs(mla_g_kn[i]))
        bound_dif = (DIF_HEAD_DIM ** 0.5 * LOG2E) * jnp.max(jnp.abs(dif_g_qn[i])) * jnp.max(jnp.abs(dif_g_kn[i]))
        conv_w = jnp.zeros((8, SSM_XBC), F32).at[:SSM_CONV].set(ssm_conv_w[i])
        mla_qkv, dif_qkv, (xbc, dt) = _run_stages("branch_prep", (n_tok // ROW_TILE,), [
            _mla_prep(u, mla_g_q[i][None], w_uq_all[i], mla_g_kv[i][None], w_uk_all[i], w_uv_all[i],
                      _pad_lanes(mla_g_qn[i][None]), _pad_lanes(mla_g_kn[i][None]),
                      jnp.full((1, LANES), bound_mla, F32), rope_mla),
            _dif_prep(u, _pad_lanes(dif_g_qn[i][None]), _pad_lanes(dif_g_kn[i][None]),
                      jnp.full((1, LANES), bound_dif, F32), rope_dif),
            _ssm_prep(u, conv_w, ssm_conv_b[i][None], _pad_lanes(ssm_dt_bias[i].reshape(1, -1)), n_ctx)])
        ya = lax.cond(bound_mla <= MAX_SHIFT, functools.partial(_attention, shifted=True, group=SHIFTED_GROUP),
                      functools.partial(_attention, shifted=False), *mla_qkv)

        lam_rows = jnp.zeros((8, LANES), F32).at[:4, :DIF_HEAD_DIM].set(dif_lambda[i])
        dif_attention = functools.partial(_attention, lam=lam_rows, gsub=dif_g_sub[i][None], lam_init=lam_init)
        yb = lax.cond(bound_dif <= MAX_SHIFT, functools.partial(dif_attention, shifted=True, group=SHIFTED_GROUP),
                      functools.partial(dif_attention, shifted=False), *dif_qkv)

        a_log = _pad_lanes(ssm_a_log[i].reshape(1, -1))
        yf, yr = _ssd(xbc, dt, a_log, n_ctx)

        dskip = jnp.repeat(ssm_d[i, 0] + ssm_d[i, 1], SSM_HEAD_DIM)[None]
        w_router = jnp.zeros((d, LANES), BF16).at[:, :N_EXPERTS].set(moe_w_router[i].astype(BF16))
        b_router = jnp.zeros((1, LANES), F32).at[0, :N_EXPERTS].set(moe_b_router[i])
        x_all, f, routed, counts = _merge(
            x_all, u, ya, yb, yf, yr, xbc, b_gate[i][None], dskip, ssm_g_norm[i][None],
            w_up_all[0][i], w_up_all[1][i], w_up_all[2][i], w_up_all[3][i],
            mod_lat[0:5], mod_ctx[0:5], g_norm2[i][None],
            w_router, b_router, n_ctx)

        gates, slot_tok, slot_of, block_e, n_used = _route(routed, counts, n_tok)
        y_slots = _moe_experts(block_e, n_used, f[slot_tok], moe_w_gu,
                               moe_b_gu.reshape(depth, N_EXPERTS, 1, 2 * D_FF), moe_w_down,
                               moe_b_down.reshape(depth, N_EXPERTS, 1, d), i)
        y_rows = [y_slots[slot_of[:, k]] for k in range(TOP_K)]
        x_all = _combine(x_all, y_rows, _pad_lanes(gates), mod_lat[5:6], mod_ctx[5:6], n_ctx)
    return x_all[n_ctx:][None]
```
